```python
import jax, jax.numpy as jnp
from jax import lax
import numpy as np

D_MODEL = 1024
BATCH = 32
SEQ = 256
DEPTH = 1
DEC_BATCH = 2
DEC_SEQ = 4096
PAST_LEN = 256

GRID_W = 64
C_CONV = D_MODEL // 2
CONV_K = 31
GLA_WIDTH = D_MODEL - C_CONV
N_GLA_HEADS = 4
DV_HEAD = GLA_WIDTH // N_GLA_HEADS
DK_HEAD = DV_HEAD // 2
DK_TOT = DK_HEAD * N_GLA_HEADS
DV_TOT = DV_HEAD * N_GLA_HEADS
GATE_RANK = 16
GATE_TEMP = 16.0
CHUNK = 64
N_EXPERTS = 32
TOP_K = 4
D_FF = D_MODEL
SWIGLU_LIMIT = 7.0
SWIGLU_ALPHA = 1.702
EPS = 1e-6
IN_COLS = 2 * C_CONV + 2 * DK_TOT + 2 * DV_TOT + 2 * GATE_RANK
SPLITS = list(np.cumsum([C_CONV, C_CONV, DK_TOT, DK_TOT, DV_TOT, DV_TOT, GATE_RANK])[:])

kernel_name = "hybrid_conv_gla_moe_diffusion_step"


def rmsnorm(x, g):
    xf = x.astype(jnp.float32)
    y = xf * lax.rsqrt(jnp.mean(xf * xf, axis=-1, keepdims=True) + EPS)
    return (y * g.astype(jnp.float32)).astype(x.dtype)


def depthwise_conv(u, w, b):
    out = lax.conv_general_dilated(
        u, w[:, None, :].astype(u.dtype), window_strides=(1,),
        padding=[(CONV_K // 2, CONV_K // 2)],
        dimension_numbers=("NWC", "WIO", "NWC"), feature_group_count=u.shape[-1])
    return out + b.astype(u.dtype)


def conv_module(u_a, u_b, w, b, ln_g, ln_b, grid):
    B, L, C = u_a.shape
    h = u_a * jax.nn.sigmoid(u_b)
    if grid:
        rows = L // GRID_W
        h = depthwise_conv(h.reshape(B * rows, GRID_W, C), w, b).reshape(B, L, C)
    else:
        h = depthwise_conv(h, w, b)
    hf = h.astype(jnp.float32)
    mu = jnp.mean(hf, axis=-1, keepdims=True)
    var = jnp.mean(jnp.square(hf - mu), axis=-1, keepdims=True)
    hf = (hf - mu) * lax.rsqrt(var + EPS) * ln_g.astype(jnp.float32) + ln_b.astype(jnp.float32)
    return jax.nn.silu(hf).astype(u_a.dtype)


def gla_chunked(q, k, v, log_a, s0):
    B, L = q.shape[:2]
    n = L // CHUNK
    blk = lambda t: t.astype(jnp.float32).reshape(B, n, CHUNK, *t.shape[2:])
    q, k, v, la = blk(q), blk(k), blk(v), blk(log_a)
    bcum = jnp.cumsum(la, axis=2)
    b_last = bcum[:, :, -1]
    q_t = q * jnp.exp(bcum) * (DK_HEAD ** -0.5)
    k_t = k * jnp.exp(-bcum)
    k_end = k * jnp.exp(b_last[:, :, None] - bcum)
    att = jnp.einsum("bnthd,bnshd->bnhts", q_t, k_t)
    causal = jnp.tril(jnp.ones((CHUNK, CHUNK), dtype=bool))
    att = jnp.where(causal, att, 0.0)
    o_intra = jnp.einsum("bnhts,bnshe->bnthe", att, v)
    upd = jnp.einsum("bnshd,bnshe->bnhde", k_end, v)
    dec = jnp.exp(b_last)

    def step(S, inp):
        d, u = inp
        return d[..., None] * S + u, S

    s_final, s_prev = lax.scan(step, s0.astype(jnp.float32),
                               (jnp.moveaxis(dec, 1, 0), jnp.moveaxis(upd, 1, 0)))
    s_prev = jnp.moveaxis(s_prev, 0, 1)
    o_inter = jnp.einsum("bnthd,bnhde->bnthe", q_t, s_prev)
    o = (o_intra + o_inter).reshape(B, L, q.shape[3], v.shape[-1])
    return o, s_final


def gla_bidir(q, k, v, la_f, la_b, s0_f, s0_b):
    flip = lambda t: t[:, ::-1]
    o_f, s_f = gla_chunked(q, k, v, la_f, s0_f)
    o_b, s_b = gla_chunked(flip(q), flip(k), flip(v), flip(la_b), s0_b)
    return o_f + flip(o_b), s_f, s_b


def token_mixer(h, grid, s0_f, s0_b, w_in, conv_w, conv_b, conv_ln_g, conv_ln_b,
                gate_w, gate_b, gla_norm_g, w_out):
    B, L, _ = h.shape
    proj = h @ w_in
    u_a, u_b, q, k, v, g, low_f, low_b = jnp.split(proj, SPLITS, axis=-1)
    conv_out = conv_module(u_a, u_b, conv_w, conv_b, conv_ln_g, conv_ln_b, grid)
    heads = lambda t, d: t.reshape(B, L, N_GLA_HEADS, d)
    la_f = jax.nn.log_sigmoid((low_f @ gate_w[0] + gate_b[0]).astype(jnp.float32)) / GATE_TEMP
    la_b = jax.nn.log_sigmoid((low_b @ gate_w[1] + gate_b[1]).astype(jnp.float32)) / GATE_TEMP
    o, s_f, s_b = gla_bidir(heads(q, DK_HEAD), heads(k, DK_HEAD), heads(v, DV_HEAD),
                            heads(la_f, DK_HEAD), heads(la_b, DK_HEAD), s0_f, s0_b)
    o = o * lax.rsqrt(jnp.mean(o * o, axis=-1, keepdims=True) + EPS) * gla_norm_g.astype(jnp.float32)
    o = (o.reshape(B, L, DV_TOT) * jax.nn.silu(g.astype(jnp.float32))).astype(h.dtype)
    y = jnp.concatenate([conv_out, o], axis=-1) @ w_out
    return y, s_f, s_b


def moe(h, router_w, router_b, w_gu, b_gu, w_dn, b_dn):
    B, L, D = h.shape
    t = h.reshape(B * L, D)
    logits = (t @ router_w + router_b).astype(jnp.float32)
    vals, idx = lax.top_k(logits, TOP_K)
    gates = jax.nn.softmax(vals, axis=-1)
    combine = jnp.sum(jax.nn.one_hot(idx, N_EXPERTS, dtype=jnp.float32) * gates[..., None], axis=1)
    acc = jnp.zeros((B * L, D), jnp.float32)
    for e in range(N_EXPERTS):
        gu = t @ w_gu[e] + b_gu[e]
        gate = jnp.minimum(gu[:, :D_FF], SWIGLU_LIMIT)
        up = jnp.clip(gu[:, D_FF:], -SWIGLU_LIMIT, SWIGLU_LIMIT)
        act = gate * jax.nn.sigmoid(SWIGLU_ALPHA * gate) * (up + 1.0)
        out = act @ w_dn[e] + b_dn[e]
        acc = acc + combine[:, e:e + 1] * out.astype(jnp.float32)
    return acc.astype(h.dtype).reshape(B, L, D)


def trunk_layer(x, mod, grid, s0_f, s0_b, n1g, n2g, w_in, conv_w, conv_b, conv_ln_g,
                conv_ln_b, gate_w, gate_b, gla_norm_g, w_out, router_w, router_b,
                w_gu, b_gu, w_dn, b_dn):
    shift1, scale1, gate1, shift2, scale2, gate2 = jnp.split(mod.astype(x.dtype), 6, axis=-1)
    h = rmsnorm(x, n1g) * (1.0 + scale1) + shift1
    y, s_f, s_b = token_mixer(h, grid, s0_f, s0_b, w_in, conv_w, conv_b, conv_ln_g,
                              conv_ln_b, gate_w, gate_b, gla_norm_g, w_out)
    x = x + gate1 * y
    h = rmsnorm(x, n2g) * (1.0 + scale2) + shift2
    x = x + gate2 * moe(h, router_w, router_b, w_gu, b_gu, w_dn, b_dn)
    return x, s_f, s_b


def setup_inputs(seed: int = 0) -> dict:
    key = jax.random.key(seed)
    ks = jax.random.split(key, 32)
    nrm = lambda k, shape, s: jax.random.normal(k, shape, jnp.float32) * s
    D, E, F = D_MODEL, N_EXPERTS, D_FF
    return {
        "x_prompt": nrm(ks[0], (BATCH, SEQ, D), 1.0),
        "x_sample": nrm(ks[1], (DEC_BATCH, DEC_SEQ, D), 1.0),
        "state_gla": nrm(ks[2], (DEC_BATCH, DEPTH, 2, N_GLA_HEADS, DK_HEAD, DV_HEAD), 1.0),
        "c": nrm(ks[3], (DEC_BATCH, D), 1.0),
        "c_ctx": nrm(ks[4], (D,), 1.0),
        "ada_w": nrm(ks[5], (DEPTH, D, 6 * D), D ** -0.5),
        "ada_b": nrm(ks[6], (DEPTH, 6 * D), 0.02),
        "norm1_g": 1.0 + nrm(ks[7], (DEPTH, D), 0.02),
        "norm2_g": 1.0 + nrm(ks[8], (DEPTH, D), 0.02),
        "w_in": nrm(ks[9], (DEPTH, D, IN_COLS), D ** -0.5),
        "conv_w": nrm(ks[10], (DEPTH, CONV_K, C_CONV), CONV_K ** -0.5),
        "conv_b": nrm(ks[11], (DEPTH, C_CONV), 0.02),
        "conv_ln_g": 1.0 + nrm(ks[12], (DEPTH, C_CONV), 0.02),
        "conv_ln_b": nrm(ks[13], (DEPTH, C_CONV), 0.02),
        "gate_w": nrm(ks[14], (DEPTH, 2, GATE_RANK, DK_TOT), GATE_RANK ** -0.5),
        "gate_b": 1.0 + nrm(ks[15], (DEPTH, 2, DK_TOT), 0.1),
        "gla_norm_g": 1.0 + nrm(ks[16], (DEPTH, DV_HEAD), 0.02),
        "w_out": nrm(ks[17], (DEPTH, D, D), D ** -0.5),
        "router_w": nrm(ks[18], (DEPTH, D, E), D ** -0.5),
        "router_b": nrm(ks[19], (DEPTH, E), 0.01),
        "moe_w_gu": nrm(ks[20], (DEPTH, E, D, 2 * F), D ** -0.5),
        "moe_b_gu": nrm(ks[21], (DEPTH, E, 2 * F), 0.01),
        "moe_w_dn": nrm(ks[22], (DEPTH, E, F, D), F ** -0.5),
        "moe_b_dn": nrm(ks[23], (DEPTH, E, D), 0.01),
        "final_g": 1.0 + nrm(ks[24], (D,), 0.02),
    }


def reference(x_prompt, x_sample, state_gla, c, c_ctx, ada_w, ada_b, norm1_g, norm2_g,
              w_in, conv_w, conv_b, conv_ln_g, conv_ln_b, gate_w, gate_b, gla_norm_g,
              w_out, router_w, router_b, moe_w_gu, moe_b_gu, moe_w_dn, moe_b_dn, final_g):
    bp = x_prompt.shape[0]
    zero_state = jnp.zeros((bp, N_GLA_HEADS, DK_HEAD, DV_HEAD), jnp.float32)
    hp, hs = x_prompt, x_sample
    ctx_states = []
    for l in range(DEPTH):
        mod_ctx = (jax.nn.silu(c_ctx) @ ada_w[l] + ada_b[l])[None, None, :]
        mod_lat = (jax.nn.silu(c) @ ada_w[l] + ada_b[l])[:, None, :]
        wl = (norm1_g[l], norm2_g[l], w_in[l], conv_w[l], conv_b[l], conv_ln_g[l],
              conv_ln_b[l], gate_w[l], gate_b[l], gla_norm_g[l], w_out[l], router_w[l],
              router_b[l], moe_w_gu[l], moe_b_gu[l], moe_w_dn[l], moe_b_dn[l])
        hp, s_f, s_b = trunk_layer(hp, mod_ctx, False, zero_state, zero_state, *wl)
        ctx_states.append(jnp.stack([s_f, s_b], axis=1))
        hs, _, _ = trunk_layer(hs, mod_lat, True, state_gla[:, l, 0], state_gla[:, l, 1], *wl)
    new_state_gla = jnp.stack(ctx_states, axis=1).astype(x_prompt.dtype)
    y_prompt = rmsnorm(hp, final_g)
    y_sample = rmsnorm(hs, final_g)
    return (y_prompt, y_sample, new_state_gla)
```

```python
import functools

import numpy as np
import jax
import jax.numpy as jnp
from jax import lax
from jax.experimental import pallas as pl
from jax.experimental.pallas import tpu as pltpu

D_MODEL = 1024
GRID_W = 64
C_CONV = D_MODEL // 2
CONV_K = 31
N_GLA_HEADS = 4
DV_HEAD = 128
DK_HEAD = 64
DK_TOT = DK_HEAD * N_GLA_HEADS
DV_TOT = DV_HEAD * N_GLA_HEADS
GATE_RANK = 16
GATE_TEMP = 16.0
CHUNK = 64
N_EXPERTS = 32
TOP_K = 4
D_FF = D_MODEL
SWIGLU_LIMIT = 7.0
SWIGLU_ALPHA = 1.702
EPS = 1e-6

LANES = 128
TM = 256
TE = 512
HALF = D_MODEL // 2
CONV_PAD = 16
COPY_ROWS = 1024
VMEM_LIMIT = 56 * 1024 * 1024

F32 = jnp.float32
BF16 = jnp.bfloat16
HI = lax.Precision.HIGHEST
NEG = -1e30


def _cparams(sem, vmem=None):
    return pltpu.CompilerParams(dimension_semantics=sem, vmem_limit_bytes=vmem)


def _pack_halves(lo, hi):
    lo = pltpu.bitcast(lo.astype(BF16).astype(F32), jnp.uint32)
    hi = pltpu.bitcast(hi.astype(BF16).astype(F32), jnp.uint32)
    return (lo >> 16) | (hi & jnp.uint32(0xFFFF0000))


def _unpack_halves(w):
    lo = pltpu.bitcast(w << 16, F32)
    hi = pltpu.bitcast(w & jnp.uint32(0xFFFF0000), F32)
    return lo, hi


def _mod_kernel(ct_ref, w_ref, b_ref, o_ref, *, n_cond):
    ct = ct_ref[...]
    s = ct * jax.nn.sigmoid(ct)
    w = w_ref[...]
    rows = [jnp.sum(s[:, r:r + 1] * w, axis=0, keepdims=True) + b_ref[...] for r in range(n_cond)]
    rows.append(jnp.zeros((8 - n_cond, w.shape[1]), F32))
    o_ref[...] = jnp.concatenate(rows, axis=0)


def _modulation(cond_t, ada_w, ada_b, n_cond):
    d, n = ada_w.shape
    nb = 768
    return pl.pallas_call(
        functools.partial(_mod_kernel, n_cond=n_cond),
        grid=(n // nb,),
        in_specs=[pl.BlockSpec((d, 8), lambda i: (0, 0)),
                  pl.BlockSpec((d, nb), lambda i: (0, i)),
                  pl.BlockSpec((1, nb), lambda i: (0, i))],
        out_specs=pl.BlockSpec((8, nb), lambda i: (0, i)),
        out_shape=jax.ShapeDtypeStruct((8, n), F32),
        compiler_params=_cparams(("arbitrary",)),
        name="mod",
    )(cond_t, ada_w, ada_b.reshape(1, n))


def _rms(x, g):
    return x * lax.rsqrt(jnp.mean(x * x, axis=-1, keepdims=True) + EPS) * g


def _inproj_kernel(xp_ref, xs_ref, mod_ref, g1_ref, wu_ref, wqk_ref, wv_ref, wg_ref, wlow_ref,
                   gw_ref, gb_ref, u_ref, qk_ref, v_ref, g_ref, la_ref, *, n_ctx_tiles):
    i = pl.program_id(0)
    x = jnp.where(i < n_ctx_tiles, xp_ref[...], xs_ref[...])
    shift = mod_ref[0, :, 0:D_MODEL]
    scale = mod_ref[0, :, D_MODEL:2 * D_MODEL]
    h = (_rms(x, g1_ref[...]) * (1.0 + scale) + shift).astype(BF16)
    dot = functools.partial(jnp.dot, preferred_element_type=F32)
    u_ref[...] = dot(h, wu_ref[...]).astype(BF16)
    qk_ref[...] = dot(h, wqk_ref[...]).astype(BF16)
    v_ref[...] = dot(h, wv_ref[...]).astype(BF16)
    g_ref[...] = dot(h, wg_ref[...]).astype(BF16)
    low = dot(h, wlow_ref[...])
    z = jnp.dot(low, gw_ref[...], precision=HI, preferred_element_type=F32) + gb_ref[...]
    la_ref[...] = (jnp.minimum(z, 0.0) - jnp.log1p(jnp.exp(-jnp.abs(z)))) * (1.0 / GATE_TEMP)


def _inproj(xp, xs, mod3, g1, w_in, gate_w, gate_b, lat_tiles_per_seq):
    n_c, n_l = xp.shape[0] // TM, xs.shape[0] // TM
    t_all = xp.shape[0] + xs.shape[0]
    sp = np.cumsum([0, C_CONV, C_CONV, DK_TOT, DK_TOT, DV_TOT, DV_TOT, 2 * GATE_RANK])
    wb = w_in.astype(BF16)
    wu, wqk, wv, wg, wlow = (wb[:, sp[0]:sp[2]], wb[:, sp[2]:sp[4]], wb[:, sp[4]:sp[5]],
                             wb[:, sp[5]:sp[6]], wb[:, sp[6]:sp[7]])
    gw = jnp.zeros((2 * GATE_RANK, 2 * DK_TOT), F32)
    gw = gw.at[:GATE_RANK, :DK_TOT].set(gate_w[0]).at[GATE_RANK:, DK_TOT:].set(gate_w[1])
    gb = gate_b.reshape(1, 2 * DK_TOT)
    const = lambda a: pl.BlockSpec(a.shape, lambda i: (0,) * a.ndim)
    row = lambda w: pl.BlockSpec((TM, w), lambda i: (i, 0))
    seg = lambda i: jnp.where(i < n_c, 0, 1 + jnp.maximum(i - n_c, 0) // lat_tiles_per_seq)
    return pl.pallas_call(
        functools.partial(_inproj_kernel, n_ctx_tiles=n_c),
        grid=(n_c + n_l,),
        in_specs=[pl.BlockSpec((TM, D_MODEL), lambda i: (jnp.minimum(i, n_c - 1), 0)),
                  pl.BlockSpec((TM, D_MODEL), lambda i: (jnp.maximum(i - n_c, 0), 0)),
                  pl.BlockSpec((1, 1, 6 * D_MODEL), lambda i: (seg(i), 0, 0)),
                  const(g1), const(wu), const(wqk), const(wv), const(wg), const(wlow),
                  const(gw), const(gb)],
        out_specs=[row(2 * C_CONV), row(2 * DK_TOT), row(DV_TOT), row(DV_TOT), row(2 * DK_TOT)],
        out_shape=[jax.ShapeDtypeStruct((t_all, 2 * C_CONV), BF16),
                   jax.ShapeDtypeStruct((t_all, 2 * DK_TOT), BF16),
                   jax.ShapeDtypeStruct((t_all, DV_TOT), BF16),
                   jax.ShapeDtypeStruct((t_all, DV_TOT), BF16),
                   jax.ShapeDtypeStruct((t_all, 2 * DK_TOT), F32)],
        compiler_params=_cparams(("arbitrary",), VMEM_LIMIT),
        name="inproj",
    )(xp, xs, mod3, g1, wu, wqk, wv, wg, wlow, gw, gb)


def _gla_direction(qk_ref, v_ref, la_ref, tri_ref, o_ref, s_scr, reverse):
    la = la_ref[...]
    bcum = jnp.dot(tri_ref[...], la, precision=HI, preferred_element_type=F32)
    q = qk_ref[:, 0:DK_TOT].astype(F32)
    k = qk_ref[:, DK_TOT:2 * DK_TOT].astype(F32)
    lane = lax.broadcasted_iota(jnp.int32, (CHUNK, LANES), 1)
    row2 = lax.broadcasted_iota(jnp.int32, (2 * CHUNK, CHUNK), 0) % CHUNK
    col2 = lax.broadcasted_iota(jnp.int32, (2 * CHUNK, CHUNK), 1)
    keep = (col2 >= row2) if reverse else (col2 <= row2)
    srow = lax.broadcasted_iota(jnp.int32, (LANES, LANES), 0)
    ones = jnp.ones((CHUNK, LANES), F32)
    n_chunks = TM // CHUNK
    order = range(n_chunks - 1, -1, -1) if reverse else range(n_chunks)
    nt = (((1,), (1,)), ((), ()))
    tn = (((0,), (0,)), ((), ()))
    for c in order:
        r0 = c * CHUNK
        bc = bcum[r0:r0 + CHUNK]
        la_c = la[r0:r0 + CHUNK]
        bl = bc[0:1] if reverse else bc[CHUNK - 1:CHUNK]
        qt = q[r0:r0 + CHUNK] * jnp.exp(bc) * (DK_HEAD ** -0.5)
        kt = k[r0:r0 + CHUNK] * jnp.exp(-bc)
        ke = k[r0:r0 + CHUNK] * jnp.exp(bl - bc)
        for p in range(2):
            cs = slice(p * LANES, (p + 1) * LANES)
            qt_p = qt[:, cs]
            qs = jnp.concatenate([jnp.where(lane < DK_HEAD, qt_p, 0.0),
                                  jnp.where(lane >= DK_HEAD, qt_p, 0.0)], axis=0).astype(BF16)
            att = lax.dot_general(qs, kt[:, cs].astype(BF16), nt, preferred_element_type=F32)
            att = jnp.where(keep, att, 0.0).astype(BF16)
            s_p = s_scr[p]
            o_inter = jnp.dot(qs, s_p.astype(BF16), preferred_element_type=F32)
            ke_p = ke[:, cs].astype(BF16)
            upd = []
            for hh in range(2):
                h = 2 * p + hh
                v_h = v_ref[r0:r0 + CHUNK, h * DV_HEAD:(h + 1) * DV_HEAD]
                o_h = jnp.dot(att[hh * CHUNK:(hh + 1) * CHUNK], v_h, preferred_element_type=F32)
                o_ref[r0:r0 + CHUNK, h * DV_HEAD:(h + 1) * DV_HEAD] = (
                    o_h + o_inter[hh * CHUNK:(hh + 1) * CHUNK])
                upd.append(lax.dot_general(ke_p, v_h, tn, preferred_element_type=F32))
            bl_col = lax.dot_general(la_c[:, cs], ones, tn, precision=HI, preferred_element_type=F32)
            s_scr[p] = jnp.exp(bl_col) * s_p + jnp.where(srow < DK_HEAD, upd[0], upd[1])


def _gla_kernel(qkf_ref, vf_ref, laf_ref, qkb_ref, vb_ref, lab_ref, tril_ref, triu_ref, s0_ref,
                of_ref, ob_ref, sout_ref, sf_scr, sb_scr, *, n_tiles):
    j = pl.program_id(1)

    @pl.when(j == 0)
    def _():
        sf_scr[...] = s0_ref[0, 0]
        sb_scr[...] = s0_ref[0, 1]

    _gla_direction(qkf_ref, vf_ref, laf_ref, tril_ref, of_ref, sf_scr, False)
    _gla_direction(qkb_ref, vb_ref, lab_ref, triu_ref, ob_ref, sb_scr, True)

    @pl.when(j == n_tiles - 1)
    def _():
        sout_ref[0, 0] = sf_scr[...]
        sout_ref[0, 1] = sb_scr[...]


def _gla(qk, v, la, s0, n_seq, n_tiles, tile_off):
    t_path = n_seq * n_tiles * TM
    blk = np.arange(TM) // CHUNK
    same = blk[:, None] == blk[None, :]
    r = np.arange(TM)
    tril = jnp.asarray((same & (r[None, :] <= r[:, None])).astype(np.float32))
    triu = jnp.asarray((same & (r[None, :] >= r[:, None])).astype(np.float32))
    fwd = lambda s, j: s * n_tiles + j + tile_off
    bwd = lambda s, j: s * n_tiles + (n_tiles - 1 - j) + tile_off
    ofwd = lambda s, j: s * n_tiles + j
    obwd = lambda s, j: s * n_tiles + (n_tiles - 1 - j)
    tok = lambda w, f, cb=0: pl.BlockSpec((TM, w), lambda s, j: (f(s, j), cb))
    const = lambda a: pl.BlockSpec(a.shape, lambda s, j: (0,) * a.ndim)
    st = pl.BlockSpec((1, 2, 2, LANES, LANES), lambda s, j: (s, 0, 0, 0, 0))
    return pl.pallas_call(
        functools.partial(_gla_kernel, n_tiles=n_tiles),
        grid=(n_seq, n_tiles),
        in_specs=[tok(2 * DK_TOT, fwd), tok(DV_TOT, fwd), tok(DK_TOT, fwd, 0),
                  tok(2 * DK_TOT, bwd), tok(DV_TOT, bwd), tok(DK_TOT, bwd, 1),
                  const(tril), const(triu), st],
        out_specs=[tok(DV_TOT, ofwd), tok(DV_TOT, obwd), st],
        out_shape=[jax.ShapeDtypeStruct((t_path, DV_TOT), F32),
                   jax.ShapeDtypeStruct((t_path, DV_TOT), F32),
                   jax.ShapeDtypeStruct(s0.shape, F32)],
        scratch_shapes=[pltpu.VMEM((2, LANES, LANES), F32), pltpu.VMEM((2, LANES, LANES), F32)],
        compiler_params=_cparams(("arbitrary", "arbitrary"), VMEM_LIMIT),
        name="gla",
    )(qk, v, la, qk, v, la, tril, triu, s0)


def _mix_kernel(x_ref, u_ref, g_ref, of_ref, ob_ref, mod_ref, cw_ref, cb_ref, lng_ref, lnb_ref,
                gng_ref, wout_ref, n2g_ref, rw_ref, rb_ref, lt_ref, cin_ref,
                x1_ref, h2_ref, route_ref, gates_ref, cnt_ref, pad_scr, cnt_scr, *, seq_len):
    i = pl.program_id(0)
    n_seq = TM // seq_len

    @pl.when(i == 0)
    def _():
        cnt_scr[...] = cin_ref[...]

    u = u_ref[...].astype(F32)
    hg = u[:, 0:C_CONV] * jax.nn.sigmoid(u[:, C_CONV:2 * C_CONV])
    zero = jnp.zeros((n_seq, CONV_PAD, C_CONV), F32)
    pad_scr[:, 0:CONV_PAD, :] = zero
    pad_scr[:, CONV_PAD + seq_len:2 * CONV_PAD + seq_len, :] = zero
    pad_scr[:, CONV_PAD:CONV_PAD + seq_len, :] = hg.reshape(n_seq, seq_len, C_CONV)
    acc = jnp.zeros((n_seq, seq_len, C_CONV), F32)
    for tap in range(CONV_K):
        start = CONV_PAD - CONV_K // 2 + tap
        acc = acc + cw_ref[tap:tap + 1, :] * pad_scr[:, start:start + seq_len, :]
    cv = acc.reshape(TM, C_CONV) + cb_ref[...]
    mu = jnp.mean(cv, axis=-1, keepdims=True)
    var = jnp.mean(jnp.square(cv - mu), axis=-1, keepdims=True)
    cv = (cv - mu) * lax.rsqrt(var + EPS) * lng_ref[...] + lnb_ref[...]
    conv_out = (cv * jax.nn.sigmoid(cv)).astype(BF16)

    o = of_ref[...] + ob_ref[...]
    g = g_ref[...].astype(F32)
    heads = []
    for h in range(N_GLA_HEADS):
        cs = slice(h * DV_HEAD, (h + 1) * DV_HEAD)
        oh = o[:, cs]
        oh = oh * lax.rsqrt(jnp.mean(oh * oh, axis=-1, keepdims=True) + EPS) * gng_ref[...]
        gh = g[:, cs]
        heads.append((oh * (gh * jax.nn.sigmoid(gh))).astype(BF16))
    y = jnp.dot(conv_out, wout_ref[0:C_CONV, :], preferred_element_type=F32)
    for h in range(N_GLA_HEADS):
        y = y + jnp.dot(heads[h], wout_ref[C_CONV + h * DV_HEAD:C_CONV + (h + 1) * DV_HEAD, :],
                        preferred_element_type=F32)

    gate1 = mod_ref[0, :, 2 * D_MODEL:3 * D_MODEL]
    shift2 = mod_ref[0, :, 3 * D_MODEL:4 * D_MODEL]
    scale2 = mod_ref[0, :, 4 * D_MODEL:5 * D_MODEL]
    x1 = x_ref[...] + gate1 * y
    x1_ref[...] = x1
    h2 = _rms(x1, n2g_ref[...]) * (1.0 + scale2) + shift2
    h2_ref[...] = _pack_halves(h2[:, 0:HALF], h2[:, HALF:D_MODEL])

    logits = jnp.dot(h2, rw_ref[...], precision=HI, preferred_element_type=F32) + rb_ref[...]
    lane = lax.broadcasted_iota(jnp.int32, (TM, LANES), 1)
    lanef = lane.astype(F32)
    l = logits
    vals, idxs, hots = [], [], []
    for _ in range(TOP_K):
        m = jnp.max(l, axis=-1, keepdims=True)
        idx = jnp.min(jnp.where(l == m, lanef, float(LANES)), axis=-1, keepdims=True)
        hot = lanef == idx
        l = jnp.where(hot, -jnp.inf, l)
        vals.append(m)
        idxs.append(idx)
        hots.append(hot)
    es = [jnp.exp(vk - vals[0]) for vk in vals]
    inv = 1.0 / (es[0] + es[1] + es[2] + es[3])
    sel = jnp.zeros((TM, LANES), F32)
    for hot in hots:
        sel = jnp.where(hot, 1.0, sel)
    cum = jnp.dot(lt_ref[...], sel.astype(BF16), preferred_element_type=F32) + cnt_scr[0:1, :]
    route = jnp.zeros((TM, LANES), F32)
    gates = jnp.zeros((TM, LANES), F32)
    for kk in range(TOP_K):
        rank = jnp.sum(jnp.where(hots[kk], cum, 0.0), axis=-1, keepdims=True)
        route = jnp.where(lane == kk, idxs[kk], route)
        route = jnp.where(lane == TOP_K + kk, rank, route)
        gates = jnp.where(lane == kk, es[kk] * inv, gates)
    route_ref[...] = route.astype(jnp.int32)
    gates_ref[...] = gates
    cnt = cnt_scr[...] + jnp.sum(sel, axis=0, keepdims=True)
    cnt_scr[...] = cnt
    cnt_ref[...] = cnt


def _mix(x, u, g, o_f, o_b, mod3, seg_fn, tile_off, seq_len, counts_in, wts):
    (conv_w, conv_b, ln_g, ln_b, gng, wout, n2g, rw, rb) = wts
    t_path = x.shape[0]
    n = t_path // TM
    lt = jnp.asarray(np.tril(np.ones((TM, TM), np.float32), -1)).astype(BF16)
    const = lambda a: pl.BlockSpec(a.shape, lambda i: (0,) * a.ndim)
    loc = lambda w: pl.BlockSpec((TM, w), lambda i: (i, 0))
    uni = lambda w: pl.BlockSpec((TM, w), lambda i: (i + tile_off, 0))
    n_seq = TM // seq_len
    return pl.pallas_call(
        functools.partial(_mix_kernel, seq_len=seq_len),
        grid=(n,),
        in_specs=[loc(D_MODEL), uni(2 * C_CONV), uni(DV_TOT), loc(DV_TOT), loc(DV_TOT),
                  pl.BlockSpec((1, 1, 6 * D_MODEL), lambda i: (seg_fn(i), 0, 0)),
                  const(conv_w), const(conv_b), const(ln_g), const(ln_b), const(gng), const(wout),
                  const(n2g), const(rw), const(rb), const(lt), const(counts_in)],
        out_specs=[loc(D_MODEL), loc(HALF), loc(LANES), loc(LANES),
                   pl.BlockSpec((8, LANES), lambda i: (0, 0))],
        out_shape=[jax.ShapeDtypeStruct((t_path, D_MODEL), F32),
                   jax.ShapeDtypeStruct((t_path, HALF), jnp.uint32),
                   jax.ShapeDtypeStruct((t_path, LANES), jnp.int32),
                   jax.ShapeDtypeStruct((t_path, LANES), F32),
                   jax.ShapeDtypeStruct((8, LANES), F32)],
        scratch_shapes=[pltpu.VMEM((n_seq, seq_len + 2 * CONV_PAD, C_CONV), F32),
                        pltpu.VMEM((8, LANES), F32)],
        compiler_params=_cparams(("arbitrary",), VMEM_LIMIT),
        name="mix",
    )(x, u, g, o_f, o_b, mod3, conv_w, conv_b, ln_g, ln_b, gng, wout, n2g, rw, rb, lt, counts_in)


def _row_copy_kernel(idx_ref, src_ref, *rest, scatter):
    dst_ref, sem = rest[-2:]
    base = pl.program_id(0) * COPY_ROWS

    def copy(n):
        j = idx_ref[n]
        if scatter:
            s, d = lax.shift_right_logical(n, 2), j
        else:
            s, d = j, n
        return pltpu.make_async_copy(src_ref.at[pl.ds(s, 1)], dst_ref.at[pl.ds(d, 1)], sem)

    def start(r, carry):
        copy(base + r).start()
        return carry

    def wait(r, carry):
        copy(base + r).wait()
        return carry

    lax.fori_loop(0, COPY_ROWS, start, 0)
    lax.fori_loop(0, COPY_ROWS, wait, 0)


def _row_copy(idx, src, dst=None, scatter=False):
    n = idx.shape[0]
    any_spec = pl.BlockSpec(memory_space=pl.ANY)
    out_shape = (dst.shape if scatter else (n, src.shape[1]))
    return pl.pallas_call(
        functools.partial(_row_copy_kernel, scatter=scatter),
        grid_spec=pltpu.PrefetchScalarGridSpec(
            num_scalar_prefetch=1,
            grid=(n // COPY_ROWS,),
            in_specs=[any_spec, any_spec] if scatter else [any_spec],
            out_specs=any_spec,
            scratch_shapes=[pltpu.SemaphoreType.DMA]),
        out_shape=jax.ShapeDtypeStruct(out_shape, src.dtype),
        input_output_aliases={2: 0} if scatter else {},
        compiler_params=_cparams(("arbitrary",)),
        name="scatter_rows" if scatter else "gather_rows",
    )(*((idx, src, dst) if scatter else (idx, src)))


def _expert_kernel(te_ref, nv_ref, x_ref, wgu_ref, bgu_ref, wdn_ref, bdn_ref, y_ref, wgu_s, wdn_s):
    i = pl.program_id(0)

    @pl.when(i < nv_ref[0])
    def _():
        e = te_ref[i]
        prev = te_ref[jnp.maximum(i - 1, 0)]

        @pl.when(jnp.logical_or(i == 0, prev != e))
        def _():
            rows = 64

            def cast(r, carry):
                sl = pl.ds(pl.multiple_of(r * rows, rows), rows)
                wgu_s[sl, :] = wgu_ref[0, sl, :].astype(BF16)
                wdn_s[sl, :] = wdn_ref[0, sl, :].astype(BF16)
                return carry

            lax.fori_loop(0, D_MODEL // rows, cast, 0)

        lo, hi = _unpack_halves(x_ref[...])
        gu = (jnp.dot(lo.astype(BF16), wgu_s[0:HALF, :], preferred_element_type=F32)
              + jnp.dot(hi.astype(BF16), wgu_s[HALF:D_MODEL, :], preferred_element_type=F32)
              + bgu_ref[0])
        gate = jnp.minimum(gu[:, 0:D_FF], SWIGLU_LIMIT)
        up = jnp.clip(gu[:, D_FF:2 * D_FF], -SWIGLU_LIMIT, SWIGLU_LIMIT)
        act = (gate * jax.nn.sigmoid(SWIGLU_ALPHA * gate) * (up + 1.0)).astype(BF16)
        out = jnp.dot(act, wdn_s[...], preferred_element_type=F32) + bdn_ref[0]
        y_ref[...] = _pack_halves(out[:, 0:HALF], out[:, HALF:D_MODEL])

    @pl.when(i >= nv_ref[0])
    def _():
        y_ref[...] = jnp.zeros(y_ref.shape, y_ref.dtype)


def _experts(tile_expert, n_valid, xs, w_gu, b_gu, w_dn, b_dn):
    n_rows = xs.shape[0]
    nt = n_rows // TE
    row = lambda i, te, nv: (jnp.minimum(i, nv[0] - 1), 0)
    exp3 = lambda i, te, nv: (te[i], 0, 0)
    return pl.pallas_call(
        _expert_kernel,
        grid_spec=pltpu.PrefetchScalarGridSpec(
            num_scalar_prefetch=2,
            grid=(nt,),
            in_specs=[pl.BlockSpec((TE, HALF), row),
                      pl.BlockSpec((1, D_MODEL, 2 * D_FF), exp3),
                      pl.BlockSpec((1, 1, 2 * D_FF), exp3),
                      pl.BlockSpec((1, D_FF, D_MODEL), exp3),
                      pl.BlockSpec((1, 1, D_MODEL), exp3)],
            out_specs=pl.BlockSpec((TE, HALF), lambda i, te, nv: (i, 0)),
            scratch_shapes=[pltpu.VMEM((D_MODEL, 2 * D_FF), BF16), pltpu.VMEM((D_FF, D_MODEL), BF16)]),
        out_shape=jax.ShapeDtypeStruct((n_rows, HALF), jnp.uint32),
        compiler_params=_cparams(("arbitrary",), VMEM_LIMIT),
        name="experts",
    )(tile_expert, n_valid, xs, w_gu, b_gu.reshape(N_EXPERTS, 1, 2 * D_FF), w_dn,
      b_dn.reshape(N_EXPERTS, 1, D_MODEL))


def _final_kernel(x1_ref, yk_ref, gates_ref, mod_ref, fg_ref, y_ref):
    gates = gates_ref[...]
    lo = jnp.zeros((TM, HALF), F32)
    hi = jnp.zeros((TM, HALF), F32)
    for kk in range(TOP_K):
        a, b = _unpack_halves(yk_ref[kk])
        gk = gates[:, kk:kk + 1]
        lo = lo + gk * a
        hi = hi + gk * b
    gate2 = mod_ref[0, :, 5 * D_MODEL:6 * D_MODEL]
    x2 = x1_ref[...] + gate2 * jnp.concatenate([lo, hi], axis=-1)
    y_ref[...] = _rms(x2, fg_ref[...])


def _final(x1, yk, gates, mod3, seg_fn, tile_off, fg):
    t_path = x1.shape[0]
    return pl.pallas_call(
        _final_kernel,
        grid=(t_path // TM,),
        in_specs=[pl.BlockSpec((TM, D_MODEL), lambda i: (i, 0)),
                  pl.BlockSpec((TOP_K, TM, HALF), lambda i: (0, i + tile_off, 0)),
                  pl.BlockSpec((TM, LANES), lambda i: (i, 0)),
                  pl.BlockSpec((1, 1, 6 * D_MODEL), lambda i: (seg_fn(i), 0, 0)),
                  pl.BlockSpec((1, D_MODEL), lambda i: (0, 0))],
        out_specs=pl.BlockSpec((TM, D_MODEL), lambda i: (i, 0)),
        out_shape=jax.ShapeDtypeStruct((t_path, D_MODEL), F32),
        compiler_params=_cparams(("arbitrary",), VMEM_LIMIT),
        name="final",
    )(x1, yk, gates, mod3, fg)


def _layer(x_prompt, x_sample, state, c, c_ctx, ada_w, ada_b, norm1_g, norm2_g, w_in, conv_w, conv_b,
           conv_ln_g, conv_ln_b, gate_w, gate_b, gla_norm_g, w_out, router_w, router_b,
           moe_w_gu, moe_b_gu, moe_w_dn, moe_b_dn, final_g):
    bp, lp, d = x_prompt.shape
    bs, ls, _ = x_sample.shape
    assert lp == TM and ls % TM == 0 and d == D_MODEL
    xp = x_prompt.reshape(bp * lp, d)
    xs = x_sample.reshape(bs * ls, d)
    n_c, n_l = bp, bs * ls // TM
    lat_tiles = ls // TM
    t_all = (n_c + n_l) * TM

    n_cond = 1 + bs
    cond_t = jnp.zeros((d, 8), F32).at[:, 0].set(c_ctx).at[:, 1:n_cond].set(c.T)
    mod3 = _modulation(cond_t, ada_w, ada_b, n_cond).reshape(8, 1, 6 * d)

    row = lambda a: a.reshape(1, -1)
    u, qk, v, g, la = _inproj(xp, xs, mod3, row(norm1_g), w_in, gate_w, gate_b, lat_tiles)

    pair = lambda s: s.reshape(s.shape[0], 2, 2, LANES, LANES)
    of_c, ob_c, s_ctx = _gla(qk, v, la, jnp.zeros((bp, 2, 2, LANES, LANES), F32), bp, 1, 0)
    of_l, ob_l, _ = _gla(qk, v, la, pair(state), bs, lat_tiles, n_c)

    cw = jnp.zeros((32, C_CONV), F32).at[:CONV_K].set(conv_w)
    rw = jnp.zeros((d, LANES), F32).at[:, :N_EXPERTS].set(router_w)
    rb = jnp.full((1, LANES), NEG, F32).at[0, :N_EXPERTS].set(router_b)
    wts = (cw, row(conv_b), row(conv_ln_g), row(conv_ln_b), row(gla_norm_g), w_out.astype(BF16),
           row(norm2_g), rw, rb)
    seg_c = lambda i: 0
    seg_l = lambda i: 1 + i // lat_tiles
    x1_c, h2_c, route_c, gates_c, cnt_c = _mix(xp, u, g, of_c, ob_c, mod3, seg_c, 0, lp,
                                               jnp.zeros((8, LANES), F32), wts)
    x1_l, h2_l, route_l, gates_l, cnt = _mix(xs, u, g, of_l, ob_l, mod3, seg_l, n_c, GRID_W,
                                             cnt_c, wts)

    counts = cnt[0, :N_EXPERTS].astype(jnp.int32)
    cpad = (counts + TE - 1) // TE * TE
    ends = jnp.cumsum(cpad)
    offs = ends - cpad
    n_rows = TOP_K * t_all + N_EXPERTS * TE
    nt = n_rows // TE
    n_valid = (ends[-1] // TE).astype(jnp.int32)
    tile_ids = jnp.minimum(jnp.arange(nt, dtype=jnp.int32), n_valid - 1)
    tile_expert = jnp.sum((ends[None, :] <= (tile_ids * TE)[:, None]).astype(jnp.int32), axis=1)
    tile_expert = jnp.minimum(tile_expert, N_EXPERTS - 1)

    def positions(route):
        return jnp.take(offs, route[:, 0:TOP_K]) + route[:, TOP_K:2 * TOP_K]

    pos_c, pos_l = positions(route_c), positions(route_l)
    xsrt = jnp.zeros((n_rows, HALF), jnp.uint32)
    xsrt = _row_copy(pos_c.reshape(-1), h2_c, xsrt, scatter=True)
    xsrt = _row_copy(pos_l.reshape(-1), h2_l, xsrt, scatter=True)
    ysrt = _experts(tile_expert, n_valid.reshape(1), xsrt, moe_w_gu, moe_b_gu, moe_w_dn, moe_b_dn)
    pos_t = jnp.concatenate([pos_c, pos_l], axis=0).T.reshape(-1)
    yk = _row_copy(pos_t, ysrt).reshape(TOP_K, t_all, HALF)

    y_c = _final(x1_c, yk, gates_c, mod3, seg_c, 0, row(final_g))
    y_l = _final(x1_l, yk, gates_l, mod3, seg_l, n_c, row(final_g))
    new_state = s_ctx.reshape(bp, 1, 2, N_GLA_HEADS, DK_HEAD, DV_HEAD)
    return y_c.reshape(bp, lp, d), y_l.reshape(bs, ls, d), new_state


def kernel(x_prompt, x_sample, state_gla, c, c_ctx, ada_w, ada_b, norm1_g, norm2_g, w_in, conv_w,
           conv_b, conv_ln_g, conv_ln_b, gate_w, gate_b, gla_norm_g, w_out, router_w, router_b,
           moe_w_gu, moe_b_gu, moe_w_dn, moe_b_dn, final_g):
    assert ada_w.shape[0] == 1, "single-layer step"
    return _layer(x_prompt, x_sample, state_gla[:, 0], c, c_ctx, ada_w[0], ada_b[0], norm1_g[0],
                  norm2_g[0], w_in[0], conv_w[0], conv_b[0], conv_ln_g[0], conv_ln_b[0], gate_w[0],
                  gate_b[0], gla_norm_g[0], w_out[0], router_w[0], router_b[0], moe_w_gu[0],
                  moe_b_gu[0], moe_w_dn[0], moe_b_dn[0], final_g)
```

```python
import functools

import numpy as np
import jax
import jax.numpy as jnp
from jax import lax
from jax.experimental import pallas as pl
from jax.experimental.pallas import tpu as pltpu

D_MODEL = 1024
GRID_W = 64
C_CONV = D_MODEL // 2
CONV_K = 31
N_GLA_HEADS = 4
DV_HEAD = 128
DK_HEAD = 64
DK_TOT = DK_HEAD * N_GLA_HEADS
DV_TOT = DV_HEAD * N_GLA_HEADS
GATE_RANK = 16
GATE_TEMP = 16.0
CHUNK = 64
N_EXPERTS = 32
TOP_K = 4
D_FF = D_MODEL
SWIGLU_LIMIT = 7.0
SWIGLU_ALPHA = 1.702
EPS = 1e-6

LANES = 128
SEG = 8
TM = 256
TE = 512
NR = TOP_K * TM + TM
CONV_PAD = 16
VMEM_LIMIT = 56 * 1024 * 1024

F32 = jnp.float32
BF16 = jnp.bfloat16
HI = lax.Precision.HIGHEST
NEG = -1e30
TN = (((0,), (0,)), ((), ()))
NT = (((1,), (1,)), ((), ()))

assert N_EXPERTS * (SEG - 1) <= NR - TOP_K * TM


def _cparams(sem, vmem=None):
    return pltpu.CompilerParams(dimension_semantics=sem, vmem_limit_bytes=vmem)


def _mod_kernel(ct_ref, w_ref, b_ref, o_ref, *, n_cond):
    ct = ct_ref[...]
    s = ct * jax.nn.sigmoid(ct)
    w = w_ref[...]
    rows = [jnp.sum(s[:, r:r + 1] * w, axis=0, keepdims=True) + b_ref[...] for r in range(n_cond)]
    rows.append(jnp.zeros((8 - n_cond, w.shape[1]), F32))
    o_ref[...] = jnp.concatenate(rows, axis=0)


def _modulation(cond_t, ada_w, ada_b, n_cond):
    d, n = ada_w.shape
    nb = 768
    return pl.pallas_call(
        functools.partial(_mod_kernel, n_cond=n_cond),
        grid=(n // nb,),
        in_specs=[pl.BlockSpec((d, 8), lambda i: (0, 0)),
                  pl.BlockSpec((d, nb), lambda i: (0, i)),
                  pl.BlockSpec((1, nb), lambda i: (0, i))],
        out_specs=pl.BlockSpec((8, nb), lambda i: (0, i)),
        out_shape=jax.ShapeDtypeStruct((8, n), F32),
        compiler_params=_cparams(("arbitrary",)),
        name="mod",
    )(cond_t, ada_w, ada_b.reshape(1, n))


def _rms(x, g):
    return x * lax.rsqrt(jnp.mean(x * x, axis=-1, keepdims=True) + EPS) * g


def _inproj_kernel(xp_ref, xs_ref, mod_ref, g1_ref, wu_ref, wqk_ref, wv_ref, wg_ref, wlow_ref,
                   gw_ref, gb_ref, u_ref, qk_ref, v_ref, g_ref, la_ref, *, n_ctx_tiles):
    i = pl.program_id(0)
    x = jnp.where(i < n_ctx_tiles, xp_ref[...], xs_ref[...])
    shift = mod_ref[0, :, 0:D_MODEL]
    scale = mod_ref[0, :, D_MODEL:2 * D_MODEL]
    h = (_rms(x, g1_ref[...]) * (1.0 + scale) + shift).astype(BF16)
    dot = functools.partial(jnp.dot, preferred_element_type=F32)
    u_ref[...] = dot(h, wu_ref[...]).astype(BF16)
    qk_ref[...] = dot(h, wqk_ref[...]).astype(BF16)
    v_ref[...] = dot(h, wv_ref[...]).astype(BF16)
    g_ref[...] = dot(h, wg_ref[...]).astype(BF16)
    low = dot(h, wlow_ref[...])
    z = jnp.dot(low, gw_ref[...], precision=HI, preferred_element_type=F32) + gb_ref[...]
    la_ref[...] = (jnp.minimum(z, 0.0) - jnp.log1p(jnp.exp(-jnp.abs(z)))) * (1.0 / GATE_TEMP)


def _inproj(xp, xs, mod3, g1, w_in, gate_w, gate_b, lat_tiles_per_seq):
    n_c, n_l = xp.shape[0] // TM, xs.shape[0] // TM
    t_all = xp.shape[0] + xs.shape[0]
    sp = np.cumsum([0, C_CONV, C_CONV, DK_TOT, DK_TOT, DV_TOT, DV_TOT, 2 * GATE_RANK])
    wb = w_in.astype(BF16)
    wu, wqk, wv, wg, wlow = (wb[:, sp[0]:sp[2]], wb[:, sp[2]:sp[4]], wb[:, sp[4]:sp[5]],
                             wb[:, sp[5]:sp[6]], wb[:, sp[6]:sp[7]])
    gw = jnp.zeros((2 * GATE_RANK, 2 * DK_TOT), F32)
    gw = gw.at[:GATE_RANK, :DK_TOT].set(gate_w[0]).at[GATE_RANK:, DK_TOT:].set(gate_w[1])
    gb = gate_b.reshape(1, 2 * DK_TOT)
    const = lambda a: pl.BlockSpec(a.shape, lambda i: (0,) * a.ndim)
    row = lambda w: pl.BlockSpec((TM, w), lambda i: (i, 0))
    seg = lambda i: jnp.where(i < n_c, 0, 1 + jnp.maximum(i - n_c, 0) // lat_tiles_per_seq)
    return pl.pallas_call(
        functools.partial(_inproj_kernel, n_ctx_tiles=n_c),
        grid=(n_c + n_l,),
        in_specs=[pl.BlockSpec((TM, D_MODEL), lambda i: (jnp.minimum(i, n_c - 1), 0)),
                  pl.BlockSpec((TM, D_MODEL), lambda i: (jnp.maximum(i - n_c, 0), 0)),
                  pl.BlockSpec((1, 1, 6 * D_MODEL), lambda i: (seg(i), 0, 0)),
                  const(g1), const(wu), const(wqk), const(wv), const(wg), const(wlow),
                  const(gw), const(gb)],
        out_specs=[row(2 * C_CONV), row(2 * DK_TOT), row(DV_TOT), row(DV_TOT), row(2 * DK_TOT)],
        out_shape=[jax.ShapeDtypeStruct((t_all, 2 * C_CONV), BF16),
                   jax.ShapeDtypeStruct((t_all, 2 * DK_TOT), BF16),
                   jax.ShapeDtypeStruct((t_all, DV_TOT), BF16),
                   jax.ShapeDtypeStruct((t_all, DV_TOT), BF16),
                   jax.ShapeDtypeStruct((t_all, 2 * DK_TOT), F32)],
        compiler_params=_cparams(("arbitrary",), VMEM_LIMIT),
        name="inproj",
    )(xp, xs, mod3, g1, wu, wqk, wv, wg, wlow, gw, gb)


def _gla_direction(qk_ref, v_ref, la_ref, tri_ref, o_ref, s_scr, reverse):
    la = la_ref[...]
    bcum = jnp.dot(tri_ref[...], la, precision=HI, preferred_element_type=F32)
    q = qk_ref[:, 0:DK_TOT].astype(F32)
    k = qk_ref[:, DK_TOT:2 * DK_TOT].astype(F32)
    lane = lax.broadcasted_iota(jnp.int32, (CHUNK, LANES), 1)
    row2 = lax.broadcasted_iota(jnp.int32, (2 * CHUNK, CHUNK), 0) % CHUNK
    col2 = lax.broadcasted_iota(jnp.int32, (2 * CHUNK, CHUNK), 1)
    keep = (col2 >= row2) if reverse else (col2 <= row2)
    srow = lax.broadcasted_iota(jnp.int32, (LANES, LANES), 0)
    ones = jnp.ones((CHUNK, LANES), F32)
    n_chunks = TM // CHUNK
    order = range(n_chunks - 1, -1, -1) if reverse else range(n_chunks)
    for c in order:
        r0 = c * CHUNK
        bc = bcum[r0:r0 + CHUNK]
        la_c = la[r0:r0 + CHUNK]
        bl = bc[0:1] if reverse else bc[CHUNK - 1:CHUNK]
        qt = q[r0:r0 + CHUNK] * jnp.exp(bc) * (DK_HEAD ** -0.5)
        kt = k[r0:r0 + CHUNK] * jnp.exp(-bc)
        ke = k[r0:r0 + CHUNK] * jnp.exp(bl - bc)
        for p in range(2):
            cs = slice(p * LANES, (p + 1) * LANES)
            qt_p = qt[:, cs]
            qs = jnp.concatenate([jnp.where(lane < DK_HEAD, qt_p, 0.0),
                                  jnp.where(lane >= DK_HEAD, qt_p, 0.0)], axis=0).astype(BF16)
            att = lax.dot_general(qs, kt[:, cs].astype(BF16), NT, preferred_element_type=F32)
            att = jnp.where(keep, att, 0.0).astype(BF16)
            s_p = s_scr[p]
            o_inter = jnp.dot(qs, s_p.astype(BF16), preferred_element_type=F32)
            ke_p = ke[:, cs].astype(BF16)
            upd = []
            for hh in range(2):
                h = 2 * p + hh
                v_h = v_ref[r0:r0 + CHUNK, h * DV_HEAD:(h + 1) * DV_HEAD]
                o_h = jnp.dot(att[hh * CHUNK:(hh + 1) * CHUNK], v_h, preferred_element_type=F32)
                o_ref[r0:r0 + CHUNK, h * DV_HEAD:(h + 1) * DV_HEAD] = (
                    o_h + o_inter[hh * CHUNK:(hh + 1) * CHUNK])
                upd.append(lax.dot_general(ke_p, v_h, TN, preferred_element_type=F32))
            bl_col = lax.dot_general(la_c[:, cs], ones, TN, precision=HI, preferred_element_type=F32)
            s_scr[p] = jnp.exp(bl_col) * s_p + jnp.where(srow < DK_HEAD, upd[0], upd[1])


def _gla_kernel(qkf_ref, vf_ref, laf_ref, qkb_ref, vb_ref, lab_ref, tril_ref, triu_ref, s0_ref,
                of_ref, ob_ref, sout_ref, sf_scr, sb_scr, *, n_tiles):
    j = pl.program_id(1)

    @pl.when(j == 0)
    def _():
        sf_scr[...] = s0_ref[0, 0]
        sb_scr[...] = s0_ref[0, 1]

    _gla_direction(qkf_ref, vf_ref, laf_ref, tril_ref, of_ref, sf_scr, False)
    _gla_direction(qkb_ref, vb_ref, lab_ref, triu_ref, ob_ref, sb_scr, True)

    @pl.when(j == n_tiles - 1)
    def _():
        sout_ref[0, 0] = sf_scr[...]
        sout_ref[0, 1] = sb_scr[...]


def _gla(qk, v, la, s0, n_seq, n_tiles, tile_off):
    t_path = n_seq * n_tiles * TM
    blk = np.arange(TM) // CHUNK
    same = blk[:, None] == blk[None, :]
    r = np.arange(TM)
    tril = jnp.asarray((same & (r[None, :] <= r[:, None])).astype(np.float32))
    triu = jnp.asarray((same & (r[None, :] >= r[:, None])).astype(np.float32))
    fwd = lambda s, j: s * n_tiles + j + tile_off
    bwd = lambda s, j: s * n_tiles + (n_tiles - 1 - j) + tile_off
    ofwd = lambda s, j: s * n_tiles + j
    obwd = lambda s, j: s * n_tiles + (n_tiles - 1 - j)
    tok = lambda w, f, cb=0: pl.BlockSpec((TM, w), lambda s, j: (f(s, j), cb))
    const = lambda a: pl.BlockSpec(a.shape, lambda s, j: (0,) * a.ndim)
    st = pl.BlockSpec((1, 2, 2, LANES, LANES), lambda s, j: (s, 0, 0, 0, 0))
    return pl.pallas_call(
        functools.partial(_gla_kernel, n_tiles=n_tiles),
        grid=(n_seq, n_tiles),
        in_specs=[tok(2 * DK_TOT, fwd), tok(DV_TOT, fwd), tok(DK_TOT, fwd, 0),
                  tok(2 * DK_TOT, bwd), tok(DV_TOT, bwd), tok(DK_TOT, bwd, 1),
                  const(tril), const(triu), st],
        out_specs=[tok(DV_TOT, ofwd), tok(DV_TOT, obwd), st],
        out_shape=[jax.ShapeDtypeStruct((t_path, DV_TOT), F32),
                   jax.ShapeDtypeStruct((t_path, DV_TOT), F32),
                   jax.ShapeDtypeStruct(s0.shape, F32)],
        scratch_shapes=[pltpu.VMEM((2, LANES, LANES), F32), pltpu.VMEM((2, LANES, LANES), F32)],
        compiler_params=_cparams(("arbitrary", "arbitrary"), VMEM_LIMIT),
        name="gla",
    )(qk, v, la, qk, v, la, tril, triu, s0)


def _mix_kernel(x_ref, u_ref, g_ref, of_ref, ob_ref, mod_ref, cw_ref, cb_ref, lng_ref, lnb_ref,
                gng_ref, wout_ref, n2g_ref, rw_ref, rb_ref, lt_ref, ut_ref,
                x1_ref, h2_ref, route_ref, gates_ref, cnt_ref, pad_scr, *, seq_len):
    n_seq = TM // seq_len

    u = u_ref[...].astype(F32)
    hg = u[:, 0:C_CONV] * jax.nn.sigmoid(u[:, C_CONV:2 * C_CONV])
    zero = jnp.zeros((n_seq, CONV_PAD, C_CONV), F32)
    pad_scr[:, 0:CONV_PAD, :] = zero
    pad_scr[:, CONV_PAD + seq_len:2 * CONV_PAD + seq_len, :] = zero
    pad_scr[:, CONV_PAD:CONV_PAD + seq_len, :] = hg.reshape(n_seq, seq_len, C_CONV)
    acc = jnp.zeros((n_seq, seq_len, C_CONV), F32)
    for tap in range(CONV_K):
        start = CONV_PAD - CONV_K // 2 + tap
        acc = acc + cw_ref[tap:tap + 1, :] * pad_scr[:, start:start + seq_len, :]
    cv = acc.reshape(TM, C_CONV) + cb_ref[...]
    mu = jnp.mean(cv, axis=-1, keepdims=True)
    var = jnp.mean(jnp.square(cv - mu), axis=-1, keepdims=True)
    cv = (cv - mu) * lax.rsqrt(var + EPS) * lng_ref[...] + lnb_ref[...]
    conv_out = (cv * jax.nn.sigmoid(cv)).astype(BF16)

    o = of_ref[...] + ob_ref[...]
    g = g_ref[...].astype(F32)
    heads = []
    for h in range(N_GLA_HEADS):
        cs = slice(h * DV_HEAD, (h + 1) * DV_HEAD)
        oh = o[:, cs]
        oh = oh * lax.rsqrt(jnp.mean(oh * oh, axis=-1, keepdims=True) + EPS) * gng_ref[...]
        gh = g[:, cs]
        heads.append((oh * (gh * jax.nn.sigmoid(gh))).astype(BF16))
    y = jnp.dot(conv_out, wout_ref[0:C_CONV, :], preferred_element_type=F32)
    for h in range(N_GLA_HEADS):
        y = y + jnp.dot(heads[h], wout_ref[C_CONV + h * DV_HEAD:C_CONV + (h + 1) * DV_HEAD, :],
                        preferred_element_type=F32)

    gate1 = mod_ref[0, :, 2 * D_MODEL:3 * D_MODEL]
    shift2 = mod_ref[0, :, 3 * D_MODEL:4 * D_MODEL]
    scale2 = mod_ref[0, :, 4 * D_MODEL:5 * D_MODEL]
    x1 = x_ref[...] + gate1 * y
    x1_ref[...] = x1
    h2 = _rms(x1, n2g_ref[...]) * (1.0 + scale2) + shift2
    h2_ref[...] = h2.astype(BF16)

    logits = jnp.dot(h2, rw_ref[...], precision=HI, preferred_element_type=F32) + rb_ref[...]
    lane = lax.broadcasted_iota(jnp.int32, (TM, LANES), 1)
    lanef = lane.astype(F32)
    l = logits
    vals, idxs, hots = [], [], []
    for _ in range(TOP_K):
        m = jnp.max(l, axis=-1, keepdims=True)
        idx = jnp.min(jnp.where(l == m, lanef, float(LANES)), axis=-1, keepdims=True)
        hot = lanef == idx
        l = jnp.where(hot, -jnp.inf, l)
        vals.append(m)
        idxs.append(idx)
        hots.append(hot)
    es = [jnp.exp(vk - vals[0]) for vk in vals]
    inv = 1.0 / (es[0] + es[1] + es[2] + es[3])
    sel = jnp.zeros((TM, LANES), F32)
    for hot in hots:
        sel = jnp.where(hot, 1.0, sel)

    cnt = jnp.sum(sel, axis=0, keepdims=True)
    strip = jnp.ceil(cnt * (1.0 / SEG)) * float(SEG)
    starts = jnp.dot(jnp.broadcast_to(strip, (8, LANES)), ut_ref[...], precision=HI,
                     preferred_element_type=F32)[0:1]
    slot_of = jnp.dot(lt_ref[...], sel.astype(BF16), preferred_element_type=F32) + starts
    route = jnp.zeros((TM, LANES), F32)
    gates = jnp.zeros((TM, LANES), F32)
    for kk in range(TOP_K):
        slot = jnp.sum(jnp.where(hots[kk], slot_of, 0.0), axis=-1, keepdims=True)
        route = jnp.where(lane == kk, slot, route)
        route = jnp.where(lane == TOP_K + kk, idxs[kk], route)
        gates = jnp.where(lane == kk, es[kk] * inv, gates)
    route_ref[...] = route.astype(jnp.int32)
    gates_ref[...] = gates
    cnt_ref[0] = jnp.broadcast_to(cnt, (8, LANES))


def _mix(x, u, g, o_f, o_b, mod3, seg_fn, tile_off, seq_len, wts):
    (conv_w, conv_b, ln_g, ln_b, gng, wout, n2g, rw, rb) = wts
    t_path = x.shape[0]
    n = t_path // TM
    lt = jnp.asarray(np.tril(np.ones((TM, TM), np.float32), -1)).astype(BF16)
    ut = jnp.asarray(np.triu(np.ones((LANES, LANES), np.float32), 1))
    const = lambda a: pl.BlockSpec(a.shape, lambda i: (0,) * a.ndim)
    loc = lambda w: pl.BlockSpec((TM, w), lambda i: (i, 0))
    uni = lambda w: pl.BlockSpec((TM, w), lambda i: (i + tile_off, 0))
    n_seq = TM // seq_len
    return pl.pallas_call(
        functools.partial(_mix_kernel, seq_len=seq_len),
        grid=(n,),
        in_specs=[loc(D_MODEL), uni(2 * C_CONV), uni(DV_TOT), loc(DV_TOT), loc(DV_TOT),
                  pl.BlockSpec((1, 1, 6 * D_MODEL), lambda i: (seg_fn(i), 0, 0)),
                  const(conv_w), const(conv_b), const(ln_g), const(ln_b), const(gng), const(wout),
                  const(n2g), const(rw), const(rb), const(lt), const(ut)],
        out_specs=[loc(D_MODEL), loc(D_MODEL), loc(LANES), loc(LANES),
                   pl.BlockSpec((1, 8, LANES), lambda i: (i, 0, 0))],
        out_shape=[jax.ShapeDtypeStruct((t_path, D_MODEL), F32),
                   jax.ShapeDtypeStruct((t_path, D_MODEL), BF16),
                   jax.ShapeDtypeStruct((t_path, LANES), jnp.int32),
                   jax.ShapeDtypeStruct((t_path, LANES), F32),
                   jax.ShapeDtypeStruct((n, 8, LANES), F32)],
        scratch_shapes=[pltpu.VMEM((n_seq, seq_len + 2 * CONV_PAD, C_CONV), F32)],
        compiler_params=_cparams(("arbitrary",), VMEM_LIMIT),
        name="mix",
    )(x, u, g, o_f, o_b, mod3, conv_w, conv_b, ln_g, ln_b, gng, wout, n2g, rw, rb, lt, ut)


def _strip_copies(tile, ls_ref, gs_ref, sg_ref, local, rows_hbm, sem, to_hbm, wait):
    def body(e, carry):
        j = tile * N_EXPERTS + e
        n = pl.multiple_of(sg_ref[j], SEG)

        @pl.when(n > 0)
        def _():
            loc = local.at[pl.ds(pl.multiple_of(ls_ref[j], SEG), n)]
            glob = rows_hbm.at[pl.ds(pl.multiple_of(gs_ref[j], SEG), n)]
            cp = pltpu.make_async_copy(loc, glob, sem) if to_hbm else pltpu.make_async_copy(glob, loc, sem)
            if wait:
                cp.wait()
            else:
                cp.start()
        return carry

    lax.fori_loop(0, N_EXPERTS, body, 0)


def _dispatch_kernel(ls_ref, gs_ref, sg_ref, ts_ref, tl_ref, nv_ref, h2c_ref, h2l_ref, rc_ref, rl_ref,
                     xs_ref, cbuf, zbuf, sem, zsem, *, n_ctx_tiles, n_tiles, n_row_tiles):
    b = pl.program_id(0)
    slot = b % 2

    def zero_fill(wait):
        def tail(e, carry):
            n = pl.multiple_of(tl_ref[e], SEG)

            @pl.when(n > 0)
            def _():
                cp = pltpu.make_async_copy(zbuf.at[pl.ds(0, n)],
                                           xs_ref.at[pl.ds(pl.multiple_of(ts_ref[e], SEG), n)], zsem)
                cp.wait() if wait else cp.start()
            return carry

        def unused(t, carry):
            cp = pltpu.make_async_copy(zbuf, xs_ref.at[pl.ds(pl.multiple_of(t * TE, TE), TE)], zsem)
            cp.wait() if wait else cp.start()
            return carry

        lax.fori_loop(0, N_EXPERTS, tail, 0)
        lax.fori_loop(nv_ref[0], n_row_tiles, unused, 0)

    @pl.when(b == 0)
    def _():
        zbuf[...] = jnp.zeros(zbuf.shape, F32)
        zero_fill(False)

    @pl.when(b >= 2)
    def _():
        _strip_copies(b - 2, ls_ref, gs_ref, sg_ref, cbuf.at[slot], xs_ref, sem.at[slot], True, True)

    first = b < n_ctx_tiles
    h2 = jnp.where(first, h2c_ref[...], h2l_ref[...])
    route = jnp.where(first, rc_ref[...], rl_ref[...])
    r = lax.broadcasted_iota(jnp.int32, (TM, NR), 1)
    perm = jnp.zeros((TM, NR), F32)
    for kk in range(TOP_K):
        perm = jnp.where(route[:, kk:kk + 1] == r, 1.0, perm)
    cbuf[slot] = lax.dot_general(perm.astype(BF16), h2, TN, preferred_element_type=F32)
    _strip_copies(b, ls_ref, gs_ref, sg_ref, cbuf.at[slot], xs_ref, sem.at[slot], True, False)

    @pl.when(b == n_tiles - 1)
    def _():
        if n_tiles > 1:
            _strip_copies(b - 1, ls_ref, gs_ref, sg_ref, cbuf.at[1 - slot], xs_ref, sem.at[1 - slot],
                          True, True)
        _strip_copies(b, ls_ref, gs_ref, sg_ref, cbuf.at[slot], xs_ref, sem.at[slot], True, True)
        zero_fill(True)


def _dispatch(plan, h2_c, h2_l, route_c, route_l, n_rows):
    n_c, n_l = h2_c.shape[0] // TM, h2_l.shape[0] // TM
    n_tiles = n_c + n_l
    lo = lambda w: pl.BlockSpec((TM, w), lambda i, *_: (jnp.minimum(i, n_c - 1), 0))
    hi = lambda w: pl.BlockSpec((TM, w), lambda i, *_: (jnp.maximum(i - n_c, 0), 0))
    return pl.pallas_call(
        functools.partial(_dispatch_kernel, n_ctx_tiles=n_c, n_tiles=n_tiles, n_row_tiles=n_rows // TE),
        grid_spec=pltpu.PrefetchScalarGridSpec(
            num_scalar_prefetch=6,
            grid=(n_tiles,),
            in_specs=[lo(D_MODEL), hi(D_MODEL), lo(LANES), hi(LANES)],
            out_specs=pl.BlockSpec(memory_space=pl.ANY),
            scratch_shapes=[pltpu.VMEM((2, NR, D_MODEL), F32), pltpu.VMEM((TE, D_MODEL), F32),
                            pltpu.SemaphoreType.DMA((2,)), pltpu.SemaphoreType.DMA]),
        out_shape=jax.ShapeDtypeStruct((n_rows, D_MODEL), F32),
        compiler_params=_cparams(("arbitrary",), VMEM_LIMIT),
        name="dispatch",
    )(plan["lstart"], plan["gstart"], plan["strip"], plan["tail_start"], plan["tail_len"],
      plan["n_valid"], h2_c, h2_l, route_c, route_l)


def _expert_kernel(te_ref, nv_ref, x_ref, wgu_ref, bgu_ref, wdn_ref, bdn_ref, y_ref, wgu_s, wdn_s):
    i = pl.program_id(0)

    @pl.when(i < nv_ref[0])
    def _():
        e = te_ref[i]
        prev = te_ref[jnp.maximum(i - 1, 0)]

        @pl.when(jnp.logical_or(i == 0, prev != e))
        def _():
            rows = 64

            def cast(r, carry):
                sl = pl.ds(pl.multiple_of(r * rows, rows), rows)
                wgu_s[sl, :] = wgu_ref[0, sl, :].astype(BF16)
                wdn_s[sl, :] = wdn_ref[0, sl, :].astype(BF16)
                return carry

            lax.fori_loop(0, D_MODEL // rows, cast, 0)

        gu = jnp.dot(x_ref[...].astype(BF16), wgu_s[...], preferred_element_type=F32) + bgu_ref[0]
        gate = jnp.minimum(gu[:, 0:D_FF], SWIGLU_LIMIT)
        up = jnp.clip(gu[:, D_FF:2 * D_FF], -SWIGLU_LIMIT, SWIGLU_LIMIT)
        act = (gate * jax.nn.sigmoid(SWIGLU_ALPHA * gate) * (up + 1.0)).astype(BF16)
        y_ref[...] = jnp.dot(act, wdn_s[...], preferred_element_type=F32) + bdn_ref[0]

    @pl.when(i >= nv_ref[0])
    def _():
        y_ref[...] = jnp.zeros(y_ref.shape, y_ref.dtype)


def _experts(tile_expert, n_valid, xs, w_gu, b_gu, w_dn, b_dn):
    n_rows = xs.shape[0]
    nt = n_rows // TE
    row = lambda i, te, nv: (jnp.minimum(i, nv[0] - 1), 0)
    exp3 = lambda i, te, nv: (te[i], 0, 0)
    return pl.pallas_call(
        _expert_kernel,
        grid_spec=pltpu.PrefetchScalarGridSpec(
            num_scalar_prefetch=2,
            grid=(nt,),
            in_specs=[pl.BlockSpec((TE, D_MODEL), row),
                      pl.BlockSpec((1, D_MODEL, 2 * D_FF), exp3),
                      pl.BlockSpec((1, 1, 2 * D_FF), exp3),
                      pl.BlockSpec((1, D_FF, D_MODEL), exp3),
                      pl.BlockSpec((1, 1, D_MODEL), exp3)],
            out_specs=pl.BlockSpec((TE, D_MODEL), lambda i, te, nv: (i, 0)),
            scratch_shapes=[pltpu.VMEM((D_MODEL, 2 * D_FF), BF16), pltpu.VMEM((D_FF, D_MODEL), BF16)]),
        out_shape=jax.ShapeDtypeStruct((n_rows, D_MODEL), F32),
        compiler_params=_cparams(("arbitrary",), VMEM_LIMIT),
        name="experts",
    )(tile_expert, n_valid, xs, w_gu, b_gu.reshape(N_EXPERTS, 1, 2 * D_FF), w_dn,
      b_dn.reshape(N_EXPERTS, 1, D_MODEL))


def _final_kernel(ls_ref, gs_ref, sg_ref, x1_ref, route_ref, gates_ref, mod_ref, fg_ref, ys_ref, y_ref,
                  ybuf, sem, *, tile_off, n_tiles):
    i = pl.program_id(0)
    slot = i % 2

    def fetch(step, sl, wait):
        _strip_copies(step + tile_off, ls_ref, gs_ref, sg_ref, ybuf.at[sl], ys_ref, sem.at[sl], False, wait)

    @pl.when(i == 0)
    def _():
        ybuf[...] = jnp.zeros(ybuf.shape, F32)
        fetch(i, slot, False)

    @pl.when(i + 1 < n_tiles)
    def _():
        fetch(i + 1, 1 - slot, False)

    fetch(i, slot, True)

    route = route_ref[...]
    gates = gates_ref[...]
    r = lax.broadcasted_iota(jnp.int32, (TM, NR), 1)
    comb = jnp.zeros((TM, NR), F32)
    for kk in range(TOP_K):
        comb = jnp.where(route[:, kk:kk + 1] == r, gates[:, kk:kk + 1], comb)
    moe = jnp.dot(comb.astype(BF16), ybuf[slot].astype(BF16), preferred_element_type=F32)
    gate2 = mod_ref[0, :, 5 * D_MODEL:6 * D_MODEL]
    y_ref[...] = _rms(x1_ref[...] + gate2 * moe, fg_ref[...])


def _final(plan, x1, route, gates, mod3, seg_fn, tile_off, fg, ys):
    t_path = x1.shape[0]
    n = t_path // TM
    loc = lambda w: pl.BlockSpec((TM, w), lambda i, *_: (i, 0))
    return pl.pallas_call(
        functools.partial(_final_kernel, tile_off=tile_off, n_tiles=n),
        grid_spec=pltpu.PrefetchScalarGridSpec(
            num_scalar_prefetch=3,
            grid=(n,),
            in_specs=[loc(D_MODEL), loc(LANES), loc(LANES),
                      pl.BlockSpec((1, 1, 6 * D_MODEL), lambda i, *_: (seg_fn(i), 0, 0)),
                      pl.BlockSpec((1, D_MODEL), lambda i, *_: (0, 0)),
                      pl.BlockSpec(memory_space=pl.ANY)],
            out_specs=loc(D_MODEL),
            scratch_shapes=[pltpu.VMEM((2, NR, D_MODEL), F32), pltpu.SemaphoreType.DMA((2,))]),
        out_shape=jax.ShapeDtypeStruct((t_path, D_MODEL), F32),
        compiler_params=_cparams(("arbitrary",), VMEM_LIMIT),
        name="final",
    )(plan["lstart"], plan["gstart"], plan["strip"], x1, route, gates, mod3, fg, ys)


def _plan(tile_counts, n_rows):
    nt = n_rows // TE
    strip = (tile_counts + SEG - 1) // SEG * SEG
    lstart = jnp.cumsum(strip, axis=1) - strip
    rows_e = jnp.sum(strip, axis=0)
    rpad = (rows_e + TE - 1) // TE * TE
    ends = jnp.cumsum(rpad)
    offs = ends - rpad
    gstart = offs[None, :] + jnp.cumsum(strip, axis=0) - strip
    n_valid = (ends[-1] // TE).astype(jnp.int32)
    tile_ids = jnp.minimum(jnp.arange(nt, dtype=jnp.int32), n_valid - 1)
    tile_expert = jnp.sum((ends[None, :] <= (tile_ids * TE)[:, None]).astype(jnp.int32), axis=1)
    i32 = lambda a: a.reshape(-1).astype(jnp.int32)
    return dict(strip=i32(strip), lstart=i32(lstart), gstart=i32(gstart), tail_start=i32(offs + rows_e),
                tail_len=i32(rpad - rows_e), n_valid=n_valid.reshape(1),
                tile_expert=jnp.minimum(tile_expert, N_EXPERTS - 1).astype(jnp.int32))


def _layer(x_prompt, x_sample, state, c, c_ctx, ada_w, ada_b, norm1_g, norm2_g, w_in, conv_w, conv_b,
           conv_ln_g, conv_ln_b, gate_w, gate_b, gla_norm_g, w_out, router_w, router_b,
           moe_w_gu, moe_b_gu, moe_w_dn, moe_b_dn, final_g):
    bp, lp, d = x_prompt.shape
    bs, ls, _ = x_sample.shape
    assert lp == TM and ls % TM == 0 and d == D_MODEL
    xp = x_prompt.reshape(bp * lp, d)
    xs = x_sample.reshape(bs * ls, d)
    n_c, n_l = bp, bs * ls // TM
    lat_tiles = ls // TM
    t_all = (n_c + n_l) * TM

    n_cond = 1 + bs
    cond_t = jnp.zeros((d, 8), F32).at[:, 0].set(c_ctx).at[:, 1:n_cond].set(c.T)
    mod3 = _modulation(cond_t, ada_w, ada_b, n_cond).reshape(8, 1, 6 * d)

    row = lambda a: a.reshape(1, -1)
    u, qk, v, g, la = _inproj(xp, xs, mod3, row(norm1_g), w_in, gate_w, gate_b, lat_tiles)

    pair = lambda s: s.reshape(s.shape[0], 2, 2, LANES, LANES)
    of_c, ob_c, s_ctx = _gla(qk, v, la, jnp.zeros((bp, 2, 2, LANES, LANES), F32), bp, 1, 0)
    of_l, ob_l, _ = _gla(qk, v, la, pair(state), bs, lat_tiles, n_c)

    cw = jnp.zeros((32, C_CONV), F32).at[:CONV_K].set(conv_w)
    rw = jnp.zeros((d, LANES), F32).at[:, :N_EXPERTS].set(router_w)
    rb = jnp.full((1, LANES), NEG, F32).at[0, :N_EXPERTS].set(router_b)
    wts = (cw, row(conv_b), row(conv_ln_g), row(conv_ln_b), row(gla_norm_g), w_out.astype(BF16),
           row(norm2_g), rw, rb)
    seg_c = lambda i: 0
    seg_l = lambda i: 1 + i // lat_tiles
    x1_c, h2_c, route_c, gates_c, cnt_c = _mix(xp, u, g, of_c, ob_c, mod3, seg_c, 0, lp, wts)
    x1_l, h2_l, route_l, gates_l, cnt_l = _mix(xs, u, g, of_l, ob_l, mod3, seg_l, n_c, GRID_W, wts)

    tile_counts = jnp.concatenate([cnt_c[:, 0, :N_EXPERTS], cnt_l[:, 0, :N_EXPERTS]]).astype(jnp.int32)
    n_rows = TOP_K * t_all + (SEG - 1) * N_EXPERTS * (n_c + n_l) + N_EXPERTS * TE
    n_rows = (n_rows + TE - 1) // TE * TE
    plan = _plan(tile_counts, n_rows)
    xsrt = _dispatch(plan, h2_c, h2_l, route_c, route_l, n_rows)
    ysrt = _experts(plan["tile_expert"], plan["n_valid"], xsrt, moe_w_gu, moe_b_gu, moe_w_dn, moe_b_dn)
    y_c = _final(plan, x1_c, route_c, gates_c, mod3, seg_c, 0, row(final_g), ysrt)
    y_l = _final(plan, x1_l, route_l, gates_l, mod3, seg_l, n_c, row(final_g), ysrt)
    new_state = s_ctx.reshape(bp, 1, 2, N_GLA_HEADS, DK_HEAD, DV_HEAD)
    return y_c.reshape(bp, lp, d), y_l.reshape(bs, ls, d), new_state


def kernel(x_prompt, x_sample, state_gla, c, c_ctx, ada_w, ada_b, norm1_g, norm2_g, w_in, conv_w,
           conv_b, conv_ln_g, conv_ln_b, gate_w, gate_b, gla_norm_g, w_out, router_w, router_b,
           moe_w_gu, moe_b_gu, moe_w_dn, moe_b_dn, final_g):
    assert ada_w.shape[0] == 1, "single-layer step"
    return _layer(x_prompt, x_sample, state_gla[:, 0], c, c_ctx, ada_w[0], ada_b[0], norm1_g[0],
                  norm2_g[0], w_in[0], conv_w[0], conv_b[0], conv_ln_g[0], conv_ln_b[0], gate_w[0],
                  gate_b[0], gla_norm_g[0], w_out[0], router_w[0], router_b[0], moe_w_gu[0],
                  moe_b_gu[0], moe_w_dn[0], moe_b_dn[0], final_g)
```

```python
import functools

import numpy as np
import jax
import jax.numpy as jnp
from jax import lax
from jax.experimental import pallas as pl
from jax.experimental.pallas import tpu as pltpu

D_MODEL = 1024
GRID_W = 64
C_CONV = D_MODEL // 2
CONV_K = 31
N_GLA_HEADS = 4
DV_HEAD = 128
DK_HEAD = 64
DK_TOT = DK_HEAD * N_GLA_HEADS
DV_TOT = DV_HEAD * N_GLA_HEADS
GATE_RANK = 16
GATE_TEMP = 16.0
CHUNK = 64
N_EXPERTS = 32
TOP_K = 4
D_FF = D_MODEL
SWIGLU_LIMIT = 7.0
SWIGLU_ALPHA = 1.702
EPS = 1e-6

LANES = 128
SEG = 8
TM = 256
TE = 512
NR = TOP_K * TM + TM
CONV_PAD = 16
VMEM_LIMIT = 56 * 1024 * 1024

F32 = jnp.float32
BF16 = jnp.bfloat16
HI = lax.Precision.HIGHEST
NEG = -1e30
TN = (((0,), (0,)), ((), ()))
NT = (((1,), (1,)), ((), ()))

assert N_EXPERTS * (SEG - 1) <= NR - TOP_K * TM


def _split(x):
    hi = x.astype(BF16)
    return hi, (x - hi.astype(F32)).astype(BF16)


def _cparams(sem, vmem=None):
    return pltpu.CompilerParams(dimension_semantics=sem, vmem_limit_bytes=vmem)


def _mod_kernel(ct_ref, w_ref, b_ref, o_ref, *, n_cond):
    ct = ct_ref[...]
    s = ct * jax.nn.sigmoid(ct)
    w = w_ref[...]
    rows = [jnp.sum(s[:, r:r + 1] * w, axis=0, keepdims=True) + b_ref[...] for r in range(n_cond)]
    rows.append(jnp.zeros((8 - n_cond, w.shape[1]), F32))
    o_ref[...] = jnp.concatenate(rows, axis=0)


def _modulation(cond_t, ada_w, ada_b, n_cond):
    d, n = ada_w.shape
    nb = 768
    return pl.pallas_call(
        functools.partial(_mod_kernel, n_cond=n_cond),
        grid=(n // nb,),
        in_specs=[pl.BlockSpec((d, 8), lambda i: (0, 0)),
                  pl.BlockSpec((d, nb), lambda i: (0, i)),
                  pl.BlockSpec((1, nb), lambda i: (0, i))],
        out_specs=pl.BlockSpec((8, nb), lambda i: (0, i)),
        out_shape=jax.ShapeDtypeStruct((8, n), F32),
        compiler_params=_cparams(("arbitrary",)),
        name="mod",
    )(cond_t, ada_w, ada_b.reshape(1, n))


def _rms(x, g):
    return x * lax.rsqrt(jnp.mean(x * x, axis=-1, keepdims=True) + EPS) * g


def _inproj_kernel(xp_ref, xs_ref, mod_ref, g1_ref, wu_ref, wqk_ref, wv_ref, wg_ref, wlow_ref,
                   gwh_ref, gwl_ref, gb_ref, u_ref, qk_ref, v_ref, g_ref, la_ref, *, n_ctx_tiles):
    i = pl.program_id(0)
    x = jnp.where(i < n_ctx_tiles, xp_ref[...], xs_ref[...])
    shift = mod_ref[0, :, 0:D_MODEL]
    scale = mod_ref[0, :, D_MODEL:2 * D_MODEL]
    h = (_rms(x, g1_ref[...]) * (1.0 + scale) + shift).astype(BF16)
    dot = functools.partial(jnp.dot, preferred_element_type=F32)
    u_ref[...] = dot(h, wu_ref[...]).astype(BF16)
    qk_ref[...] = dot(h, wqk_ref[...]).astype(BF16)
    v_ref[...] = dot(h, wv_ref[...]).astype(BF16)
    g_ref[...] = dot(h, wg_ref[...]).astype(BF16)
    low = dot(h, wlow_ref[...])
    low_hi, low_lo = _split(low)
    z = (dot(low_hi, gwh_ref[...]) + dot(low_lo, gwh_ref[...]) + dot(low_hi, gwl_ref[...])
         + gb_ref[...])
    la_ref[...] = (jnp.minimum(z, 0.0) - jnp.log1p(jnp.exp(-jnp.abs(z)))) * (1.0 / GATE_TEMP)


def _inproj(xp, xs, mod3, g1, w_in, gate_w, gate_b, lat_len):
    ti = next(t for t in (1024, 512, TM) if xp.shape[0] % t == 0 and lat_len % t == 0)
    lat_tiles_per_seq = lat_len // ti
    n_c, n_l = xp.shape[0] // ti, xs.shape[0] // ti
    t_all = xp.shape[0] + xs.shape[0]
    sp = np.cumsum([0, C_CONV, C_CONV, DK_TOT, DK_TOT, DV_TOT, DV_TOT, 2 * GATE_RANK])
    wb = w_in.astype(BF16)
    wu, wqk, wv, wg, wlow = (wb[:, sp[0]:sp[2]], wb[:, sp[2]:sp[4]], wb[:, sp[4]:sp[5]],
                             wb[:, sp[5]:sp[6]], wb[:, sp[6]:sp[7]])
    gw = jnp.zeros((2 * GATE_RANK, 2 * DK_TOT), F32)
    gw = gw.at[:GATE_RANK, :DK_TOT].set(gate_w[0]).at[GATE_RANK:, DK_TOT:].set(gate_w[1])
    gwh, gwl = _split(gw)
    gb = gate_b.reshape(1, 2 * DK_TOT)
    const = lambda a: pl.BlockSpec(a.shape, lambda i: (0,) * a.ndim)
    row = lambda w: pl.BlockSpec((ti, w), lambda i: (i, 0))
    seg = lambda i: jnp.where(i < n_c, 0, 1 + jnp.maximum(i - n_c, 0) // lat_tiles_per_seq)
    return pl.pallas_call(
        functools.partial(_inproj_kernel, n_ctx_tiles=n_c),
        grid=(n_c + n_l,),
        in_specs=[pl.BlockSpec((ti, D_MODEL), lambda i: (jnp.minimum(i, n_c - 1), 0)),
                  pl.BlockSpec((ti, D_MODEL), lambda i: (jnp.maximum(i - n_c, 0), 0)),
                  pl.BlockSpec((1, 1, 6 * D_MODEL), lambda i: (seg(i), 0, 0)),
                  const(g1), const(wu), const(wqk), const(wv), const(wg), const(wlow),
                  const(gwh), const(gwl), const(gb)],
        out_specs=[row(2 * C_CONV), row(2 * DK_TOT), row(DV_TOT), row(DV_TOT), row(2 * DK_TOT)],
        out_shape=[jax.ShapeDtypeStruct((t_all, 2 * C_CONV), BF16),
                   jax.ShapeDtypeStruct((t_all, 2 * DK_TOT), BF16),
                   jax.ShapeDtypeStruct((t_all, DV_TOT), BF16),
                   jax.ShapeDtypeStruct((t_all, DV_TOT), BF16),
                   jax.ShapeDtypeStruct((t_all, 2 * DK_TOT), F32)],
        compiler_params=_cparams(("arbitrary",), VMEM_LIMIT),
        name="inproj",
    )(xp, xs, mod3, g1, wu, wqk, wv, wg, wlow, gwh, gwl, gb)


def _gla_direction(qk_ref, v_ref, la_ref, tri_ref, blk_ref, o_ref, s_scr, reverse):
    la_hi, la_lo = _split(la_ref[...])
    tri = tri_ref[...]
    bcum = (jnp.dot(tri, la_hi, preferred_element_type=F32)
            + jnp.dot(tri, la_lo, preferred_element_type=F32))
    blk = blk_ref[...]
    bl_cols = (lax.dot_general(la_hi, blk, TN, preferred_element_type=F32)
               + lax.dot_general(la_lo, blk, TN, preferred_element_type=F32))
    q = qk_ref[:, 0:DK_TOT].astype(F32)
    k = qk_ref[:, DK_TOT:2 * DK_TOT].astype(F32)
    lane = lax.broadcasted_iota(jnp.int32, (CHUNK, LANES), 1)
    row2 = lax.broadcasted_iota(jnp.int32, (2 * CHUNK, CHUNK), 0) % CHUNK
    col2 = lax.broadcasted_iota(jnp.int32, (2 * CHUNK, CHUNK), 1)
    keep = (col2 >= row2) if reverse else (col2 <= row2)
    srow = lax.broadcasted_iota(jnp.int32, (LANES, LANES), 0)
    n_chunks = TM // CHUNK
    order = range(n_chunks - 1, -1, -1) if reverse else range(n_chunks)
    for c in order:
        r0 = c * CHUNK
        bc = bcum[r0:r0 + CHUNK]
        bl = bc[0:1] if reverse else bc[CHUNK - 1:CHUNK]
        qt = q[r0:r0 + CHUNK] * jnp.exp(bc) * (DK_HEAD ** -0.5)
        kt = k[r0:r0 + CHUNK] * jnp.exp(-bc)
        ke = k[r0:r0 + CHUNK] * jnp.exp(bl - bc)
        for p in range(2):
            cs = slice(p * LANES, (p + 1) * LANES)
            qt_p = qt[:, cs]
            qs = jnp.concatenate([jnp.where(lane < DK_HEAD, qt_p, 0.0),
                                  jnp.where(lane >= DK_HEAD, qt_p, 0.0)], axis=0).astype(BF16)
            att = lax.dot_general(qs, kt[:, cs].astype(BF16), NT, preferred_element_type=F32)
            att = jnp.where(keep, att, 0.0).astype(BF16)
            s_p = s_scr[p]
            o_inter = jnp.dot(qs, s_p.astype(BF16), preferred_element_type=F32)
            ke_p = ke[:, cs].astype(BF16)
            upd = []
            for hh in range(2):
                h = 2 * p + hh
                v_h = v_ref[r0:r0 + CHUNK, h * DV_HEAD:(h + 1) * DV_HEAD]
                o_h = jnp.dot(att[hh * CHUNK:(hh + 1) * CHUNK], v_h, preferred_element_type=F32)
                o_ref[r0:r0 + CHUNK, h * DV_HEAD:(h + 1) * DV_HEAD] = (
                    o_h + o_inter[hh * CHUNK:(hh + 1) * CHUNK])
                upd.append(lax.dot_general(ke_p, v_h, TN, preferred_element_type=F32))
            bl_col = bl_cols[p * LANES:(p + 1) * LANES, c * LANES:(c + 1) * LANES]
            s_scr[p] = jnp.exp(bl_col) * s_p + jnp.where(srow < DK_HEAD, upd[0], upd[1])


def _gla_kernel(qkf_ref, vf_ref, laf_ref, qkb_ref, vb_ref, lab_ref, tril_ref, triu_ref, blk_ref, s0_ref,
                of_ref, ob_ref, sout_ref, sf_scr, sb_scr, *, n_tiles):
    j = pl.program_id(1)

    @pl.when(j == 0)
    def _():
        sf_scr[...] = s0_ref[0, 0]
        sb_scr[...] = s0_ref[0, 1]

    _gla_direction(qkf_ref, vf_ref, laf_ref, tril_ref, blk_ref, of_ref, sf_scr, False)
    _gla_direction(qkb_ref, vb_ref, lab_ref, triu_ref, blk_ref, ob_ref, sb_scr, True)

    @pl.when(j == n_tiles - 1)
    def _():
        sout_ref[0, 0] = sf_scr[...]
        sout_ref[0, 1] = sb_scr[...]


def _gla(qk, v, la, s0, n_seq, n_tiles, tile_off):
    t_path = n_seq * n_tiles * TM
    blk = np.arange(TM) // CHUNK
    same = blk[:, None] == blk[None, :]
    r = np.arange(TM)
    tril = jnp.asarray((same & (r[None, :] <= r[:, None])).astype(np.float32)).astype(BF16)
    triu = jnp.asarray((same & (r[None, :] >= r[:, None])).astype(np.float32)).astype(BF16)
    chunk_of_lane = np.arange(TM // CHUNK * LANES) // LANES
    blkm = jnp.asarray((blk[:, None] == chunk_of_lane[None, :]).astype(np.float32)).astype(BF16)
    fwd = lambda s, j: s * n_tiles + j + tile_off
    bwd = lambda s, j: s * n_tiles + (n_tiles - 1 - j) + tile_off
    ofwd = lambda s, j: s * n_tiles + j
    obwd = lambda s, j: s * n_tiles + (n_tiles - 1 - j)
    tok = lambda w, f, cb=0: pl.BlockSpec((TM, w), lambda s, j: (f(s, j), cb))
    const = lambda a: pl.BlockSpec(a.shape, lambda s, j: (0,) * a.ndim)
    st = pl.BlockSpec((1, 2, 2, LANES, LANES), lambda s, j: (s, 0, 0, 0, 0))
    return pl.pallas_call(
        functools.partial(_gla_kernel, n_tiles=n_tiles),
        grid=(n_seq, n_tiles),
        in_specs=[tok(2 * DK_TOT, fwd), tok(DV_TOT, fwd), tok(DK_TOT, fwd, 0),
                  tok(2 * DK_TOT, bwd), tok(DV_TOT, bwd), tok(DK_TOT, bwd, 1),
                  const(tril), const(triu), const(blkm), st],
        out_specs=[tok(DV_TOT, ofwd), tok(DV_TOT, obwd), st],
        out_shape=[jax.ShapeDtypeStruct((t_path, DV_TOT), F32),
                   jax.ShapeDtypeStruct((t_path, DV_TOT), F32),
                   jax.ShapeDtypeStruct(s0.shape, F32)],
        scratch_shapes=[pltpu.VMEM((2, LANES, LANES), F32), pltpu.VMEM((2, LANES, LANES), F32)],
        compiler_params=_cparams(("arbitrary", "arbitrary"), VMEM_LIMIT),
        name="gla",
    )(qk, v, la, qk, v, la, tril, triu, blkm, s0)


def _mix_kernel(x_ref, u_ref, g_ref, of_ref, ob_ref, mod_ref, cw_ref, cb_ref, lng_ref, lnb_ref,
                gng_ref, wout_ref, n2g_ref, rwh_ref, rwl_ref, rb_ref, lt_ref, ut_ref,
                x1_ref, h2_ref, route_ref, gates_ref, cnt_ref, pad_scr, shf_scr, cv_scr, *, seq_len):
    n_seq = TM // seq_len
    pr = seq_len + 2 * CONV_PAD
    qr = seq_len + 3 * SEG

    u = u_ref[...].astype(F32)
    hg = u[:, 0:C_CONV] * jax.nn.sigmoid(u[:, C_CONV:2 * C_CONV])
    zero = jnp.zeros((CONV_PAD, C_CONV), F32)
    for s in range(n_seq):
        pad_scr[s * pr:s * pr + CONV_PAD, :] = zero
        pad_scr[s * pr + CONV_PAD:s * pr + CONV_PAD + seq_len, :] = hg[s * seq_len:(s + 1) * seq_len]
        pad_scr[s * pr + CONV_PAD + seq_len:(s + 1) * pr, :] = zero
        for b in range(1, SEG):
            shf_scr[b - 1, s * qr:(s + 1) * qr, :] = pad_scr[s * pr + b:s * pr + b + qr, :]
    rb = 4 * SEG
    per_seq = seq_len // rb
    for gi in range(C_CONV // LANES):
        gs = slice(gi * LANES, (gi + 1) * LANES)
        wv = [jnp.broadcast_to(cw_ref[tap:tap + 1, gs], (SEG, LANES)) for tap in range(CONV_K)]

        def conv_rows(i, carry, gs=gs, wv=wv):
            s = i // per_seq
            r0 = (i % per_seq) * rb
            for j in range(rb // SEG):
                acc = jnp.zeros((SEG, LANES), F32)
                for tap in range(CONV_K):
                    a, b = divmod(CONV_PAD - CONV_K // 2 + tap, SEG)
                    if b == 0:
                        row = pl.multiple_of(s * pr + r0 + (j + a) * SEG, SEG)
                        acc = acc + wv[tap] * pad_scr[pl.ds(row, SEG), gs]
                    else:
                        row = pl.multiple_of(s * qr + r0 + (j + a) * SEG, SEG)
                        acc = acc + wv[tap] * shf_scr[b - 1, pl.ds(row, SEG), gs]
                cv_scr[pl.ds(pl.multiple_of(s * seq_len + r0 + j * SEG, SEG), SEG), gs] = acc
            return carry

        lax.fori_loop(0, TM // rb, conv_rows, 0)
    cv = cv_scr[...] + cb_ref[...]
    mu = jnp.mean(cv, axis=-1, keepdims=True)
    var = jnp.mean(jnp.square(cv - mu), axis=-1, keepdims=True)
    cv = (cv - mu) * lax.rsqrt(var + EPS) * lng_ref[...] + lnb_ref[...]
    conv_out = (cv * jax.nn.sigmoid(cv)).astype(BF16)

    o = of_ref[...] + ob_ref[...]
    g = g_ref[...].astype(F32)
    heads = []
    for h in range(N_GLA_HEADS):
        cs = slice(h * DV_HEAD, (h + 1) * DV_HEAD)
        oh = o[:, cs]
        oh = oh * lax.rsqrt(jnp.mean(oh * oh, axis=-1, keepdims=True) + EPS) * gng_ref[...]
        gh = g[:, cs]
        heads.append((oh * (gh * jax.nn.sigmoid(gh))).astype(BF16))
    y = jnp.dot(conv_out, wout_ref[0:C_CONV, :], preferred_element_type=F32)
    for h in range(N_GLA_HEADS):
        y = y + jnp.dot(heads[h], wout_ref[C_CONV + h * DV_HEAD:C_CONV + (h + 1) * DV_HEAD, :],
                        preferred_element_type=F32)

    gate1 = mod_ref[0, :, 2 * D_MODEL:3 * D_MODEL]
    shift2 = mod_ref[0, :, 3 * D_MODEL:4 * D_MODEL]
    scale2 = mod_ref[0, :, 4 * D_MODEL:5 * D_MODEL]
    x1 = x_ref[...] + gate1 * y
    x1_ref[...] = x1
    h2 = _rms(x1, n2g_ref[...]) * (1.0 + scale2) + shift2
    h2_hi, h2_lo = _split(h2)
    h2_ref[...] = h2_hi

    dot = functools.partial(jnp.dot, preferred_element_type=F32)
    logits = (dot(h2_hi, rwh_ref[...]) + dot(h2_lo, rwh_ref[...]) + dot(h2_hi, rwl_ref[...])
              + rb_ref[...])
    lane = lax.broadcasted_iota(jnp.int32, (TM, LANES), 1)
    lanef = lane.astype(F32)
    l = logits
    vals, idxs, hots = [], [], []
    for _ in range(TOP_K):
        m = jnp.max(l, axis=-1, keepdims=True)
        idx = jnp.min(jnp.where(l == m, lanef, float(LANES)), axis=-1, keepdims=True)
        hot = lanef == idx
        l = jnp.where(hot, -jnp.inf, l)
        vals.append(m)
        idxs.append(idx)
        hots.append(hot)
    es = [jnp.exp(vk - vals[0]) for vk in vals]
    inv = 1.0 / (es[0] + es[1] + es[2] + es[3])
    sel = jnp.zeros((TM, LANES), F32)
    for hot in hots:
        sel = jnp.where(hot, 1.0, sel)

    cnt = jnp.sum(sel, axis=0, keepdims=True)
    strip = jnp.ceil(cnt * (1.0 / SEG)) * float(SEG)
    starts = jnp.dot(jnp.broadcast_to(strip, (8, LANES)), ut_ref[...], precision=HI,
                     preferred_element_type=F32)[0:1]
    slot_of = jnp.dot(lt_ref[...], sel.astype(BF16), preferred_element_type=F32) + starts
    route = jnp.zeros((TM, LANES), F32)
    gates = jnp.zeros((TM, LANES), F32)
    for kk in range(TOP_K):
        slot = jnp.sum(jnp.where(hots[kk], slot_of, 0.0), axis=-1, keepdims=True)
        route = jnp.where(lane == kk, slot, route)
        route = jnp.where(lane == TOP_K + kk, idxs[kk], route)
        gates = jnp.where(lane == kk, es[kk] * inv, gates)
    route_ref[...] = route.astype(jnp.int32)
    gates_ref[...] = gates
    cnt_ref[0] = jnp.broadcast_to(cnt, (8, LANES))


def _mix(x, u, g, o_f, o_b, mod3, seg_fn, tile_off, seq_len, wts):
    (conv_w, conv_b, ln_g, ln_b, gng, wout, n2g, rwh, rwl, rb) = wts
    t_path = x.shape[0]
    n = t_path // TM
    lt = jnp.asarray(np.tril(np.ones((TM, TM), np.float32), -1)).astype(BF16)
    ut = jnp.asarray(np.triu(np.ones((LANES, LANES), np.float32), 1))
    const = lambda a: pl.BlockSpec(a.shape, lambda i: (0,) * a.ndim)
    loc = lambda w: pl.BlockSpec((TM, w), lambda i: (i, 0))
    uni = lambda w: pl.BlockSpec((TM, w), lambda i: (i + tile_off, 0))
    n_seq = TM // seq_len
    return pl.pallas_call(
        functools.partial(_mix_kernel, seq_len=seq_len),
        grid=(n,),
        in_specs=[loc(D_MODEL), uni(2 * C_CONV), uni(DV_TOT), loc(DV_TOT), loc(DV_TOT),
                  pl.BlockSpec((1, 1, 6 * D_MODEL), lambda i: (seg_fn(i), 0, 0)),
                  const(conv_w), const(conv_b), const(ln_g), const(ln_b), const(gng), const(wout),
                  const(n2g), const(rwh), const(rwl), const(rb), const(lt), const(ut)],
        out_specs=[loc(D_MODEL), loc(D_MODEL), loc(LANES), loc(LANES),
                   pl.BlockSpec((1, 8, LANES), lambda i: (i, 0, 0))],
        out_shape=[jax.ShapeDtypeStruct((t_path, D_MODEL), F32),
                   jax.ShapeDtypeStruct((t_path, D_MODEL), BF16),
                   jax.ShapeDtypeStruct((t_path, LANES), jnp.int32),
                   jax.ShapeDtypeStruct((t_path, LANES), F32),
                   jax.ShapeDtypeStruct((n, 8, LANES), F32)],
        scratch_shapes=[pltpu.VMEM((n_seq * (seq_len + 2 * CONV_PAD), C_CONV), F32),
                        pltpu.VMEM((SEG - 1, n_seq * (seq_len + 3 * SEG), C_CONV), F32),
                        pltpu.VMEM((TM, C_CONV), F32)],
        compiler_params=_cparams(("arbitrary",), VMEM_LIMIT),
        name="mix",
    )(x, u, g, o_f, o_b, mod3, conv_w, conv_b, ln_g, ln_b, gng, wout, n2g, rwh, rwl, rb, lt, ut)


def _strip_copies(tile, ls_ref, gs_ref, sg_ref, local, rows_hbm, sem, to_hbm, wait):
    def body(e, carry):
        j = tile * N_EXPERTS + e
        n = pl.multiple_of(sg_ref[j], SEG)

        @pl.when(n > 0)
        def _():
            loc = local.at[pl.ds(pl.multiple_of(ls_ref[j], SEG), n)]
            glob = rows_hbm.at[pl.ds(pl.multiple_of(gs_ref[j], SEG), n)]
            cp = pltpu.make_async_copy(loc, glob, sem) if to_hbm else pltpu.make_async_copy(glob, loc, sem)
            if wait:
                cp.wait()
            else:
                cp.start()
        return carry

    lax.fori_loop(0, N_EXPERTS, body, 0)


def _dispatch_kernel(ls_ref, gs_ref, sg_ref, ts_ref, tl_ref, nv_ref, h2c_ref, h2l_ref, rc_ref, rl_ref,
                     xs_ref, cbuf, zbuf, sem, zsem, *, n_ctx_tiles, n_tiles, n_row_tiles):
    b = pl.program_id(0)
    slot = b % 2

    def zero_fill(wait):
        def tail(e, carry):
            n = pl.multiple_of(tl_ref[e], SEG)

            @pl.when(n > 0)
            def _():
                cp = pltpu.make_async_copy(zbuf.at[pl.ds(0, n)],
                                           xs_ref.at[pl.ds(pl.multiple_of(ts_ref[e], SEG), n)], zsem)
                cp.wait() if wait else cp.start()
            return carry

        def unused(t, carry):
            cp = pltpu.make_async_copy(zbuf, xs_ref.at[pl.ds(pl.multiple_of(t * TE, TE), TE)], zsem)
            cp.wait() if wait else cp.start()
            return carry

        lax.fori_loop(0, N_EXPERTS, tail, 0)
        lax.fori_loop(nv_ref[0], n_row_tiles, unused, 0)

    @pl.when(b == 0)
    def _():
        zbuf[...] = jnp.zeros(zbuf.shape, F32)
        zero_fill(False)

    @pl.when(b >= 2)
    def _():
        _strip_copies(b - 2, ls_ref, gs_ref, sg_ref, cbuf.at[slot], xs_ref, sem.at[slot], True, True)

    first = b < n_ctx_tiles
    h2 = jnp.where(first, h2c_ref[...], h2l_ref[...])
    route = jnp.where(first, rc_ref[...], rl_ref[...])
    r = lax.broadcasted_iota(jnp.int32, (TM, NR), 1)
    perm = jnp.zeros((TM, NR), F32)
    for kk in range(TOP_K):
        perm = jnp.where(route[:, kk:kk + 1] == r, 1.0, perm)
    cbuf[slot] = lax.dot_general(perm.astype(BF16), h2, TN, preferred_element_type=F32)
    _strip_copies(b, ls_ref, gs_ref, sg_ref, cbuf.at[slot], xs_ref, sem.at[slot], True, False)

    @pl.when(b == n_tiles - 1)
    def _():
        if n_tiles > 1:
            _strip_copies(b - 1, ls_ref, gs_ref, sg_ref, cbuf.at[1 - slot], xs_ref, sem.at[1 - slot],
                          True, True)
        _strip_copies(b, ls_ref, gs_ref, sg_ref, cbuf.at[slot], xs_ref, sem.at[slot], True, True)
        zero_fill(True)


def _dispatch(plan, h2_c, h2_l, route_c, route_l, n_rows):
    n_c, n_l = h2_c.shape[0] // TM, h2_l.shape[0] // TM
    n_tiles = n_c + n_l
    lo = lambda w: pl.BlockSpec((TM, w), lambda i, *_: (jnp.minimum(i, n_c - 1), 0))
    hi = lambda w: pl.BlockSpec((TM, w), lambda i, *_: (jnp.maximum(i - n_c, 0), 0))
    return pl.pallas_call(
        functools.partial(_dispatch_kernel, n_ctx_tiles=n_c, n_tiles=n_tiles, n_row_tiles=n_rows // TE),
        grid_spec=pltpu.PrefetchScalarGridSpec(
            num_scalar_prefetch=6,
            grid=(n_tiles,),
            in_specs=[lo(D_MODEL), hi(D_MODEL), lo(LANES), hi(LANES)],
            out_specs=pl.BlockSpec(memory_space=pl.ANY),
            scratch_shapes=[pltpu.VMEM((2, NR, D_MODEL), F32), pltpu.VMEM((TE, D_MODEL), F32),
                            pltpu.SemaphoreType.DMA((2,)), pltpu.SemaphoreType.DMA]),
        out_shape=jax.ShapeDtypeStruct((n_rows, D_MODEL), F32),
        compiler_params=_cparams(("arbitrary",), VMEM_LIMIT),
        name="dispatch",
    )(plan["lstart"], plan["gstart"], plan["strip"], plan["tail_start"], plan["tail_len"],
      plan["n_valid"], h2_c, h2_l, route_c, route_l)


def _expert_kernel(te_ref, nv_ref, first_ref, next_ref, slot_ref, x_ref, wgu_hbm, bgu_ref, wdn_hbm, bdn_ref,
                   y_ref, wgu_f, wdn_f, wgu_s, wdn_s, sem):
    i = pl.program_id(0)

    def weight_copies(e, s):
        return (pltpu.make_async_copy(wgu_hbm.at[e], wgu_f.at[s], sem.at[0, s]),
                pltpu.make_async_copy(wdn_hbm.at[e], wdn_f.at[s], sem.at[1, s]))

    @pl.when(i < nv_ref[0])
    def _():
        e = te_ref[i]
        s = slot_ref[i]

        @pl.when(first_ref[i] == 1)
        def _():
            @pl.when(i == 0)
            def _():
                for cp in weight_copies(e, s):
                    cp.start()

            for cp in weight_copies(e, s):
                cp.wait()
            nxt = next_ref[i]

            @pl.when(nxt >= 0)
            def _():
                for cp in weight_copies(nxt, 1 - s):
                    cp.start()

            rows = 64

            def cast(r, carry):
                sl = pl.ds(pl.multiple_of(r * rows, rows), rows)
                wgu_s[sl, :] = wgu_f[s, sl, :].astype(BF16)
                wdn_s[sl, :] = wdn_f[s, sl, :].astype(BF16)
                return carry

            lax.fori_loop(0, D_MODEL // rows, cast, 0)

        gu = jnp.dot(x_ref[...].astype(BF16), wgu_s[...], preferred_element_type=F32) + bgu_ref[0]
        gate = jnp.minimum(gu[:, 0:D_FF], SWIGLU_LIMIT)
        up = jnp.clip(gu[:, D_FF:2 * D_FF], -SWIGLU_LIMIT, SWIGLU_LIMIT)
        act = (gate * jax.nn.sigmoid(SWIGLU_ALPHA * gate) * (up + 1.0)).astype(BF16)
        y_ref[...] = jnp.dot(act, wdn_s[...], preferred_element_type=F32) + bdn_ref[0]

    @pl.when(i >= nv_ref[0])
    def _():
        y_ref[...] = jnp.zeros(y_ref.shape, y_ref.dtype)


def _experts(plan, xs, w_gu, b_gu, w_dn, b_dn):
    n_rows = xs.shape[0]
    nt = n_rows // TE
    row = lambda i, te, nv, *_: (jnp.maximum(jnp.minimum(i, nv[0] - 1), 0), 0)
    exp3 = lambda i, te, *_: (te[i], 0, 0)
    any_spec = pl.BlockSpec(memory_space=pl.ANY)
    return pl.pallas_call(
        _expert_kernel,
        grid_spec=pltpu.PrefetchScalarGridSpec(
            num_scalar_prefetch=5,
            grid=(nt,),
            in_specs=[pl.BlockSpec((TE, D_MODEL), row), any_spec,
                      pl.BlockSpec((1, 1, 2 * D_FF), exp3), any_spec,
                      pl.BlockSpec((1, 1, D_MODEL), exp3)],
            out_specs=pl.BlockSpec((TE, D_MODEL), lambda i, *_: (i, 0)),
            scratch_shapes=[pltpu.VMEM((2, D_MODEL, 2 * D_FF), F32), pltpu.VMEM((2, D_FF, D_MODEL), F32),
                            pltpu.VMEM((D_MODEL, 2 * D_FF), BF16), pltpu.VMEM((D_FF, D_MODEL), BF16),
                            pltpu.SemaphoreType.DMA((2, 2))]),
        out_shape=jax.ShapeDtypeStruct((n_rows, D_MODEL), F32),
        compiler_params=_cparams(("arbitrary",), VMEM_LIMIT),
        name="experts",
    )(plan["tile_expert"], plan["n_valid"], plan["tile_first"], plan["tile_next"], plan["tile_slot"],
      xs, w_gu, b_gu.reshape(N_EXPERTS, 1, 2 * D_FF), w_dn, b_dn.reshape(N_EXPERTS, 1, D_MODEL))


def _final_kernel(ls_ref, gs_ref, sg_ref, x1_ref, route_ref, gates_ref, mod_ref, fg_ref, ys_ref, y_ref,
                  ybuf, sem, *, tile_off, n_tiles):
    i = pl.program_id(0)
    slot = i % 2

    def fetch(step, sl, wait):
        _strip_copies(step + tile_off, ls_ref, gs_ref, sg_ref, ybuf.at[sl], ys_ref, sem.at[sl], False, wait)

    @pl.when(i == 0)
    def _():
        ybuf[...] = jnp.zeros(ybuf.shape, F32)
        fetch(i, slot, False)

    @pl.when(i + 1 < n_tiles)
    def _():
        fetch(i + 1, 1 - slot, False)

    fetch(i, slot, True)

    route = route_ref[...]
    gates = gates_ref[...]
    r = lax.broadcasted_iota(jnp.int32, (TM, NR), 1)
    comb = jnp.zeros((TM, NR), F32)
    for kk in range(TOP_K):
        comb = jnp.where(route[:, kk:kk + 1] == r, gates[:, kk:kk + 1], comb)
    moe = jnp.dot(comb.astype(BF16), ybuf[slot].astype(BF16), preferred_element_type=F32)
    gate2 = mod_ref[0, :, 5 * D_MODEL:6 * D_MODEL]
    y_ref[...] = _rms(x1_ref[...] + gate2 * moe, fg_ref[...])


def _final(plan, x1, route, gates, mod3, seg_fn, tile_off, fg, ys):
    t_path = x1.shape[0]
    n = t_path // TM
    loc = lambda w: pl.BlockSpec((TM, w), lambda i, *_: (i, 0))
    return pl.pallas_call(
        functools.partial(_final_kernel, tile_off=tile_off, n_tiles=n),
        grid_spec=pltpu.PrefetchScalarGridSpec(
            num_scalar_prefetch=3,
            grid=(n,),
            in_specs=[loc(D_MODEL), loc(LANES), loc(LANES),
                      pl.BlockSpec((1, 1, 6 * D_MODEL), lambda i, *_: (seg_fn(i), 0, 0)),
                      pl.BlockSpec((1, D_MODEL), lambda i, *_: (0, 0)),
                      pl.BlockSpec(memory_space=pl.ANY)],
            out_specs=loc(D_MODEL),
            scratch_shapes=[pltpu.VMEM((2, NR, D_MODEL), F32), pltpu.SemaphoreType.DMA((2,))]),
        out_shape=jax.ShapeDtypeStruct((t_path, D_MODEL), F32),
        compiler_params=_cparams(("arbitrary",), VMEM_LIMIT),
        name="final",
    )(plan["lstart"], plan["gstart"], plan["strip"], x1, route, gates, mod3, fg, ys)


def _plan(tile_counts, n_rows):
    nt = n_rows // TE
    strip = (tile_counts + SEG - 1) // SEG * SEG
    lstart = jnp.cumsum(strip, axis=1) - strip
    rows_e = jnp.sum(strip, axis=0)
    rpad = (rows_e + TE - 1) // TE * TE
    ends = jnp.cumsum(rpad)
    offs = ends - rpad
    gstart = offs[None, :] + jnp.cumsum(strip, axis=0) - strip
    n_valid = (ends[-1] // TE).astype(jnp.int32)
    tile_ids = jnp.minimum(jnp.arange(nt, dtype=jnp.int32), n_valid - 1)
    tile_expert = jnp.sum((ends[None, :] <= (tile_ids * TE)[:, None]).astype(jnp.int32), axis=1)
    tile_expert = jnp.minimum(tile_expert, N_EXPERTS - 1).astype(jnp.int32)
    prev = jnp.concatenate([jnp.full((1,), -1, jnp.int32), tile_expert[:-1]])
    tile_first = (tile_expert != prev).astype(jnp.int32)
    tile_slot = (jnp.cumsum(tile_first) - 1) % 2
    after = jnp.take(ends, tile_expert) // TE
    tile_next = jnp.where(after < n_valid, jnp.take(tile_expert, jnp.minimum(after, nt - 1)), -1)
    i32 = lambda a: a.reshape(-1).astype(jnp.int32)
    return dict(strip=i32(strip), lstart=i32(lstart), gstart=i32(gstart), tail_start=i32(offs + rows_e),
                tail_len=i32(rpad - rows_e), n_valid=n_valid.reshape(1), tile_expert=tile_expert,
                tile_first=i32(tile_first), tile_next=i32(tile_next), tile_slot=i32(tile_slot))


def _layer(x_prompt, x_sample, state, c, c_ctx, ada_w, ada_b, norm1_g, norm2_g, w_in, conv_w, conv_b,
           conv_ln_g, conv_ln_b, gate_w, gate_b, gla_norm_g, w_out, router_w, router_b,
           moe_w_gu, moe_b_gu, moe_w_dn, moe_b_dn, final_g):
    bp, lp, d = x_prompt.shape
    bs, ls, _ = x_sample.shape
    assert lp == TM and ls % TM == 0 and d == D_MODEL
    xp = x_prompt.reshape(bp * lp, d)
    xs = x_sample.reshape(bs * ls, d)
    n_c, n_l = bp, bs * ls // TM
    lat_tiles = ls // TM
    t_all = (n_c + n_l) * TM

    n_cond = 1 + bs
    cond_t = jnp.zeros((d, 8), F32).at[:, 0].set(c_ctx).at[:, 1:n_cond].set(c.T)
    mod3 = _modulation(cond_t, ada_w, ada_b, n_cond).reshape(8, 1, 6 * d)

    row = lambda a: a.reshape(1, -1)
    u, qk, v, g, la = _inproj(xp, xs, mod3, row(norm1_g), w_in, gate_w, gate_b, ls)

    pair = lambda s: s.reshape(s.shape[0], 2, 2, LANES, LANES)
    of_c, ob_c, s_ctx = _gla(qk, v, la, jnp.zeros((bp, 2, 2, LANES, LANES), F32), bp, 1, 0)
    of_l, ob_l, _ = _gla(qk, v, la, pair(state), bs, lat_tiles, n_c)

    cw = jnp.zeros((32, C_CONV), F32).at[:CONV_K].set(conv_w)
    rw = jnp.zeros((d, LANES), F32).at[:, :N_EXPERTS].set(router_w)
    rwh = rw.astype(BF16)
    rwl = (rw - rwh.astype(F32)).astype(BF16)
    rb = jnp.full((1, LANES), NEG, F32).at[0, :N_EXPERTS].set(router_b)
    wts = (cw, row(conv_b), row(conv_ln_g), row(conv_ln_b), row(gla_norm_g), w_out.astype(BF16),
           row(norm2_g), rwh, rwl, rb)
    seg_c = lambda i: 0
    seg_l = lambda i: 1 + i // lat_tiles
    x1_c, h2_c, route_c, gates_c, cnt_c = _mix(xp, u, g, of_c, ob_c, mod3, seg_c, 0, lp, wts)
    x1_l, h2_l, route_l, gates_l, cnt_l = _mix(xs, u, g, of_l, ob_l, mod3, seg_l, n_c, GRID_W, wts)

    tile_counts = jnp.concatenate([cnt_c[:, 0, :N_EXPERTS], cnt_l[:, 0, :N_EXPERTS]]).astype(jnp.int32)
    n_rows = TOP_K * t_all + (SEG - 1) * N_EXPERTS * (n_c + n_l) + N_EXPERTS * TE
    n_rows = (n_rows + TE - 1) // TE * TE
    plan = _plan(tile_counts, n_rows)
    xsrt = _dispatch(plan, h2_c, h2_l, route_c, route_l, n_rows)
    ysrt = _experts(plan, xsrt, moe_w_gu, moe_b_gu, moe_w_dn, moe_b_dn)
    y_c = _final(plan, x1_c, route_c, gates_c, mod3, seg_c, 0, row(final_g), ysrt)
    y_l = _final(plan, x1_l, route_l, gates_l, mod3, seg_l, n_c, row(final_g), ysrt)
    new_state = s_ctx.reshape(bp, 1, 2, N_GLA_HEADS, DK_HEAD, DV_HEAD)
    return y_c.reshape(bp, lp, d), y_l.reshape(bs, ls, d), new_state


def kernel(x_prompt, x_sample, state_gla, c, c_ctx, ada_w, ada_b, norm1_g, norm2_g, w_in, conv_w,
           conv_b, conv_ln_g, conv_ln_b, gate_w, gate_b, gla_norm_g, w_out, router_w, router_b,
           moe_w_gu, moe_b_gu, moe_w_dn, moe_b_dn, final_g):
    assert ada_w.shape[0] == 1, "single-layer step"
    return _layer(x_prompt, x_sample, state_gla[:, 0], c, c_ctx, ada_w[0], ada_b[0], norm1_g[0],
                  norm2_g[0], w_in[0], conv_w[0], conv_b[0], conv_ln_g[0], conv_ln_b[0], gate_w[0],
                  gate_b[0], gla_norm_g[0], w_out[0], router_w[0], router_b[0], moe_w_gu[0],
                  moe_b_gu[0], moe_w_dn[0], moe_b_dn[0], final_g)
```

```python
import functools

import numpy as np
import jax
import jax.numpy as jnp
from jax import lax
from jax.experimental import pallas as pl
from jax.experimental.pallas import tpu as pltpu

D_MODEL = 1024
GRID_W = 64
C_CONV = D_MODEL // 2
CONV_K = 31
N_GLA_HEADS = 4
DV_HEAD = 128
DK_HEAD = 64
DK_TOT = DK_HEAD * N_GLA_HEADS
DV_TOT = DV_HEAD * N_GLA_HEADS
GATE_RANK = 16
GATE_TEMP = 16.0
CHUNK = 64
N_EXPERTS = 32
TOP_K = 4
D_FF = D_MODEL
SWIGLU_LIMIT = 7.0
SWIGLU_ALPHA = 1.702
EPS = 1e-6

LANES = 128
SEG = 8
TM = 256
TE = 512
NR = TOP_K * TM + TM
CONV_PAD = 16
VMEM_LIMIT = 56 * 1024 * 1024

F32 = jnp.float32
BF16 = jnp.bfloat16
HI = lax.Precision.HIGHEST
NEG = -1e30
TN = (((0,), (0,)), ((), ()))
NT = (((1,), (1,)), ((), ()))

assert N_EXPERTS * (SEG - 1) <= NR - TOP_K * TM


def _split(x):
    hi = x.astype(BF16)
    return hi, (x - hi.astype(F32)).astype(BF16)


def _cparams(sem, vmem=None):
    return pltpu.CompilerParams(dimension_semantics=sem, vmem_limit_bytes=vmem)


def _mod_kernel(ct_ref, w_ref, b_ref, o_ref, *, n_cond):
    ct = ct_ref[...]
    s = ct * jax.nn.sigmoid(ct)
    w = w_ref[...]
    rows = [jnp.sum(s[:, r:r + 1] * w, axis=0, keepdims=True) + b_ref[...] for r in range(n_cond)]
    rows.append(jnp.zeros((8 - n_cond, w.shape[1]), F32))
    o_ref[...] = jnp.concatenate(rows, axis=0)


def _modulation(cond_t, ada_w, ada_b, n_cond):
    d, n = ada_w.shape
    nb = 768
    return pl.pallas_call(
        functools.partial(_mod_kernel, n_cond=n_cond),
        grid=(n // nb,),
        in_specs=[pl.BlockSpec((d, 8), lambda i: (0, 0)),
                  pl.BlockSpec((d, nb), lambda i: (0, i)),
                  pl.BlockSpec((1, nb), lambda i: (0, i))],
        out_specs=pl.BlockSpec((8, nb), lambda i: (0, i)),
        out_shape=jax.ShapeDtypeStruct((8, n), F32),
        compiler_params=_cparams(("arbitrary",)),
        name="mod",
    )(cond_t, ada_w, ada_b.reshape(1, n))


def _rms(x, g):
    return x * lax.rsqrt(jnp.mean(x * x, axis=-1, keepdims=True) + EPS) * g


def _inproj_kernel(xp_ref, xs_ref, mod_ref, g1_ref, wu_ref, wqk_ref, wv_ref, wg_ref, wlow_ref,
                   gwh_ref, gwl_ref, gb_ref, u_ref, qk_ref, v_ref, g_ref, la_ref, *, n_ctx_tiles):
    i = pl.program_id(0)
    shift = mod_ref[0, :, 0:D_MODEL]
    scale = mod_ref[0, :, D_MODEL:2 * D_MODEL]
    dot = functools.partial(jnp.dot, preferred_element_type=F32)
    rows = u_ref.shape[0]
    sub = min(rows, 2 * TM)
    for r0 in range(0, rows, sub):
        rs = slice(r0, r0 + sub)
        x = jnp.where(i < n_ctx_tiles, xp_ref[rs, :], xs_ref[rs, :])
        h = (_rms(x, g1_ref[...]) * (1.0 + scale) + shift).astype(BF16)
        u_ref[rs, :] = dot(h, wu_ref[...]).astype(BF16)
        qk_ref[rs, :] = dot(h, wqk_ref[...]).astype(BF16)
        v_ref[rs, :] = dot(h, wv_ref[...]).astype(BF16)
        g_ref[rs, :] = dot(h, wg_ref[...]).astype(BF16)
        low = dot(h, wlow_ref[...])
        low_hi, low_lo = _split(low)
        z = (dot(low_hi, gwh_ref[...]) + dot(low_lo, gwh_ref[...]) + dot(low_hi, gwl_ref[...])
             + gb_ref[...])
        la_ref[rs, :] = (jnp.minimum(z, 0.0) - jnp.log1p(jnp.exp(-jnp.abs(z)))) * (1.0 / GATE_TEMP)


def _inproj(xp, xs, mod3, g1, w_in, gate_w, gate_b, lat_len):
    ti = next(t for t in (1024, 512, TM) if xp.shape[0] % t == 0 and lat_len % t == 0)
    lat_tiles_per_seq = lat_len // ti
    n_c, n_l = xp.shape[0] // ti, xs.shape[0] // ti
    t_all = xp.shape[0] + xs.shape[0]
    sp = np.cumsum([0, C_CONV, C_CONV, DK_TOT, DK_TOT, DV_TOT, DV_TOT, 2 * GATE_RANK])
    wb = w_in.astype(BF16)
    wu, wqk, wv, wg, wlow = (wb[:, sp[0]:sp[2]], wb[:, sp[2]:sp[4]], wb[:, sp[4]:sp[5]],
                             wb[:, sp[5]:sp[6]], wb[:, sp[6]:sp[7]])
    gw = jnp.zeros((2 * GATE_RANK, 2 * DK_TOT), F32)
    gw = gw.at[:GATE_RANK, :DK_TOT].set(gate_w[0]).at[GATE_RANK:, DK_TOT:].set(gate_w[1])
    gwh, gwl = _split(gw)
    gb = gate_b.reshape(1, 2 * DK_TOT)
    const = lambda a: pl.BlockSpec(a.shape, lambda i: (0,) * a.ndim)
    row = lambda w: pl.BlockSpec((ti, w), lambda i: (i, 0))
    seg = lambda i: jnp.where(i < n_c, 0, 1 + jnp.maximum(i - n_c, 0) // lat_tiles_per_seq)
    return pl.pallas_call(
        functools.partial(_inproj_kernel, n_ctx_tiles=n_c),
        grid=(n_c + n_l,),
        in_specs=[pl.BlockSpec((ti, D_MODEL), lambda i: (jnp.minimum(i, n_c - 1), 0)),
                  pl.BlockSpec((ti, D_MODEL), lambda i: (jnp.maximum(i - n_c, 0), 0)),
                  pl.BlockSpec((1, 1, 6 * D_MODEL), lambda i: (seg(i), 0, 0)),
                  const(g1), const(wu), const(wqk), const(wv), const(wg), const(wlow),
                  const(gwh), const(gwl), const(gb)],
        out_specs=[row(2 * C_CONV), row(2 * DK_TOT), row(DV_TOT), row(DV_TOT), row(2 * DK_TOT)],
        out_shape=[jax.ShapeDtypeStruct((t_all, 2 * C_CONV), BF16),
                   jax.ShapeDtypeStruct((t_all, 2 * DK_TOT), BF16),
                   jax.ShapeDtypeStruct((t_all, DV_TOT), BF16),
                   jax.ShapeDtypeStruct((t_all, DV_TOT), BF16),
                   jax.ShapeDtypeStruct((t_all, 2 * DK_TOT), F32)],
        compiler_params=_cparams(("arbitrary",), VMEM_LIMIT),
        name="inproj",
    )(xp, xs, mod3, g1, wu, wqk, wv, wg, wlow, gwh, gwl, gb)


def _gla_direction(qk_ref, v_ref, la_ref, tri_ref, blk_ref, o_ref, s_scr, reverse):
    la_hi, la_lo = _split(la_ref[...])
    tri = tri_ref[...]
    bcum = (jnp.dot(tri, la_hi, preferred_element_type=F32)
            + jnp.dot(tri, la_lo, preferred_element_type=F32))
    blk = blk_ref[...]
    bl_cols = (lax.dot_general(la_hi, blk, TN, preferred_element_type=F32)
               + lax.dot_general(la_lo, blk, TN, preferred_element_type=F32))
    q = qk_ref[:, 0:DK_TOT].astype(F32)
    k = qk_ref[:, DK_TOT:2 * DK_TOT].astype(F32)
    lane = lax.broadcasted_iota(jnp.int32, (CHUNK, LANES), 1)
    row2 = lax.broadcasted_iota(jnp.int32, (2 * CHUNK, CHUNK), 0) % CHUNK
    col2 = lax.broadcasted_iota(jnp.int32, (2 * CHUNK, CHUNK), 1)
    keep = (col2 >= row2) if reverse else (col2 <= row2)
    srow = lax.broadcasted_iota(jnp.int32, (LANES, LANES), 0)
    n_chunks = TM // CHUNK
    order = range(n_chunks - 1, -1, -1) if reverse else range(n_chunks)
    state = [s_scr[0], s_scr[1]]
    for c in order:
        r0 = c * CHUNK
        bc = bcum[r0:r0 + CHUNK]
        bl = bc[0:1] if reverse else bc[CHUNK - 1:CHUNK]
        qt = q[r0:r0 + CHUNK] * jnp.exp(bc) * (DK_HEAD ** -0.5)
        kt = k[r0:r0 + CHUNK] * jnp.exp(-bc)
        ke = k[r0:r0 + CHUNK] * jnp.exp(bl - bc)
        for p in range(2):
            cs = slice(p * LANES, (p + 1) * LANES)
            qt_p = qt[:, cs]
            qs = jnp.concatenate([jnp.where(lane < DK_HEAD, qt_p, 0.0),
                                  jnp.where(lane >= DK_HEAD, qt_p, 0.0)], axis=0).astype(BF16)
            att = lax.dot_general(qs, kt[:, cs].astype(BF16), NT, preferred_element_type=F32)
            att = jnp.where(keep, att, 0.0).astype(BF16)
            s_p = state[p]
            o_inter = jnp.dot(qs, s_p.astype(BF16), preferred_element_type=F32)
            ke_p = ke[:, cs].astype(BF16)
            upd = []
            for hh in range(2):
                h = 2 * p + hh
                v_h = v_ref[r0:r0 + CHUNK, h * DV_HEAD:(h + 1) * DV_HEAD]
                o_h = jnp.dot(att[hh * CHUNK:(hh + 1) * CHUNK], v_h, preferred_element_type=F32)
                o_ref[r0:r0 + CHUNK, h * DV_HEAD:(h + 1) * DV_HEAD] = (
                    o_h + o_inter[hh * CHUNK:(hh + 1) * CHUNK])
                upd.append(lax.dot_general(ke_p, v_h, TN, preferred_element_type=F32))
            bl_col = bl_cols[p * LANES:(p + 1) * LANES, c * LANES:(c + 1) * LANES]
            state[p] = jnp.exp(bl_col) * s_p + jnp.where(srow < DK_HEAD, upd[0], upd[1])
    s_scr[0] = state[0]
    s_scr[1] = state[1]


def _gla_kernel(*refs, n_tiles, nb):
    seqs = [refs[6 * r:6 * r + 6] for r in range(nb)]
    tril_ref, triu_ref, blk_ref, s0_ref, of_ref, ob_ref, sout_ref, sf_scr, sb_scr = refs[6 * nb:]
    j = pl.program_id(1)

    @pl.when(j == 0)
    def _():
        sf_scr[...] = s0_ref[:, 0]
        sb_scr[...] = s0_ref[:, 1]

    for r, (qkf_ref, vf_ref, laf_ref, qkb_ref, vb_ref, lab_ref) in enumerate(seqs):
        _gla_direction(qkf_ref, vf_ref, laf_ref, tril_ref, blk_ref, of_ref.at[r], sf_scr.at[r], False)
        _gla_direction(qkb_ref, vb_ref, lab_ref, triu_ref, blk_ref, ob_ref.at[r], sb_scr.at[r], True)

    @pl.when(j == n_tiles - 1)
    def _():
        sout_ref[:, 0] = sf_scr[...]
        sout_ref[:, 1] = sb_scr[...]


def _gla(qk, v, la, s0, n_seq, n_tiles, tile_off):
    nb = 2 if n_seq % 2 == 0 else 1
    seq_len = n_tiles * TM
    blk = np.arange(TM) // CHUNK
    same = blk[:, None] == blk[None, :]
    r = np.arange(TM)
    tril = jnp.asarray((same & (r[None, :] <= r[:, None])).astype(np.float32)).astype(BF16)
    triu = jnp.asarray((same & (r[None, :] >= r[:, None])).astype(np.float32)).astype(BF16)
    chunk_of_lane = np.arange(TM // CHUNK * LANES) // LANES
    blkm = jnp.asarray((blk[:, None] == chunk_of_lane[None, :]).astype(np.float32)).astype(BF16)
    def tok(w, r, backward, cb=0):
        def index(s, j):
            jj = (n_tiles - 1 - j) if backward else j
            return ((s * nb + r) * n_tiles + jj + tile_off, cb)
        return pl.BlockSpec((TM, w), index)

    const = lambda a: pl.BlockSpec(a.shape, lambda s, j: (0,) * a.ndim)
    st = pl.BlockSpec((nb, 2, 2, LANES, LANES), lambda s, j: (s, 0, 0, 0, 0))
    seq_specs, seq_args = [], []
    for r in range(nb):
        seq_specs += [tok(2 * DK_TOT, r, False), tok(DV_TOT, r, False), tok(DK_TOT, r, False, 0),
                      tok(2 * DK_TOT, r, True), tok(DV_TOT, r, True), tok(DK_TOT, r, True, 1)]
        seq_args += [qk, v, la, qk, v, la]
    o_f, o_b, s_out = pl.pallas_call(
        functools.partial(_gla_kernel, n_tiles=n_tiles, nb=nb),
        grid=(n_seq // nb, n_tiles),
        in_specs=seq_specs + [const(tril), const(triu), const(blkm), st],
        out_specs=[pl.BlockSpec((nb, TM, DV_TOT), lambda s, j: (s, j, 0)),
                   pl.BlockSpec((nb, TM, DV_TOT), lambda s, j: (s, n_tiles - 1 - j, 0)), st],
        out_shape=[jax.ShapeDtypeStruct((n_seq, seq_len, DV_TOT), F32),
                   jax.ShapeDtypeStruct((n_seq, seq_len, DV_TOT), F32),
                   jax.ShapeDtypeStruct(s0.shape, F32)],
        scratch_shapes=[pltpu.VMEM((nb, 2, LANES, LANES), F32), pltpu.VMEM((nb, 2, LANES, LANES), F32)],
        compiler_params=_cparams(("arbitrary", "arbitrary"), VMEM_LIMIT),
        name="gla",
    )(*seq_args, tril, triu, blkm, s0)
    return o_f.reshape(n_seq * seq_len, DV_TOT), o_b.reshape(n_seq * seq_len, DV_TOT), s_out


def _mix_kernel(x_ref, u_ref, g_ref, of_ref, ob_ref, mod_ref, cw_ref, cb_ref, lng_ref, lnb_ref,
                gng_ref, wout_ref, n2g_ref, rwh_ref, rwl_ref, rb_ref, lt_ref, ut_ref,
                x1_ref, h2_ref, route_ref, gates_ref, cnt_ref, pad_scr, shf_scr, cv_scr, *, seq_len):
    n_seq = TM // seq_len
    pr = seq_len + 2 * CONV_PAD
    qr = seq_len + 3 * SEG

    u = u_ref[...].astype(F32)
    hg = u[:, 0:C_CONV] * jax.nn.sigmoid(u[:, C_CONV:2 * C_CONV])
    zero = jnp.zeros((CONV_PAD, C_CONV), F32)
    for s in range(n_seq):
        pad_scr[s * pr:s * pr + CONV_PAD, :] = zero
        pad_scr[s * pr + CONV_PAD:s * pr + CONV_PAD + seq_len, :] = hg[s * seq_len:(s + 1) * seq_len]
        pad_scr[s * pr + CONV_PAD + seq_len:(s + 1) * pr, :] = zero
        for b in range(1, SEG):
            shf_scr[b - 1, s * qr:(s + 1) * qr, :] = pad_scr[s * pr + b:s * pr + b + qr, :]
    rb = 4 * SEG
    per_seq = seq_len // rb
    for gi in range(C_CONV // LANES):
        gs = slice(gi * LANES, (gi + 1) * LANES)
        wv = [jnp.broadcast_to(cw_ref[tap:tap + 1, gs], (SEG, LANES)) for tap in range(CONV_K)]

        def conv_rows(i, carry, gs=gs, wv=wv):
            s = i // per_seq
            r0 = (i % per_seq) * rb
            for j in range(rb // SEG):
                acc = jnp.zeros((SEG, LANES), F32)
                for tap in range(CONV_K):
                    a, b = divmod(CONV_PAD - CONV_K // 2 + tap, SEG)
                    if b == 0:
                        row = pl.multiple_of(s * pr + r0 + (j + a) * SEG, SEG)
                        acc = acc + wv[tap] * pad_scr[pl.ds(row, SEG), gs]
                    else:
                        row = pl.multiple_of(s * qr + r0 + (j + a) * SEG, SEG)
                        acc = acc + wv[tap] * shf_scr[b - 1, pl.ds(row, SEG), gs]
                cv_scr[pl.ds(pl.multiple_of(s * seq_len + r0 + j * SEG, SEG), SEG), gs] = acc
            return carry

        lax.fori_loop(0, TM // rb, conv_rows, 0)
    cv = cv_scr[...] + cb_ref[...]
    mu = jnp.mean(cv, axis=-1, keepdims=True)
    var = jnp.mean(jnp.square(cv - mu), axis=-1, keepdims=True)
    cv = (cv - mu) * lax.rsqrt(var + EPS) * lng_ref[...] + lnb_ref[...]
    conv_out = (cv * jax.nn.sigmoid(cv)).astype(BF16)

    o = of_ref[...] + ob_ref[...]
    g = g_ref[...].astype(F32)
    heads = []
    for h in range(N_GLA_HEADS):
        cs = slice(h * DV_HEAD, (h + 1) * DV_HEAD)
        oh = o[:, cs]
        oh = oh * lax.rsqrt(jnp.mean(oh * oh, axis=-1, keepdims=True) + EPS) * gng_ref[...]
        gh = g[:, cs]
        heads.append((oh * (gh * jax.nn.sigmoid(gh))).astype(BF16))
    y = jnp.dot(conv_out, wout_ref[0:C_CONV, :], preferred_element_type=F32)
    for h in range(N_GLA_HEADS):
        y = y + jnp.dot(heads[h], wout_ref[C_CONV + h * DV_HEAD:C_CONV + (h + 1) * DV_HEAD, :],
                        preferred_element_type=F32)

    gate1 = mod_ref[0, :, 2 * D_MODEL:3 * D_MODEL]
    shift2 = mod_ref[0, :, 3 * D_MODEL:4 * D_MODEL]
    scale2 = mod_ref[0, :, 4 * D_MODEL:5 * D_MODEL]
    x1 = x_ref[...] + gate1 * y
    x1_ref[...] = x1
    h2 = _rms(x1, n2g_ref[...]) * (1.0 + scale2) + shift2
    h2_hi, h2_lo = _split(h2)
    h2_ref[...] = h2_hi

    dot = functools.partial(jnp.dot, preferred_element_type=F32)
    logits = (dot(h2_hi, rwh_ref[...]) + dot(h2_lo, rwh_ref[...]) + dot(h2_hi, rwl_ref[...])
              + rb_ref[...])
    lane = lax.broadcasted_iota(jnp.int32, (TM, LANES), 1)
    lanef = lane.astype(F32)
    l = logits
    vals, idxs, hots = [], [], []
    for _ in range(TOP_K):
        m = jnp.max(l, axis=-1, keepdims=True)
        idx = jnp.min(jnp.where(l == m, lanef, float(LANES)), axis=-1, keepdims=True)
        hot = lanef == idx
        l = jnp.where(hot, -jnp.inf, l)
        vals.append(m)
        idxs.append(idx)
        hots.append(hot)
    es = [jnp.exp(vk - vals[0]) for vk in vals]
    inv = 1.0 / (es[0] + es[1] + es[2] + es[3])
    sel = jnp.zeros((TM, LANES), F32)
    for hot in hots:
        sel = jnp.where(hot, 1.0, sel)

    cnt = jnp.sum(sel, axis=0, keepdims=True)
    strip = jnp.ceil(cnt * (1.0 / SEG)) * float(SEG)
    starts = jnp.dot(jnp.broadcast_to(strip, (8, LANES)), ut_ref[...], precision=HI,
                     preferred_element_type=F32)[0:1]
    slot_of = jnp.dot(lt_ref[...], sel.astype(BF16), preferred_element_type=F32) + starts
    route = jnp.zeros((TM, LANES), F32)
    gates = jnp.zeros((TM, LANES), F32)
    for kk in range(TOP_K):
        slot = jnp.sum(jnp.where(hots[kk], slot_of, 0.0), axis=-1, keepdims=True)
        route = jnp.where(lane == kk, slot, route)
        route = jnp.where(lane == TOP_K + kk, idxs[kk], route)
        gates = jnp.where(lane == kk, es[kk] * inv, gates)
    route_ref[...] = route.astype(jnp.int32)
    gates_ref[...] = gates
    cnt_ref[0] = jnp.broadcast_to(cnt, (8, LANES))


def _mix(x, u, g, o_f, o_b, mod3, seg_fn, tile_off, seq_len, wts):
    (conv_w, conv_b, ln_g, ln_b, gng, wout, n2g, rwh, rwl, rb) = wts
    t_path = x.shape[0]
    n = t_path // TM
    lt = jnp.asarray(np.tril(np.ones((TM, TM), np.float32), -1)).astype(BF16)
    ut = jnp.asarray(np.triu(np.ones((LANES, LANES), np.float32), 1))
    const = lambda a: pl.BlockSpec(a.shape, lambda i: (0,) * a.ndim)
    loc = lambda w: pl.BlockSpec((TM, w), lambda i: (i, 0))
    uni = lambda w: pl.BlockSpec((TM, w), lambda i: (i + tile_off, 0))
    n_seq = TM // seq_len
    return pl.pallas_call(
        functools.partial(_mix_kernel, seq_len=seq_len),
        grid=(n,),
        in_specs=[loc(D_MODEL), uni(2 * C_CONV), uni(DV_TOT), loc(DV_TOT), loc(DV_TOT),
                  pl.BlockSpec((1, 1, 6 * D_MODEL), lambda i: (seg_fn(i), 0, 0)),
                  const(conv_w), const(conv_b), const(ln_g), const(ln_b), const(gng), const(wout),
                  const(n2g), const(rwh), const(rwl), const(rb), const(lt), const(ut)],
        out_specs=[loc(D_MODEL), loc(D_MODEL), loc(LANES), loc(LANES),
                   pl.BlockSpec((1, 8, LANES), lambda i: (i, 0, 0))],
        out_shape=[jax.ShapeDtypeStruct((t_path, D_MODEL), F32),
                   jax.ShapeDtypeStruct((t_path, D_MODEL), BF16),
                   jax.ShapeDtypeStruct((t_path, LANES), jnp.int32),
                   jax.ShapeDtypeStruct((t_path, LANES), F32),
                   jax.ShapeDtypeStruct((n, 8, LANES), F32)],
        scratch_shapes=[pltpu.VMEM((n_seq * (seq_len + 2 * CONV_PAD), C_CONV), F32),
                        pltpu.VMEM((SEG - 1, n_seq * (seq_len + 3 * SEG), C_CONV), F32),
                        pltpu.VMEM((TM, C_CONV), F32)],
        compiler_params=_cparams(("arbitrary",), VMEM_LIMIT),
        name="mix",
    )(x, u, g, o_f, o_b, mod3, conv_w, conv_b, ln_g, ln_b, gng, wout, n2g, rwh, rwl, rb, lt, ut)


def _strip_copies(tile, ls_ref, gs_ref, sg_ref, local, rows_hbm, sem, to_hbm, wait):
    def body(e, carry):
        j = tile * N_EXPERTS + e
        n = pl.multiple_of(sg_ref[j], SEG)

        @pl.when(n > 0)
        def _():
            loc = local.at[pl.ds(pl.multiple_of(ls_ref[j], SEG), n)]
            glob = rows_hbm.at[pl.ds(pl.multiple_of(gs_ref[j], SEG), n)]
            cp = pltpu.make_async_copy(loc, glob, sem) if to_hbm else pltpu.make_async_copy(glob, loc, sem)
            if wait:
                cp.wait()
            else:
                cp.start()
        return carry

    lax.fori_loop(0, N_EXPERTS, body, 0)


def _dispatch_kernel(ls_ref, gs_ref, sg_ref, ts_ref, tl_ref, nv_ref, h2c_ref, h2l_ref, rc_ref, rl_ref,
                     xs_ref, cbuf, zbuf, sem, zsem, *, n_ctx_tiles, n_tiles, n_row_tiles):
    b = pl.program_id(0)
    slot = b % 2

    def zero_fill(wait):
        def tail(e, carry):
            n = pl.multiple_of(tl_ref[e], SEG)

            @pl.when(n > 0)
            def _():
                cp = pltpu.make_async_copy(zbuf.at[pl.ds(0, n)],
                                           xs_ref.at[pl.ds(pl.multiple_of(ts_ref[e], SEG), n)], zsem)
                cp.wait() if wait else cp.start()
            return carry

        def unused(t, carry):
            cp = pltpu.make_async_copy(zbuf, xs_ref.at[pl.ds(pl.multiple_of(t * TE, TE), TE)], zsem)
            cp.wait() if wait else cp.start()
            return carry

        lax.fori_loop(0, N_EXPERTS, tail, 0)
        lax.fori_loop(nv_ref[0], n_row_tiles, unused, 0)

    @pl.when(b == 0)
    def _():
        zbuf[...] = jnp.zeros(zbuf.shape, F32)
        zero_fill(False)

    @pl.when(b >= 2)
    def _():
        _strip_copies(b - 2, ls_ref, gs_ref, sg_ref, cbuf.at[slot], xs_ref, sem.at[slot], True, True)

    first = b < n_ctx_tiles
    h2 = jnp.where(first, h2c_ref[...], h2l_ref[...])
    route = jnp.where(first, rc_ref[...], rl_ref[...])
    r = lax.broadcasted_iota(jnp.int32, (TM, NR), 1)
    perm = jnp.zeros((TM, NR), F32)
    for kk in range(TOP_K):
        perm = jnp.where(route[:, kk:kk + 1] == r, 1.0, perm)
    cbuf[slot] = lax.dot_general(perm.astype(BF16), h2, TN, preferred_element_type=F32)
    _strip_copies(b, ls_ref, gs_ref, sg_ref, cbuf.at[slot], xs_ref, sem.at[slot], True, False)

    @pl.when(b == n_tiles - 1)
    def _():
        if n_tiles > 1:
            _strip_copies(b - 1, ls_ref, gs_ref, sg_ref, cbuf.at[1 - slot], xs_ref, sem.at[1 - slot],
                          True, True)
        _strip_copies(b, ls_ref, gs_ref, sg_ref, cbuf.at[slot], xs_ref, sem.at[slot], True, True)
        zero_fill(True)


def _dispatch(plan, h2_c, h2_l, route_c, route_l, n_rows):
    n_c, n_l = h2_c.shape[0] // TM, h2_l.shape[0] // TM
    n_tiles = n_c + n_l
    lo = lambda w: pl.BlockSpec((TM, w), lambda i, *_: (jnp.minimum(i, n_c - 1), 0))
    hi = lambda w: pl.BlockSpec((TM, w), lambda i, *_: (jnp.maximum(i - n_c, 0), 0))
    return pl.pallas_call(
        functools.partial(_dispatch_kernel, n_ctx_tiles=n_c, n_tiles=n_tiles, n_row_tiles=n_rows // TE),
        grid_spec=pltpu.PrefetchScalarGridSpec(
            num_scalar_prefetch=6,
            grid=(n_tiles,),
            in_specs=[lo(D_MODEL), hi(D_MODEL), lo(LANES), hi(LANES)],
            out_specs=pl.BlockSpec(memory_space=pl.ANY),
            scratch_shapes=[pltpu.VMEM((2, NR, D_MODEL), F32), pltpu.VMEM((TE, D_MODEL), F32),
                            pltpu.SemaphoreType.DMA((2,)), pltpu.SemaphoreType.DMA]),
        out_shape=jax.ShapeDtypeStruct((n_rows, D_MODEL), F32),
        compiler_params=_cparams(("arbitrary",), VMEM_LIMIT),
        name="dispatch",
    )(plan["lstart"], plan["gstart"], plan["strip"], plan["tail_start"], plan["tail_len"],
      plan["n_valid"], h2_c, h2_l, route_c, route_l)


def _expert_kernel(te_ref, nv_ref, first_ref, next_ref, slot_ref, x_ref, wgu_hbm, bgu_ref, wdn_hbm, bdn_ref,
                   y_ref, wgu_f, wdn_f, wgu_s, wdn_s, sem):
    i = pl.program_id(0)

    def weight_copies(e, s):
        return (pltpu.make_async_copy(wgu_hbm.at[e], wgu_f.at[s], sem.at[0, s]),
                pltpu.make_async_copy(wdn_hbm.at[e], wdn_f.at[s], sem.at[1, s]))

    @pl.when(i < nv_ref[0])
    def _():
        e = te_ref[i]
        s = slot_ref[i]

        @pl.when(first_ref[i] == 1)
        def _():
            @pl.when(i == 0)
            def _():
                for cp in weight_copies(e, s):
                    cp.start()

            for cp in weight_copies(e, s):
                cp.wait()
            nxt = next_ref[i]

            @pl.when(nxt >= 0)
            def _():
                for cp in weight_copies(nxt, 1 - s):
                    cp.start()

            rows = 64

            def cast(r, carry):
                sl = pl.ds(pl.multiple_of(r * rows, rows), rows)
                wgu_s[sl, :] = wgu_f[s, sl, :].astype(BF16)
                wdn_s[sl, :] = wdn_f[s, sl, :].astype(BF16)
                return carry

            lax.fori_loop(0, D_MODEL // rows, cast, 0)

        gu = jnp.dot(x_ref[...].astype(BF16), wgu_s[...], preferred_element_type=F32) + bgu_ref[0]
        gate = jnp.minimum(gu[:, 0:D_FF], SWIGLU_LIMIT)
        up = jnp.clip(gu[:, D_FF:2 * D_FF], -SWIGLU_LIMIT, SWIGLU_LIMIT)
        act = (gate * jax.nn.sigmoid(SWIGLU_ALPHA * gate) * (up + 1.0)).astype(BF16)
        y_ref[...] = jnp.dot(act, wdn_s[...], preferred_element_type=F32) + bdn_ref[0]

    @pl.when(i >= nv_ref[0])
    def _():
        y_ref[...] = jnp.zeros(y_ref.shape, y_ref.dtype)


def _experts(plan, xs, w_gu, b_gu, w_dn, b_dn):
    n_rows = xs.shape[0]
    nt = n_rows // TE
    row = lambda i, te, nv, *_: (jnp.maximum(jnp.minimum(i, nv[0] - 1), 0), 0)
    exp3 = lambda i, te, *_: (te[i], 0, 0)
    any_spec = pl.BlockSpec(memory_space=pl.ANY)
    return pl.pallas_call(
        _expert_kernel,
        grid_spec=pltpu.PrefetchScalarGridSpec(
            num_scalar_prefetch=5,
            grid=(nt,),
            in_specs=[pl.BlockSpec((TE, D_MODEL), row), any_spec,
                      pl.BlockSpec((1, 1, 2 * D_FF), exp3), any_spec,
                      pl.BlockSpec((1, 1, D_MODEL), exp3)],
            out_specs=pl.BlockSpec((TE, D_MODEL), lambda i, *_: (i, 0)),
            scratch_shapes=[pltpu.VMEM((2, D_MODEL, 2 * D_FF), F32), pltpu.VMEM((2, D_FF, D_MODEL), F32),
                            pltpu.VMEM((D_MODEL, 2 * D_FF), BF16), pltpu.VMEM((D_FF, D_MODEL), BF16),
                            pltpu.SemaphoreType.DMA((2, 2))]),
        out_shape=jax.ShapeDtypeStruct((n_rows, D_MODEL), F32),
        compiler_params=_cparams(("arbitrary",), VMEM_LIMIT),
        name="experts",
    )(plan["tile_expert"], plan["n_valid"], plan["tile_first"], plan["tile_next"], plan["tile_slot"],
      xs, w_gu, b_gu.reshape(N_EXPERTS, 1, 2 * D_FF), w_dn, b_dn.reshape(N_EXPERTS, 1, D_MODEL))


def _final_kernel(ls_ref, gs_ref, sg_ref, x1_ref, route_ref, gates_ref, mod_ref, fg_ref, ys_ref, y_ref,
                  ybuf, sem, *, tile_off, n_tiles):
    i = pl.program_id(0)
    slot = i % 2

    def fetch(step, sl, wait):
        _strip_copies(step + tile_off, ls_ref, gs_ref, sg_ref, ybuf.at[sl], ys_ref, sem.at[sl], False, wait)

    @pl.when(i == 0)
    def _():
        ybuf[...] = jnp.zeros(ybuf.shape, F32)
        fetch(i, slot, False)

    @pl.when(i + 1 < n_tiles)
    def _():
        fetch(i + 1, 1 - slot, False)

    fetch(i, slot, True)

    route = route_ref[...]
    gates = gates_ref[...]
    r = lax.broadcasted_iota(jnp.int32, (TM, NR), 1)
    comb = jnp.zeros((TM, NR), F32)
    for kk in range(TOP_K):
        comb = jnp.where(route[:, kk:kk + 1] == r, gates[:, kk:kk + 1], comb)
    moe = jnp.dot(comb.astype(BF16), ybuf[slot].astype(BF16), preferred_element_type=F32)
    gate2 = mod_ref[0, :, 5 * D_MODEL:6 * D_MODEL]
    y_ref[...] = _rms(x1_ref[...] + gate2 * moe, fg_ref[...])


def _final(plan, x1, route, gates, mod3, seg_fn, tile_off, fg, ys):
    t_path = x1.shape[0]
    n = t_path // TM
    loc = lambda w: pl.BlockSpec((TM, w), lambda i, *_: (i, 0))
    return pl.pallas_call(
        functools.partial(_final_kernel, tile_off=tile_off, n_tiles=n),
        grid_spec=pltpu.PrefetchScalarGridSpec(
            num_scalar_prefetch=3,
            grid=(n,),
            in_specs=[loc(D_MODEL), loc(LANES), loc(LANES),
                      pl.BlockSpec((1, 1, 6 * D_MODEL), lambda i, *_: (seg_fn(i), 0, 0)),
                      pl.BlockSpec((1, D_MODEL), lambda i, *_: (0, 0)),
                      pl.BlockSpec(memory_space=pl.ANY)],
            out_specs=loc(D_MODEL),
            scratch_shapes=[pltpu.VMEM((2, NR, D_MODEL), F32), pltpu.SemaphoreType.DMA((2,))]),
        out_shape=jax.ShapeDtypeStruct((t_path, D_MODEL), F32),
        compiler_params=_cparams(("arbitrary",), VMEM_LIMIT),
        name="final",
    )(plan["lstart"], plan["gstart"], plan["strip"], x1, route, gates, mod3, fg, ys)


def _plan(tile_counts, n_rows):
    nt = n_rows // TE
    strip = (tile_counts + SEG - 1) // SEG * SEG
    lstart = jnp.cumsum(strip, axis=1) - strip
    rows_e = jnp.sum(strip, axis=0)
    rpad = (rows_e + TE - 1) // TE * TE
    ends = jnp.cumsum(rpad)
    offs = ends - rpad
    gstart = offs[None, :] + jnp.cumsum(strip, axis=0) - strip
    n_valid = (ends[-1] // TE).astype(jnp.int32)
    tile_ids = jnp.minimum(jnp.arange(nt, dtype=jnp.int32), n_valid - 1)
    tile_expert = jnp.sum((ends[None, :] <= (tile_ids * TE)[:, None]).astype(jnp.int32), axis=1)
    tile_expert = jnp.minimum(tile_expert, N_EXPERTS - 1).astype(jnp.int32)
    prev = jnp.concatenate([jnp.full((1,), -1, jnp.int32), tile_expert[:-1]])
    tile_first = (tile_expert != prev).astype(jnp.int32)
    tile_slot = (jnp.cumsum(tile_first) - 1) % 2
    after = jnp.take(ends, tile_expert) // TE
    tile_next = jnp.where(after < n_valid, jnp.take(tile_expert, jnp.minimum(after, nt - 1)), -1)
    i32 = lambda a: a.reshape(-1).astype(jnp.int32)
    return dict(strip=i32(strip), lstart=i32(lstart), gstart=i32(gstart), tail_start=i32(offs + rows_e),
                tail_len=i32(rpad - rows_e), n_valid=n_valid.reshape(1), tile_expert=tile_expert,
                tile_first=i32(tile_first), tile_next=i32(tile_next), tile_slot=i32(tile_slot))


def _layer(x_prompt, x_sample, state, c, c_ctx, ada_w, ada_b, norm1_g, norm2_g, w_in, conv_w, conv_b,
           conv_ln_g, conv_ln_b, gate_w, gate_b, gla_norm_g, w_out, router_w, router_b,
           moe_w_gu, moe_b_gu, moe_w_dn, moe_b_dn, final_g):
    bp, lp, d = x_prompt.shape
    bs, ls, _ = x_sample.shape
    assert lp == TM and ls % TM == 0 and d == D_MODEL
    xp = x_prompt.reshape(bp * lp, d)
    xs = x_sample.reshape(bs * ls, d)
    n_c, n_l = bp, bs * ls // TM
    lat_tiles = ls // TM
    t_all = (n_c + n_l) * TM

    n_cond = 1 + bs
    cond_t = jnp.zeros((d, 8), F32).at[:, 0].set(c_ctx).at[:, 1:n_cond].set(c.T)
    mod3 = _modulation(cond_t, ada_w, ada_b, n_cond).reshape(8, 1, 6 * d)

    row = lambda a: a.reshape(1, -1)
    u, qk, v, g, la = _inproj(xp, xs, mod3, row(norm1_g), w_in, gate_w, gate_b, ls)

    pair = lambda s: s.reshape(s.shape[0], 2, 2, LANES, LANES)
    of_c, ob_c, s_ctx = _gla(qk, v, la, jnp.zeros((bp, 2, 2, LANES, LANES), F32), bp, 1, 0)
    of_l, ob_l, _ = _gla(qk, v, la, pair(state), bs, lat_tiles, n_c)

    cw = jnp.zeros((32, C_CONV), F32).at[:CONV_K].set(conv_w)
    rw = jnp.zeros((d, LANES), F32).at[:, :N_EXPERTS].set(router_w)
    rwh = rw.astype(BF16)
    rwl = (rw - rwh.astype(F32)).astype(BF16)
    rb = jnp.full((1, LANES), NEG, F32).at[0, :N_EXPERTS].set(router_b)
    wts = (cw, row(conv_b), row(conv_ln_g), row(conv_ln_b), row(gla_norm_g), w_out.astype(BF16),
           row(norm2_g), rwh, rwl, rb)
    seg_c = lambda i: 0
    seg_l = lambda i: 1 + i // lat_tiles
    x1_c, h2_c, route_c, gates_c, cnt_c = _mix(xp, u, g, of_c, ob_c, mod3, seg_c, 0, lp, wts)
    x1_l, h2_l, route_l, gates_l, cnt_l = _mix(xs, u, g, of_l, ob_l, mod3, seg_l, n_c, GRID_W, wts)

    tile_counts = jnp.concatenate([cnt_c[:, 0, :N_EXPERTS], cnt_l[:, 0, :N_EXPERTS]]).astype(jnp.int32)
    n_rows = TOP_K * t_all + (SEG - 1) * N_EXPERTS * (n_c + n_l) + N_EXPERTS * TE
    n_rows = (n_rows + TE - 1) // TE * TE
    plan = _plan(tile_counts, n_rows)
    xsrt = _dispatch(plan, h2_c, h2_l, route_c, route_l, n_rows)
    ysrt = _experts(plan, xsrt, moe_w_gu, moe_b_gu, moe_w_dn, moe_b_dn)
    y_c = _final(plan, x1_c, route_c, gates_c, mod3, seg_c, 0, row(final_g), ysrt)
    y_l = _final(plan, x1_l, route_l, gates_l, mod3, seg_l, n_c, row(final_g), ysrt)
    new_state = s_ctx.reshape(bp, 1, 2, N_GLA_HEADS, DK_HEAD, DV_HEAD)
    return y_c.reshape(bp, lp, d), y_l.reshape(bs, ls, d), new_state


def kernel(x_prompt, x_sample, state_gla, c, c_ctx, ada_w, ada_b, norm1_g, norm2_g, w_in, conv_w,
           conv_b, conv_ln_g, conv_ln_b, gate_w, gate_b, gla_norm_g, w_out, router_w, router_b,
           moe_w_gu, moe_b_gu, moe_w_dn, moe_b_dn, final_g):
    assert ada_w.shape[0] == 1, "single-layer step"
    return _layer(x_prompt, x_sample, state_gla[:, 0], c, c_ctx, ada_w[0], ada_b[0], norm1_g[0],
                  norm2_g[0], w_in[0], conv_w[0], conv_b[0], conv_ln_g[0], conv_ln_b[0], gate_w[0],
                  gate_b[0], gla_norm_g[0], w_out[0], router_w[0], router_b[0], moe_w_gu[0],
                  moe_b_gu[0], moe_w_dn[0], moe_b_dn[0], final_g)
```

```python
import functools

import numpy as np
import jax
import jax.numpy as jnp
from jax import lax
from jax.experimental import pallas as pl
from jax.experimental.pallas import tpu as pltpu

D_MODEL = 1024
GRID_W = 64
C_CONV = D_MODEL // 2
CONV_K = 31
N_GLA_HEADS = 4
DV_HEAD = 128
DK_HEAD = 64
DK_TOT = DK_HEAD * N_GLA_HEADS
DV_TOT = DV_HEAD * N_GLA_HEADS
GATE_RANK = 16
GATE_TEMP = 16.0
CHUNK = 64
N_EXPERTS = 32
TOP_K = 4
D_FF = D_MODEL
SWIGLU_LIMIT = 7.0
SWIGLU_ALPHA = 1.702
EPS = 1e-6

LANES = 128
SEG = 8
TM = 256
TE = 512
NR = TOP_K * TM + TM
CONV_PAD = 16
VMEM_LIMIT = 56 * 1024 * 1024

F32 = jnp.float32
BF16 = jnp.bfloat16
HI = lax.Precision.HIGHEST
NEG = -1e30
TN = (((0,), (0,)), ((), ()))
NT = (((1,), (1,)), ((), ()))

assert N_EXPERTS * (SEG - 1) <= NR - TOP_K * TM


def _split(x):
    hi = x.astype(BF16)
    return hi, (x - hi.astype(F32)).astype(BF16)


def _cparams(sem, vmem=None):
    return pltpu.CompilerParams(dimension_semantics=sem, vmem_limit_bytes=vmem)


def _mod_kernel(ct_ref, w_ref, b_ref, o_ref, *, n_cond):
    ct = ct_ref[...]
    s = ct * jax.nn.sigmoid(ct)
    w = w_ref[...]
    rows = [jnp.sum(s[:, r:r + 1] * w, axis=0, keepdims=True) + b_ref[...] for r in range(n_cond)]
    rows.append(jnp.zeros((8 - n_cond, w.shape[1]), F32))
    o_ref[...] = jnp.concatenate(rows, axis=0)


def _modulation(cond_t, ada_w, ada_b, n_cond):
    d, n = ada_w.shape
    nb = 768
    return pl.pallas_call(
        functools.partial(_mod_kernel, n_cond=n_cond),
        grid=(n // nb,),
        in_specs=[pl.BlockSpec((d, 8), lambda i: (0, 0)),
                  pl.BlockSpec((d, nb), lambda i: (0, i)),
                  pl.BlockSpec((1, nb), lambda i: (0, i))],
        out_specs=pl.BlockSpec((8, nb), lambda i: (0, i)),
        out_shape=jax.ShapeDtypeStruct((8, n), F32),
        compiler_params=_cparams(("arbitrary",)),
        name="mod",
    )(cond_t, ada_w, ada_b.reshape(1, n))


def _rms(x, g):
    return x * lax.rsqrt(jnp.mean(x * x, axis=-1, keepdims=True) + EPS) * g


def _inproj_kernel(xp_ref, xs_ref, mod_ref, g1_ref, wu_ref, wqk_ref, wv_ref, wg_ref, wlow_ref,
                   gwh_ref, gwl_ref, gb_ref, u_ref, qk_ref, v_ref, g_ref, la_ref, *, n_ctx_tiles):
    i = pl.program_id(0)
    shift = mod_ref[0, :, 0:D_MODEL]
    scale = mod_ref[0, :, D_MODEL:2 * D_MODEL]
    dot = functools.partial(jnp.dot, preferred_element_type=F32)
    rows = u_ref.shape[0]
    sub = min(rows, TM)
    for r0 in range(0, rows, sub):
        rs = slice(r0, r0 + sub)
        x = jnp.where(i < n_ctx_tiles, xp_ref[rs, :], xs_ref[rs, :])
        h = (_rms(x, g1_ref[...]) * (1.0 + scale) + shift).astype(BF16)
        u_ref[rs, :] = dot(h, wu_ref[...]).astype(BF16)
        qk_ref[rs, :] = dot(h, wqk_ref[...]).astype(BF16)
        v_ref[rs, :] = dot(h, wv_ref[...]).astype(BF16)
        g_ref[rs, :] = dot(h, wg_ref[...]).astype(BF16)
        low = dot(h, wlow_ref[...])
        low_hi, low_lo = _split(low)
        z = (dot(low_hi, gwh_ref[...]) + dot(low_lo, gwh_ref[...]) + dot(low_hi, gwl_ref[...])
             + gb_ref[...])
        la_ref[rs, :] = (jnp.minimum(z, 0.0) - jnp.log1p(jnp.exp(-jnp.abs(z)))) * (1.0 / GATE_TEMP)


def _inproj(xp, xs, mod3, g1, w_in, gate_w, gate_b, lat_len):
    ti = next(t for t in (1024, 512, TM) if xp.shape[0] % t == 0 and lat_len % t == 0)
    lat_tiles_per_seq = lat_len // ti
    n_c, n_l = xp.shape[0] // ti, xs.shape[0] // ti
    t_all = xp.shape[0] + xs.shape[0]
    sp = np.cumsum([0, C_CONV, C_CONV, DK_TOT, DK_TOT, DV_TOT, DV_TOT, 2 * GATE_RANK])
    wb = w_in.astype(BF16)
    wu, wqk, wv, wg, wlow = (wb[:, sp[0]:sp[2]], wb[:, sp[2]:sp[4]], wb[:, sp[4]:sp[5]],
                             wb[:, sp[5]:sp[6]], wb[:, sp[6]:sp[7]])
    gw = jnp.zeros((2 * GATE_RANK, 2 * DK_TOT), F32)
    gw = gw.at[:GATE_RANK, :DK_TOT].set(gate_w[0]).at[GATE_RANK:, DK_TOT:].set(gate_w[1])
    gwh, gwl = _split(gw)
    gb = gate_b.reshape(1, 2 * DK_TOT)
    const = lambda a: pl.BlockSpec(a.shape, lambda i: (0,) * a.ndim)
    row = lambda w: pl.BlockSpec((ti, w), lambda i: (i, 0))
    seg = lambda i: jnp.where(i < n_c, 0, 1 + jnp.maximum(i - n_c, 0) // lat_tiles_per_seq)
    return pl.pallas_call(
        functools.partial(_inproj_kernel, n_ctx_tiles=n_c),
        grid=(n_c + n_l,),
        in_specs=[pl.BlockSpec((ti, D_MODEL), lambda i: (jnp.minimum(i, n_c - 1), 0)),
                  pl.BlockSpec((ti, D_MODEL), lambda i: (jnp.maximum(i - n_c, 0), 0)),
                  pl.BlockSpec((1, 1, 6 * D_MODEL), lambda i: (seg(i), 0, 0)),
                  const(g1), const(wu), const(wqk), const(wv), const(wg), const(wlow),
                  const(gwh), const(gwl), const(gb)],
        out_specs=[row(2 * C_CONV), row(2 * DK_TOT), row(DV_TOT), row(DV_TOT), row(2 * DK_TOT)],
        out_shape=[jax.ShapeDtypeStruct((t_all, 2 * C_CONV), BF16),
                   jax.ShapeDtypeStruct((t_all, 2 * DK_TOT), BF16),
                   jax.ShapeDtypeStruct((t_all, DV_TOT), BF16),
                   jax.ShapeDtypeStruct((t_all, DV_TOT), BF16),
                   jax.ShapeDtypeStruct((t_all, 2 * DK_TOT), F32)],
        compiler_params=_cparams(("arbitrary",), VMEM_LIMIT),
        name="inproj",
    )(xp, xs, mod3, g1, wu, wqk, wv, wg, wlow, gwh, gwl, gb)


def _gla_direction(qk_ref, v_ref, la_ref, tri_ref, blk_ref, o_ref, s_scr, reverse):
    la_hi, la_lo = _split(la_ref[...])
    tri = tri_ref[...]
    bcum = (jnp.dot(tri, la_hi, preferred_element_type=F32)
            + jnp.dot(tri, la_lo, preferred_element_type=F32))
    blk = blk_ref[...]
    bl_cols = (lax.dot_general(la_hi, blk, TN, preferred_element_type=F32)
               + lax.dot_general(la_lo, blk, TN, preferred_element_type=F32))
    q = qk_ref[:, 0:DK_TOT].astype(F32)
    k = qk_ref[:, DK_TOT:2 * DK_TOT].astype(F32)
    lane = lax.broadcasted_iota(jnp.int32, (CHUNK, LANES), 1)
    row2 = lax.broadcasted_iota(jnp.int32, (2 * CHUNK, CHUNK), 0) % CHUNK
    col2 = lax.broadcasted_iota(jnp.int32, (2 * CHUNK, CHUNK), 1)
    keep = (col2 >= row2) if reverse else (col2 <= row2)
    srow = lax.broadcasted_iota(jnp.int32, (LANES, LANES), 0)
    n_chunks = TM // CHUNK
    order = range(n_chunks - 1, -1, -1) if reverse else range(n_chunks)
    state = [s_scr[0], s_scr[1]]
    for c in order:
        r0 = c * CHUNK
        bc = bcum[r0:r0 + CHUNK]
        bl = bc[0:1] if reverse else bc[CHUNK - 1:CHUNK]
        qt = q[r0:r0 + CHUNK] * jnp.exp(bc) * (DK_HEAD ** -0.5)
        kt = k[r0:r0 + CHUNK] * jnp.exp(-bc)
        ke = k[r0:r0 + CHUNK] * jnp.exp(bl - bc)
        for p in range(2):
            cs = slice(p * LANES, (p + 1) * LANES)
            qt_p = qt[:, cs]
            qs = jnp.concatenate([jnp.where(lane < DK_HEAD, qt_p, 0.0),
                                  jnp.where(lane >= DK_HEAD, qt_p, 0.0)], axis=0).astype(BF16)
            att = lax.dot_general(qs, kt[:, cs].astype(BF16), NT, preferred_element_type=F32)
            att = jnp.where(keep, att, 0.0).astype(BF16)
            s_p = state[p]
            o_inter = jnp.dot(qs, s_p.astype(BF16), preferred_element_type=F32)
            ke_p = ke[:, cs].astype(BF16)
            upd = []
            for hh in range(2):
                h = 2 * p + hh
                v_h = v_ref[r0:r0 + CHUNK, h * DV_HEAD:(h + 1) * DV_HEAD]
                o_h = jnp.dot(att[hh * CHUNK:(hh + 1) * CHUNK], v_h, preferred_element_type=F32)
                o_ref[r0:r0 + CHUNK, h * DV_HEAD:(h + 1) * DV_HEAD] = (
                    o_h + o_inter[hh * CHUNK:(hh + 1) * CHUNK])
                upd.append(lax.dot_general(ke_p, v_h, TN, preferred_element_type=F32))
            bl_col = bl_cols[p * LANES:(p + 1) * LANES, c * LANES:(c + 1) * LANES]
            state[p] = jnp.exp(bl_col) * s_p + jnp.where(srow < DK_HEAD, upd[0], upd[1])
    s_scr[0] = state[0]
    s_scr[1] = state[1]


def _gla_kernel(*refs, n_tiles, nb):
    seqs = [refs[6 * r:6 * r + 6] for r in range(nb)]
    tril_ref, triu_ref, blk_ref, s0_ref, of_ref, ob_ref, sout_ref, sf_scr, sb_scr = refs[6 * nb:]
    j = pl.program_id(1)

    @pl.when(j == 0)
    def _():
        sf_scr[...] = s0_ref[:, 0]
        sb_scr[...] = s0_ref[:, 1]

    for r, (qkf_ref, vf_ref, laf_ref, qkb_ref, vb_ref, lab_ref) in enumerate(seqs):
        _gla_direction(qkf_ref, vf_ref, laf_ref, tril_ref, blk_ref, of_ref.at[r], sf_scr.at[r], False)
        _gla_direction(qkb_ref, vb_ref, lab_ref, triu_ref, blk_ref, ob_ref.at[r], sb_scr.at[r], True)

    @pl.when(j == n_tiles - 1)
    def _():
        sout_ref[:, 0] = sf_scr[...]
        sout_ref[:, 1] = sb_scr[...]


def _gla(qk, v, la, s0, n_seq, n_tiles, tile_off):
    nb = 2 if n_seq % 2 == 0 else 1
    seq_len = n_tiles * TM
    blk = np.arange(TM) // CHUNK
    same = blk[:, None] == blk[None, :]
    r = np.arange(TM)
    tril = jnp.asarray((same & (r[None, :] <= r[:, None])).astype(np.float32)).astype(BF16)
    triu = jnp.asarray((same & (r[None, :] >= r[:, None])).astype(np.float32)).astype(BF16)
    chunk_of_lane = np.arange(TM // CHUNK * LANES) // LANES
    blkm = jnp.asarray((blk[:, None] == chunk_of_lane[None, :]).astype(np.float32)).astype(BF16)
    def tok(w, r, backward, cb=0):
        def index(s, j):
            jj = (n_tiles - 1 - j) if backward else j
            return ((s * nb + r) * n_tiles + jj + tile_off, cb)
        return pl.BlockSpec((TM, w), index)

    const = lambda a: pl.BlockSpec(a.shape, lambda s, j: (0,) * a.ndim)
    st = pl.BlockSpec((nb, 2, 2, LANES, LANES), lambda s, j: (s, 0, 0, 0, 0))
    seq_specs, seq_args = [], []
    for r in range(nb):
        seq_specs += [tok(2 * DK_TOT, r, False), tok(DV_TOT, r, False), tok(DK_TOT, r, False, 0),
                      tok(2 * DK_TOT, r, True), tok(DV_TOT, r, True), tok(DK_TOT, r, True, 1)]
        seq_args += [qk, v, la, qk, v, la]
    o_f, o_b, s_out = pl.pallas_call(
        functools.partial(_gla_kernel, n_tiles=n_tiles, nb=nb),
        grid=(n_seq // nb, n_tiles),
        in_specs=seq_specs + [const(tril), const(triu), const(blkm), st],
        out_specs=[pl.BlockSpec((nb, TM, DV_TOT), lambda s, j: (s, j, 0)),
                   pl.BlockSpec((nb, TM, DV_TOT), lambda s, j: (s, n_tiles - 1 - j, 0)), st],
        out_shape=[jax.ShapeDtypeStruct((n_seq, seq_len, DV_TOT), F32),
                   jax.ShapeDtypeStruct((n_seq, seq_len, DV_TOT), F32),
                   jax.ShapeDtypeStruct(s0.shape, F32)],
        scratch_shapes=[pltpu.VMEM((nb, 2, LANES, LANES), F32), pltpu.VMEM((nb, 2, LANES, LANES), F32)],
        compiler_params=_cparams(("arbitrary", "arbitrary"), VMEM_LIMIT),
        name="gla",
    )(*seq_args, tril, triu, blkm, s0)
    return o_f.reshape(n_seq * seq_len, DV_TOT), o_b.reshape(n_seq * seq_len, DV_TOT), s_out


def _mix_kernel(x_ref, u_ref, g_ref, of_ref, ob_ref, mod_ref, cw_ref, cb_ref, lng_ref, lnb_ref,
                gng_ref, wout_ref, n2g_ref, rwh_ref, rwl_ref, rb_ref, lt_ref, ut_ref,
                x1_ref, h2_ref, route_ref, gates_ref, cnt_ref, pad_scr, shf_scr, cv_scr, *, seq_len):
    n_seq = TM // seq_len
    pr = seq_len + 2 * CONV_PAD
    qr = seq_len + 3 * SEG

    u = u_ref[...].astype(F32)
    hg = u[:, 0:C_CONV] * jax.nn.sigmoid(u[:, C_CONV:2 * C_CONV])
    zero = jnp.zeros((CONV_PAD, C_CONV), F32)
    for s in range(n_seq):
        pad_scr[s * pr:s * pr + CONV_PAD, :] = zero
        pad_scr[s * pr + CONV_PAD:s * pr + CONV_PAD + seq_len, :] = hg[s * seq_len:(s + 1) * seq_len]
        pad_scr[s * pr + CONV_PAD + seq_len:(s + 1) * pr, :] = zero
        for b in range(1, SEG):
            shf_scr[b - 1, s * qr:(s + 1) * qr, :] = pad_scr[s * pr + b:s * pr + b + qr, :]
    rb = 8 * SEG
    per_seq = seq_len // rb
    for gi in range(C_CONV // LANES):
        gs = slice(gi * LANES, (gi + 1) * LANES)
        wv = [jnp.broadcast_to(cw_ref[tap:tap + 1, gs], (SEG, LANES)) for tap in range(CONV_K)]

        def conv_rows(i, carry, gs=gs, wv=wv):
            s = lax.shift_right_logical(i, per_seq.bit_length() - 1)
            r0 = (i & (per_seq - 1)) * rb
            nj = rb // SEG
            acc = [None] * nj
            for b in range(SEG):
                taps = [(a, SEG * a + b - (CONV_PAD - CONV_K // 2)) for a in range(CONV_PAD * 2 // SEG)]
                taps = [(a, t) for a, t in taps if 0 <= t < CONV_K]
                loaded = {}
                for a, t in taps:
                    for j in range(nj):
                        if a + j not in loaded:
                            if b == 0:
                                row = pl.multiple_of(s * pr + r0 + (a + j) * SEG, SEG)
                                loaded[a + j] = pad_scr[pl.ds(row, SEG), gs]
                            else:
                                row = pl.multiple_of(s * qr + r0 + (a + j) * SEG, SEG)
                                loaded[a + j] = shf_scr[b - 1, pl.ds(row, SEG), gs]
                        term = wv[t] * loaded[a + j]
                        acc[j] = term if acc[j] is None else acc[j] + term
            for j in range(nj):
                cv_scr[pl.ds(pl.multiple_of(s * seq_len + r0 + j * SEG, SEG), SEG), gs] = acc[j]
            return carry

        lax.fori_loop(0, TM // rb, conv_rows, 0)
    cv = cv_scr[...] + cb_ref[...]
    mu = jnp.mean(cv, axis=-1, keepdims=True)
    var = jnp.mean(jnp.square(cv - mu), axis=-1, keepdims=True)
    cv = (cv - mu) * lax.rsqrt(var + EPS) * lng_ref[...] + lnb_ref[...]
    conv_out = (cv * jax.nn.sigmoid(cv)).astype(BF16)

    o = of_ref[...] + ob_ref[...]
    g = g_ref[...].astype(F32)
    heads = []
    for h in range(N_GLA_HEADS):
        cs = slice(h * DV_HEAD, (h + 1) * DV_HEAD)
        oh = o[:, cs]
        oh = oh * lax.rsqrt(jnp.mean(oh * oh, axis=-1, keepdims=True) + EPS) * gng_ref[...]
        gh = g[:, cs]
        heads.append((oh * (gh * jax.nn.sigmoid(gh))).astype(BF16))
    y = jnp.dot(conv_out, wout_ref[0:C_CONV, :], preferred_element_type=F32)
    for h in range(N_GLA_HEADS):
        y = y + jnp.dot(heads[h], wout_ref[C_CONV + h * DV_HEAD:C_CONV + (h + 1) * DV_HEAD, :],
                        preferred_element_type=F32)

    gate1 = mod_ref[0, :, 2 * D_MODEL:3 * D_MODEL]
    shift2 = mod_ref[0, :, 3 * D_MODEL:4 * D_MODEL]
    scale2 = mod_ref[0, :, 4 * D_MODEL:5 * D_MODEL]
    x1 = x_ref[...] + gate1 * y
    x1_ref[...] = x1
    h2 = _rms(x1, n2g_ref[...]) * (1.0 + scale2) + shift2
    h2_hi, h2_lo = _split(h2)
    h2_ref[...] = h2_hi

    dot = functools.partial(jnp.dot, preferred_element_type=F32)
    logits = (dot(h2_hi, rwh_ref[...]) + dot(h2_lo, rwh_ref[...]) + dot(h2_hi, rwl_ref[...])
              + rb_ref[...])
    lane = lax.broadcasted_iota(jnp.int32, (TM, LANES), 1)
    lanef = lane.astype(F32)
    l = logits
    vals, idxs, hots = [], [], []
    for _ in range(TOP_K):
        m = jnp.max(l, axis=-1, keepdims=True)
        idx = jnp.min(jnp.where(l == m, lanef, float(LANES)), axis=-1, keepdims=True)
        hot = lanef == idx
        l = jnp.where(hot, -jnp.inf, l)
        vals.append(m)
        idxs.append(idx)
        hots.append(hot)
    es = [jnp.exp(vk - vals[0]) for vk in vals]
    inv = 1.0 / (es[0] + es[1] + es[2] + es[3])
    sel = jnp.zeros((TM, LANES), F32)
    for hot in hots:
        sel = jnp.where(hot, 1.0, sel)

    cnt = jnp.sum(sel, axis=0, keepdims=True)
    strip = jnp.ceil(cnt * (1.0 / SEG)) * float(SEG)
    starts = jnp.dot(jnp.broadcast_to(strip, (8, LANES)), ut_ref[...], precision=HI,
                     preferred_element_type=F32)[0:1]
    slot_of = jnp.dot(lt_ref[...], sel.astype(BF16), preferred_element_type=F32) + starts
    route = jnp.zeros((TM, LANES), F32)
    gates = jnp.zeros((TM, LANES), F32)
    for kk in range(TOP_K):
        slot = jnp.sum(jnp.where(hots[kk], slot_of, 0.0), axis=-1, keepdims=True)
        route = jnp.where(lane == kk, slot, route)
        route = jnp.where(lane == TOP_K + kk, idxs[kk], route)
        gates = jnp.where(lane == kk, es[kk] * inv, gates)
    route_ref[...] = route.astype(jnp.int32)
    gates_ref[...] = gates
    cnt_ref[0] = jnp.broadcast_to(cnt, (8, LANES))


def _mix(x, u, g, o_f, o_b, mod3, seg_fn, tile_off, seq_len, wts):
    (conv_w, conv_b, ln_g, ln_b, gng, wout, n2g, rwh, rwl, rb) = wts
    t_path = x.shape[0]
    n = t_path // TM
    lt = jnp.asarray(np.tril(np.ones((TM, TM), np.float32), -1)).astype(BF16)
    ut = jnp.asarray(np.triu(np.ones((LANES, LANES), np.float32), 1))
    const = lambda a: pl.BlockSpec(a.shape, lambda i: (0,) * a.ndim)
    loc = lambda w: pl.BlockSpec((TM, w), lambda i: (i, 0))
    uni = lambda w: pl.BlockSpec((TM, w), lambda i: (i + tile_off, 0))
    n_seq = TM // seq_len
    return pl.pallas_call(
        functools.partial(_mix_kernel, seq_len=seq_len),
        grid=(n,),
        in_specs=[loc(D_MODEL), uni(2 * C_CONV), uni(DV_TOT), loc(DV_TOT), loc(DV_TOT),
                  pl.BlockSpec((1, 1, 6 * D_MODEL), lambda i: (seg_fn(i), 0, 0)),
                  const(conv_w), const(conv_b), const(ln_g), const(ln_b), const(gng), const(wout),
                  const(n2g), const(rwh), const(rwl), const(rb), const(lt), const(ut)],
        out_specs=[loc(D_MODEL), loc(D_MODEL), loc(LANES), loc(LANES),
                   pl.BlockSpec((1, 8, LANES), lambda i: (i, 0, 0))],
        out_shape=[jax.ShapeDtypeStruct((t_path, D_MODEL), F32),
                   jax.ShapeDtypeStruct((t_path, D_MODEL), BF16),
                   jax.ShapeDtypeStruct((t_path, LANES), jnp.int32),
                   jax.ShapeDtypeStruct((t_path, LANES), F32),
                   jax.ShapeDtypeStruct((n, 8, LANES), F32)],
        scratch_shapes=[pltpu.VMEM((n_seq * (seq_len + 2 * CONV_PAD), C_CONV), F32),
                        pltpu.VMEM((SEG - 1, n_seq * (seq_len + 3 * SEG), C_CONV), F32),
                        pltpu.VMEM((TM, C_CONV), F32)],
        compiler_params=_cparams(("arbitrary",), VMEM_LIMIT),
        name="mix",
    )(x, u, g, o_f, o_b, mod3, conv_w, conv_b, ln_g, ln_b, gng, wout, n2g, rwh, rwl, rb, lt, ut)


def _strip_copies(tile, ls_ref, gs_ref, sg_ref, local, rows_hbm, sem, to_hbm, wait):
    def body(e, carry):
        j = tile * N_EXPERTS + e
        n = pl.multiple_of(sg_ref[j], SEG)

        @pl.when(n > 0)
        def _():
            loc = local.at[pl.ds(pl.multiple_of(ls_ref[j], SEG), n)]
            glob = rows_hbm.at[pl.ds(pl.multiple_of(gs_ref[j], SEG), n)]
            cp = pltpu.make_async_copy(loc, glob, sem) if to_hbm else pltpu.make_async_copy(glob, loc, sem)
            if wait:
                cp.wait()
            else:
                cp.start()
        return carry

    lax.fori_loop(0, N_EXPERTS, body, 0)


def _dispatch_kernel(ls_ref, gs_ref, sg_ref, ts_ref, tl_ref, nv_ref, h2c_ref, h2l_ref, rc_ref, rl_ref,
                     xs_ref, cbuf, zbuf, sem, zsem, *, n_ctx_tiles, n_tiles, n_row_tiles):
    b = pl.program_id(0)
    slot = b % 2

    def zero_fill(wait):
        def tail(e, carry):
            n = pl.multiple_of(tl_ref[e], SEG)

            @pl.when(n > 0)
            def _():
                cp = pltpu.make_async_copy(zbuf.at[pl.ds(0, n)],
                                           xs_ref.at[pl.ds(pl.multiple_of(ts_ref[e], SEG), n)], zsem)
                cp.wait() if wait else cp.start()
            return carry

        def unused(t, carry):
            cp = pltpu.make_async_copy(zbuf, xs_ref.at[pl.ds(pl.multiple_of(t * TE, TE), TE)], zsem)
            cp.wait() if wait else cp.start()
            return carry

        lax.fori_loop(0, N_EXPERTS, tail, 0)
        lax.fori_loop(nv_ref[0], n_row_tiles, unused, 0)

    @pl.when(b == 0)
    def _():
        zbuf[...] = jnp.zeros(zbuf.shape, F32)
        zero_fill(False)

    @pl.when(b >= 2)
    def _():
        _strip_copies(b - 2, ls_ref, gs_ref, sg_ref, cbuf.at[slot], xs_ref, sem.at[slot], True, True)

    first = b < n_ctx_tiles
    h2 = jnp.where(first, h2c_ref[...], h2l_ref[...])
    route = jnp.where(first, rc_ref[...], rl_ref[...])
    r = lax.broadcasted_iota(jnp.int32, (TM, NR), 1)
    perm = jnp.zeros((TM, NR), F32)
    for kk in range(TOP_K):
        perm = jnp.where(route[:, kk:kk + 1] == r, 1.0, perm)
    cbuf[slot] = lax.dot_general(perm.astype(BF16), h2, TN, preferred_element_type=F32)
    _strip_copies(b, ls_ref, gs_ref, sg_ref, cbuf.at[slot], xs_ref, sem.at[slot], True, False)

    @pl.when(b == n_tiles - 1)
    def _():
        if n_tiles > 1:
            _strip_copies(b - 1, ls_ref, gs_ref, sg_ref, cbuf.at[1 - slot], xs_ref, sem.at[1 - slot],
                          True, True)
        _strip_copies(b, ls_ref, gs_ref, sg_ref, cbuf.at[slot], xs_ref, sem.at[slot], True, True)
        zero_fill(True)


def _dispatch(plan, h2_c, h2_l, route_c, route_l, n_rows):
    n_c, n_l = h2_c.shape[0] // TM, h2_l.shape[0] // TM
    n_tiles = n_c + n_l
    lo = lambda w: pl.BlockSpec((TM, w), lambda i, *_: (jnp.minimum(i, n_c - 1), 0))
    hi = lambda w: pl.BlockSpec((TM, w), lambda i, *_: (jnp.maximum(i - n_c, 0), 0))
    return pl.pallas_call(
        functools.partial(_dispatch_kernel, n_ctx_tiles=n_c, n_tiles=n_tiles, n_row_tiles=n_rows // TE),
        grid_spec=pltpu.PrefetchScalarGridSpec(
            num_scalar_prefetch=6,
            grid=(n_tiles,),
            in_specs=[lo(D_MODEL), hi(D_MODEL), lo(LANES), hi(LANES)],
            out_specs=pl.BlockSpec(memory_space=pl.ANY),
            scratch_shapes=[pltpu.VMEM((2, NR, D_MODEL), F32), pltpu.VMEM((TE, D_MODEL), F32),
                            pltpu.SemaphoreType.DMA((2,)), pltpu.SemaphoreType.DMA]),
        out_shape=jax.ShapeDtypeStruct((n_rows, D_MODEL), F32),
        compiler_params=_cparams(("arbitrary",), VMEM_LIMIT),
        name="dispatch",
    )(plan["lstart"], plan["gstart"], plan["strip"], plan["tail_start"], plan["tail_len"],
      plan["n_valid"], h2_c, h2_l, route_c, route_l)


def _expert_kernel(te_ref, nv_ref, first_ref, next_ref, slot_ref, x_ref, wgu_hbm, bgu_ref, wdn_hbm, bdn_ref,
                   y_ref, wgu_f, wdn_f, wgu_s, wdn_s, sem):
    i = pl.program_id(0)

    def weight_copies(e, s):
        return (pltpu.make_async_copy(wgu_hbm.at[e], wgu_f.at[s], sem.at[0, s]),
                pltpu.make_async_copy(wdn_hbm.at[e], wdn_f.at[s], sem.at[1, s]))

    @pl.when(i < nv_ref[0])
    def _():
        e = te_ref[i]
        s = slot_ref[i]

        @pl.when(first_ref[i] == 1)
        def _():
            @pl.when(i == 0)
            def _():
                for cp in weight_copies(e, s):
                    cp.start()

            for cp in weight_copies(e, s):
                cp.wait()
            nxt = next_ref[i]

            @pl.when(nxt >= 0)
            def _():
                for cp in weight_copies(nxt, 1 - s):
                    cp.start()

            rows = 64

            def cast(r, carry):
                sl = pl.ds(pl.multiple_of(r * rows, rows), rows)
                wgu_s[sl, :] = wgu_f[s, sl, :].astype(BF16)
                wdn_s[sl, :] = wdn_f[s, sl, :].astype(BF16)
                return carry

            lax.fori_loop(0, D_MODEL // rows, cast, 0)

        gu = jnp.dot(x_ref[...].astype(BF16), wgu_s[...], preferred_element_type=F32) + bgu_ref[0]
        gate = jnp.minimum(gu[:, 0:D_FF], SWIGLU_LIMIT)
        up = jnp.clip(gu[:, D_FF:2 * D_FF], -SWIGLU_LIMIT, SWIGLU_LIMIT)
        act = (gate * jax.nn.sigmoid(SWIGLU_ALPHA * gate) * (up + 1.0)).astype(BF16)
        y_ref[...] = jnp.dot(act, wdn_s[...], preferred_element_type=F32) + bdn_ref[0]

    @pl.when(i >= nv_ref[0])
    def _():
        y_ref[...] = jnp.zeros(y_ref.shape, y_ref.dtype)


def _experts(plan, xs, w_gu, b_gu, w_dn, b_dn):
    n_rows = xs.shape[0]
    nt = n_rows // TE
    row = lambda i, te, nv, *_: (jnp.maximum(jnp.minimum(i, nv[0] - 1), 0), 0)
    exp3 = lambda i, te, *_: (te[i], 0, 0)
    any_spec = pl.BlockSpec(memory_space=pl.ANY)
    return pl.pallas_call(
        _expert_kernel,
        grid_spec=pltpu.PrefetchScalarGridSpec(
            num_scalar_prefetch=5,
            grid=(nt,),
            in_specs=[pl.BlockSpec((TE, D_MODEL), row), any_spec,
                      pl.BlockSpec((1, 1, 2 * D_FF), exp3), any_spec,
                      pl.BlockSpec((1, 1, D_MODEL), exp3)],
            out_specs=pl.BlockSpec((TE, D_MODEL), lambda i, *_: (i, 0)),
            scratch_shapes=[pltpu.VMEM((2, D_MODEL, 2 * D_FF), F32), pltpu.VMEM((2, D_FF, D_MODEL), F32),
                            pltpu.VMEM((D_MODEL, 2 * D_FF), BF16), pltpu.VMEM((D_FF, D_MODEL), BF16),
                            pltpu.SemaphoreType.DMA((2, 2))]),
        out_shape=jax.ShapeDtypeStruct((n_rows, D_MODEL), F32),
        compiler_params=_cparams(("arbitrary",), VMEM_LIMIT),
        name="experts",
    )(plan["tile_expert"], plan["n_valid"], plan["tile_first"], plan["tile_next"], plan["tile_slot"],
      xs, w_gu, b_gu.reshape(N_EXPERTS, 1, 2 * D_FF), w_dn, b_dn.reshape(N_EXPERTS, 1, D_MODEL))


def _final_kernel(ls_ref, gs_ref, sg_ref, x1_ref, route_ref, gates_ref, mod_ref, fg_ref, ys_ref, y_ref,
                  ybuf, sem, *, tile_off, n_tiles):
    i = pl.program_id(0)
    slot = i % 2

    def fetch(step, sl, wait):
        _strip_copies(step + tile_off, ls_ref, gs_ref, sg_ref, ybuf.at[sl], ys_ref, sem.at[sl], False, wait)

    @pl.when(i == 0)
    def _():
        ybuf[...] = jnp.zeros(ybuf.shape, F32)
        fetch(i, slot, False)

    @pl.when(i + 1 < n_tiles)
    def _():
        fetch(i + 1, 1 - slot, False)

    fetch(i, slot, True)

    route = route_ref[...]
    gates = gates_ref[...]
    r = lax.broadcasted_iota(jnp.int32, (TM, NR), 1)
    comb = jnp.zeros((TM, NR), F32)
    for kk in range(TOP_K):
        comb = jnp.where(route[:, kk:kk + 1] == r, gates[:, kk:kk + 1], comb)
    moe = jnp.dot(comb.astype(BF16), ybuf[slot].astype(BF16), preferred_element_type=F32)
    gate2 = mod_ref[0, :, 5 * D_MODEL:6 * D_MODEL]
    y_ref[...] = _rms(x1_ref[...] + gate2 * moe, fg_ref[...])


def _final(plan, x1, route, gates, mod3, seg_fn, tile_off, fg, ys):
    t_path = x1.shape[0]
    n = t_path // TM
    loc = lambda w: pl.BlockSpec((TM, w), lambda i, *_: (i, 0))
    return pl.pallas_call(
        functools.partial(_final_kernel, tile_off=tile_off, n_tiles=n),
        grid_spec=pltpu.PrefetchScalarGridSpec(
            num_scalar_prefetch=3,
            grid=(n,),
            in_specs=[loc(D_MODEL), loc(LANES), loc(LANES),
                      pl.BlockSpec((1, 1, 6 * D_MODEL), lambda i, *_: (seg_fn(i), 0, 0)),
                      pl.BlockSpec((1, D_MODEL), lambda i, *_: (0, 0)),
                      pl.BlockSpec(memory_space=pl.ANY)],
            out_specs=loc(D_MODEL),
            scratch_shapes=[pltpu.VMEM((2, NR, D_MODEL), F32), pltpu.SemaphoreType.DMA((2,))]),
        out_shape=jax.ShapeDtypeStruct((t_path, D_MODEL), F32),
        compiler_params=_cparams(("arbitrary",), VMEM_LIMIT),
        name="final",
    )(plan["lstart"], plan["gstart"], plan["strip"], x1, route, gates, mod3, fg, ys)


def _plan(tile_counts, n_rows):
    nt = n_rows // TE
    strip = (tile_counts + SEG - 1) // SEG * SEG
    lstart = jnp.cumsum(strip, axis=1) - strip
    rows_e = jnp.sum(strip, axis=0)
    rpad = (rows_e + TE - 1) // TE * TE
    ends = jnp.cumsum(rpad)
    offs = ends - rpad
    gstart = offs[None, :] + jnp.cumsum(strip, axis=0) - strip
    n_valid = (ends[-1] // TE).astype(jnp.int32)
    tile_ids = jnp.minimum(jnp.arange(nt, dtype=jnp.int32), n_valid - 1)
    tile_expert = jnp.sum((ends[None, :] <= (tile_ids * TE)[:, None]).astype(jnp.int32), axis=1)
    tile_expert = jnp.minimum(tile_expert, N_EXPERTS - 1).astype(jnp.int32)
    prev = jnp.concatenate([jnp.full((1,), -1, jnp.int32), tile_expert[:-1]])
    tile_first = (tile_expert != prev).astype(jnp.int32)
    tile_slot = (jnp.cumsum(tile_first) - 1) % 2
    after = jnp.take(ends, tile_expert) // TE
    tile_next = jnp.where(after < n_valid, jnp.take(tile_expert, jnp.minimum(after, nt - 1)), -1)
    i32 = lambda a: a.reshape(-1).astype(jnp.int32)
    return dict(strip=i32(strip), lstart=i32(lstart), gstart=i32(gstart), tail_start=i32(offs + rows_e),
                tail_len=i32(rpad - rows_e), n_valid=n_valid.reshape(1), tile_expert=tile_expert,
                tile_first=i32(tile_first), tile_next=i32(tile_next), tile_slot=i32(tile_slot))


def _layer(x_prompt, x_sample, state, c, c_ctx, ada_w, ada_b, norm1_g, norm2_g, w_in, conv_w, conv_b,
           conv_ln_g, conv_ln_b, gate_w, gate_b, gla_norm_g, w_out, router_w, router_b,
           moe_w_gu, moe_b_gu, moe_w_dn, moe_b_dn, final_g):
    bp, lp, d = x_prompt.shape
    bs, ls, _ = x_sample.shape
    assert lp == TM and ls % TM == 0 and d == D_MODEL
    xp = x_prompt.reshape(bp * lp, d)
    xs = x_sample.reshape(bs * ls, d)
    n_c, n_l = bp, bs * ls // TM
    lat_tiles = ls // TM
    t_all = (n_c + n_l) * TM

    n_cond = 1 + bs
    cond_t = jnp.zeros((d, 8), F32).at[:, 0].set(c_ctx).at[:, 1:n_cond].set(c.T)
    mod3 = _modulation(cond_t, ada_w, ada_b, n_cond).reshape(8, 1, 6 * d)

    row = lambda a: a.reshape(1, -1)
    u, qk, v, g, la = _inproj(xp, xs, mod3, row(norm1_g), w_in, gate_w, gate_b, ls)

    pair = lambda s: s.reshape(s.shape[0], 2, 2, LANES, LANES)
    of_c, ob_c, s_ctx = _gla(qk, v, la, jnp.zeros((bp, 2, 2, LANES, LANES), F32), bp, 1, 0)
    of_l, ob_l, _ = _gla(qk, v, la, pair(state), bs, lat_tiles, n_c)

    cw = jnp.zeros((32, C_CONV), F32).at[:CONV_K].set(conv_w)
    rw = jnp.zeros((d, LANES), F32).at[:, :N_EXPERTS].set(router_w)
    rwh = rw.astype(BF16)
    rwl = (rw - rwh.astype(F32)).astype(BF16)
    rb = jnp.full((1, LANES), NEG, F32).at[0, :N_EXPERTS].set(router_b)
    wts = (cw, row(conv_b), row(conv_ln_g), row(conv_ln_b), row(gla_norm_g), w_out.astype(BF16),
           row(norm2_g), rwh, rwl, rb)
    seg_c = lambda i: 0
    seg_l = lambda i: 1 + i // lat_tiles
    x1_c, h2_c, route_c, gates_c, cnt_c = _mix(xp, u, g, of_c, ob_c, mod3, seg_c, 0, lp, wts)
    x1_l, h2_l, route_l, gates_l, cnt_l = _mix(xs, u, g, of_l, ob_l, mod3, seg_l, n_c, GRID_W, wts)

    tile_counts = jnp.concatenate([cnt_c[:, 0, :N_EXPERTS], cnt_l[:, 0, :N_EXPERTS]]).astype(jnp.int32)
    n_rows = TOP_K * t_all + (SEG - 1) * N_EXPERTS * (n_c + n_l) + N_EXPERTS * TE
    n_rows = (n_rows + TE - 1) // TE * TE
    plan = _plan(tile_counts, n_rows)
    xsrt = _dispatch(plan, h2_c, h2_l, route_c, route_l, n_rows)
    ysrt = _experts(plan, xsrt, moe_w_gu, moe_b_gu, moe_w_dn, moe_b_dn)
    y_c = _final(plan, x1_c, route_c, gates_c, mod3, seg_c, 0, row(final_g), ysrt)
    y_l = _final(plan, x1_l, route_l, gates_l, mod3, seg_l, n_c, row(final_g), ysrt)
    new_state = s_ctx.reshape(bp, 1, 2, N_GLA_HEADS, DK_HEAD, DV_HEAD)
    return y_c.reshape(bp, lp, d), y_l.reshape(bs, ls, d), new_state


def kernel(x_prompt, x_sample, state_gla, c, c_ctx, ada_w, ada_b, norm1_g, norm2_g, w_in, conv_w,
           conv_b, conv_ln_g, conv_ln_b, gate_w, gate_b, gla_norm_g, w_out, router_w, router_b,
           moe_w_gu, moe_b_gu, moe_w_dn, moe_b_dn, final_g):
    assert ada_w.shape[0] == 1, "single-layer step"
    return _layer(x_prompt, x_sample, state_gla[:, 0], c, c_ctx, ada_w[0], ada_b[0], norm1_g[0],
                  norm2_g[0], w_in[0], conv_w[0], conv_b[0], conv_ln_g[0], conv_ln_b[0], gate_w[0],
                  gate_b[0], gla_norm_g[0], w_out[0], router_w[0], router_b[0], moe_w_gu[0],
                  moe_b_gu[0], moe_w_dn[0], moe_b_dn[0], final_g)
```

```python
import functools

import numpy as np
import jax
import jax.numpy as jnp
from jax import lax
from jax.experimental import pallas as pl
from jax.experimental.pallas import tpu as pltpu

D_MODEL = 1024
GRID_W = 64
C_CONV = D_MODEL // 2
CONV_K = 31
N_GLA_HEADS = 4
DV_HEAD = 128
DK_HEAD = 64
DK_TOT = DK_HEAD * N_GLA_HEADS
DV_TOT = DV_HEAD * N_GLA_HEADS
GATE_RANK = 16
GATE_TEMP = 16.0
CHUNK = 64
N_EXPERTS = 32
TOP_K = 4
D_FF = D_MODEL
SWIGLU_LIMIT = 7.0
SWIGLU_ALPHA = 1.702
EPS = 1e-6

LANES = 128
SEG = 8
TM = 256
TE = 512
NR = TOP_K * TM + TM
CONV_PAD = 16
VMEM_LIMIT = 56 * 1024 * 1024

F32 = jnp.float32
BF16 = jnp.bfloat16
HI = lax.Precision.HIGHEST
NEG = -1e30
TN = (((0,), (0,)), ((), ()))
NT = (((1,), (1,)), ((), ()))

assert N_EXPERTS * (SEG - 1) <= NR - TOP_K * TM


def _split(x):
    hi = x.astype(BF16)
    return hi, (x - hi.astype(F32)).astype(BF16)


def _cparams(sem, vmem=None):
    return pltpu.CompilerParams(dimension_semantics=sem, vmem_limit_bytes=vmem)


def _mod_kernel(ct_ref, w_ref, b_ref, o_ref, *, n_cond):
    ct = ct_ref[...]
    s = ct * jax.nn.sigmoid(ct)
    w = w_ref[...]
    rows = [jnp.sum(s[:, r:r + 1] * w, axis=0, keepdims=True) + b_ref[...] for r in range(n_cond)]
    rows.append(jnp.zeros((8 - n_cond, w.shape[1]), F32))
    o_ref[...] = jnp.concatenate(rows, axis=0)


def _modulation(cond_t, ada_w, ada_b, n_cond):
    d, n = ada_w.shape
    nb = 768
    return pl.pallas_call(
        functools.partial(_mod_kernel, n_cond=n_cond),
        grid=(n // nb,),
        in_specs=[pl.BlockSpec((d, 8), lambda i: (0, 0)),
                  pl.BlockSpec((d, nb), lambda i: (0, i)),
                  pl.BlockSpec((1, nb), lambda i: (0, i))],
        out_specs=pl.BlockSpec((8, nb), lambda i: (0, i)),
        out_shape=jax.ShapeDtypeStruct((8, n), F32),
        compiler_params=_cparams(("arbitrary",)),
        name="mod",
    )(cond_t, ada_w, ada_b.reshape(1, n))


def _rms(x, g):
    return x * lax.rsqrt(jnp.mean(x * x, axis=-1, keepdims=True) + EPS) * g


def _inproj_kernel(xp_ref, xs_ref, mod_ref, g1_ref, wu_ref, wqk_ref, wv_ref, wg_ref, wlow_ref,
                   gwh_ref, gwl_ref, gb_ref, u_ref, qk_ref, v_ref, g_ref, la_ref, *, n_ctx_tiles):
    i = pl.program_id(0)
    shift = mod_ref[0, :, 0:D_MODEL]
    scale = mod_ref[0, :, D_MODEL:2 * D_MODEL]
    dot = functools.partial(jnp.dot, preferred_element_type=F32)
    rows = u_ref.shape[0]
    sub = min(rows, TM)
    for r0 in range(0, rows, sub):
        rs = slice(r0, r0 + sub)
        x = jnp.where(i < n_ctx_tiles, xp_ref[rs, :], xs_ref[rs, :])
        h = (_rms(x, g1_ref[...]) * (1.0 + scale) + shift).astype(BF16)
        u_ref[rs, :] = dot(h, wu_ref[...]).astype(BF16)
        qk_ref[rs, :] = dot(h, wqk_ref[...]).astype(BF16)
        v_ref[rs, :] = dot(h, wv_ref[...]).astype(BF16)
        g_ref[rs, :] = dot(h, wg_ref[...]).astype(BF16)
        low = dot(h, wlow_ref[...])
        low_hi, low_lo = _split(low)
        z = (dot(low_hi, gwh_ref[...]) + dot(low_lo, gwh_ref[...]) + dot(low_hi, gwl_ref[...])
             + gb_ref[...])
        la_ref[rs, :] = (jnp.minimum(z, 0.0) - jnp.log1p(jnp.exp(-jnp.abs(z)))) * (1.0 / GATE_TEMP)


def _inproj(xp, xs, mod3, g1, w_in, gate_w, gate_b, lat_len):
    ti = next(t for t in (1024, 512, TM) if xp.shape[0] % t == 0 and lat_len % t == 0)
    lat_tiles_per_seq = lat_len // ti
    n_c, n_l = xp.shape[0] // ti, xs.shape[0] // ti
    t_all = xp.shape[0] + xs.shape[0]
    sp = np.cumsum([0, C_CONV, C_CONV, DK_TOT, DK_TOT, DV_TOT, DV_TOT, 2 * GATE_RANK])
    wb = w_in.astype(BF16)
    wu, wqk, wv, wg, wlow = (wb[:, sp[0]:sp[2]], wb[:, sp[2]:sp[4]], wb[:, sp[4]:sp[5]],
                             wb[:, sp[5]:sp[6]], wb[:, sp[6]:sp[7]])
    gw = jnp.zeros((2 * GATE_RANK, 2 * DK_TOT), F32)
    gw = gw.at[:GATE_RANK, :DK_TOT].set(gate_w[0]).at[GATE_RANK:, DK_TOT:].set(gate_w[1])
    gwh, gwl = _split(gw)
    gb = gate_b.reshape(1, 2 * DK_TOT)
    const = lambda a: pl.BlockSpec(a.shape, lambda i: (0,) * a.ndim)
    row = lambda w: pl.BlockSpec((ti, w), lambda i: (i, 0))
    seg = lambda i: jnp.where(i < n_c, 0, 1 + jnp.maximum(i - n_c, 0) // lat_tiles_per_seq)
    return pl.pallas_call(
        functools.partial(_inproj_kernel, n_ctx_tiles=n_c),
        grid=(n_c + n_l,),
        in_specs=[pl.BlockSpec((ti, D_MODEL), lambda i: (jnp.minimum(i, n_c - 1), 0)),
                  pl.BlockSpec((ti, D_MODEL), lambda i: (jnp.maximum(i - n_c, 0), 0)),
                  pl.BlockSpec((1, 1, 6 * D_MODEL), lambda i: (seg(i), 0, 0)),
                  const(g1), const(wu), const(wqk), const(wv), const(wg), const(wlow),
                  const(gwh), const(gwl), const(gb)],
        out_specs=[row(2 * C_CONV), row(2 * DK_TOT), row(DV_TOT), row(DV_TOT), row(2 * DK_TOT)],
        out_shape=[jax.ShapeDtypeStruct((t_all, 2 * C_CONV), BF16),
                   jax.ShapeDtypeStruct((t_all, 2 * DK_TOT), BF16),
                   jax.ShapeDtypeStruct((t_all, DV_TOT), BF16),
                   jax.ShapeDtypeStruct((t_all, DV_TOT), BF16),
                   jax.ShapeDtypeStruct((t_all, 2 * DK_TOT), F32)],
        compiler_params=_cparams(("arbitrary",), VMEM_LIMIT),
        name="inproj",
    )(xp, xs, mod3, g1, wu, wqk, wv, wg, wlow, gwh, gwl, gb)


def _gla_direction(qk_ref, v_ref, la_ref, tri_ref, blk_ref, o_ref, s_scr, reverse):
    la_hi, la_lo = _split(la_ref[...])
    tri = tri_ref[...]
    bcum = (jnp.dot(tri, la_hi, preferred_element_type=F32)
            + jnp.dot(tri, la_lo, preferred_element_type=F32))
    blk = blk_ref[...]
    bl_cols = (lax.dot_general(la_hi, blk, TN, preferred_element_type=F32)
               + lax.dot_general(la_lo, blk, TN, preferred_element_type=F32))
    q = qk_ref[:, 0:DK_TOT].astype(F32)
    k = qk_ref[:, DK_TOT:2 * DK_TOT].astype(F32)
    lane = lax.broadcasted_iota(jnp.int32, (CHUNK, LANES), 1)
    row2 = lax.broadcasted_iota(jnp.int32, (2 * CHUNK, CHUNK), 0) % CHUNK
    col2 = lax.broadcasted_iota(jnp.int32, (2 * CHUNK, CHUNK), 1)
    keep = (col2 >= row2) if reverse else (col2 <= row2)
    srow = lax.broadcasted_iota(jnp.int32, (LANES, LANES), 0)
    n_chunks = TM // CHUNK
    order = range(n_chunks - 1, -1, -1) if reverse else range(n_chunks)
    state = [s_scr[0], s_scr[1]]
    for c in order:
        r0 = c * CHUNK
        bc = bcum[r0:r0 + CHUNK]
        bl = bc[0:1] if reverse else bc[CHUNK - 1:CHUNK]
        qt = q[r0:r0 + CHUNK] * jnp.exp(bc) * (DK_HEAD ** -0.5)
        kt = k[r0:r0 + CHUNK] * jnp.exp(-bc)
        ke = k[r0:r0 + CHUNK] * jnp.exp(bl - bc)
        for p in range(2):
            cs = slice(p * LANES, (p + 1) * LANES)
            qt_p = qt[:, cs]
            qs = jnp.concatenate([jnp.where(lane < DK_HEAD, qt_p, 0.0),
                                  jnp.where(lane >= DK_HEAD, qt_p, 0.0)], axis=0).astype(BF16)
            att = lax.dot_general(qs, kt[:, cs].astype(BF16), NT, preferred_element_type=F32)
            att = jnp.where(keep, att, 0.0).astype(BF16)
            s_p = state[p]
            o_inter = jnp.dot(qs, s_p.astype(BF16), preferred_element_type=F32)
            ke_p = ke[:, cs].astype(BF16)
            upd = []
            for hh in range(2):
                h = 2 * p + hh
                v_h = v_ref[r0:r0 + CHUNK, h * DV_HEAD:(h + 1) * DV_HEAD]
                o_h = jnp.dot(att[hh * CHUNK:(hh + 1) * CHUNK], v_h, preferred_element_type=F32)
                o_ref[r0:r0 + CHUNK, h * DV_HEAD:(h + 1) * DV_HEAD] = (
                    o_h + o_inter[hh * CHUNK:(hh + 1) * CHUNK])
                upd.append(lax.dot_general(ke_p, v_h, TN, preferred_element_type=F32))
            bl_col = bl_cols[p * LANES:(p + 1) * LANES, c * LANES:(c + 1) * LANES]
            state[p] = jnp.exp(bl_col) * s_p + jnp.where(srow < DK_HEAD, upd[0], upd[1])
    s_scr[0] = state[0]
    s_scr[1] = state[1]


def _gla_kernel(*refs, n_tiles, nb):
    seqs = [refs[6 * r:6 * r + 6] for r in range(nb)]
    tril_ref, triu_ref, blk_ref, s0_ref, of_ref, ob_ref, sout_ref, sf_scr, sb_scr = refs[6 * nb:]
    j = pl.program_id(1)

    @pl.when(j == 0)
    def _():
        sf_scr[...] = s0_ref[:, 0]
        sb_scr[...] = s0_ref[:, 1]

    for r, (qkf_ref, vf_ref, laf_ref, qkb_ref, vb_ref, lab_ref) in enumerate(seqs):
        _gla_direction(qkf_ref, vf_ref, laf_ref, tril_ref, blk_ref, of_ref.at[r], sf_scr.at[r], False)
        _gla_direction(qkb_ref, vb_ref, lab_ref, triu_ref, blk_ref, ob_ref.at[r], sb_scr.at[r], True)

    @pl.when(j == n_tiles - 1)
    def _():
        sout_ref[:, 0] = sf_scr[...]
        sout_ref[:, 1] = sb_scr[...]


def _gla(qk, v, la, s0, n_seq, n_tiles, tile_off):
    nb = 2 if n_seq % 2 == 0 else 1
    seq_len = n_tiles * TM
    blk = np.arange(TM) // CHUNK
    same = blk[:, None] == blk[None, :]
    r = np.arange(TM)
    tril = jnp.asarray((same & (r[None, :] <= r[:, None])).astype(np.float32)).astype(BF16)
    triu = jnp.asarray((same & (r[None, :] >= r[:, None])).astype(np.float32)).astype(BF16)
    chunk_of_lane = np.arange(TM // CHUNK * LANES) // LANES
    blkm = jnp.asarray((blk[:, None] == chunk_of_lane[None, :]).astype(np.float32)).astype(BF16)
    def tok(w, r, backward, cb=0):
        def index(s, j):
            jj = (n_tiles - 1 - j) if backward else j
            return ((s * nb + r) * n_tiles + jj + tile_off, cb)
        return pl.BlockSpec((TM, w), index)

    const = lambda a: pl.BlockSpec(a.shape, lambda s, j: (0,) * a.ndim)
    st = pl.BlockSpec((nb, 2, 2, LANES, LANES), lambda s, j: (s, 0, 0, 0, 0))
    seq_specs, seq_args = [], []
    for r in range(nb):
        seq_specs += [tok(2 * DK_TOT, r, False), tok(DV_TOT, r, False), tok(DK_TOT, r, False, 0),
                      tok(2 * DK_TOT, r, True), tok(DV_TOT, r, True), tok(DK_TOT, r, True, 1)]
        seq_args += [qk, v, la, qk, v, la]
    o_f, o_b, s_out = pl.pallas_call(
        functools.partial(_gla_kernel, n_tiles=n_tiles, nb=nb),
        grid=(n_seq // nb, n_tiles),
        in_specs=seq_specs + [const(tril), const(triu), const(blkm), st],
        out_specs=[pl.BlockSpec((nb, TM, DV_TOT), lambda s, j: (s, j, 0)),
                   pl.BlockSpec((nb, TM, DV_TOT), lambda s, j: (s, n_tiles - 1 - j, 0)), st],
        out_shape=[jax.ShapeDtypeStruct((n_seq, seq_len, DV_TOT), F32),
                   jax.ShapeDtypeStruct((n_seq, seq_len, DV_TOT), F32),
                   jax.ShapeDtypeStruct(s0.shape, F32)],
        scratch_shapes=[pltpu.VMEM((nb, 2, LANES, LANES), F32), pltpu.VMEM((nb, 2, LANES, LANES), F32)],
        compiler_params=_cparams(("arbitrary", "arbitrary"), VMEM_LIMIT),
        name="gla",
    )(*seq_args, tril, triu, blkm, s0)
    return o_f.reshape(n_seq * seq_len, DV_TOT), o_b.reshape(n_seq * seq_len, DV_TOT), s_out


def _mix_kernel(x_ref, u_ref, g_ref, of_ref, ob_ref, mod_ref, cw_ref, cb_ref, lng_ref, lnb_ref,
                gng_ref, wout_ref, n2g_ref, rwh_ref, rwl_ref, rb_ref, lt_ref, ut_ref,
                x1_ref, grp_ref, route_ref, gates_ref, cnt_ref, pad_scr, shf_scr, cv_scr, *, seq_len):
    n_seq = TM // seq_len
    pr = seq_len + 2 * CONV_PAD
    qr = seq_len + 3 * SEG

    u = u_ref[...].astype(F32)
    hg = u[:, 0:C_CONV] * jax.nn.sigmoid(u[:, C_CONV:2 * C_CONV])
    zero = jnp.zeros((CONV_PAD, C_CONV), F32)
    for s in range(n_seq):
        pad_scr[s * pr:s * pr + CONV_PAD, :] = zero
        pad_scr[s * pr + CONV_PAD:s * pr + CONV_PAD + seq_len, :] = hg[s * seq_len:(s + 1) * seq_len]
        pad_scr[s * pr + CONV_PAD + seq_len:(s + 1) * pr, :] = zero
        for b in range(1, SEG):
            shf_scr[b - 1, s * qr:(s + 1) * qr, :] = pad_scr[s * pr + b:s * pr + b + qr, :]
    rb = 8 * SEG
    per_seq = seq_len // rb
    for gi in range(C_CONV // LANES):
        gs = slice(gi * LANES, (gi + 1) * LANES)
        wv = [jnp.broadcast_to(cw_ref[tap:tap + 1, gs], (SEG, LANES)) for tap in range(CONV_K)]

        def conv_rows(i, carry, gs=gs, wv=wv):
            s = lax.shift_right_logical(i, per_seq.bit_length() - 1)
            r0 = (i & (per_seq - 1)) * rb
            nj = rb // SEG
            acc = [None] * nj
            for b in range(SEG):
                taps = [(a, SEG * a + b - (CONV_PAD - CONV_K // 2)) for a in range(CONV_PAD * 2 // SEG)]
                taps = [(a, t) for a, t in taps if 0 <= t < CONV_K]
                loaded = {}
                for a, t in taps:
                    for j in range(nj):
                        if a + j not in loaded:
                            if b == 0:
                                row = pl.multiple_of(s * pr + r0 + (a + j) * SEG, SEG)
                                loaded[a + j] = pad_scr[pl.ds(row, SEG), gs]
                            else:
                                row = pl.multiple_of(s * qr + r0 + (a + j) * SEG, SEG)
                                loaded[a + j] = shf_scr[b - 1, pl.ds(row, SEG), gs]
                        term = wv[t] * loaded[a + j]
                        acc[j] = term if acc[j] is None else acc[j] + term
            for j in range(nj):
                cv_scr[pl.ds(pl.multiple_of(s * seq_len + r0 + j * SEG, SEG), SEG), gs] = acc[j]
            return carry

        lax.fori_loop(0, TM // rb, conv_rows, 0)
    cv = cv_scr[...] + cb_ref[...]
    mu = jnp.mean(cv, axis=-1, keepdims=True)
    var = jnp.mean(jnp.square(cv - mu), axis=-1, keepdims=True)
    cv = (cv - mu) * lax.rsqrt(var + EPS) * lng_ref[...] + lnb_ref[...]
    conv_out = (cv * jax.nn.sigmoid(cv)).astype(BF16)

    o = of_ref[...] + ob_ref[...]
    g = g_ref[...].astype(F32)
    heads = []
    for h in range(N_GLA_HEADS):
        cs = slice(h * DV_HEAD, (h + 1) * DV_HEAD)
        oh = o[:, cs]
        oh = oh * lax.rsqrt(jnp.mean(oh * oh, axis=-1, keepdims=True) + EPS) * gng_ref[...]
        gh = g[:, cs]
        heads.append((oh * (gh * jax.nn.sigmoid(gh))).astype(BF16))
    y = jnp.dot(conv_out, wout_ref[0:C_CONV, :], preferred_element_type=F32)
    for h in range(N_GLA_HEADS):
        y = y + jnp.dot(heads[h], wout_ref[C_CONV + h * DV_HEAD:C_CONV + (h + 1) * DV_HEAD, :],
                        preferred_element_type=F32)

    gate1 = mod_ref[0, :, 2 * D_MODEL:3 * D_MODEL]
    shift2 = mod_ref[0, :, 3 * D_MODEL:4 * D_MODEL]
    scale2 = mod_ref[0, :, 4 * D_MODEL:5 * D_MODEL]
    x1 = x_ref[...] + gate1 * y
    x1_ref[...] = x1
    h2 = _rms(x1, n2g_ref[...]) * (1.0 + scale2) + shift2
    h2_hi, h2_lo = _split(h2)

    dot = functools.partial(jnp.dot, preferred_element_type=F32)
    logits = (dot(h2_hi, rwh_ref[...]) + dot(h2_lo, rwh_ref[...]) + dot(h2_hi, rwl_ref[...])
              + rb_ref[...])
    lane = lax.broadcasted_iota(jnp.int32, (TM, LANES), 1)
    lanef = lane.astype(F32)
    l = logits
    vals, idxs, hots = [], [], []
    for _ in range(TOP_K):
        m = jnp.max(l, axis=-1, keepdims=True)
        idx = jnp.min(jnp.where(l == m, lanef, float(LANES)), axis=-1, keepdims=True)
        hot = lanef == idx
        l = jnp.where(hot, -jnp.inf, l)
        vals.append(m)
        idxs.append(idx)
        hots.append(hot)
    es = [jnp.exp(vk - vals[0]) for vk in vals]
    inv = 1.0 / (es[0] + es[1] + es[2] + es[3])
    sel = jnp.zeros((TM, LANES), F32)
    for hot in hots:
        sel = jnp.where(hot, 1.0, sel)

    cnt = jnp.sum(sel, axis=0, keepdims=True)
    strip = jnp.ceil(cnt * (1.0 / SEG)) * float(SEG)
    starts = jnp.dot(jnp.broadcast_to(strip, (8, LANES)), ut_ref[...], precision=HI,
                     preferred_element_type=F32)[0:1]
    slot_of = jnp.dot(lt_ref[...], sel.astype(BF16), preferred_element_type=F32) + starts
    route = jnp.zeros((TM, LANES), F32)
    gates = jnp.zeros((TM, LANES), F32)
    for kk in range(TOP_K):
        slot = jnp.sum(jnp.where(hots[kk], slot_of, 0.0), axis=-1, keepdims=True)
        route = jnp.where(lane == kk, slot, route)
        route = jnp.where(lane == TOP_K + kk, idxs[kk], route)
        gates = jnp.where(lane == kk, es[kk] * inv, gates)
    route_ref[...] = route.astype(jnp.int32)
    gates_ref[...] = gates
    cnt_ref[0] = jnp.broadcast_to(cnt, (8, LANES))
    route_t = jnp.transpose(route)
    r = lax.broadcasted_iota(jnp.int32, (NR, TM), 0).astype(F32)
    perm = jnp.zeros((NR, TM), F32)
    for kk in range(TOP_K):
        perm = jnp.where(route_t[kk:kk + 1, :] == r, 1.0, perm)
    grp_ref[...] = jnp.dot(perm.astype(BF16), h2_hi, preferred_element_type=F32)


def _mix(x, u, g, o_f, o_b, mod3, seg_fn, tile_off, seq_len, wts):
    (conv_w, conv_b, ln_g, ln_b, gng, wout, n2g, rwh, rwl, rb) = wts
    t_path = x.shape[0]
    n = t_path // TM
    lt = jnp.asarray(np.tril(np.ones((TM, TM), np.float32), -1)).astype(BF16)
    ut = jnp.asarray(np.triu(np.ones((LANES, LANES), np.float32), 1))
    const = lambda a: pl.BlockSpec(a.shape, lambda i: (0,) * a.ndim)
    loc = lambda w: pl.BlockSpec((TM, w), lambda i: (i, 0))
    uni = lambda w: pl.BlockSpec((TM, w), lambda i: (i + tile_off, 0))
    n_seq = TM // seq_len
    return pl.pallas_call(
        functools.partial(_mix_kernel, seq_len=seq_len),
        grid=(n,),
        in_specs=[loc(D_MODEL), uni(2 * C_CONV), uni(DV_TOT), loc(DV_TOT), loc(DV_TOT),
                  pl.BlockSpec((1, 1, 6 * D_MODEL), lambda i: (seg_fn(i), 0, 0)),
                  const(conv_w), const(conv_b), const(ln_g), const(ln_b), const(gng), const(wout),
                  const(n2g), const(rwh), const(rwl), const(rb), const(lt), const(ut)],
        out_specs=[loc(D_MODEL), pl.BlockSpec((NR, D_MODEL), lambda i: (i, 0)), loc(LANES), loc(LANES),
                   pl.BlockSpec((1, 8, LANES), lambda i: (i, 0, 0))],
        out_shape=[jax.ShapeDtypeStruct((t_path, D_MODEL), F32),
                   jax.ShapeDtypeStruct((n * NR, D_MODEL), F32),
                   jax.ShapeDtypeStruct((t_path, LANES), jnp.int32),
                   jax.ShapeDtypeStruct((t_path, LANES), F32),
                   jax.ShapeDtypeStruct((n, 8, LANES), F32)],
        scratch_shapes=[pltpu.VMEM((n_seq * (seq_len + 2 * CONV_PAD), C_CONV), F32),
                        pltpu.VMEM((SEG - 1, n_seq * (seq_len + 3 * SEG), C_CONV), F32),
                        pltpu.VMEM((TM, C_CONV), F32)],
        compiler_params=_cparams(("arbitrary",), VMEM_LIMIT),
        name="mix",
    )(x, u, g, o_f, o_b, mod3, conv_w, conv_b, ln_g, ln_b, gng, wout, n2g, rwh, rwl, rb, lt, ut)


def _strip_copies(tile, ls_ref, gs_ref, sg_ref, local, rows_hbm, sem, to_hbm, wait):
    def body(e, carry):
        j = tile * N_EXPERTS + e
        n = pl.multiple_of(sg_ref[j], SEG)

        @pl.when(n > 0)
        def _():
            loc = local.at[pl.ds(pl.multiple_of(ls_ref[j], SEG), n)]
            glob = rows_hbm.at[pl.ds(pl.multiple_of(gs_ref[j], SEG), n)]
            cp = pltpu.make_async_copy(loc, glob, sem) if to_hbm else pltpu.make_async_copy(glob, loc, sem)
            if wait:
                cp.wait()
            else:
                cp.start()
        return carry

    lax.fori_loop(0, N_EXPERTS, body, 0)


def _expert_kernel(te_ref, nv_ref, first_ref, next_ref, slot_ref, ptr_ref, src_ref, dst_ref, len_ref,
                   grpc_hbm, grpl_hbm, wgu_hbm, bgu_ref, wdn_hbm, bdn_ref,
                   y_ref, xbuf, wgu_f, wdn_f, wgu_s, wdn_s, xsem, sem, *, ctx_rows):
    i = pl.program_id(0)

    def weight_copies(e, s):
        return (pltpu.make_async_copy(wgu_hbm.at[e], wgu_f.at[s], sem.at[0, s]),
                pltpu.make_async_copy(wdn_hbm.at[e], wdn_f.at[s], sem.at[1, s]))

    def row_copies(tile, xs, wait):
        def body(j, carry):
            n = pl.multiple_of(len_ref[j], SEG)
            src = pl.multiple_of(src_ref[j], SEG)
            dst = xbuf.at[xs, pl.ds(pl.multiple_of(dst_ref[j], SEG), n)]

            def go(cp):
                cp.wait() if wait else cp.start()

            @pl.when(jnp.logical_and(n > 0, src < ctx_rows))
            def _():
                go(pltpu.make_async_copy(grpc_hbm.at[pl.ds(src, n)], dst, xsem.at[xs]))

            @pl.when(jnp.logical_and(n > 0, src >= ctx_rows))
            def _():
                go(pltpu.make_async_copy(grpl_hbm.at[pl.ds(src - ctx_rows, n)], dst, xsem.at[xs]))
            return carry

        lax.fori_loop(ptr_ref[tile], ptr_ref[tile + 1], body, 0)

    @pl.when(i == 0)
    def _():
        xbuf[...] = jnp.zeros(xbuf.shape, F32)
        row_copies(i, 0, False)

    @pl.when(i + 1 < nv_ref[0])
    def _():
        row_copies(i + 1, (i + 1) % 2, False)

    @pl.when(i < nv_ref[0])
    def _():
        e = te_ref[i]
        s = slot_ref[i]
        row_copies(i, i % 2, True)

        @pl.when(first_ref[i] == 1)
        def _():
            @pl.when(i == 0)
            def _():
                for cp in weight_copies(e, s):
                    cp.start()

            for cp in weight_copies(e, s):
                cp.wait()
            nxt = next_ref[i]

            @pl.when(nxt >= 0)
            def _():
                for cp in weight_copies(nxt, 1 - s):
                    cp.start()

            rows = 64

            def cast(r, carry):
                sl = pl.ds(pl.multiple_of(r * rows, rows), rows)
                wgu_s[sl, :] = wgu_f[s, sl, :].astype(BF16)
                wdn_s[sl, :] = wdn_f[s, sl, :].astype(BF16)
                return carry

            lax.fori_loop(0, D_MODEL // rows, cast, 0)

        gu = jnp.dot(xbuf[i % 2].astype(BF16), wgu_s[...], preferred_element_type=F32) + bgu_ref[0]
        gate = jnp.minimum(gu[:, 0:D_FF], SWIGLU_LIMIT)
        up = jnp.clip(gu[:, D_FF:2 * D_FF], -SWIGLU_LIMIT, SWIGLU_LIMIT)
        act = (gate * jax.nn.sigmoid(SWIGLU_ALPHA * gate) * (up + 1.0)).astype(BF16)
        y_ref[...] = jnp.dot(act, wdn_s[...], preferred_element_type=F32) + bdn_ref[0]

    @pl.when(i >= nv_ref[0])
    def _():
        y_ref[...] = jnp.zeros(y_ref.shape, y_ref.dtype)


def _experts(plan, grp_c, grp_l, n_rows, w_gu, b_gu, w_dn, b_dn):
    nt = n_rows // TE
    exp3 = lambda i, te, *_: (te[i], 0, 0)
    any_spec = pl.BlockSpec(memory_space=pl.ANY)
    return pl.pallas_call(
        functools.partial(_expert_kernel, ctx_rows=grp_c.shape[0]),
        grid_spec=pltpu.PrefetchScalarGridSpec(
            num_scalar_prefetch=9,
            grid=(nt,),
            in_specs=[any_spec, any_spec, any_spec,
                      pl.BlockSpec((1, 1, 2 * D_FF), exp3), any_spec,
                      pl.BlockSpec((1, 1, D_MODEL), exp3)],
            out_specs=pl.BlockSpec((TE, D_MODEL), lambda i, *_: (i, 0)),
            scratch_shapes=[pltpu.VMEM((2, TE, D_MODEL), F32),
                            pltpu.VMEM((2, D_MODEL, 2 * D_FF), F32), pltpu.VMEM((2, D_FF, D_MODEL), F32),
                            pltpu.VMEM((D_MODEL, 2 * D_FF), BF16), pltpu.VMEM((D_FF, D_MODEL), BF16),
                            pltpu.SemaphoreType.DMA((2,)), pltpu.SemaphoreType.DMA((2, 2))]),
        out_shape=jax.ShapeDtypeStruct((n_rows, D_MODEL), F32),
        compiler_params=_cparams(("arbitrary",), VMEM_LIMIT),
        name="experts",
    )(plan["tile_expert"], plan["n_valid"], plan["tile_first"], plan["tile_next"], plan["tile_slot"],
      plan["seg_ptr"], plan["seg_src"], plan["seg_dst"], plan["seg_len"],
      grp_c, grp_l, w_gu, b_gu.reshape(N_EXPERTS, 1, 2 * D_FF), w_dn, b_dn.reshape(N_EXPERTS, 1, D_MODEL))


def _final_kernel(ls_ref, gs_ref, sg_ref, x1_ref, route_ref, gates_ref, mod_ref, fg_ref, ys_ref, y_ref,
                  ybuf, sem, *, tile_off, n_tiles):
    i = pl.program_id(0)
    slot = i % 2

    def fetch(step, sl, wait):
        _strip_copies(step + tile_off, ls_ref, gs_ref, sg_ref, ybuf.at[sl], ys_ref, sem.at[sl], False, wait)

    @pl.when(i == 0)
    def _():
        ybuf[...] = jnp.zeros(ybuf.shape, F32)
        fetch(i, slot, False)

    @pl.when(i + 1 < n_tiles)
    def _():
        fetch(i + 1, 1 - slot, False)

    fetch(i, slot, True)

    route = route_ref[...]
    gates = gates_ref[...]
    r = lax.broadcasted_iota(jnp.int32, (TM, NR), 1)
    comb = jnp.zeros((TM, NR), F32)
    for kk in range(TOP_K):
        comb = jnp.where(route[:, kk:kk + 1] == r, gates[:, kk:kk + 1], comb)
    moe = jnp.dot(comb.astype(BF16), ybuf[slot].astype(BF16), preferred_element_type=F32)
    gate2 = mod_ref[0, :, 5 * D_MODEL:6 * D_MODEL]
    y_ref[...] = _rms(x1_ref[...] + gate2 * moe, fg_ref[...])


def _final(plan, x1, route, gates, mod3, seg_fn, tile_off, fg, ys):
    t_path = x1.shape[0]
    n = t_path // TM
    loc = lambda w: pl.BlockSpec((TM, w), lambda i, *_: (i, 0))
    return pl.pallas_call(
        functools.partial(_final_kernel, tile_off=tile_off, n_tiles=n),
        grid_spec=pltpu.PrefetchScalarGridSpec(
            num_scalar_prefetch=3,
            grid=(n,),
            in_specs=[loc(D_MODEL), loc(LANES), loc(LANES),
                      pl.BlockSpec((1, 1, 6 * D_MODEL), lambda i, *_: (seg_fn(i), 0, 0)),
                      pl.BlockSpec((1, D_MODEL), lambda i, *_: (0, 0)),
                      pl.BlockSpec(memory_space=pl.ANY)],
            out_specs=loc(D_MODEL),
            scratch_shapes=[pltpu.VMEM((2, NR, D_MODEL), F32), pltpu.SemaphoreType.DMA((2,))]),
        out_shape=jax.ShapeDtypeStruct((t_path, D_MODEL), F32),
        compiler_params=_cparams(("arbitrary",), VMEM_LIMIT),
        name="final",
    )(plan["lstart"], plan["gstart"], plan["strip"], x1, route, gates, mod3, fg, ys)


def _plan(tile_counts, n_rows):
    nt = n_rows // TE
    strip = (tile_counts + SEG - 1) // SEG * SEG
    lstart = jnp.cumsum(strip, axis=1) - strip
    rows_e = jnp.sum(strip, axis=0)
    rpad = (rows_e + TE - 1) // TE * TE
    ends = jnp.cumsum(rpad)
    offs = ends - rpad
    gstart = offs[None, :] + jnp.cumsum(strip, axis=0) - strip
    n_valid = (ends[-1] // TE).astype(jnp.int32)
    tile_ids = jnp.minimum(jnp.arange(nt, dtype=jnp.int32), n_valid - 1)
    tile_expert = jnp.sum((ends[None, :] <= (tile_ids * TE)[:, None]).astype(jnp.int32), axis=1)
    tile_expert = jnp.minimum(tile_expert, N_EXPERTS - 1).astype(jnp.int32)
    prev = jnp.concatenate([jnp.full((1,), -1, jnp.int32), tile_expert[:-1]])
    tile_first = (tile_expert != prev).astype(jnp.int32)
    tile_slot = (jnp.cumsum(tile_first) - 1) % 2
    after = jnp.take(ends, tile_expert) // TE
    tile_next = jnp.where(after < n_valid, jnp.take(tile_expert, jnp.minimum(after, nt - 1)), -1)
    n_tok = tile_counts.shape[0]
    g0 = gstart.T.reshape(-1)
    ln = strip.T.reshape(-1)
    src0 = (jnp.arange(n_tok, dtype=jnp.int32)[:, None] * NR + lstart).T.reshape(-1)
    len1 = jnp.minimum(ln, TE - g0 % TE)
    pairs = lambda a, b: jnp.stack([a, b], axis=1).reshape(-1)
    seg_row = pairs(g0, g0 + len1)
    seg_tile = seg_row // TE
    seg_ptr = jnp.sum((seg_tile[None, :] < jnp.arange(nt + 1, dtype=jnp.int32)[:, None]).astype(jnp.int32),
                      axis=1)
    i32 = lambda a: a.reshape(-1).astype(jnp.int32)
    return dict(strip=i32(strip), lstart=i32(lstart), gstart=i32(gstart), n_valid=n_valid.reshape(1),
                tile_expert=tile_expert, tile_first=i32(tile_first), tile_next=i32(tile_next),
                tile_slot=i32(tile_slot), seg_ptr=i32(seg_ptr), seg_src=i32(pairs(src0, src0 + len1)),
                seg_dst=i32(seg_row % TE), seg_len=i32(pairs(len1, ln - len1)))


def _layer(x_prompt, x_sample, state, c, c_ctx, ada_w, ada_b, norm1_g, norm2_g, w_in, conv_w, conv_b,
           conv_ln_g, conv_ln_b, gate_w, gate_b, gla_norm_g, w_out, router_w, router_b,
           moe_w_gu, moe_b_gu, moe_w_dn, moe_b_dn, final_g):
    bp, lp, d = x_prompt.shape
    bs, ls, _ = x_sample.shape
    assert lp == TM and ls % TM == 0 and d == D_MODEL
    xp = x_prompt.reshape(bp * lp, d)
    xs = x_sample.reshape(bs * ls, d)
    n_c, n_l = bp, bs * ls // TM
    lat_tiles = ls // TM
    t_all = (n_c + n_l) * TM

    n_cond = 1 + bs
    cond_t = jnp.zeros((d, 8), F32).at[:, 0].set(c_ctx).at[:, 1:n_cond].set(c.T)
    mod3 = _modulation(cond_t, ada_w, ada_b, n_cond).reshape(8, 1, 6 * d)

    row = lambda a: a.reshape(1, -1)
    u, qk, v, g, la = _inproj(xp, xs, mod3, row(norm1_g), w_in, gate_w, gate_b, ls)

    pair = lambda s: s.reshape(s.shape[0], 2, 2, LANES, LANES)
    of_c, ob_c, s_ctx = _gla(qk, v, la, jnp.zeros((bp, 2, 2, LANES, LANES), F32), bp, 1, 0)
    of_l, ob_l, _ = _gla(qk, v, la, pair(state), bs, lat_tiles, n_c)

    cw = jnp.zeros((32, C_CONV), F32).at[:CONV_K].set(conv_w)
    rw = jnp.zeros((d, LANES), F32).at[:, :N_EXPERTS].set(router_w)
    rwh = rw.astype(BF16)
    rwl = (rw - rwh.astype(F32)).astype(BF16)
    rb = jnp.full((1, LANES), NEG, F32).at[0, :N_EXPERTS].set(router_b)
    wts = (cw, row(conv_b), row(conv_ln_g), row(conv_ln_b), row(gla_norm_g), w_out.astype(BF16),
           row(norm2_g), rwh, rwl, rb)
    seg_c = lambda i: 0
    seg_l = lambda i: 1 + i // lat_tiles
    x1_c, grp_c, route_c, gates_c, cnt_c = _mix(xp, u, g, of_c, ob_c, mod3, seg_c, 0, lp, wts)
    x1_l, grp_l, route_l, gates_l, cnt_l = _mix(xs, u, g, of_l, ob_l, mod3, seg_l, n_c, GRID_W, wts)

    tile_counts = jnp.concatenate([cnt_c[:, 0, :N_EXPERTS], cnt_l[:, 0, :N_EXPERTS]]).astype(jnp.int32)
    n_rows = TOP_K * t_all + (SEG - 1) * N_EXPERTS * (n_c + n_l) + N_EXPERTS * TE
    n_rows = (n_rows + TE - 1) // TE * TE
    plan = _plan(tile_counts, n_rows)
    ysrt = _experts(plan, grp_c, grp_l, n_rows, moe_w_gu, moe_b_gu, moe_w_dn, moe_b_dn)
    y_c = _final(plan, x1_c, route_c, gates_c, mod3, seg_c, 0, row(final_g), ysrt)
    y_l = _final(plan, x1_l, route_l, gates_l, mod3, seg_l, n_c, row(final_g), ysrt)
    new_state = s_ctx.reshape(bp, 1, 2, N_GLA_HEADS, DK_HEAD, DV_HEAD)
    return y_c.reshape(bp, lp, d), y_l.reshape(bs, ls, d), new_state


def kernel(x_prompt, x_sample, state_gla, c, c_ctx, ada_w, ada_b, norm1_g, norm2_g, w_in, conv_w,
           conv_b, conv_ln_g, conv_ln_b, gate_w, gate_b, gla_norm_g, w_out, router_w, router_b,
           moe_w_gu, moe_b_gu, moe_w_dn, moe_b_dn, final_g):
    assert ada_w.shape[0] == 1, "single-layer step"
    return _layer(x_prompt, x_sample, state_gla[:, 0], c, c_ctx, ada_w[0], ada_b[0], norm1_g[0],
                  norm2_g[0], w_in[0], conv_w[0], conv_b[0], conv_ln_g[0], conv_ln_b[0], gate_w[0],
                  gate_b[0], gla_norm_g[0], w_out[0], router_w[0], router_b[0], moe_w_gu[0],
                  moe_b_gu[0], moe_w_dn[0], moe_b_dn[0], final_g)
```

```python
import functools

import numpy as np
import jax
import jax.numpy as jnp
from jax import lax
from jax.experimental import pallas as pl
from jax.experimental.pallas import tpu as pltpu

D_MODEL = 1024
GRID_W = 64
C_CONV = D_MODEL // 2
CONV_K = 31
N_GLA_HEADS = 4
DV_HEAD = 128
DK_HEAD = 64
DK_TOT = DK_HEAD * N_GLA_HEADS
DV_TOT = DV_HEAD * N_GLA_HEADS
GATE_RANK = 16
GATE_TEMP = 16.0
CHUNK = 64
N_EXPERTS = 32
TOP_K = 4
D_FF = D_MODEL
SWIGLU_LIMIT = 7.0
SWIGLU_ALPHA = 1.702
EPS = 1e-6

LANES = 128
SEG = 8
TM = 256
TE = 512
NR = TOP_K * TM + TM
CONV_PAD = 16
VMEM_LIMIT = 56 * 1024 * 1024

F32 = jnp.float32
BF16 = jnp.bfloat16
HI = lax.Precision.HIGHEST
NEG = -1e30
TN = (((0,), (0,)), ((), ()))
NT = (((1,), (1,)), ((), ()))

assert N_EXPERTS * (SEG - 1) <= NR - TOP_K * TM


def _split(x):
    hi = x.astype(BF16)
    return hi, (x - hi.astype(F32)).astype(BF16)


def _cparams(sem, vmem=None):
    return pltpu.CompilerParams(dimension_semantics=sem, vmem_limit_bytes=vmem)


def _mod_kernel(ct_ref, w_ref, b_ref, o_ref, *, n_cond):
    ct = ct_ref[...]
    s = ct * jax.nn.sigmoid(ct)
    w = w_ref[...]
    rows = [jnp.sum(s[:, r:r + 1] * w, axis=0, keepdims=True) + b_ref[...] for r in range(n_cond)]
    rows.append(jnp.zeros((8 - n_cond, w.shape[1]), F32))
    o_ref[...] = jnp.concatenate(rows, axis=0)


def _modulation(cond_t, ada_w, ada_b, n_cond):
    d, n = ada_w.shape
    nb = 768
    return pl.pallas_call(
        functools.partial(_mod_kernel, n_cond=n_cond),
        grid=(n // nb,),
        in_specs=[pl.BlockSpec((d, 8), lambda i: (0, 0)),
                  pl.BlockSpec((d, nb), lambda i: (0, i)),
                  pl.BlockSpec((1, nb), lambda i: (0, i))],
        out_specs=pl.BlockSpec((8, nb), lambda i: (0, i)),
        out_shape=jax.ShapeDtypeStruct((8, n), F32),
        compiler_params=_cparams(("arbitrary",)),
        name="mod",
    )(cond_t, ada_w, ada_b.reshape(1, n))


def _rms(x, g):
    return x * lax.rsqrt(jnp.mean(x * x, axis=-1, keepdims=True) + EPS) * g


def _inproj_kernel(xp_ref, xs_ref, mod_ref, g1_ref, wu_ref, wqk_ref, wv_ref, wg_ref, wlow_ref,
                   gwh_ref, gwl_ref, gb_ref, u_ref, qk_ref, v_ref, g_ref, la_ref, *, n_ctx_tiles):
    i = pl.program_id(0)
    shift = mod_ref[0, :, 0:D_MODEL]
    scale = mod_ref[0, :, D_MODEL:2 * D_MODEL]
    dot = functools.partial(jnp.dot, preferred_element_type=F32)
    rows = u_ref.shape[0]
    sub = min(rows, TM)
    for r0 in range(0, rows, sub):
        rs = slice(r0, r0 + sub)
        x = jnp.where(i < n_ctx_tiles, xp_ref[rs, :], xs_ref[rs, :])
        h = (_rms(x, g1_ref[...]) * (1.0 + scale) + shift).astype(BF16)
        u_ref[rs, :] = dot(h, wu_ref[...]).astype(BF16)
        qk_ref[rs, :] = dot(h, wqk_ref[...]).astype(BF16)
        v_ref[rs, :] = dot(h, wv_ref[...]).astype(BF16)
        g_ref[rs, :] = dot(h, wg_ref[...]).astype(BF16)
        low = dot(h, wlow_ref[...])
        low_hi, low_lo = _split(low)
        z = (dot(low_hi, gwh_ref[...]) + dot(low_lo, gwh_ref[...]) + dot(low_hi, gwl_ref[...])
             + gb_ref[...])
        la_ref[rs, :] = (jnp.minimum(z, 0.0) - jnp.log1p(jnp.exp(-jnp.abs(z)))) * (1.0 / GATE_TEMP)


def _inproj(xp, xs, mod3, g1, w_in, gate_w, gate_b, lat_len):
    ti = next(t for t in (1024, 512, TM) if xp.shape[0] % t == 0 and lat_len % t == 0)
    lat_tiles_per_seq = lat_len // ti
    n_c, n_l = xp.shape[0] // ti, xs.shape[0] // ti
    t_all = xp.shape[0] + xs.shape[0]
    sp = np.cumsum([0, C_CONV, C_CONV, DK_TOT, DK_TOT, DV_TOT, DV_TOT, 2 * GATE_RANK])
    wb = w_in.astype(BF16)
    wu, wqk, wv, wg, wlow = (wb[:, sp[0]:sp[2]], wb[:, sp[2]:sp[4]], wb[:, sp[4]:sp[5]],
                             wb[:, sp[5]:sp[6]], wb[:, sp[6]:sp[7]])
    gw = jnp.zeros((2 * GATE_RANK, 2 * DK_TOT), F32)
    gw = gw.at[:GATE_RANK, :DK_TOT].set(gate_w[0]).at[GATE_RANK:, DK_TOT:].set(gate_w[1])
    gwh, gwl = _split(gw)
    gb = gate_b.reshape(1, 2 * DK_TOT)
    const = lambda a: pl.BlockSpec(a.shape, lambda i: (0,) * a.ndim)
    row = lambda w: pl.BlockSpec((ti, w), lambda i: (i, 0))
    seg = lambda i: jnp.where(i < n_c, 0, 1 + jnp.maximum(i - n_c, 0) // lat_tiles_per_seq)
    return pl.pallas_call(
        functools.partial(_inproj_kernel, n_ctx_tiles=n_c),
        grid=(n_c + n_l,),
        in_specs=[pl.BlockSpec((ti, D_MODEL), lambda i: (jnp.minimum(i, n_c - 1), 0)),
                  pl.BlockSpec((ti, D_MODEL), lambda i: (jnp.maximum(i - n_c, 0), 0)),
                  pl.BlockSpec((1, 1, 6 * D_MODEL), lambda i: (seg(i), 0, 0)),
                  const(g1), const(wu), const(wqk), const(wv), const(wg), const(wlow),
                  const(gwh), const(gwl), const(gb)],
        out_specs=[row(2 * C_CONV), row(2 * DK_TOT), row(DV_TOT), row(DV_TOT), row(2 * DK_TOT)],
        out_shape=[jax.ShapeDtypeStruct((t_all, 2 * C_CONV), BF16),
                   jax.ShapeDtypeStruct((t_all, 2 * DK_TOT), BF16),
                   jax.ShapeDtypeStruct((t_all, DV_TOT), BF16),
                   jax.ShapeDtypeStruct((t_all, DV_TOT), BF16),
                   jax.ShapeDtypeStruct((t_all, 2 * DK_TOT), F32)],
        compiler_params=_cparams(("arbitrary",), VMEM_LIMIT),
        name="inproj",
    )(xp, xs, mod3, g1, wu, wqk, wv, wg, wlow, gwh, gwl, gb)


def _gla_direction(qk_ref, v_ref, la_ref, tri_ref, blk_ref, o_ref, s_scr, reverse):
    la_hi, la_lo = _split(la_ref[...])
    tri = tri_ref[...]
    bcum = (jnp.dot(tri, la_hi, preferred_element_type=F32)
            + jnp.dot(tri, la_lo, preferred_element_type=F32))
    blk = blk_ref[...]
    bl_cols = (lax.dot_general(la_hi, blk, TN, preferred_element_type=F32)
               + lax.dot_general(la_lo, blk, TN, preferred_element_type=F32))
    q = qk_ref[:, 0:DK_TOT].astype(F32)
    k = qk_ref[:, DK_TOT:2 * DK_TOT].astype(F32)
    lane = lax.broadcasted_iota(jnp.int32, (CHUNK, LANES), 1)
    row2 = lax.broadcasted_iota(jnp.int32, (2 * CHUNK, CHUNK), 0) % CHUNK
    col2 = lax.broadcasted_iota(jnp.int32, (2 * CHUNK, CHUNK), 1)
    keep = (col2 >= row2) if reverse else (col2 <= row2)
    srow = lax.broadcasted_iota(jnp.int32, (LANES, LANES), 0)
    n_chunks = TM // CHUNK
    order = range(n_chunks - 1, -1, -1) if reverse else range(n_chunks)
    state = [s_scr[0], s_scr[1]]
    for c in order:
        r0 = c * CHUNK
        bc = bcum[r0:r0 + CHUNK]
        bl = bc[0:1] if reverse else bc[CHUNK - 1:CHUNK]
        qt = q[r0:r0 + CHUNK] * jnp.exp(bc) * (DK_HEAD ** -0.5)
        kt = k[r0:r0 + CHUNK] * jnp.exp(-bc)
        ke = k[r0:r0 + CHUNK] * jnp.exp(bl - bc)
        for p in range(2):
            cs = slice(p * LANES, (p + 1) * LANES)
            qt_p = qt[:, cs]
            qs = jnp.concatenate([jnp.where(lane < DK_HEAD, qt_p, 0.0),
                                  jnp.where(lane >= DK_HEAD, qt_p, 0.0)], axis=0).astype(BF16)
            att = lax.dot_general(qs, kt[:, cs].astype(BF16), NT, preferred_element_type=F32)
            att = jnp.where(keep, att, 0.0).astype(BF16)
            s_p = state[p]
            o_inter = jnp.dot(qs, s_p.astype(BF16), preferred_element_type=F32)
            ke_p = ke[:, cs].astype(BF16)
            upd = []
            for hh in range(2):
                h = 2 * p + hh
                v_h = v_ref[r0:r0 + CHUNK, h * DV_HEAD:(h + 1) * DV_HEAD]
                o_h = jnp.dot(att[hh * CHUNK:(hh + 1) * CHUNK], v_h, preferred_element_type=F32)
                o_ref[r0:r0 + CHUNK, h * DV_HEAD:(h + 1) * DV_HEAD] = (
                    o_h + o_inter[hh * CHUNK:(hh + 1) * CHUNK])
                upd.append(lax.dot_general(ke_p, v_h, TN, preferred_element_type=F32))
            bl_col = bl_cols[p * LANES:(p + 1) * LANES, c * LANES:(c + 1) * LANES]
            state[p] = jnp.exp(bl_col) * s_p + jnp.where(srow < DK_HEAD, upd[0], upd[1])
    s_scr[0] = state[0]
    s_scr[1] = state[1]


def _gla_kernel(*refs, n_tiles, nb):
    seqs = [refs[6 * r:6 * r + 6] for r in range(nb)]
    tril_ref, triu_ref, blk_ref, s0_ref, of_ref, ob_ref, sout_ref, sf_scr, sb_scr = refs[6 * nb:]
    j = pl.program_id(1)

    @pl.when(j == 0)
    def _():
        sf_scr[...] = s0_ref[:, 0]
        sb_scr[...] = s0_ref[:, 1]

    for r, (qkf_ref, vf_ref, laf_ref, qkb_ref, vb_ref, lab_ref) in enumerate(seqs):
        _gla_direction(qkf_ref, vf_ref, laf_ref, tril_ref, blk_ref, of_ref.at[r], sf_scr.at[r], False)
        _gla_direction(qkb_ref, vb_ref, lab_ref, triu_ref, blk_ref, ob_ref.at[r], sb_scr.at[r], True)

    @pl.when(j == n_tiles - 1)
    def _():
        sout_ref[:, 0] = sf_scr[...]
        sout_ref[:, 1] = sb_scr[...]


def _gla(qk, v, la, s0, n_seq, n_tiles, tile_off):
    nb = 2 if n_seq % 2 == 0 else 1
    seq_len = n_tiles * TM
    blk = np.arange(TM) // CHUNK
    same = blk[:, None] == blk[None, :]
    r = np.arange(TM)
    tril = jnp.asarray((same & (r[None, :] <= r[:, None])).astype(np.float32)).astype(BF16)
    triu = jnp.asarray((same & (r[None, :] >= r[:, None])).astype(np.float32)).astype(BF16)
    chunk_of_lane = np.arange(TM // CHUNK * LANES) // LANES
    blkm = jnp.asarray((blk[:, None] == chunk_of_lane[None, :]).astype(np.float32)).astype(BF16)
    def tok(w, r, backward, cb=0):
        def index(s, j):
            jj = (n_tiles - 1 - j) if backward else j
            return ((s * nb + r) * n_tiles + jj + tile_off, cb)
        return pl.BlockSpec((TM, w), index)

    const = lambda a: pl.BlockSpec(a.shape, lambda s, j: (0,) * a.ndim)
    st = pl.BlockSpec((nb, 2, 2, LANES, LANES), lambda s, j: (s, 0, 0, 0, 0))
    seq_specs, seq_args = [], []
    for r in range(nb):
        seq_specs += [tok(2 * DK_TOT, r, False), tok(DV_TOT, r, False), tok(DK_TOT, r, False, 0),
                      tok(2 * DK_TOT, r, True), tok(DV_TOT, r, True), tok(DK_TOT, r, True, 1)]
        seq_args += [qk, v, la, qk, v, la]
    o_f, o_b, s_out = pl.pallas_call(
        functools.partial(_gla_kernel, n_tiles=n_tiles, nb=nb),
        grid=(n_seq // nb, n_tiles),
        in_specs=seq_specs + [const(tril), const(triu), const(blkm), st],
        out_specs=[pl.BlockSpec((nb, TM, DV_TOT), lambda s, j: (s, j, 0)),
                   pl.BlockSpec((nb, TM, DV_TOT), lambda s, j: (s, n_tiles - 1 - j, 0)), st],
        out_shape=[jax.ShapeDtypeStruct((n_seq, seq_len, DV_TOT), F32),
                   jax.ShapeDtypeStruct((n_seq, seq_len, DV_TOT), F32),
                   jax.ShapeDtypeStruct(s0.shape, F32)],
        scratch_shapes=[pltpu.VMEM((nb, 2, LANES, LANES), F32), pltpu.VMEM((nb, 2, LANES, LANES), F32)],
        compiler_params=_cparams(("arbitrary", "arbitrary"), VMEM_LIMIT),
        name="gla",
    )(*seq_args, tril, triu, blkm, s0)
    return o_f.reshape(n_seq * seq_len, DV_TOT), o_b.reshape(n_seq * seq_len, DV_TOT), s_out


def _mix_kernel(x_ref, u_ref, g_ref, of_ref, ob_ref, mod_ref, cw_ref, cb_ref, lng_ref, lnb_ref,
                gng_ref, wout_ref, n2g_ref, rwh_ref, rwl_ref, rb_ref, lt_ref, ut_ref,
                x1_ref, grp_ref, route_ref, gates_ref, cnt_ref, pad_scr, shf_scr, cv_scr, *, seq_len):
    n_seq = TM // seq_len
    pr = seq_len + 2 * CONV_PAD
    qr = seq_len + 3 * SEG

    u = u_ref[...].astype(F32)
    hg = u[:, 0:C_CONV] * jax.nn.sigmoid(u[:, C_CONV:2 * C_CONV])
    zero = jnp.zeros((CONV_PAD, C_CONV), F32)
    for s in range(n_seq):
        pad_scr[s * pr:s * pr + CONV_PAD, :] = zero
        pad_scr[s * pr + CONV_PAD:s * pr + CONV_PAD + seq_len, :] = hg[s * seq_len:(s + 1) * seq_len]
        pad_scr[s * pr + CONV_PAD + seq_len:(s + 1) * pr, :] = zero
        for b in range(1, SEG):
            shf_scr[b - 1, s * qr:(s + 1) * qr, :] = pad_scr[s * pr + b:s * pr + b + qr, :]
    rb = 8 * SEG
    per_seq = seq_len // rb
    for gi in range(C_CONV // LANES):
        gs = slice(gi * LANES, (gi + 1) * LANES)
        wv = [jnp.broadcast_to(cw_ref[tap:tap + 1, gs], (SEG, LANES)) for tap in range(CONV_K)]

        def conv_rows(i, carry, gs=gs, wv=wv):
            s = lax.shift_right_logical(i, per_seq.bit_length() - 1)
            r0 = (i & (per_seq - 1)) * rb
            nj = rb // SEG
            acc = [None] * nj
            for b in range(SEG):
                taps = [(a, SEG * a + b - (CONV_PAD - CONV_K // 2)) for a in range(CONV_PAD * 2 // SEG)]
                taps = [(a, t) for a, t in taps if 0 <= t < CONV_K]
                loaded = {}
                for a, t in taps:
                    for j in range(nj):
                        if a + j not in loaded:
                            if b == 0:
                                row = pl.multiple_of(s * pr + r0 + (a + j) * SEG, SEG)
                                loaded[a + j] = pad_scr[pl.ds(row, SEG), gs]
                            else:
                                row = pl.multiple_of(s * qr + r0 + (a + j) * SEG, SEG)
                                loaded[a + j] = shf_scr[b - 1, pl.ds(row, SEG), gs]
                        term = wv[t] * loaded[a + j]
                        acc[j] = term if acc[j] is None else acc[j] + term
            for j in range(nj):
                cv_scr[pl.ds(pl.multiple_of(s * seq_len + r0 + j * SEG, SEG), SEG), gs] = acc[j]
            return carry

        lax.fori_loop(0, TM // rb, conv_rows, 0)
    cv = cv_scr[...] + cb_ref[...]
    mu = jnp.mean(cv, axis=-1, keepdims=True)
    var = jnp.mean(jnp.square(cv - mu), axis=-1, keepdims=True)
    cv = (cv - mu) * lax.rsqrt(var + EPS) * lng_ref[...] + lnb_ref[...]
    conv_out = (cv * jax.nn.sigmoid(cv)).astype(BF16)

    o = of_ref[...] + ob_ref[...]
    g = g_ref[...].astype(F32)
    heads = []
    for h in range(N_GLA_HEADS):
        cs = slice(h * DV_HEAD, (h + 1) * DV_HEAD)
        oh = o[:, cs]
        oh = oh * lax.rsqrt(jnp.mean(oh * oh, axis=-1, keepdims=True) + EPS) * gng_ref[...]
        gh = g[:, cs]
        heads.append((oh * (gh * jax.nn.sigmoid(gh))).astype(BF16))
    y = jnp.dot(conv_out, wout_ref[0:C_CONV, :], preferred_element_type=F32)
    for h in range(N_GLA_HEADS):
        y = y + jnp.dot(heads[h], wout_ref[C_CONV + h * DV_HEAD:C_CONV + (h + 1) * DV_HEAD, :],
                        preferred_element_type=F32)

    gate1 = mod_ref[0, :, 2 * D_MODEL:3 * D_MODEL]
    shift2 = mod_ref[0, :, 3 * D_MODEL:4 * D_MODEL]
    scale2 = mod_ref[0, :, 4 * D_MODEL:5 * D_MODEL]
    x1 = x_ref[...] + gate1 * y
    x1_ref[...] = x1
    h2 = _rms(x1, n2g_ref[...]) * (1.0 + scale2) + shift2
    h2_hi, h2_lo = _split(h2)

    dot = functools.partial(jnp.dot, preferred_element_type=F32)
    logits = (dot(h2_hi, rwh_ref[...]) + dot(h2_lo, rwh_ref[...]) + dot(h2_hi, rwl_ref[...])
              + rb_ref[...])
    lane = lax.broadcasted_iota(jnp.int32, (TM, LANES), 1)
    lanef = lane.astype(F32)
    l = logits
    vals, idxs, hots = [], [], []
    for _ in range(TOP_K):
        m = jnp.max(l, axis=-1, keepdims=True)
        idx = jnp.min(jnp.where(l == m, lanef, float(LANES)), axis=-1, keepdims=True)
        hot = lanef == idx
        l = jnp.where(hot, -jnp.inf, l)
        vals.append(m)
        idxs.append(idx)
        hots.append(hot)
    es = [jnp.exp(vk - vals[0]) for vk in vals]
    inv = 1.0 / (es[0] + es[1] + es[2] + es[3])
    sel = jnp.zeros((TM, LANES), F32)
    for hot in hots:
        sel = jnp.where(hot, 1.0, sel)

    cnt = jnp.sum(sel, axis=0, keepdims=True)
    strip = jnp.ceil(cnt * (1.0 / SEG)) * float(SEG)
    starts = jnp.dot(jnp.broadcast_to(strip, (8, LANES)), ut_ref[...], precision=HI,
                     preferred_element_type=F32)[0:1]
    slot_of = jnp.dot(lt_ref[...], sel.astype(BF16), preferred_element_type=F32) + starts
    route = jnp.zeros((TM, LANES), F32)
    gates = jnp.zeros((TM, LANES), F32)
    for kk in range(TOP_K):
        slot = jnp.sum(jnp.where(hots[kk], slot_of, 0.0), axis=-1, keepdims=True)
        route = jnp.where(lane == kk, slot, route)
        route = jnp.where(lane == TOP_K + kk, idxs[kk], route)
        gates = jnp.where(lane == kk, es[kk] * inv, gates)
    route_ref[...] = route.astype(jnp.int32)
    gates_ref[...] = gates
    cnt_ref[0] = jnp.broadcast_to(cnt, (8, LANES))
    route_t = jnp.transpose(route)
    r = lax.broadcasted_iota(jnp.int32, (NR, TM), 0).astype(F32)
    perm = jnp.zeros((NR, TM), F32)
    for kk in range(TOP_K):
        perm = jnp.where(route_t[kk:kk + 1, :] == r, 1.0, perm)
    grp_ref[...] = jnp.dot(perm.astype(BF16), h2_hi, preferred_element_type=F32)


def _mix(x, u, g, o_f, o_b, mod3, seg_fn, tile_off, seq_len, wts):
    (conv_w, conv_b, ln_g, ln_b, gng, wout, n2g, rwh, rwl, rb) = wts
    t_path = x.shape[0]
    n = t_path // TM
    lt = jnp.asarray(np.tril(np.ones((TM, TM), np.float32), -1)).astype(BF16)
    ut = jnp.asarray(np.triu(np.ones((LANES, LANES), np.float32), 1))
    const = lambda a: pl.BlockSpec(a.shape, lambda i: (0,) * a.ndim)
    loc = lambda w: pl.BlockSpec((TM, w), lambda i: (i, 0))
    uni = lambda w: pl.BlockSpec((TM, w), lambda i: (i + tile_off, 0))
    n_seq = TM // seq_len
    return pl.pallas_call(
        functools.partial(_mix_kernel, seq_len=seq_len),
        grid=(n,),
        in_specs=[loc(D_MODEL), uni(2 * C_CONV), uni(DV_TOT), loc(DV_TOT), loc(DV_TOT),
                  pl.BlockSpec((1, 1, 6 * D_MODEL), lambda i: (seg_fn(i), 0, 0)),
                  const(conv_w), const(conv_b), const(ln_g), const(ln_b), const(gng), const(wout),
                  const(n2g), const(rwh), const(rwl), const(rb), const(lt), const(ut)],
        out_specs=[loc(D_MODEL), pl.BlockSpec((NR, D_MODEL), lambda i: (i, 0)), loc(LANES), loc(LANES),
                   pl.BlockSpec((1, 8, LANES), lambda i: (i, 0, 0))],
        out_shape=[jax.ShapeDtypeStruct((t_path, D_MODEL), F32),
                   jax.ShapeDtypeStruct((n * NR, D_MODEL), F32),
                   jax.ShapeDtypeStruct((t_path, LANES), jnp.int32),
                   jax.ShapeDtypeStruct((t_path, LANES), F32),
                   jax.ShapeDtypeStruct((n, 8, LANES), F32)],
        scratch_shapes=[pltpu.VMEM((n_seq * (seq_len + 2 * CONV_PAD), C_CONV), F32),
                        pltpu.VMEM((SEG - 1, n_seq * (seq_len + 3 * SEG), C_CONV), F32),
                        pltpu.VMEM((TM, C_CONV), F32)],
        compiler_params=_cparams(("arbitrary",), VMEM_LIMIT),
        name="mix",
    )(x, u, g, o_f, o_b, mod3, conv_w, conv_b, ln_g, ln_b, gng, wout, n2g, rwh, rwl, rb, lt, ut)


def _strip_copies(tile, ls_ref, gs_ref, sg_ref, local, rows_hbm, sem, to_hbm, wait):
    def body(e, carry):
        j = tile * N_EXPERTS + e
        n = pl.multiple_of(sg_ref[j], SEG)

        @pl.when(n > 0)
        def _():
            loc = local.at[pl.ds(pl.multiple_of(ls_ref[j], SEG), n)]
            glob = rows_hbm.at[pl.ds(pl.multiple_of(gs_ref[j], SEG), n)]
            cp = pltpu.make_async_copy(loc, glob, sem) if to_hbm else pltpu.make_async_copy(glob, loc, sem)
            if wait:
                cp.wait()
            else:
                cp.start()
        return carry

    lax.fori_loop(0, N_EXPERTS, body, 0)


def _expert_kernel(te_ref, nv_ref, first_ref, next_ref, slot_ref, ptr_ref, mid_ref, src_ref, dst_ref, len_ref,
                   csrc_ref, clen_ref, grpc_hbm, grpl_hbm, wgu_hbm, bgu_ref, wdn_hbm, bdn_ref,
                   y_ref, xbuf, wgu_f, wdn_f, wgu_s, wdn_s, xsem, sem, *, ctx_rows):
    i = pl.program_id(0)

    def weight_copies(e, s):
        return (pltpu.make_async_copy(wgu_hbm.at[e], wgu_f.at[s], sem.at[0, s]),
                pltpu.make_async_copy(wdn_hbm.at[e], wdn_f.at[s], sem.at[1, s]))

    def row_copies(tile, xs, wait):
        def strip(hbm, src, dst, n):
            cp = pltpu.make_async_copy(hbm.at[pl.ds(pl.multiple_of(src, SEG), n)],
                                       xbuf.at[xs, pl.ds(pl.multiple_of(dst, SEG), n)], xsem.at[xs])
            cp.wait() if wait else cp.start()

        def strips_of(hbm, base):
            def body(j, carry):
                n = pl.multiple_of(len_ref[j], SEG)

                @pl.when(n > 0)
                def _():
                    strip(hbm, src_ref[j] - base, dst_ref[j], n)
                return carry
            return body

        lax.fori_loop(ptr_ref[tile], mid_ref[tile], strips_of(grpc_hbm, 0), 0)
        lax.fori_loop(mid_ref[tile], ptr_ref[tile + 1], strips_of(grpl_hbm, ctx_rows), 0)
        n = pl.multiple_of(clen_ref[tile], SEG)
        src = csrc_ref[tile]

        @pl.when(jnp.logical_and(n > 0, src < ctx_rows))
        def _():
            strip(grpc_hbm, src, 0, n)

        @pl.when(jnp.logical_and(n > 0, src >= ctx_rows))
        def _():
            strip(grpl_hbm, src - ctx_rows, 0, n)

    @pl.when(i == 0)
    def _():
        xbuf[...] = jnp.zeros(xbuf.shape, F32)
        row_copies(i, 0, False)

    @pl.when(i + 1 < nv_ref[0])
    def _():
        row_copies(i + 1, (i + 1) % 2, False)

    @pl.when(i < nv_ref[0])
    def _():
        e = te_ref[i]
        s = slot_ref[i]
        row_copies(i, i % 2, True)

        @pl.when(first_ref[i] == 1)
        def _():
            @pl.when(i == 0)
            def _():
                for cp in weight_copies(e, s):
                    cp.start()

            for cp in weight_copies(e, s):
                cp.wait()
            nxt = next_ref[i]

            @pl.when(nxt >= 0)
            def _():
                for cp in weight_copies(nxt, 1 - s):
                    cp.start()

            rows = 64

            def cast(r, carry):
                sl = pl.ds(pl.multiple_of(r * rows, rows), rows)
                wgu_s[sl, :] = wgu_f[s, sl, :].astype(BF16)
                wdn_s[sl, :] = wdn_f[s, sl, :].astype(BF16)
                return carry

            lax.fori_loop(0, D_MODEL // rows, cast, 0)

        gu = jnp.dot(xbuf[i % 2].astype(BF16), wgu_s[...], preferred_element_type=F32) + bgu_ref[0]
        gate = jnp.minimum(gu[:, 0:D_FF], SWIGLU_LIMIT)
        up = jnp.clip(gu[:, D_FF:2 * D_FF], -SWIGLU_LIMIT, SWIGLU_LIMIT)
        act = (gate * jax.nn.sigmoid(SWIGLU_ALPHA * gate) * (up + 1.0)).astype(BF16)
        y_ref[...] = jnp.dot(act, wdn_s[...], preferred_element_type=F32) + bdn_ref[0]

    @pl.when(i >= nv_ref[0])
    def _():
        y_ref[...] = jnp.zeros(y_ref.shape, y_ref.dtype)


def _experts(plan, grp_c, grp_l, n_rows, w_gu, b_gu, w_dn, b_dn):
    nt = n_rows // TE
    exp3 = lambda i, te, *_: (te[i], 0, 0)
    any_spec = pl.BlockSpec(memory_space=pl.ANY)
    return pl.pallas_call(
        functools.partial(_expert_kernel, ctx_rows=grp_c.shape[0]),
        grid_spec=pltpu.PrefetchScalarGridSpec(
            num_scalar_prefetch=12,
            grid=(nt,),
            in_specs=[any_spec, any_spec, any_spec,
                      pl.BlockSpec((1, 1, 2 * D_FF), exp3), any_spec,
                      pl.BlockSpec((1, 1, D_MODEL), exp3)],
            out_specs=pl.BlockSpec((TE, D_MODEL), lambda i, *_: (i, 0)),
            scratch_shapes=[pltpu.VMEM((2, TE, D_MODEL), F32),
                            pltpu.VMEM((2, D_MODEL, 2 * D_FF), F32), pltpu.VMEM((2, D_FF, D_MODEL), F32),
                            pltpu.VMEM((D_MODEL, 2 * D_FF), BF16), pltpu.VMEM((D_FF, D_MODEL), BF16),
                            pltpu.SemaphoreType.DMA((2,)), pltpu.SemaphoreType.DMA((2, 2))]),
        out_shape=jax.ShapeDtypeStruct((n_rows, D_MODEL), F32),
        compiler_params=_cparams(("arbitrary",), VMEM_LIMIT),
        name="experts",
    )(plan["tile_expert"], plan["n_valid"], plan["tile_first"], plan["tile_next"], plan["tile_slot"],
      plan["seg_ptr"], plan["seg_mid"], plan["seg_src"], plan["seg_dst"], plan["seg_len"],
      plan["carry_src"], plan["carry_len"], grp_c, grp_l, w_gu, b_gu.reshape(N_EXPERTS, 1, 2 * D_FF), w_dn, b_dn.reshape(N_EXPERTS, 1, D_MODEL))


def _final_kernel(ls_ref, gs_ref, sg_ref, x1_ref, route_ref, gates_ref, mod_ref, fg_ref, ys_ref, y_ref,
                  ybuf, sem, *, tile_off, n_tiles):
    i = pl.program_id(0)
    slot = i % 2

    def fetch(step, sl, wait):
        _strip_copies(step + tile_off, ls_ref, gs_ref, sg_ref, ybuf.at[sl], ys_ref, sem.at[sl], False, wait)

    @pl.when(i == 0)
    def _():
        ybuf[...] = jnp.zeros(ybuf.shape, F32)
        fetch(i, slot, False)

    @pl.when(i + 1 < n_tiles)
    def _():
        fetch(i + 1, 1 - slot, False)

    fetch(i, slot, True)

    route = route_ref[...]
    gates = gates_ref[...]
    r = lax.broadcasted_iota(jnp.int32, (TM, NR), 1)
    comb = jnp.zeros((TM, NR), F32)
    for kk in range(TOP_K):
        comb = jnp.where(route[:, kk:kk + 1] == r, gates[:, kk:kk + 1], comb)
    moe = jnp.dot(comb.astype(BF16), ybuf[slot].astype(BF16), preferred_element_type=F32)
    gate2 = mod_ref[0, :, 5 * D_MODEL:6 * D_MODEL]
    y_ref[...] = _rms(x1_ref[...] + gate2 * moe, fg_ref[...])


def _final(plan, x1, route, gates, mod3, seg_fn, tile_off, fg, ys):
    t_path = x1.shape[0]
    n = t_path // TM
    loc = lambda w: pl.BlockSpec((TM, w), lambda i, *_: (i, 0))
    return pl.pallas_call(
        functools.partial(_final_kernel, tile_off=tile_off, n_tiles=n),
        grid_spec=pltpu.PrefetchScalarGridSpec(
            num_scalar_prefetch=3,
            grid=(n,),
            in_specs=[loc(D_MODEL), loc(LANES), loc(LANES),
                      pl.BlockSpec((1, 1, 6 * D_MODEL), lambda i, *_: (seg_fn(i), 0, 0)),
                      pl.BlockSpec((1, D_MODEL), lambda i, *_: (0, 0)),
                      pl.BlockSpec(memory_space=pl.ANY)],
            out_specs=loc(D_MODEL),
            scratch_shapes=[pltpu.VMEM((2, NR, D_MODEL), F32), pltpu.SemaphoreType.DMA((2,))]),
        out_shape=jax.ShapeDtypeStruct((t_path, D_MODEL), F32),
        compiler_params=_cparams(("arbitrary",), VMEM_LIMIT),
        name="final",
    )(plan["lstart"], plan["gstart"], plan["strip"], x1, route, gates, mod3, fg, ys)


def _plan(tile_counts, n_rows, n_ctx):
    nt = n_rows // TE
    n_tok = tile_counts.shape[0]
    strip = (tile_counts + SEG - 1) // SEG * SEG
    lstart = jnp.cumsum(strip, axis=1) - strip
    rows_e = jnp.sum(strip, axis=0)
    rpad = (rows_e + TE - 1) // TE * TE
    ends = jnp.cumsum(rpad)
    offs = ends - rpad
    gstart = offs[None, :] + jnp.cumsum(strip, axis=0) - strip
    n_valid = (ends[-1] // TE).astype(jnp.int32)
    tile_ids = jnp.minimum(jnp.arange(nt, dtype=jnp.int32), n_valid - 1)
    tile_expert = jnp.sum((ends[None, :] <= (tile_ids * TE)[:, None]).astype(jnp.int32), axis=1)
    tile_expert = jnp.minimum(tile_expert, N_EXPERTS - 1).astype(jnp.int32)
    prev = jnp.concatenate([jnp.full((1,), -1, jnp.int32), tile_expert[:-1]])
    tile_first = (tile_expert != prev).astype(jnp.int32)
    tile_slot = (jnp.cumsum(tile_first) - 1) % 2
    pick = lambda table, idx: jnp.sum(
        jnp.where(idx[:, None] == jnp.arange(table.shape[0], dtype=jnp.int32)[None, :], table[None, :], 0), axis=1)
    after = pick(ends, tile_expert) // TE
    tile_next = jnp.where(after < n_valid, pick(tile_expert, jnp.minimum(after, nt - 1)), -1)
    tok = jnp.broadcast_to(jnp.arange(n_tok, dtype=jnp.int32)[:, None], strip.shape)
    g0 = gstart.T.reshape(-1)
    ln = strip.T.reshape(-1)
    src0 = (tok * NR + lstart).T.reshape(-1)
    is_ctx = (tok < n_ctx).T.reshape(-1)
    len1 = jnp.minimum(ln, TE - g0 % TE)
    tile0 = g0 // TE
    tiles = jnp.arange(nt + 1, dtype=jnp.int32)[:, None]
    count = lambda m: jnp.sum(m.astype(jnp.int32), axis=1)
    seg_ptr = count(tile0[None, :] < tiles)
    seg_mid = seg_ptr + count((tile0[None, :] == tiles) & is_ctx[None, :])
    into = ((g0 + len1) // TE)[None, :] == tiles[:nt]
    carry_len = jnp.sum(jnp.where(into, (ln - len1)[None, :], 0), axis=1)
    carry_src = jnp.sum(jnp.where(into & (ln > len1)[None, :], (src0 + len1)[None, :], 0), axis=1)
    i32 = lambda a: a.reshape(-1).astype(jnp.int32)
    return dict(strip=i32(strip), lstart=i32(lstart), gstart=i32(gstart), n_valid=n_valid.reshape(1),
                tile_expert=tile_expert, tile_first=i32(tile_first), tile_next=i32(tile_next),
                tile_slot=i32(tile_slot), seg_ptr=i32(seg_ptr), seg_mid=i32(seg_mid), seg_src=i32(src0),
                seg_dst=i32(g0 % TE), seg_len=i32(len1), carry_src=i32(carry_src),
                carry_len=i32(carry_len))


def _layer(x_prompt, x_sample, state, c, c_ctx, ada_w, ada_b, norm1_g, norm2_g, w_in, conv_w, conv_b,
           conv_ln_g, conv_ln_b, gate_w, gate_b, gla_norm_g, w_out, router_w, router_b,
           moe_w_gu, moe_b_gu, moe_w_dn, moe_b_dn, final_g):
    bp, lp, d = x_prompt.shape
    bs, ls, _ = x_sample.shape
    assert lp == TM and ls % TM == 0 and d == D_MODEL
    xp = x_prompt.reshape(bp * lp, d)
    xs = x_sample.reshape(bs * ls, d)
    n_c, n_l = bp, bs * ls // TM
    lat_tiles = ls // TM
    t_all = (n_c + n_l) * TM

    n_cond = 1 + bs
    cond_t = jnp.zeros((d, 8), F32).at[:, 0].set(c_ctx).at[:, 1:n_cond].set(c.T)
    mod3 = _modulation(cond_t, ada_w, ada_b, n_cond).reshape(8, 1, 6 * d)

    row = lambda a: a.reshape(1, -1)
    u, qk, v, g, la = _inproj(xp, xs, mod3, row(norm1_g), w_in, gate_w, gate_b, ls)

    pair = lambda s: s.reshape(s.shape[0], 2, 2, LANES, LANES)
    of_c, ob_c, s_ctx = _gla(qk, v, la, jnp.zeros((bp, 2, 2, LANES, LANES), F32), bp, 1, 0)
    of_l, ob_l, _ = _gla(qk, v, la, pair(state), bs, lat_tiles, n_c)

    cw = jnp.zeros((32, C_CONV), F32).at[:CONV_K].set(conv_w)
    rw = jnp.zeros((d, LANES), F32).at[:, :N_EXPERTS].set(router_w)
    rwh = rw.astype(BF16)
    rwl = (rw - rwh.astype(F32)).astype(BF16)
    rb = jnp.full((1, LANES), NEG, F32).at[0, :N_EXPERTS].set(router_b)
    wts = (cw, row(conv_b), row(conv_ln_g), row(conv_ln_b), row(gla_norm_g), w_out.astype(BF16),
           row(norm2_g), rwh, rwl, rb)
    seg_c = lambda i: 0
    seg_l = lambda i: 1 + i // lat_tiles
    x1_c, grp_c, route_c, gates_c, cnt_c = _mix(xp, u, g, of_c, ob_c, mod3, seg_c, 0, lp, wts)
    x1_l, grp_l, route_l, gates_l, cnt_l = _mix(xs, u, g, of_l, ob_l, mod3, seg_l, n_c, GRID_W, wts)

    tile_counts = jnp.concatenate([cnt_c[:, 0, :N_EXPERTS], cnt_l[:, 0, :N_EXPERTS]]).astype(jnp.int32)
    n_rows = TOP_K * t_all + (SEG - 1) * N_EXPERTS * (n_c + n_l) + N_EXPERTS * TE
    n_rows = (n_rows + TE - 1) // TE * TE
    plan = _plan(tile_counts, n_rows, n_c)
    ysrt = _experts(plan, grp_c, grp_l, n_rows, moe_w_gu, moe_b_gu, moe_w_dn, moe_b_dn)
    y_c = _final(plan, x1_c, route_c, gates_c, mod3, seg_c, 0, row(final_g), ysrt)
    y_l = _final(plan, x1_l, route_l, gates_l, mod3, seg_l, n_c, row(final_g), ysrt)
    new_state = s_ctx.reshape(bp, 1, 2, N_GLA_HEADS, DK_HEAD, DV_HEAD)
    return y_c.reshape(bp, lp, d), y_l.reshape(bs, ls, d), new_state


def kernel(x_prompt, x_sample, state_gla, c, c_ctx, ada_w, ada_b, norm1_g, norm2_g, w_in, conv_w,
           conv_b, conv_ln_g, conv_ln_b, gate_w, gate_b, gla_norm_g, w_out, router_w, router_b,
           moe_w_gu, moe_b_gu, moe_w_dn, moe_b_dn, final_g):
    assert ada_w.shape[0] == 1, "single-layer step"
    return _layer(x_prompt, x_sample, state_gla[:, 0], c, c_ctx, ada_w[0], ada_b[0], norm1_g[0],
                  norm2_g[0], w_in[0], conv_w[0], conv_b[0], conv_ln_g[0], conv_ln_b[0], gate_w[0],
                  gate_b[0], gla_norm_g[0], w_out[0], router_w[0], router_b[0], moe_w_gu[0],
                  moe_b_gu[0], moe_w_dn[0], moe_b_dn[0], final_g)
```

```python
import functools

import numpy as np
import jax
import jax.numpy as jnp
from jax import lax
from jax.experimental import pallas as pl
from jax.experimental.pallas import tpu as pltpu

D_MODEL = 1024
GRID_W = 64
C_CONV = D_MODEL // 2
CONV_K = 31
N_GLA_HEADS = 4
DV_HEAD = 128
DK_HEAD = 64
DK_TOT = DK_HEAD * N_GLA_HEADS
DV_TOT = DV_HEAD * N_GLA_HEADS
GATE_RANK = 16
GATE_TEMP = 16.0
CHUNK = 64
N_EXPERTS = 32
TOP_K = 4
D_FF = D_MODEL
SWIGLU_LIMIT = 7.0
SWIGLU_ALPHA = 1.702
EPS = 1e-6

LANES = 128
SEG = 8
TM = 256
TE = 512
NR = TOP_K * TM + TM
CONV_PAD = 16
VMEM_LIMIT = 56 * 1024 * 1024

F32 = jnp.float32
BF16 = jnp.bfloat16
HI = lax.Precision.HIGHEST
NEG = -1e30
TN = (((0,), (0,)), ((), ()))
NT = (((1,), (1,)), ((), ()))

assert N_EXPERTS * (SEG - 1) <= NR - TOP_K * TM


def _split(x):
    hi = x.astype(BF16)
    return hi, (x - hi.astype(F32)).astype(BF16)


def _cparams(sem, vmem=None):
    return pltpu.CompilerParams(dimension_semantics=sem, vmem_limit_bytes=vmem)


def _mod_kernel(ct_ref, w_ref, b_ref, o_ref, *, n_cond):
    ct = ct_ref[...]
    s = ct * jax.nn.sigmoid(ct)
    w = w_ref[...]
    rows = [jnp.sum(s[:, r:r + 1] * w, axis=0, keepdims=True) + b_ref[...] for r in range(n_cond)]
    rows.append(jnp.zeros((8 - n_cond, w.shape[1]), F32))
    o_ref[...] = jnp.concatenate(rows, axis=0)


def _modulation(cond_t, ada_w, ada_b, n_cond):
    d, n = ada_w.shape
    nb = 768
    return pl.pallas_call(
        functools.partial(_mod_kernel, n_cond=n_cond),
        grid=(n // nb,),
        in_specs=[pl.BlockSpec((d, 8), lambda i: (0, 0)),
                  pl.BlockSpec((d, nb), lambda i: (0, i)),
                  pl.BlockSpec((1, nb), lambda i: (0, i))],
        out_specs=pl.BlockSpec((8, nb), lambda i: (0, i)),
        out_shape=jax.ShapeDtypeStruct((8, n), F32),
        compiler_params=_cparams(("arbitrary",)),
        name="mod",
    )(cond_t, ada_w, ada_b.reshape(1, n))


def _rms(x, g):
    return x * lax.rsqrt(jnp.mean(x * x, axis=-1, keepdims=True) + EPS) * g


def _inproj_kernel(xp_ref, xs_ref, mod_ref, g1_ref, wu_ref, wqk_ref, wv_ref, wg_ref, wlow_ref,
                   gwh_ref, gwl_ref, gb_ref, u_ref, qk_ref, v_ref, g_ref, la_ref, *, n_ctx_tiles):
    i = pl.program_id(0)
    shift = mod_ref[0, :, 0:D_MODEL]
    scale = mod_ref[0, :, D_MODEL:2 * D_MODEL]
    dot = functools.partial(jnp.dot, preferred_element_type=F32)
    rows = u_ref.shape[0]
    sub = min(rows, TM)
    for r0 in range(0, rows, sub):
        rs = slice(r0, r0 + sub)
        x = jnp.where(i < n_ctx_tiles, xp_ref[rs, :], xs_ref[rs, :])
        h = (_rms(x, g1_ref[...]) * (1.0 + scale) + shift).astype(BF16)
        u_ref[rs, :] = dot(h, wu_ref[...]).astype(BF16)
        qk_ref[rs, :] = dot(h, wqk_ref[...]).astype(BF16)
        v_ref[rs, :] = dot(h, wv_ref[...]).astype(BF16)
        g_ref[rs, :] = dot(h, wg_ref[...]).astype(BF16)
        low = dot(h, wlow_ref[...])
        low_hi, low_lo = _split(low)
        z = (dot(low_hi, gwh_ref[...]) + dot(low_lo, gwh_ref[...]) + dot(low_hi, gwl_ref[...])
             + gb_ref[...])
        la_ref[rs, :] = (jnp.minimum(z, 0.0) - jnp.log1p(jnp.exp(-jnp.abs(z)))) * (1.0 / GATE_TEMP)


def _inproj(xp, xs, mod3, g1, w_in, gate_w, gate_b, lat_len):
    ti = next(t for t in (1024, 512, TM) if xp.shape[0] % t == 0 and lat_len % t == 0)
    lat_tiles_per_seq = lat_len // ti
    n_c, n_l = xp.shape[0] // ti, xs.shape[0] // ti
    t_all = xp.shape[0] + xs.shape[0]
    sp = np.cumsum([0, C_CONV, C_CONV, DK_TOT, DK_TOT, DV_TOT, DV_TOT, 2 * GATE_RANK])
    wb = w_in.astype(BF16)
    wu, wqk, wv, wg, wlow = (wb[:, sp[0]:sp[2]], wb[:, sp[2]:sp[4]], wb[:, sp[4]:sp[5]],
                             wb[:, sp[5]:sp[6]], wb[:, sp[6]:sp[7]])
    gw = jnp.zeros((2 * GATE_RANK, 2 * DK_TOT), F32)
    gw = gw.at[:GATE_RANK, :DK_TOT].set(gate_w[0]).at[GATE_RANK:, DK_TOT:].set(gate_w[1])
    gwh, gwl = _split(gw)
    gb = gate_b.reshape(1, 2 * DK_TOT)
    const = lambda a: pl.BlockSpec(a.shape, lambda i: (0,) * a.ndim)
    row = lambda w: pl.BlockSpec((ti, w), lambda i: (i, 0))
    seg = lambda i: jnp.where(i < n_c, 0, 1 + jnp.maximum(i - n_c, 0) // lat_tiles_per_seq)
    return pl.pallas_call(
        functools.partial(_inproj_kernel, n_ctx_tiles=n_c),
        grid=(n_c + n_l,),
        in_specs=[pl.BlockSpec((ti, D_MODEL), lambda i: (jnp.minimum(i, n_c - 1), 0)),
                  pl.BlockSpec((ti, D_MODEL), lambda i: (jnp.maximum(i - n_c, 0), 0)),
                  pl.BlockSpec((1, 1, 6 * D_MODEL), lambda i: (seg(i), 0, 0)),
                  const(g1), const(wu), const(wqk), const(wv), const(wg), const(wlow),
                  const(gwh), const(gwl), const(gb)],
        out_specs=[row(2 * C_CONV), row(2 * DK_TOT), row(DV_TOT), row(DV_TOT), row(2 * DK_TOT)],
        out_shape=[jax.ShapeDtypeStruct((t_all, 2 * C_CONV), BF16),
                   jax.ShapeDtypeStruct((t_all, 2 * DK_TOT), BF16),
                   jax.ShapeDtypeStruct((t_all, DV_TOT), BF16),
                   jax.ShapeDtypeStruct((t_all, DV_TOT), BF16),
                   jax.ShapeDtypeStruct((t_all, 2 * DK_TOT), F32)],
        compiler_params=_cparams(("arbitrary",), VMEM_LIMIT),
        name="inproj",
    )(xp, xs, mod3, g1, wu, wqk, wv, wg, wlow, gwh, gwl, gb)


def _gla_direction(qk_ref, v_ref, la_ref, tri_ref, blk_ref, o_ref, s_scr, reverse):
    la_hi, la_lo = _split(la_ref[...])
    tri = tri_ref[...]
    bcum = (jnp.dot(tri, la_hi, preferred_element_type=F32)
            + jnp.dot(tri, la_lo, preferred_element_type=F32))
    blk = blk_ref[...]
    bl_cols = (lax.dot_general(la_hi, blk, TN, preferred_element_type=F32)
               + lax.dot_general(la_lo, blk, TN, preferred_element_type=F32))
    q = qk_ref[:, 0:DK_TOT].astype(F32)
    k = qk_ref[:, DK_TOT:2 * DK_TOT].astype(F32)
    lane = lax.broadcasted_iota(jnp.int32, (CHUNK, LANES), 1)
    row2 = lax.broadcasted_iota(jnp.int32, (2 * CHUNK, CHUNK), 0) % CHUNK
    col2 = lax.broadcasted_iota(jnp.int32, (2 * CHUNK, CHUNK), 1)
    keep = (col2 >= row2) if reverse else (col2 <= row2)
    srow = lax.broadcasted_iota(jnp.int32, (LANES, LANES), 0)
    n_chunks = TM // CHUNK
    order = range(n_chunks - 1, -1, -1) if reverse else range(n_chunks)
    state = [s_scr[0], s_scr[1]]
    for c in order:
        r0 = c * CHUNK
        bc = bcum[r0:r0 + CHUNK]
        bl = bc[0:1] if reverse else bc[CHUNK - 1:CHUNK]
        qt = q[r0:r0 + CHUNK] * jnp.exp(bc) * (DK_HEAD ** -0.5)
        kt = k[r0:r0 + CHUNK] * jnp.exp(-bc)
        ke = k[r0:r0 + CHUNK] * jnp.exp(bl - bc)
        for p in range(2):
            cs = slice(p * LANES, (p + 1) * LANES)
            qt_p = qt[:, cs]
            qs = jnp.concatenate([jnp.where(lane < DK_HEAD, qt_p, 0.0),
                                  jnp.where(lane >= DK_HEAD, qt_p, 0.0)], axis=0).astype(BF16)
            att = lax.dot_general(qs, kt[:, cs].astype(BF16), NT, preferred_element_type=F32)
            att = jnp.where(keep, att, 0.0).astype(BF16)
            s_p = state[p]
            o_inter = jnp.dot(qs, s_p.astype(BF16), preferred_element_type=F32)
            ke_t = jnp.transpose(ke[:, cs]).astype(BF16)
            upd = []
            for hh in range(2):
                h = 2 * p + hh
                v_h = v_ref[r0:r0 + CHUNK, h * DV_HEAD:(h + 1) * DV_HEAD]
                both = jnp.dot(jnp.concatenate([att[hh * CHUNK:(hh + 1) * CHUNK], ke_t], axis=0), v_h,
                               preferred_element_type=F32)
                o_ref[r0:r0 + CHUNK, h * DV_HEAD:(h + 1) * DV_HEAD] = (
                    both[0:CHUNK] + o_inter[hh * CHUNK:(hh + 1) * CHUNK])
                upd.append(both[CHUNK:CHUNK + LANES])
            bl_col = bl_cols[p * LANES:(p + 1) * LANES, c * LANES:(c + 1) * LANES]
            state[p] = jnp.exp(bl_col) * s_p + jnp.where(srow < DK_HEAD, upd[0], upd[1])
    s_scr[0] = state[0]
    s_scr[1] = state[1]


def _gla_kernel(*refs, n_tiles, nb):
    seqs = [refs[6 * r:6 * r + 6] for r in range(nb)]
    tril_ref, triu_ref, blk_ref, s0_ref, of_ref, ob_ref, sout_ref, sf_scr, sb_scr = refs[6 * nb:]
    j = pl.program_id(1)

    @pl.when(j == 0)
    def _():
        sf_scr[...] = s0_ref[:, 0]
        sb_scr[...] = s0_ref[:, 1]

    for r, (qkf_ref, vf_ref, laf_ref, qkb_ref, vb_ref, lab_ref) in enumerate(seqs):
        _gla_direction(qkf_ref, vf_ref, laf_ref, tril_ref, blk_ref, of_ref.at[r], sf_scr.at[r], False)
        _gla_direction(qkb_ref, vb_ref, lab_ref, triu_ref, blk_ref, ob_ref.at[r], sb_scr.at[r], True)

    @pl.when(j == n_tiles - 1)
    def _():
        sout_ref[:, 0] = sf_scr[...]
        sout_ref[:, 1] = sb_scr[...]


def _gla(qk, v, la, s0, n_seq, n_tiles, tile_off):
    nb = 2 if n_seq % 2 == 0 else 1
    seq_len = n_tiles * TM
    blk = np.arange(TM) // CHUNK
    same = blk[:, None] == blk[None, :]
    r = np.arange(TM)
    tril = jnp.asarray((same & (r[None, :] <= r[:, None])).astype(np.float32)).astype(BF16)
    triu = jnp.asarray((same & (r[None, :] >= r[:, None])).astype(np.float32)).astype(BF16)
    chunk_of_lane = np.arange(TM // CHUNK * LANES) // LANES
    blkm = jnp.asarray((blk[:, None] == chunk_of_lane[None, :]).astype(np.float32)).astype(BF16)
    def tok(w, r, backward, cb=0):
        def index(s, j):
            jj = (n_tiles - 1 - j) if backward else j
            return ((s * nb + r) * n_tiles + jj + tile_off, cb)
        return pl.BlockSpec((TM, w), index)

    const = lambda a: pl.BlockSpec(a.shape, lambda s, j: (0,) * a.ndim)
    st = pl.BlockSpec((nb, 2, 2, LANES, LANES), lambda s, j: (s, 0, 0, 0, 0))
    seq_specs, seq_args = [], []
    for r in range(nb):
        seq_specs += [tok(2 * DK_TOT, r, False), tok(DV_TOT, r, False), tok(DK_TOT, r, False, 0),
                      tok(2 * DK_TOT, r, True), tok(DV_TOT, r, True), tok(DK_TOT, r, True, 1)]
        seq_args += [qk, v, la, qk, v, la]
    o_f, o_b, s_out = pl.pallas_call(
        functools.partial(_gla_kernel, n_tiles=n_tiles, nb=nb),
        grid=(n_seq // nb, n_tiles),
        in_specs=seq_specs + [const(tril), const(triu), const(blkm), st],
        out_specs=[pl.BlockSpec((nb, TM, DV_TOT), lambda s, j: (s, j, 0)),
                   pl.BlockSpec((nb, TM, DV_TOT), lambda s, j: (s, n_tiles - 1 - j, 0)), st],
        out_shape=[jax.ShapeDtypeStruct((n_seq, seq_len, DV_TOT), F32),
                   jax.ShapeDtypeStruct((n_seq, seq_len, DV_TOT), F32),
                   jax.ShapeDtypeStruct(s0.shape, F32)],
        scratch_shapes=[pltpu.VMEM((nb, 2, LANES, LANES), F32), pltpu.VMEM((nb, 2, LANES, LANES), F32)],
        compiler_params=_cparams(("arbitrary", "arbitrary"), VMEM_LIMIT),
        name="gla",
    )(*seq_args, tril, triu, blkm, s0)
    return o_f.reshape(n_seq * seq_len, DV_TOT), o_b.reshape(n_seq * seq_len, DV_TOT), s_out


def _mix_kernel(x_ref, u_ref, g_ref, of_ref, ob_ref, mod_ref, cw_ref, cb_ref, lng_ref, lnb_ref,
                gng_ref, wout_ref, n2g_ref, rwh_ref, rwl_ref, rb_ref, le_ref, ut_ref,
                x1_ref, grp_ref, route_ref, cnt_ref, pad_scr, shf_scr, cv_scr, *, seq_len):
    n_seq = TM // seq_len
    pr = seq_len + 2 * CONV_PAD
    qr = seq_len + 3 * SEG

    u = u_ref[...].astype(F32)
    hg = u[:, 0:C_CONV] * jax.nn.sigmoid(u[:, C_CONV:2 * C_CONV])
    zero = jnp.zeros((CONV_PAD, C_CONV), F32)
    for s in range(n_seq):
        pad_scr[s * pr:s * pr + CONV_PAD, :] = zero
        pad_scr[s * pr + CONV_PAD:s * pr + CONV_PAD + seq_len, :] = hg[s * seq_len:(s + 1) * seq_len]
        pad_scr[s * pr + CONV_PAD + seq_len:(s + 1) * pr, :] = zero
        for b in range(1, SEG):
            shf_scr[b - 1, s * qr:(s + 1) * qr, :] = pad_scr[s * pr + b:s * pr + b + qr, :]
    rb = 8 * SEG
    per_seq = seq_len // rb
    for gi in range(C_CONV // LANES):
        gs = slice(gi * LANES, (gi + 1) * LANES)
        wv = [jnp.broadcast_to(cw_ref[tap:tap + 1, gs], (SEG, LANES)) for tap in range(CONV_K)]

        def conv_rows(i, carry, gs=gs, wv=wv):
            s = lax.shift_right_logical(i, per_seq.bit_length() - 1)
            r0 = (i & (per_seq - 1)) * rb
            nj = rb // SEG
            acc = [None] * nj
            for b in range(SEG):
                taps = [(a, SEG * a + b - (CONV_PAD - CONV_K // 2)) for a in range(CONV_PAD * 2 // SEG)]
                taps = [(a, t) for a, t in taps if 0 <= t < CONV_K]
                loaded = {}
                for a, t in taps:
                    for j in range(nj):
                        if a + j not in loaded:
                            if b == 0:
                                row = pl.multiple_of(s * pr + r0 + (a + j) * SEG, SEG)
                                loaded[a + j] = pad_scr[pl.ds(row, SEG), gs]
                            else:
                                row = pl.multiple_of(s * qr + r0 + (a + j) * SEG, SEG)
                                loaded[a + j] = shf_scr[b - 1, pl.ds(row, SEG), gs]
                        term = wv[t] * loaded[a + j]
                        acc[j] = term if acc[j] is None else acc[j] + term
            for j in range(nj):
                cv_scr[pl.ds(pl.multiple_of(s * seq_len + r0 + j * SEG, SEG), SEG), gs] = acc[j]
            return carry

        lax.fori_loop(0, TM // rb, conv_rows, 0)
    cv = cv_scr[...] + cb_ref[...]
    mu = jnp.mean(cv, axis=-1, keepdims=True)
    var = jnp.mean(jnp.square(cv - mu), axis=-1, keepdims=True)
    cv = (cv - mu) * lax.rsqrt(var + EPS) * lng_ref[...] + lnb_ref[...]
    conv_out = (cv * jax.nn.sigmoid(cv)).astype(BF16)

    o = of_ref[...] + ob_ref[...]
    g = g_ref[...].astype(F32)
    heads = []
    for h in range(N_GLA_HEADS):
        cs = slice(h * DV_HEAD, (h + 1) * DV_HEAD)
        oh = o[:, cs]
        oh = oh * lax.rsqrt(jnp.mean(oh * oh, axis=-1, keepdims=True) + EPS) * gng_ref[...]
        gh = g[:, cs]
        heads.append((oh * (gh * jax.nn.sigmoid(gh))).astype(BF16))
    y = jnp.dot(conv_out, wout_ref[0:C_CONV, :], preferred_element_type=F32)
    for h in range(N_GLA_HEADS):
        y = y + jnp.dot(heads[h], wout_ref[C_CONV + h * DV_HEAD:C_CONV + (h + 1) * DV_HEAD, :],
                        preferred_element_type=F32)

    gate1 = mod_ref[0, :, 2 * D_MODEL:3 * D_MODEL]
    shift2 = mod_ref[0, :, 3 * D_MODEL:4 * D_MODEL]
    scale2 = mod_ref[0, :, 4 * D_MODEL:5 * D_MODEL]
    x1 = x_ref[...] + gate1 * y
    x1_ref[...] = x1
    h2 = _rms(x1, n2g_ref[...]) * (1.0 + scale2) + shift2
    h2_hi, h2_lo = _split(h2)

    ntdot = functools.partial(lax.dot_general, dimension_numbers=NT, preferred_element_type=F32)
    l = (ntdot(rwh_ref[...], h2_hi) + ntdot(rwh_ref[...], h2_lo) + ntdot(rwl_ref[...], h2_hi)
         + rb_ref[...])
    erow = lax.broadcasted_iota(jnp.int32, (N_EXPERTS, TM), 0).astype(F32)
    vals, idxs, hots = [], [], []
    for _ in range(TOP_K):
        m = jnp.max(l, axis=0, keepdims=True)
        idx = jnp.min(jnp.where(l == m, erow, float(N_EXPERTS)), axis=0, keepdims=True)
        hot = erow == idx
        l = jnp.where(hot, -jnp.inf, l)
        vals.append(m)
        idxs.append(idx)
        hots.append(hot)
    es = [jnp.exp(vk - vals[0]) for vk in vals]
    inv = 1.0 / (es[0] + es[1] + es[2] + es[3])
    sel = jnp.zeros((N_EXPERTS, TM), F32)
    for hot in hots:
        sel = jnp.where(hot, 1.0, sel)

    cnt = jnp.broadcast_to(jnp.sum(sel, axis=1, keepdims=True), (N_EXPERTS, LANES))
    cnt_ref[0] = cnt
    strip = jnp.ceil(cnt * (1.0 / SEG)) * float(SEG)
    starts = jnp.dot(le_ref[...], strip, precision=HI, preferred_element_type=F32)[:, 0:1]
    slot_of = jnp.dot(sel.astype(BF16), ut_ref[...], preferred_element_type=F32) + starts
    slots = [jnp.sum(jnp.where(hot, slot_of, 0.0), axis=0, keepdims=True) for hot in hots]

    rows = slots + idxs + [e * inv for e in es]
    srow = lax.broadcasted_iota(jnp.int32, (16, TM), 0)
    packed = jnp.zeros((16, TM), F32)
    for j, v in enumerate(rows):
        packed = jnp.where(srow == j, v, packed)
    packed = jnp.concatenate([packed, jnp.zeros((LANES - 16, TM), F32)], axis=0)
    route_ref[...] = jnp.transpose(packed)

    r = lax.broadcasted_iota(jnp.int32, (NR, TM), 0).astype(F32)
    perm = jnp.zeros((NR, TM), F32)
    for slot in slots:
        perm = jnp.where(slot == r, 1.0, perm)
    grp_ref[...] = jnp.dot(perm.astype(BF16), h2_hi, preferred_element_type=F32)


def _mix(x, u, g, o_f, o_b, mod3, seg_fn, tile_off, seq_len, wts):
    (conv_w, conv_b, ln_g, ln_b, gng, wout, n2g, rwh, rwl, rb) = wts
    t_path = x.shape[0]
    n = t_path // TM
    le = jnp.asarray(np.tril(np.ones((N_EXPERTS, N_EXPERTS), np.float32), -1))
    ut = jnp.asarray(np.triu(np.ones((TM, TM), np.float32), 1)).astype(BF16)
    const = lambda a: pl.BlockSpec(a.shape, lambda i: (0,) * a.ndim)
    loc = lambda w: pl.BlockSpec((TM, w), lambda i: (i, 0))
    uni = lambda w: pl.BlockSpec((TM, w), lambda i: (i + tile_off, 0))
    n_seq = TM // seq_len
    return pl.pallas_call(
        functools.partial(_mix_kernel, seq_len=seq_len),
        grid=(n,),
        in_specs=[loc(D_MODEL), uni(2 * C_CONV), uni(DV_TOT), loc(DV_TOT), loc(DV_TOT),
                  pl.BlockSpec((1, 1, 6 * D_MODEL), lambda i: (seg_fn(i), 0, 0)),
                  const(conv_w), const(conv_b), const(ln_g), const(ln_b), const(gng), const(wout),
                  const(n2g), const(rwh), const(rwl), const(rb), const(le), const(ut)],
        out_specs=[loc(D_MODEL), pl.BlockSpec((NR, D_MODEL), lambda i: (i, 0)), loc(LANES),
                   pl.BlockSpec((1, N_EXPERTS, LANES), lambda i: (i, 0, 0))],
        out_shape=[jax.ShapeDtypeStruct((t_path, D_MODEL), F32),
                   jax.ShapeDtypeStruct((n * NR, D_MODEL), F32),
                   jax.ShapeDtypeStruct((t_path, LANES), F32),
                   jax.ShapeDtypeStruct((n, N_EXPERTS, LANES), F32)],
        scratch_shapes=[pltpu.VMEM((n_seq * (seq_len + 2 * CONV_PAD), C_CONV), F32),
                        pltpu.VMEM((SEG - 1, n_seq * (seq_len + 3 * SEG), C_CONV), F32),
                        pltpu.VMEM((TM, C_CONV), F32)],
        compiler_params=_cparams(("arbitrary",), VMEM_LIMIT),
        name="mix",
    )(x, u, g, o_f, o_b, mod3, conv_w, conv_b, ln_g, ln_b, gng, wout, n2g, rwh, rwl, rb, le, ut)


def _strip_copies(tile, ls_ref, gs_ref, sg_ref, local, rows_hbm, sem, to_hbm, wait):
    def body(e, carry):
        j = tile * N_EXPERTS + e
        n = pl.multiple_of(sg_ref[j], SEG)

        @pl.when(n > 0)
        def _():
            loc = local.at[pl.ds(pl.multiple_of(ls_ref[j], SEG), n)]
            glob = rows_hbm.at[pl.ds(pl.multiple_of(gs_ref[j], SEG), n)]
            cp = pltpu.make_async_copy(loc, glob, sem) if to_hbm else pltpu.make_async_copy(glob, loc, sem)
            if wait:
                cp.wait()
            else:
                cp.start()
        return carry

    lax.fori_loop(0, N_EXPERTS, body, 0)


def _expert_kernel(te_ref, nv_ref, first_ref, next_ref, slot_ref, ptr_ref, mid_ref, src_ref, dst_ref, len_ref,
                   csrc_ref, clen_ref, grpc_hbm, grpl_hbm, wgu_hbm, bgu_ref, wdn_hbm, bdn_ref,
                   y_ref, xbuf, wgu_f, wdn_f, wgu_s, wdn_s, xsem, sem, *, ctx_rows):
    i = pl.program_id(0)

    def weight_copies(e, s):
        return (pltpu.make_async_copy(wgu_hbm.at[e], wgu_f.at[s], sem.at[0, s]),
                pltpu.make_async_copy(wdn_hbm.at[e], wdn_f.at[s], sem.at[1, s]))

    def row_copies(tile, xs, wait):
        def strip(hbm, src, dst, n):
            cp = pltpu.make_async_copy(hbm.at[pl.ds(pl.multiple_of(src, SEG), n)],
                                       xbuf.at[xs, pl.ds(pl.multiple_of(dst, SEG), n)], xsem.at[xs])
            cp.wait() if wait else cp.start()

        def strips_of(hbm, base):
            def body(j, carry):
                n = pl.multiple_of(len_ref[j], SEG)

                @pl.when(n > 0)
                def _():
                    strip(hbm, src_ref[j] - base, dst_ref[j], n)
                return carry
            return body

        lax.fori_loop(ptr_ref[tile], mid_ref[tile], strips_of(grpc_hbm, 0), 0)
        lax.fori_loop(mid_ref[tile], ptr_ref[tile + 1], strips_of(grpl_hbm, ctx_rows), 0)
        n = pl.multiple_of(clen_ref[tile], SEG)
        src = csrc_ref[tile]

        @pl.when(jnp.logical_and(n > 0, src < ctx_rows))
        def _():
            strip(grpc_hbm, src, 0, n)

        @pl.when(jnp.logical_and(n > 0, src >= ctx_rows))
        def _():
            strip(grpl_hbm, src - ctx_rows, 0, n)

    @pl.when(i == 0)
    def _():
        xbuf[...] = jnp.zeros(xbuf.shape, F32)
        row_copies(i, 0, False)

    @pl.when(i + 1 < nv_ref[0])
    def _():
        row_copies(i + 1, (i + 1) % 2, False)

    @pl.when(i < nv_ref[0])
    def _():
        e = te_ref[i]
        s = slot_ref[i]
        row_copies(i, i % 2, True)

        @pl.when(first_ref[i] == 1)
        def _():
            @pl.when(i == 0)
            def _():
                for cp in weight_copies(e, s):
                    cp.start()

            for cp in weight_copies(e, s):
                cp.wait()
            nxt = next_ref[i]

            @pl.when(nxt >= 0)
            def _():
                for cp in weight_copies(nxt, 1 - s):
                    cp.start()

            rows = 64

            def cast(r, carry):
                sl = pl.ds(pl.multiple_of(r * rows, rows), rows)
                wgu_s[sl, :] = wgu_f[s, sl, :].astype(BF16)
                wdn_s[sl, :] = wdn_f[s, sl, :].astype(BF16)
                return carry

            lax.fori_loop(0, D_MODEL // rows, cast, 0)

        gu = jnp.dot(xbuf[i % 2].astype(BF16), wgu_s[...], preferred_element_type=F32) + bgu_ref[0]
        gate = jnp.minimum(gu[:, 0:D_FF], SWIGLU_LIMIT)
        up = jnp.clip(gu[:, D_FF:2 * D_FF], -SWIGLU_LIMIT, SWIGLU_LIMIT)
        act = (gate * jax.nn.sigmoid(SWIGLU_ALPHA * gate) * (up + 1.0)).astype(BF16)
        y_ref[...] = jnp.dot(act, wdn_s[...], preferred_element_type=F32) + bdn_ref[0]

    @pl.when(i >= nv_ref[0])
    def _():
        y_ref[...] = jnp.zeros(y_ref.shape, y_ref.dtype)


def _experts(plan, grp_c, grp_l, n_rows, w_gu, b_gu, w_dn, b_dn):
    nt = n_rows // TE
    exp3 = lambda i, te, *_: (te[i], 0, 0)
    any_spec = pl.BlockSpec(memory_space=pl.ANY)
    return pl.pallas_call(
        functools.partial(_expert_kernel, ctx_rows=grp_c.shape[0]),
        grid_spec=pltpu.PrefetchScalarGridSpec(
            num_scalar_prefetch=12,
            grid=(nt,),
            in_specs=[any_spec, any_spec, any_spec,
                      pl.BlockSpec((1, 1, 2 * D_FF), exp3), any_spec,
                      pl.BlockSpec((1, 1, D_MODEL), exp3)],
            out_specs=pl.BlockSpec((TE, D_MODEL), lambda i, *_: (i, 0)),
            scratch_shapes=[pltpu.VMEM((2, TE, D_MODEL), F32),
                            pltpu.VMEM((2, D_MODEL, 2 * D_FF), F32), pltpu.VMEM((2, D_FF, D_MODEL), F32),
                            pltpu.VMEM((D_MODEL, 2 * D_FF), BF16), pltpu.VMEM((D_FF, D_MODEL), BF16),
                            pltpu.SemaphoreType.DMA((2,)), pltpu.SemaphoreType.DMA((2, 2))]),
        out_shape=jax.ShapeDtypeStruct((n_rows, D_MODEL), F32),
        compiler_params=_cparams(("arbitrary",), VMEM_LIMIT),
        name="experts",
    )(plan["tile_expert"], plan["n_valid"], plan["tile_first"], plan["tile_next"], plan["tile_slot"],
      plan["seg_ptr"], plan["seg_mid"], plan["seg_src"], plan["seg_dst"], plan["seg_len"],
      plan["carry_src"], plan["carry_len"], grp_c, grp_l, w_gu, b_gu.reshape(N_EXPERTS, 1, 2 * D_FF), w_dn, b_dn.reshape(N_EXPERTS, 1, D_MODEL))


def _final_kernel(ls_ref, gs_ref, sg_ref, x1_ref, route_ref, mod_ref, fg_ref, ys_ref, y_ref,
                  ybuf, sem, *, tile_off, n_tiles):
    i = pl.program_id(0)
    slot = i % 2

    def fetch(step, sl, wait):
        _strip_copies(step + tile_off, ls_ref, gs_ref, sg_ref, ybuf.at[sl], ys_ref, sem.at[sl], False, wait)

    @pl.when(i == 0)
    def _():
        ybuf[...] = jnp.zeros(ybuf.shape, F32)
        fetch(i, slot, False)

    @pl.when(i + 1 < n_tiles)
    def _():
        fetch(i + 1, 1 - slot, False)

    fetch(i, slot, True)

    route = route_ref[...]
    r = lax.broadcasted_iota(jnp.int32, (TM, NR), 1).astype(F32)
    comb = jnp.zeros((TM, NR), F32)
    for kk in range(TOP_K):
        gate = route[:, 2 * TOP_K + kk:2 * TOP_K + kk + 1]
        comb = jnp.where(route[:, kk:kk + 1] == r, gate, comb)
    moe = jnp.dot(comb.astype(BF16), ybuf[slot].astype(BF16), preferred_element_type=F32)
    gate2 = mod_ref[0, :, 5 * D_MODEL:6 * D_MODEL]
    y_ref[...] = _rms(x1_ref[...] + gate2 * moe, fg_ref[...])


def _final(plan, x1, route, mod3, seg_fn, tile_off, fg, ys):
    t_path = x1.shape[0]
    n = t_path // TM
    loc = lambda w: pl.BlockSpec((TM, w), lambda i, *_: (i, 0))
    return pl.pallas_call(
        functools.partial(_final_kernel, tile_off=tile_off, n_tiles=n),
        grid_spec=pltpu.PrefetchScalarGridSpec(
            num_scalar_prefetch=3,
            grid=(n,),
            in_specs=[loc(D_MODEL), loc(LANES),
                      pl.BlockSpec((1, 1, 6 * D_MODEL), lambda i, *_: (seg_fn(i), 0, 0)),
                      pl.BlockSpec((1, D_MODEL), lambda i, *_: (0, 0)),
                      pl.BlockSpec(memory_space=pl.ANY)],
            out_specs=loc(D_MODEL),
            scratch_shapes=[pltpu.VMEM((2, NR, D_MODEL), F32), pltpu.SemaphoreType.DMA((2,))]),
        out_shape=jax.ShapeDtypeStruct((t_path, D_MODEL), F32),
        compiler_params=_cparams(("arbitrary",), VMEM_LIMIT),
        name="final",
    )(plan["lstart"], plan["gstart"], plan["strip"], x1, route, mod3, fg, ys)


def _plan(tile_counts, n_rows, n_ctx):
    nt = n_rows // TE
    n_tok = tile_counts.shape[0]
    strip = (tile_counts + SEG - 1) // SEG * SEG
    lstart = jnp.cumsum(strip, axis=1) - strip
    rows_e = jnp.sum(strip, axis=0)
    rpad = (rows_e + TE - 1) // TE * TE
    ends = jnp.cumsum(rpad)
    offs = ends - rpad
    gstart = offs[None, :] + jnp.cumsum(strip, axis=0) - strip
    n_valid = (ends[-1] // TE).astype(jnp.int32)
    tile_ids = jnp.minimum(jnp.arange(nt, dtype=jnp.int32), n_valid - 1)
    tile_expert = jnp.sum((ends[None, :] <= (tile_ids * TE)[:, None]).astype(jnp.int32), axis=1)
    tile_expert = jnp.minimum(tile_expert, N_EXPERTS - 1).astype(jnp.int32)
    prev = jnp.concatenate([jnp.full((1,), -1, jnp.int32), tile_expert[:-1]])
    tile_first = (tile_expert != prev).astype(jnp.int32)
    tile_slot = (jnp.cumsum(tile_first) - 1) % 2
    pick = lambda table, idx: jnp.sum(
        jnp.where(idx[:, None] == jnp.arange(table.shape[0], dtype=jnp.int32)[None, :], table[None, :], 0), axis=1)
    after = pick(ends, tile_expert) // TE
    tile_next = jnp.where(after < n_valid, pick(tile_expert, jnp.minimum(after, nt - 1)), -1)
    tok = jnp.broadcast_to(jnp.arange(n_tok, dtype=jnp.int32)[:, None], strip.shape)
    g0 = gstart.T.reshape(-1)
    ln = strip.T.reshape(-1)
    src0 = (tok * NR + lstart).T.reshape(-1)
    is_ctx = (tok < n_ctx).T.reshape(-1)
    len1 = jnp.minimum(ln, TE - g0 % TE)
    tile0 = g0 // TE
    tiles = jnp.arange(nt + 1, dtype=jnp.int32)[:, None]
    count = lambda m: jnp.sum(m.astype(jnp.int32), axis=1)
    seg_ptr = count(tile0[None, :] < tiles)
    seg_mid = seg_ptr + count((tile0[None, :] == tiles) & is_ctx[None, :])
    into = ((g0 + len1) // TE)[None, :] == tiles[:nt]
    carry_len = jnp.sum(jnp.where(into, (ln - len1)[None, :], 0), axis=1)
    carry_src = jnp.sum(jnp.where(into & (ln > len1)[None, :], (src0 + len1)[None, :], 0), axis=1)
    i32 = lambda a: a.reshape(-1).astype(jnp.int32)
    return dict(strip=i32(strip), lstart=i32(lstart), gstart=i32(gstart), n_valid=n_valid.reshape(1),
                tile_expert=tile_expert, tile_first=i32(tile_first), tile_next=i32(tile_next),
                tile_slot=i32(tile_slot), seg_ptr=i32(seg_ptr), seg_mid=i32(seg_mid), seg_src=i32(src0),
                seg_dst=i32(g0 % TE), seg_len=i32(len1), carry_src=i32(carry_src),
                carry_len=i32(carry_len))


def _layer(x_prompt, x_sample, state, c, c_ctx, ada_w, ada_b, norm1_g, norm2_g, w_in, conv_w, conv_b,
           conv_ln_g, conv_ln_b, gate_w, gate_b, gla_norm_g, w_out, router_w, router_b,
           moe_w_gu, moe_b_gu, moe_w_dn, moe_b_dn, final_g):
    bp, lp, d = x_prompt.shape
    bs, ls, _ = x_sample.shape
    assert lp == TM and ls % TM == 0 and d == D_MODEL
    xp = x_prompt.reshape(bp * lp, d)
    xs = x_sample.reshape(bs * ls, d)
    n_c, n_l = bp, bs * ls // TM
    lat_tiles = ls // TM
    t_all = (n_c + n_l) * TM

    n_cond = 1 + bs
    cond_t = jnp.zeros((d, 8), F32).at[:, 0].set(c_ctx).at[:, 1:n_cond].set(c.T)
    mod3 = _modulation(cond_t, ada_w, ada_b, n_cond).reshape(8, 1, 6 * d)

    row = lambda a: a.reshape(1, -1)
    u, qk, v, g, la = _inproj(xp, xs, mod3, row(norm1_g), w_in, gate_w, gate_b, ls)

    pair = lambda s: s.reshape(s.shape[0], 2, 2, LANES, LANES)
    of_c, ob_c, s_ctx = _gla(qk, v, la, jnp.zeros((bp, 2, 2, LANES, LANES), F32), bp, 1, 0)
    of_l, ob_l, _ = _gla(qk, v, la, pair(state), bs, lat_tiles, n_c)

    cw = jnp.zeros((32, C_CONV), F32).at[:CONV_K].set(conv_w)
    rwh, rwl = _split(router_w.T)
    rb = jnp.broadcast_to(router_b[:, None], (N_EXPERTS, TM))
    wts = (cw, row(conv_b), row(conv_ln_g), row(conv_ln_b), row(gla_norm_g), w_out.astype(BF16),
           row(norm2_g), rwh, rwl, rb)
    seg_c = lambda i: 0
    seg_l = lambda i: 1 + i // lat_tiles
    x1_c, grp_c, route_c, cnt_c = _mix(xp, u, g, of_c, ob_c, mod3, seg_c, 0, lp, wts)
    x1_l, grp_l, route_l, cnt_l = _mix(xs, u, g, of_l, ob_l, mod3, seg_l, n_c, GRID_W, wts)

    tile_counts = jnp.concatenate([cnt_c[:, :, 0], cnt_l[:, :, 0]]).astype(jnp.int32)
    n_rows = TOP_K * t_all + (SEG - 1) * N_EXPERTS * (n_c + n_l) + N_EXPERTS * TE
    n_rows = (n_rows + TE - 1) // TE * TE
    plan = _plan(tile_counts, n_rows, n_c)
    ysrt = _experts(plan, grp_c, grp_l, n_rows, moe_w_gu, moe_b_gu, moe_w_dn, moe_b_dn)
    y_c = _final(plan, x1_c, route_c, mod3, seg_c, 0, row(final_g), ysrt)
    y_l = _final(plan, x1_l, route_l, mod3, seg_l, n_c, row(final_g), ysrt)
    new_state = s_ctx.reshape(bp, 1, 2, N_GLA_HEADS, DK_HEAD, DV_HEAD)
    return y_c.reshape(bp, lp, d), y_l.reshape(bs, ls, d), new_state


def kernel(x_prompt, x_sample, state_gla, c, c_ctx, ada_w, ada_b, norm1_g, norm2_g, w_in, conv_w,
           conv_b, conv_ln_g, conv_ln_b, gate_w, gate_b, gla_norm_g, w_out, router_w, router_b,
           moe_w_gu, moe_b_gu, moe_w_dn, moe_b_dn, final_g):
    assert ada_w.shape[0] == 1, "single-layer step"
    return _layer(x_prompt, x_sample, state_gla[:, 0], c, c_ctx, ada_w[0], ada_b[0], norm1_g[0],
                  norm2_g[0], w_in[0], conv_w[0], conv_b[0], conv_ln_g[0], conv_ln_b[0], gate_w[0],
                  gate_b[0], gla_norm_g[0], w_out[0], router_w[0], router_b[0], moe_w_gu[0],
                  moe_b_gu[0], moe_w_dn[0], moe_b_dn[0], final_g)
```

```python
import functools

import numpy as np
import jax
import jax.numpy as jnp
from jax import lax
from jax.experimental import pallas as pl
from jax.experimental.pallas import tpu as pltpu

D_MODEL = 1024
GRID_W = 64
C_CONV = D_MODEL // 2
CONV_K = 31
N_GLA_HEADS = 4
DV_HEAD = 128
DK_HEAD = 64
DK_TOT = DK_HEAD * N_GLA_HEADS
DV_TOT = DV_HEAD * N_GLA_HEADS
GATE_RANK = 16
GATE_TEMP = 16.0
CHUNK = 64
N_EXPERTS = 32
TOP_K = 4
D_FF = D_MODEL
SWIGLU_LIMIT = 7.0
SWIGLU_ALPHA = 1.702
EPS = 1e-6

LANES = 128
SEG = 8
TM = 256
TE = 512
NR = TOP_K * TM + TM
CONV_PAD = 16
VMEM_LIMIT = 56 * 1024 * 1024

F32 = jnp.float32
BF16 = jnp.bfloat16
HI = lax.Precision.HIGHEST
NEG = -1e30
TN = (((0,), (0,)), ((), ()))
NT = (((1,), (1,)), ((), ()))

assert N_EXPERTS * (SEG - 1) <= NR - TOP_K * TM


def _split(x):
    hi = x.astype(BF16)
    return hi, (x - hi.astype(F32)).astype(BF16)


def _cparams(sem, vmem=None):
    return pltpu.CompilerParams(dimension_semantics=sem, vmem_limit_bytes=vmem)


def _mod_kernel(ct_ref, w_ref, b_ref, o_ref, *, n_cond):
    ct = ct_ref[...]
    s = ct * jax.nn.sigmoid(ct)
    w = w_ref[...]
    rows = [jnp.sum(s[:, r:r + 1] * w, axis=0, keepdims=True) + b_ref[...] for r in range(n_cond)]
    rows.append(jnp.zeros((8 - n_cond, w.shape[1]), F32))
    o_ref[...] = jnp.concatenate(rows, axis=0)


def _modulation(cond_t, ada_w, ada_b, n_cond):
    d, n = ada_w.shape
    nb = 768
    return pl.pallas_call(
        functools.partial(_mod_kernel, n_cond=n_cond),
        grid=(n // nb,),
        in_specs=[pl.BlockSpec((d, 8), lambda i: (0, 0)),
                  pl.BlockSpec((d, nb), lambda i: (0, i)),
                  pl.BlockSpec((1, nb), lambda i: (0, i))],
        out_specs=pl.BlockSpec((8, nb), lambda i: (0, i)),
        out_shape=jax.ShapeDtypeStruct((8, n), F32),
        compiler_params=_cparams(("arbitrary",)),
        name="mod",
    )(cond_t, ada_w, ada_b.reshape(1, n))


def _rms(x, g):
    return x * lax.rsqrt(jnp.mean(x * x, axis=-1, keepdims=True) + EPS) * g


def _inproj_kernel(xp_ref, xs_ref, mod_ref, g1_ref, wu_ref, wqk_ref, wv_ref, wg_ref, wlow_ref,
                   gwh_ref, gwl_ref, gb_ref, u_ref, qk_ref, v_ref, g_ref, la_ref, *, n_ctx_tiles):
    i = pl.program_id(0)
    shift = mod_ref[0, :, 0:D_MODEL]
    scale = mod_ref[0, :, D_MODEL:2 * D_MODEL]
    dot = functools.partial(jnp.dot, preferred_element_type=F32)
    rows = u_ref.shape[0]
    sub = min(rows, TM)
    for r0 in range(0, rows, sub):
        rs = slice(r0, r0 + sub)
        x = jnp.where(i < n_ctx_tiles, xp_ref[rs, :], xs_ref[rs, :])
        h = (_rms(x, g1_ref[...]) * (1.0 + scale) + shift).astype(BF16)
        u_ref[rs, :] = dot(h, wu_ref[...]).astype(BF16)
        qk_ref[rs, :] = dot(h, wqk_ref[...]).astype(BF16)
        v_ref[rs, :] = dot(h, wv_ref[...]).astype(BF16)
        g_ref[rs, :] = dot(h, wg_ref[...]).astype(BF16)
        low = dot(h, wlow_ref[...])
        low_hi, low_lo = _split(low)
        z = (dot(low_hi, gwh_ref[...]) + dot(low_lo, gwh_ref[...]) + dot(low_hi, gwl_ref[...])
             + gb_ref[...])
        la_ref[rs, :] = (jnp.minimum(z, 0.0) - jnp.log1p(jnp.exp(-jnp.abs(z)))) * (1.0 / GATE_TEMP)


def _inproj(xp, xs, mod3, g1, w_in, gate_w, gate_b, lat_len):
    ti = next(t for t in (1024, 512, TM) if xp.shape[0] % t == 0 and lat_len % t == 0)
    lat_tiles_per_seq = lat_len // ti
    n_c, n_l = xp.shape[0] // ti, xs.shape[0] // ti
    t_all = xp.shape[0] + xs.shape[0]
    sp = np.cumsum([0, C_CONV, C_CONV, DK_TOT, DK_TOT, DV_TOT, DV_TOT, 2 * GATE_RANK])
    wb = w_in.astype(BF16)
    wu, wqk, wv, wg, wlow = (wb[:, sp[0]:sp[2]], wb[:, sp[2]:sp[4]], wb[:, sp[4]:sp[5]],
                             wb[:, sp[5]:sp[6]], wb[:, sp[6]:sp[7]])
    gw = jnp.zeros((2 * GATE_RANK, 2 * DK_TOT), F32)
    gw = gw.at[:GATE_RANK, :DK_TOT].set(gate_w[0]).at[GATE_RANK:, DK_TOT:].set(gate_w[1])
    gwh, gwl = _split(gw)
    gb = gate_b.reshape(1, 2 * DK_TOT)
    const = lambda a: pl.BlockSpec(a.shape, lambda i: (0,) * a.ndim)
    row = lambda w: pl.BlockSpec((ti, w), lambda i: (i, 0))
    seg = lambda i: jnp.where(i < n_c, 0, 1 + jnp.maximum(i - n_c, 0) // lat_tiles_per_seq)
    return pl.pallas_call(
        functools.partial(_inproj_kernel, n_ctx_tiles=n_c),
        grid=(n_c + n_l,),
        in_specs=[pl.BlockSpec((ti, D_MODEL), lambda i: (jnp.minimum(i, n_c - 1), 0)),
                  pl.BlockSpec((ti, D_MODEL), lambda i: (jnp.maximum(i - n_c, 0), 0)),
                  pl.BlockSpec((1, 1, 6 * D_MODEL), lambda i: (seg(i), 0, 0)),
                  const(g1), const(wu), const(wqk), const(wv), const(wg), const(wlow),
                  const(gwh), const(gwl), const(gb)],
        out_specs=[row(2 * C_CONV), row(2 * DK_TOT), row(DV_TOT), row(DV_TOT), row(2 * DK_TOT)],
        out_shape=[jax.ShapeDtypeStruct((t_all, 2 * C_CONV), BF16),
                   jax.ShapeDtypeStruct((t_all, 2 * DK_TOT), BF16),
                   jax.ShapeDtypeStruct((t_all, DV_TOT), BF16),
                   jax.ShapeDtypeStruct((t_all, DV_TOT), BF16),
                   jax.ShapeDtypeStruct((t_all, 2 * DK_TOT), F32)],
        compiler_params=_cparams(("arbitrary",), VMEM_LIMIT),
        name="inproj",
    )(xp, xs, mod3, g1, wu, wqk, wv, wg, wlow, gwh, gwl, gb)


def _gla_direction(qk_ref, v_ref, la_ref, tri_ref, blk_ref, o_ref, s_scr, reverse):
    la_hi, la_lo = _split(la_ref[...])
    tri = tri_ref[...]
    bcum = (jnp.dot(tri, la_hi, preferred_element_type=F32)
            + jnp.dot(tri, la_lo, preferred_element_type=F32))
    blk = blk_ref[...]
    bl_cols = (lax.dot_general(la_hi, blk, TN, preferred_element_type=F32)
               + lax.dot_general(la_lo, blk, TN, preferred_element_type=F32))
    q = qk_ref[:, 0:DK_TOT].astype(F32)
    k = qk_ref[:, DK_TOT:2 * DK_TOT].astype(F32)
    lane = lax.broadcasted_iota(jnp.int32, (CHUNK, LANES), 1)
    row2 = lax.broadcasted_iota(jnp.int32, (2 * CHUNK, CHUNK), 0) % CHUNK
    col2 = lax.broadcasted_iota(jnp.int32, (2 * CHUNK, CHUNK), 1)
    keep = (col2 >= row2) if reverse else (col2 <= row2)
    srow = lax.broadcasted_iota(jnp.int32, (LANES, LANES), 0)
    n_chunks = TM // CHUNK
    order = range(n_chunks - 1, -1, -1) if reverse else range(n_chunks)
    state = [s_scr[0], s_scr[1]]
    for c in order:
        r0 = c * CHUNK
        bc = bcum[r0:r0 + CHUNK]
        bl = bc[0:1] if reverse else bc[CHUNK - 1:CHUNK]
        qt = q[r0:r0 + CHUNK] * jnp.exp(bc) * (DK_HEAD ** -0.5)
        kt = k[r0:r0 + CHUNK] * jnp.exp(-bc)
        ke = k[r0:r0 + CHUNK] * jnp.exp(bl - bc)
        for p in range(2):
            cs = slice(p * LANES, (p + 1) * LANES)
            qt_p = qt[:, cs]
            qs = jnp.concatenate([jnp.where(lane < DK_HEAD, qt_p, 0.0),
                                  jnp.where(lane >= DK_HEAD, qt_p, 0.0)], axis=0).astype(BF16)
            att = lax.dot_general(qs, kt[:, cs].astype(BF16), NT, preferred_element_type=F32)
            att = jnp.where(keep, att, 0.0).astype(BF16)
            s_p = state[p]
            o_inter = jnp.dot(qs, s_p.astype(BF16), preferred_element_type=F32)
            ke_t = jnp.transpose(ke[:, cs]).astype(BF16)
            upd = []
            for hh in range(2):
                h = 2 * p + hh
                v_h = v_ref[r0:r0 + CHUNK, h * DV_HEAD:(h + 1) * DV_HEAD]
                both = jnp.dot(jnp.concatenate([att[hh * CHUNK:(hh + 1) * CHUNK], ke_t], axis=0), v_h,
                               preferred_element_type=F32)
                o_ref[r0:r0 + CHUNK, h * DV_HEAD:(h + 1) * DV_HEAD] = (
                    both[0:CHUNK] + o_inter[hh * CHUNK:(hh + 1) * CHUNK])
                upd.append(both[CHUNK:CHUNK + LANES])
            bl_col = bl_cols[p * LANES:(p + 1) * LANES, c * LANES:(c + 1) * LANES]
            state[p] = jnp.exp(bl_col) * s_p + jnp.where(srow < DK_HEAD, upd[0], upd[1])
    s_scr[0] = state[0]
    s_scr[1] = state[1]


def _gla_kernel(*refs, n_tiles, nb):
    seqs = [refs[6 * r:6 * r + 6] for r in range(nb)]
    tril_ref, triu_ref, blk_ref, s0_ref, of_ref, ob_ref, sout_ref, sf_scr, sb_scr = refs[6 * nb:]
    j = pl.program_id(1)

    @pl.when(j == 0)
    def _():
        sf_scr[...] = s0_ref[:, 0]
        sb_scr[...] = s0_ref[:, 1]

    for r, (qkf_ref, vf_ref, laf_ref, qkb_ref, vb_ref, lab_ref) in enumerate(seqs):
        _gla_direction(qkf_ref, vf_ref, laf_ref, tril_ref, blk_ref, of_ref.at[r], sf_scr.at[r], False)
        _gla_direction(qkb_ref, vb_ref, lab_ref, triu_ref, blk_ref, ob_ref.at[r], sb_scr.at[r], True)

    @pl.when(j == n_tiles - 1)
    def _():
        sout_ref[:, 0] = sf_scr[...]
        sout_ref[:, 1] = sb_scr[...]


def _gla(qk, v, la, s0, n_seq, n_tiles, tile_off):
    nb = 2 if n_seq % 2 == 0 else 1
    seq_len = n_tiles * TM
    blk = np.arange(TM) // CHUNK
    same = blk[:, None] == blk[None, :]
    r = np.arange(TM)
    tril = jnp.asarray((same & (r[None, :] <= r[:, None])).astype(np.float32)).astype(BF16)
    triu = jnp.asarray((same & (r[None, :] >= r[:, None])).astype(np.float32)).astype(BF16)
    chunk_of_lane = np.arange(TM // CHUNK * LANES) // LANES
    blkm = jnp.asarray((blk[:, None] == chunk_of_lane[None, :]).astype(np.float32)).astype(BF16)
    def tok(w, r, backward, cb=0):
        def index(s, j):
            jj = (n_tiles - 1 - j) if backward else j
            return ((s * nb + r) * n_tiles + jj + tile_off, cb)
        return pl.BlockSpec((TM, w), index)

    const = lambda a: pl.BlockSpec(a.shape, lambda s, j: (0,) * a.ndim)
    st = pl.BlockSpec((nb, 2, 2, LANES, LANES), lambda s, j: (s, 0, 0, 0, 0))
    seq_specs, seq_args = [], []
    for r in range(nb):
        seq_specs += [tok(2 * DK_TOT, r, False), tok(DV_TOT, r, False), tok(DK_TOT, r, False, 0),
                      tok(2 * DK_TOT, r, True), tok(DV_TOT, r, True), tok(DK_TOT, r, True, 1)]
        seq_args += [qk, v, la, qk, v, la]
    o_f, o_b, s_out = pl.pallas_call(
        functools.partial(_gla_kernel, n_tiles=n_tiles, nb=nb),
        grid=(n_seq // nb, n_tiles),
        in_specs=seq_specs + [const(tril), const(triu), const(blkm), st],
        out_specs=[pl.BlockSpec((nb, TM, DV_TOT), lambda s, j: (s, j, 0)),
                   pl.BlockSpec((nb, TM, DV_TOT), lambda s, j: (s, n_tiles - 1 - j, 0)), st],
        out_shape=[jax.ShapeDtypeStruct((n_seq, seq_len, DV_TOT), F32),
                   jax.ShapeDtypeStruct((n_seq, seq_len, DV_TOT), F32),
                   jax.ShapeDtypeStruct(s0.shape, F32)],
        scratch_shapes=[pltpu.VMEM((nb, 2, LANES, LANES), F32), pltpu.VMEM((nb, 2, LANES, LANES), F32)],
        compiler_params=_cparams(("arbitrary", "arbitrary"), VMEM_LIMIT),
        name="gla",
    )(*seq_args, tril, triu, blkm, s0)
    return o_f.reshape(n_seq * seq_len, DV_TOT), o_b.reshape(n_seq * seq_len, DV_TOT), s_out


def _mix_kernel(x_ref, u_ref, g_ref, of_ref, ob_ref, mod_ref, cw_ref, cb_ref, lng_ref, lnb_ref,
                gng_ref, wout_ref, n2g_ref, rwh_ref, rwl_ref, rb_ref, le_ref, ut_ref,
                x1_ref, grp_ref, route_ref, cnt_ref, pad_scr, shf_scr, cv_scr, *, seq_len):
    n_seq = TM // seq_len
    pr = seq_len + 2 * CONV_PAD
    qr = seq_len + 3 * SEG

    u = u_ref[...].astype(F32)
    hg = u[:, 0:C_CONV] * jax.nn.sigmoid(u[:, C_CONV:2 * C_CONV])
    zero = jnp.zeros((CONV_PAD, C_CONV), F32)
    for s in range(n_seq):
        pad_scr[s * pr:s * pr + CONV_PAD, :] = zero
        pad_scr[s * pr + CONV_PAD:s * pr + CONV_PAD + seq_len, :] = hg[s * seq_len:(s + 1) * seq_len]
        pad_scr[s * pr + CONV_PAD + seq_len:(s + 1) * pr, :] = zero
        for b in range(1, SEG):
            shf_scr[b - 1, s * qr:(s + 1) * qr, :] = pad_scr[s * pr + b:s * pr + b + qr, :]
    rb = 8 * SEG
    per_seq = seq_len // rb
    for gi in range(C_CONV // LANES):
        gs = slice(gi * LANES, (gi + 1) * LANES)
        wv = [jnp.broadcast_to(cw_ref[tap:tap + 1, gs], (SEG, LANES)) for tap in range(CONV_K)]

        def conv_rows(i, carry, gs=gs, wv=wv):
            s = lax.shift_right_logical(i, per_seq.bit_length() - 1)
            r0 = (i & (per_seq - 1)) * rb
            nj = rb // SEG
            acc = [None] * nj
            for b in range(SEG):
                taps = [(a, SEG * a + b - (CONV_PAD - CONV_K // 2)) for a in range(CONV_PAD * 2 // SEG)]
                taps = [(a, t) for a, t in taps if 0 <= t < CONV_K]
                loaded = {}
                for a, t in taps:
                    for j in range(nj):
                        if a + j not in loaded:
                            if b == 0:
                                row = pl.multiple_of(s * pr + r0 + (a + j) * SEG, SEG)
                                loaded[a + j] = pad_scr[pl.ds(row, SEG), gs]
                            else:
                                row = pl.multiple_of(s * qr + r0 + (a + j) * SEG, SEG)
                                loaded[a + j] = shf_scr[b - 1, pl.ds(row, SEG), gs]
                        term = wv[t] * loaded[a + j]
                        acc[j] = term if acc[j] is None else acc[j] + term
            for j in range(nj):
                cv_scr[pl.ds(pl.multiple_of(s * seq_len + r0 + j * SEG, SEG), SEG), gs] = acc[j]
            return carry

        lax.fori_loop(0, TM // rb, conv_rows, 0)
    cv = cv_scr[...] + cb_ref[...]
    mu = jnp.mean(cv, axis=-1, keepdims=True)
    var = jnp.mean(jnp.square(cv - mu), axis=-1, keepdims=True)
    cv = (cv - mu) * lax.rsqrt(var + EPS) * lng_ref[...] + lnb_ref[...]
    conv_out = (cv * jax.nn.sigmoid(cv)).astype(BF16)

    o = of_ref[...] + ob_ref[...]
    g = g_ref[...].astype(F32)
    heads = []
    for h in range(N_GLA_HEADS):
        cs = slice(h * DV_HEAD, (h + 1) * DV_HEAD)
        oh = o[:, cs]
        oh = oh * lax.rsqrt(jnp.mean(oh * oh, axis=-1, keepdims=True) + EPS) * gng_ref[...]
        gh = g[:, cs]
        heads.append((oh * (gh * jax.nn.sigmoid(gh))).astype(BF16))
    y = jnp.dot(conv_out, wout_ref[0:C_CONV, :], preferred_element_type=F32)
    for h in range(N_GLA_HEADS):
        y = y + jnp.dot(heads[h], wout_ref[C_CONV + h * DV_HEAD:C_CONV + (h + 1) * DV_HEAD, :],
                        preferred_element_type=F32)

    gate1 = mod_ref[0, :, 2 * D_MODEL:3 * D_MODEL]
    shift2 = mod_ref[0, :, 3 * D_MODEL:4 * D_MODEL]
    scale2 = mod_ref[0, :, 4 * D_MODEL:5 * D_MODEL]
    x1 = x_ref[...] + gate1 * y
    x1_ref[...] = x1
    h2 = _rms(x1, n2g_ref[...]) * (1.0 + scale2) + shift2
    h2_hi, h2_lo = _split(h2)

    ntdot = functools.partial(lax.dot_general, dimension_numbers=NT, preferred_element_type=F32)
    l = (ntdot(rwh_ref[...], h2_hi) + ntdot(rwh_ref[...], h2_lo) + ntdot(rwl_ref[...], h2_hi)
         + rb_ref[...])
    erow = lax.broadcasted_iota(jnp.int32, (N_EXPERTS, TM), 0).astype(F32)
    vals, idxs, hots = [], [], []
    for _ in range(TOP_K):
        m = jnp.max(l, axis=0, keepdims=True)
        idx = jnp.min(jnp.where(l == m, erow, float(N_EXPERTS)), axis=0, keepdims=True)
        hot = erow == idx
        l = jnp.where(hot, -jnp.inf, l)
        vals.append(m)
        idxs.append(idx)
        hots.append(hot)
    es = [jnp.exp(vk - vals[0]) for vk in vals]
    inv = 1.0 / (es[0] + es[1] + es[2] + es[3])
    sel = jnp.zeros((N_EXPERTS, TM), F32)
    for hot in hots:
        sel = jnp.where(hot, 1.0, sel)

    cnt = jnp.broadcast_to(jnp.sum(sel, axis=1, keepdims=True), (N_EXPERTS, LANES))
    cnt_ref[0] = cnt
    strip = jnp.ceil(cnt * (1.0 / SEG)) * float(SEG)
    starts = jnp.dot(le_ref[...], strip, precision=HI, preferred_element_type=F32)[:, 0:1]
    slot_of = jnp.dot(sel.astype(BF16), ut_ref[...], preferred_element_type=F32) + starts
    slots = [jnp.sum(jnp.where(hot, slot_of, 0.0), axis=0, keepdims=True) for hot in hots]

    rows = slots + idxs + [e * inv for e in es]
    srow = lax.broadcasted_iota(jnp.int32, (16, TM), 0)
    packed = jnp.zeros((16, TM), F32)
    for j, v in enumerate(rows):
        packed = jnp.where(srow == j, v, packed)
    packed = jnp.concatenate([packed, jnp.zeros((LANES - 16, TM), F32)], axis=0)
    route_ref[...] = jnp.transpose(packed)

    r = lax.broadcasted_iota(jnp.int32, (NR, TM), 0).astype(F32)
    perm = jnp.zeros((NR, TM), F32)
    for slot in slots:
        perm = jnp.where(slot == r, 1.0, perm)
    grp_ref[...] = jnp.dot(perm.astype(BF16), h2_hi, preferred_element_type=F32)


def _mix(x, u, g, o_f, o_b, mod3, seg_fn, tile_off, seq_len, wts):
    (conv_w, conv_b, ln_g, ln_b, gng, wout, n2g, rwh, rwl, rb) = wts
    t_path = x.shape[0]
    n = t_path // TM
    le = jnp.asarray(np.tril(np.ones((N_EXPERTS, N_EXPERTS), np.float32), -1))
    ut = jnp.asarray(np.triu(np.ones((TM, TM), np.float32), 1)).astype(BF16)
    const = lambda a: pl.BlockSpec(a.shape, lambda i: (0,) * a.ndim)
    loc = lambda w: pl.BlockSpec((TM, w), lambda i: (i, 0))
    uni = lambda w: pl.BlockSpec((TM, w), lambda i: (i + tile_off, 0))
    n_seq = TM // seq_len
    return pl.pallas_call(
        functools.partial(_mix_kernel, seq_len=seq_len),
        grid=(n,),
        in_specs=[loc(D_MODEL), uni(2 * C_CONV), uni(DV_TOT), loc(DV_TOT), loc(DV_TOT),
                  pl.BlockSpec((1, 1, 6 * D_MODEL), lambda i: (seg_fn(i), 0, 0)),
                  const(conv_w), const(conv_b), const(ln_g), const(ln_b), const(gng), const(wout),
                  const(n2g), const(rwh), const(rwl), const(rb), const(le), const(ut)],
        out_specs=[loc(D_MODEL), pl.BlockSpec((NR, D_MODEL), lambda i: (i, 0)), loc(LANES),
                   pl.BlockSpec((1, N_EXPERTS, LANES), lambda i: (i, 0, 0))],
        out_shape=[jax.ShapeDtypeStruct((t_path, D_MODEL), F32),
                   jax.ShapeDtypeStruct((n * NR, D_MODEL), F32),
                   jax.ShapeDtypeStruct((t_path, LANES), F32),
                   jax.ShapeDtypeStruct((n, N_EXPERTS, LANES), F32)],
        scratch_shapes=[pltpu.VMEM((n_seq * (seq_len + 2 * CONV_PAD), C_CONV), F32),
                        pltpu.VMEM((SEG - 1, n_seq * (seq_len + 3 * SEG), C_CONV), F32),
                        pltpu.VMEM((TM, C_CONV), F32)],
        compiler_params=_cparams(("arbitrary",), VMEM_LIMIT),
        name="mix",
    )(x, u, g, o_f, o_b, mod3, conv_w, conv_b, ln_g, ln_b, gng, wout, n2g, rwh, rwl, rb, le, ut)


def _strip_copies(tile, ls_ref, gs_ref, sg_ref, local, rows_hbm, sem, to_hbm, wait):
    def body(e, carry):
        j = tile * N_EXPERTS + e
        n = pl.multiple_of(sg_ref[j], SEG)

        @pl.when(n > 0)
        def _():
            loc = local.at[pl.ds(pl.multiple_of(ls_ref[j], SEG), n)]
            glob = rows_hbm.at[pl.ds(pl.multiple_of(gs_ref[j], SEG), n)]
            cp = pltpu.make_async_copy(loc, glob, sem) if to_hbm else pltpu.make_async_copy(glob, loc, sem)
            if wait:
                cp.wait()
            else:
                cp.start()
        return carry

    lax.fori_loop(0, N_EXPERTS, body, 0)


def _expert_kernel(te_ref, nv_ref, first_ref, next_ref, slot_ref, rows_ref, ptr_ref, mid_ref, src_ref, dst_ref,
                   len_ref, csrc_ref, clen_ref, grpc_hbm, grpl_hbm, wgu_hbm, bgu_ref, wdn_hbm, bdn_ref,
                   y_ref, xbuf, wgu_f, wdn_f, wgu_s, wdn_s, xsem, sem, *, ctx_rows):
    i = pl.program_id(0)

    def weight_copies(e, s):
        return (pltpu.make_async_copy(wgu_hbm.at[e], wgu_f.at[s], sem.at[0, s]),
                pltpu.make_async_copy(wdn_hbm.at[e], wdn_f.at[s], sem.at[1, s]))

    def row_copies(tile, xs, wait):
        def strip(hbm, src, dst, n):
            cp = pltpu.make_async_copy(hbm.at[pl.ds(pl.multiple_of(src, SEG), n)],
                                       xbuf.at[xs, pl.ds(pl.multiple_of(dst, SEG), n)], xsem.at[xs])
            cp.wait() if wait else cp.start()

        def strips_of(hbm, base):
            def body(j, carry):
                n = pl.multiple_of(len_ref[j], SEG)

                @pl.when(n > 0)
                def _():
                    strip(hbm, src_ref[j] - base, dst_ref[j], n)
                return carry
            return body

        lax.fori_loop(ptr_ref[tile], mid_ref[tile], strips_of(grpc_hbm, 0), 0)
        lax.fori_loop(mid_ref[tile], ptr_ref[tile + 1], strips_of(grpl_hbm, ctx_rows), 0)
        n = pl.multiple_of(clen_ref[tile], SEG)
        src = csrc_ref[tile]

        @pl.when(jnp.logical_and(n > 0, src < ctx_rows))
        def _():
            strip(grpc_hbm, src, 0, n)

        @pl.when(jnp.logical_and(n > 0, src >= ctx_rows))
        def _():
            strip(grpl_hbm, src - ctx_rows, 0, n)

    @pl.when(i == 0)
    def _():
        xbuf[...] = jnp.zeros(xbuf.shape, F32)
        row_copies(i, 0, False)

    @pl.when(i + 1 < nv_ref[0])
    def _():
        row_copies(i + 1, (i + 1) % 2, False)

    @pl.when(i < nv_ref[0])
    def _():
        e = te_ref[i]
        s = slot_ref[i]
        row_copies(i, i % 2, True)

        @pl.when(first_ref[i] == 1)
        def _():
            @pl.when(i == 0)
            def _():
                for cp in weight_copies(e, s):
                    cp.start()

            for cp in weight_copies(e, s):
                cp.wait()
            nxt = next_ref[i]

            @pl.when(nxt >= 0)
            def _():
                for cp in weight_copies(nxt, 1 - s):
                    cp.start()

            rows = 64

            def cast(r, carry):
                sl = pl.ds(pl.multiple_of(r * rows, rows), rows)
                wgu_s[sl, :] = wgu_f[s, sl, :].astype(BF16)
                wdn_s[sl, :] = wdn_f[s, sl, :].astype(BF16)
                return carry

            lax.fori_loop(0, D_MODEL // rows, cast, 0)

        def compute(m):
            x = xbuf[i % 2, 0:m, :].astype(BF16)
            gu = jnp.dot(x, wgu_s[...], preferred_element_type=F32) + bgu_ref[0]
            gate = jnp.minimum(gu[:, 0:D_FF], SWIGLU_LIMIT)
            up = jnp.clip(gu[:, D_FF:2 * D_FF], -SWIGLU_LIMIT, SWIGLU_LIMIT)
            act = (gate * jax.nn.sigmoid(SWIGLU_ALPHA * gate) * (up + 1.0)).astype(BF16)
            y_ref[0:m, :] = jnp.dot(act, wdn_s[...], preferred_element_type=F32) + bdn_ref[0]
            if m < TE:
                y_ref[m:TE, :] = jnp.zeros((TE - m, D_MODEL), F32)

        for m in range(TE // 4, TE + 1, TE // 4):
            pl.when(rows_ref[i] == m)(functools.partial(compute, m))

    @pl.when(i >= nv_ref[0])
    def _():
        y_ref[...] = jnp.zeros(y_ref.shape, y_ref.dtype)


def _experts(plan, grp_c, grp_l, n_rows, w_gu, b_gu, w_dn, b_dn):
    nt = n_rows // TE
    exp3 = lambda i, te, *_: (te[i], 0, 0)
    any_spec = pl.BlockSpec(memory_space=pl.ANY)
    return pl.pallas_call(
        functools.partial(_expert_kernel, ctx_rows=grp_c.shape[0]),
        grid_spec=pltpu.PrefetchScalarGridSpec(
            num_scalar_prefetch=13,
            grid=(nt,),
            in_specs=[any_spec, any_spec, any_spec,
                      pl.BlockSpec((1, 1, 2 * D_FF), exp3), any_spec,
                      pl.BlockSpec((1, 1, D_MODEL), exp3)],
            out_specs=pl.BlockSpec((TE, D_MODEL), lambda i, *_: (i, 0)),
            scratch_shapes=[pltpu.VMEM((2, TE, D_MODEL), F32),
                            pltpu.VMEM((2, D_MODEL, 2 * D_FF), F32), pltpu.VMEM((2, D_FF, D_MODEL), F32),
                            pltpu.VMEM((D_MODEL, 2 * D_FF), BF16), pltpu.VMEM((D_FF, D_MODEL), BF16),
                            pltpu.SemaphoreType.DMA((2,)), pltpu.SemaphoreType.DMA((2, 2))]),
        out_shape=jax.ShapeDtypeStruct((n_rows, D_MODEL), F32),
        compiler_params=_cparams(("arbitrary",), VMEM_LIMIT),
        name="experts",
    )(plan["tile_expert"], plan["n_valid"], plan["tile_first"], plan["tile_next"], plan["tile_slot"],
      plan["tile_rows"], plan["seg_ptr"], plan["seg_mid"], plan["seg_src"], plan["seg_dst"], plan["seg_len"],
      plan["carry_src"], plan["carry_len"], grp_c, grp_l, w_gu, b_gu.reshape(N_EXPERTS, 1, 2 * D_FF), w_dn, b_dn.reshape(N_EXPERTS, 1, D_MODEL))


def _final_kernel(ls_ref, gs_ref, sg_ref, x1_ref, route_ref, mod_ref, fg_ref, ys_ref, y_ref,
                  ybuf, sem, *, tile_off, n_tiles):
    i = pl.program_id(0)
    slot = i % 2

    def fetch(step, sl, wait):
        _strip_copies(step + tile_off, ls_ref, gs_ref, sg_ref, ybuf.at[sl], ys_ref, sem.at[sl], False, wait)

    @pl.when(i == 0)
    def _():
        ybuf[...] = jnp.zeros(ybuf.shape, F32)
        fetch(i, slot, False)

    @pl.when(i + 1 < n_tiles)
    def _():
        fetch(i + 1, 1 - slot, False)

    fetch(i, slot, True)

    route = route_ref[...]
    r = lax.broadcasted_iota(jnp.int32, (TM, NR), 1).astype(F32)
    comb = jnp.zeros((TM, NR), F32)
    for kk in range(TOP_K):
        gate = route[:, 2 * TOP_K + kk:2 * TOP_K + kk + 1]
        comb = jnp.where(route[:, kk:kk + 1] == r, gate, comb)
    moe = jnp.dot(comb.astype(BF16), ybuf[slot].astype(BF16), preferred_element_type=F32)
    gate2 = mod_ref[0, :, 5 * D_MODEL:6 * D_MODEL]
    y_ref[...] = _rms(x1_ref[...] + gate2 * moe, fg_ref[...])


def _final(plan, x1, route, mod3, seg_fn, tile_off, fg, ys):
    t_path = x1.shape[0]
    n = t_path // TM
    loc = lambda w: pl.BlockSpec((TM, w), lambda i, *_: (i, 0))
    return pl.pallas_call(
        functools.partial(_final_kernel, tile_off=tile_off, n_tiles=n),
        grid_spec=pltpu.PrefetchScalarGridSpec(
            num_scalar_prefetch=3,
            grid=(n,),
            in_specs=[loc(D_MODEL), loc(LANES),
                      pl.BlockSpec((1, 1, 6 * D_MODEL), lambda i, *_: (seg_fn(i), 0, 0)),
                      pl.BlockSpec((1, D_MODEL), lambda i, *_: (0, 0)),
                      pl.BlockSpec(memory_space=pl.ANY)],
            out_specs=loc(D_MODEL),
            scratch_shapes=[pltpu.VMEM((2, NR, D_MODEL), F32), pltpu.SemaphoreType.DMA((2,))]),
        out_shape=jax.ShapeDtypeStruct((t_path, D_MODEL), F32),
        compiler_params=_cparams(("arbitrary",), VMEM_LIMIT),
        name="final",
    )(plan["lstart"], plan["gstart"], plan["strip"], x1, route, mod3, fg, ys)


def _plan(tile_counts, n_rows, n_ctx):
    nt = n_rows // TE
    n_tok = tile_counts.shape[0]
    strip = (tile_counts + SEG - 1) // SEG * SEG
    lstart = jnp.cumsum(strip, axis=1) - strip
    rows_e = jnp.sum(strip, axis=0)
    rpad = (rows_e + TE - 1) // TE * TE
    ends = jnp.cumsum(rpad)
    offs = ends - rpad
    gstart = offs[None, :] + jnp.cumsum(strip, axis=0) - strip
    n_valid = (ends[-1] // TE).astype(jnp.int32)
    tile_ids = jnp.minimum(jnp.arange(nt, dtype=jnp.int32), n_valid - 1)
    tile_expert = jnp.sum((ends[None, :] <= (tile_ids * TE)[:, None]).astype(jnp.int32), axis=1)
    tile_expert = jnp.minimum(tile_expert, N_EXPERTS - 1).astype(jnp.int32)
    prev = jnp.concatenate([jnp.full((1,), -1, jnp.int32), tile_expert[:-1]])
    tile_first = (tile_expert != prev).astype(jnp.int32)
    tile_slot = (jnp.cumsum(tile_first) - 1) % 2
    pick = lambda table, idx: jnp.sum(
        jnp.where(idx[:, None] == jnp.arange(table.shape[0], dtype=jnp.int32)[None, :], table[None, :], 0), axis=1)
    used = jnp.clip(pick(offs + rows_e, tile_expert) - jnp.arange(nt, dtype=jnp.int32) * TE, 0, TE)
    tile_rows = jnp.maximum((used + TE // 4 - 1) // (TE // 4), 1) * (TE // 4)
    after = pick(ends, tile_expert) // TE
    tile_next = jnp.where(after < n_valid, pick(tile_expert, jnp.minimum(after, nt - 1)), -1)
    tok = jnp.broadcast_to(jnp.arange(n_tok, dtype=jnp.int32)[:, None], strip.shape)
    g0 = gstart.T.reshape(-1)
    ln = strip.T.reshape(-1)
    src0 = (tok * NR + lstart).T.reshape(-1)
    is_ctx = (tok < n_ctx).T.reshape(-1)
    len1 = jnp.minimum(ln, TE - g0 % TE)
    tile0 = g0 // TE
    tiles = jnp.arange(nt + 1, dtype=jnp.int32)[:, None]
    count = lambda m: jnp.sum(m.astype(jnp.int32), axis=1)
    seg_ptr = count(tile0[None, :] < tiles)
    seg_mid = seg_ptr + count((tile0[None, :] == tiles) & is_ctx[None, :])
    into = ((g0 + len1) // TE)[None, :] == tiles[:nt]
    carry_len = jnp.sum(jnp.where(into, (ln - len1)[None, :], 0), axis=1)
    carry_src = jnp.sum(jnp.where(into & (ln > len1)[None, :], (src0 + len1)[None, :], 0), axis=1)
    i32 = lambda a: a.reshape(-1).astype(jnp.int32)
    return dict(strip=i32(strip), lstart=i32(lstart), gstart=i32(gstart), n_valid=n_valid.reshape(1),
                tile_expert=tile_expert, tile_first=i32(tile_first), tile_next=i32(tile_next),
                tile_slot=i32(tile_slot), tile_rows=i32(tile_rows), seg_ptr=i32(seg_ptr), seg_mid=i32(seg_mid), seg_src=i32(src0),
                seg_dst=i32(g0 % TE), seg_len=i32(len1), carry_src=i32(carry_src),
                carry_len=i32(carry_len))


def _layer(x_prompt, x_sample, state, c, c_ctx, ada_w, ada_b, norm1_g, norm2_g, w_in, conv_w, conv_b,
           conv_ln_g, conv_ln_b, gate_w, gate_b, gla_norm_g, w_out, router_w, router_b,
           moe_w_gu, moe_b_gu, moe_w_dn, moe_b_dn, final_g):
    bp, lp, d = x_prompt.shape
    bs, ls, _ = x_sample.shape
    assert lp == TM and ls % TM == 0 and d == D_MODEL
    xp = x_prompt.reshape(bp * lp, d)
    xs = x_sample.reshape(bs * ls, d)
    n_c, n_l = bp, bs * ls // TM
    lat_tiles = ls // TM
    t_all = (n_c + n_l) * TM

    n_cond = 1 + bs
    cond_t = jnp.zeros((d, 8), F32).at[:, 0].set(c_ctx).at[:, 1:n_cond].set(c.T)
    mod3 = _modulation(cond_t, ada_w, ada_b, n_cond).reshape(8, 1, 6 * d)

    row = lambda a: a.reshape(1, -1)
    u, qk, v, g, la = _inproj(xp, xs, mod3, row(norm1_g), w_in, gate_w, gate_b, ls)

    pair = lambda s: s.reshape(s.shape[0], 2, 2, LANES, LANES)
    of_c, ob_c, s_ctx = _gla(qk, v, la, jnp.zeros((bp, 2, 2, LANES, LANES), F32), bp, 1, 0)
    of_l, ob_l, _ = _gla(qk, v, la, pair(state), bs, lat_tiles, n_c)

    cw = jnp.zeros((32, C_CONV), F32).at[:CONV_K].set(conv_w)
    rwh, rwl = _split(router_w.T)
    rb = jnp.broadcast_to(router_b[:, None], (N_EXPERTS, TM))
    wts = (cw, row(conv_b), row(conv_ln_g), row(conv_ln_b), row(gla_norm_g), w_out.astype(BF16),
           row(norm2_g), rwh, rwl, rb)
    seg_c = lambda i: 0
    seg_l = lambda i: 1 + i // lat_tiles
    x1_c, grp_c, route_c, cnt_c = _mix(xp, u, g, of_c, ob_c, mod3, seg_c, 0, lp, wts)
    x1_l, grp_l, route_l, cnt_l = _mix(xs, u, g, of_l, ob_l, mod3, seg_l, n_c, GRID_W, wts)

    tile_counts = jnp.concatenate([cnt_c[:, :, 0], cnt_l[:, :, 0]]).astype(jnp.int32)
    n_rows = TOP_K * t_all + (SEG - 1) * N_EXPERTS * (n_c + n_l) + N_EXPERTS * TE
    n_rows = (n_rows + TE - 1) // TE * TE
    plan = _plan(tile_counts, n_rows, n_c)
    ysrt = _experts(plan, grp_c, grp_l, n_rows, moe_w_gu, moe_b_gu, moe_w_dn, moe_b_dn)
    y_c = _final(plan, x1_c, route_c, mod3, seg_c, 0, row(final_g), ysrt)
    y_l = _final(plan, x1_l, route_l, mod3, seg_l, n_c, row(final_g), ysrt)
    new_state = s_ctx.reshape(bp, 1, 2, N_GLA_HEADS, DK_HEAD, DV_HEAD)
    return y_c.reshape(bp, lp, d), y_l.reshape(bs, ls, d), new_state


def kernel(x_prompt, x_sample, state_gla, c, c_ctx, ada_w, ada_b, norm1_g, norm2_g, w_in, conv_w,
           conv_b, conv_ln_g, conv_ln_b, gate_w, gate_b, gla_norm_g, w_out, router_w, router_b,
           moe_w_gu, moe_b_gu, moe_w_dn, moe_b_dn, final_g):
    assert ada_w.shape[0] == 1, "single-layer step"
    return _layer(x_prompt, x_sample, state_gla[:, 0], c, c_ctx, ada_w[0], ada_b[0], norm1_g[0],
                  norm2_g[0], w_in[0], conv_w[0], conv_b[0], conv_ln_g[0], conv_ln_b[0], gate_w[0],
                  gate_b[0], gla_norm_g[0], w_out[0], router_w[0], router_b[0], moe_w_gu[0],
                  moe_b_gu[0], moe_w_dn[0], moe_b_dn[0], final_g)
```

```python
import functools

import numpy as np
import jax
import jax.numpy as jnp
from jax import lax
from jax.experimental import pallas as pl
from jax.experimental.pallas import tpu as pltpu

D_MODEL = 1024
GRID_W = 64
C_CONV = D_MODEL // 2
CONV_K = 31
N_GLA_HEADS = 4
DV_HEAD = 128
DK_HEAD = 64
DK_TOT = DK_HEAD * N_GLA_HEADS
DV_TOT = DV_HEAD * N_GLA_HEADS
GATE_RANK = 16
GATE_TEMP = 16.0
CHUNK = 64
N_EXPERTS = 32
TOP_K = 4
D_FF = D_MODEL
SWIGLU_LIMIT = 7.0
SWIGLU_ALPHA = 1.702
EPS = 1e-6

LANES = 128
SEG = 8
TM = 256
TE = 512
NR = TOP_K * TM + TM
CONV_PAD = 16
VMEM_LIMIT = 56 * 1024 * 1024

F32 = jnp.float32
BF16 = jnp.bfloat16
HI = lax.Precision.HIGHEST
NEG = -1e30
TN = (((0,), (0,)), ((), ()))
NT = (((1,), (1,)), ((), ()))

assert N_EXPERTS * (SEG - 1) <= NR - TOP_K * TM


def _split(x):
    hi = x.astype(BF16)
    return hi, (x - hi.astype(F32)).astype(BF16)


def _cparams(sem, vmem=None):
    return pltpu.CompilerParams(dimension_semantics=sem, vmem_limit_bytes=vmem)


def _mod_kernel(ct_ref, w_ref, b_ref, o_ref, *, n_cond):
    ct = ct_ref[...]
    s = ct * jax.nn.sigmoid(ct)
    w = w_ref[...]
    rows = [jnp.sum(s[:, r:r + 1] * w, axis=0, keepdims=True) + b_ref[...] for r in range(n_cond)]
    rows.append(jnp.zeros((8 - n_cond, w.shape[1]), F32))
    o_ref[...] = jnp.concatenate(rows, axis=0)


def _modulation(cond_t, ada_w, ada_b, n_cond):
    d, n = ada_w.shape
    nb = 768
    return pl.pallas_call(
        functools.partial(_mod_kernel, n_cond=n_cond),
        grid=(n // nb,),
        in_specs=[pl.BlockSpec((d, 8), lambda i: (0, 0)),
                  pl.BlockSpec((d, nb), lambda i: (0, i)),
                  pl.BlockSpec((1, nb), lambda i: (0, i))],
        out_specs=pl.BlockSpec((8, nb), lambda i: (0, i)),
        out_shape=jax.ShapeDtypeStruct((8, n), F32),
        compiler_params=_cparams(("arbitrary",)),
        name="mod",
    )(cond_t, ada_w, ada_b.reshape(1, n))


def _rms(x, g):
    return x * lax.rsqrt(jnp.mean(x * x, axis=-1, keepdims=True) + EPS) * g


def _inproj_kernel(xp_ref, xs_ref, mod_ref, g1_ref, wu_ref, wqk_ref, wv_ref, wg_ref, wlow_ref,
                   gwh_ref, gwl_ref, gb_ref, u_ref, qk_ref, v_ref, g_ref, la_ref, *, n_ctx_tiles):
    i = pl.program_id(0)
    shift = mod_ref[0, :, 0:D_MODEL]
    scale = mod_ref[0, :, D_MODEL:2 * D_MODEL]
    dot = functools.partial(jnp.dot, preferred_element_type=F32)
    rows = u_ref.shape[0]
    sub = min(rows, TM)
    def block(r0):
        rs = slice(r0, r0 + sub)
        x = jnp.where(i < n_ctx_tiles, xp_ref[rs, :], xs_ref[rs, :])
        h = (_rms(x, g1_ref[...]) * (1.0 + scale) + shift).astype(BF16)
        yield
        u_ref[rs, :] = dot(h, wu_ref[...]).astype(BF16)
        yield
        qk_ref[rs, :] = dot(h, wqk_ref[...]).astype(BF16)
        v_ref[rs, :] = dot(h, wv_ref[...]).astype(BF16)
        yield
        g_ref[rs, :] = dot(h, wg_ref[...]).astype(BF16)
        low = dot(h, wlow_ref[...])
        yield
        low_hi, low_lo = _split(low)
        z = (dot(low_hi, gwh_ref[...]) + dot(low_lo, gwh_ref[...]) + dot(low_hi, gwl_ref[...])
             + gb_ref[...])
        la_ref[rs, :] = (jnp.minimum(z, 0.0) - jnp.log1p(jnp.exp(-jnp.abs(z)))) * (1.0 / GATE_TEMP)

    blocks = [block(r0) for r0 in range(0, rows, sub)]
    while blocks:
        blocks = [gen for gen in blocks if next(gen, "done") != "done"]


def _inproj(xp, xs, mod3, g1, w_in, gate_w, gate_b, lat_len):
    ti = next(t for t in (1024, 512, TM) if xp.shape[0] % t == 0 and lat_len % t == 0)
    lat_tiles_per_seq = lat_len // ti
    n_c, n_l = xp.shape[0] // ti, xs.shape[0] // ti
    t_all = xp.shape[0] + xs.shape[0]
    sp = np.cumsum([0, C_CONV, C_CONV, DK_TOT, DK_TOT, DV_TOT, DV_TOT, 2 * GATE_RANK])
    wb = w_in.astype(BF16)
    wu, wqk, wv, wg, wlow = (wb[:, sp[0]:sp[2]], wb[:, sp[2]:sp[4]], wb[:, sp[4]:sp[5]],
                             wb[:, sp[5]:sp[6]], wb[:, sp[6]:sp[7]])
    gw = jnp.zeros((2 * GATE_RANK, 2 * DK_TOT), F32)
    gw = gw.at[:GATE_RANK, :DK_TOT].set(gate_w[0]).at[GATE_RANK:, DK_TOT:].set(gate_w[1])
    gwh, gwl = _split(gw)
    gb = gate_b.reshape(1, 2 * DK_TOT)
    const = lambda a: pl.BlockSpec(a.shape, lambda i: (0,) * a.ndim)
    row = lambda w: pl.BlockSpec((ti, w), lambda i: (i, 0))
    seg = lambda i: jnp.where(i < n_c, 0, 1 + jnp.maximum(i - n_c, 0) // lat_tiles_per_seq)
    return pl.pallas_call(
        functools.partial(_inproj_kernel, n_ctx_tiles=n_c),
        grid=(n_c + n_l,),
        in_specs=[pl.BlockSpec((ti, D_MODEL), lambda i: (jnp.minimum(i, n_c - 1), 0)),
                  pl.BlockSpec((ti, D_MODEL), lambda i: (jnp.maximum(i - n_c, 0), 0)),
                  pl.BlockSpec((1, 1, 6 * D_MODEL), lambda i: (seg(i), 0, 0)),
                  const(g1), const(wu), const(wqk), const(wv), const(wg), const(wlow),
                  const(gwh), const(gwl), const(gb)],
        out_specs=[row(2 * C_CONV), row(2 * DK_TOT), row(DV_TOT), row(DV_TOT), row(2 * DK_TOT)],
        out_shape=[jax.ShapeDtypeStruct((t_all, 2 * C_CONV), BF16),
                   jax.ShapeDtypeStruct((t_all, 2 * DK_TOT), BF16),
                   jax.ShapeDtypeStruct((t_all, DV_TOT), BF16),
                   jax.ShapeDtypeStruct((t_all, DV_TOT), BF16),
                   jax.ShapeDtypeStruct((t_all, 2 * DK_TOT), F32)],
        compiler_params=_cparams(("arbitrary",), VMEM_LIMIT),
        name="inproj",
    )(xp, xs, mod3, g1, wu, wqk, wv, wg, wlow, gwh, gwl, gb)


def _gla_direction(qk_ref, v_ref, la_ref, tri_ref, blk_ref, o_ref, s_scr, reverse):
    la_hi, la_lo = _split(la_ref[...])
    tri = tri_ref[...]
    bcum = (jnp.dot(tri, la_hi, preferred_element_type=F32)
            + jnp.dot(tri, la_lo, preferred_element_type=F32))
    blk = blk_ref[...]
    bl_cols = (lax.dot_general(la_hi, blk, TN, preferred_element_type=F32)
               + lax.dot_general(la_lo, blk, TN, preferred_element_type=F32))
    yield
    q = qk_ref[:, 0:DK_TOT].astype(F32)
    k = qk_ref[:, DK_TOT:2 * DK_TOT].astype(F32)
    lane = lax.broadcasted_iota(jnp.int32, (CHUNK, LANES), 1)
    row2 = lax.broadcasted_iota(jnp.int32, (2 * CHUNK, CHUNK), 0) % CHUNK
    col2 = lax.broadcasted_iota(jnp.int32, (2 * CHUNK, CHUNK), 1)
    keep = (col2 >= row2) if reverse else (col2 <= row2)
    srow = lax.broadcasted_iota(jnp.int32, (LANES, LANES), 0)
    n_chunks = TM // CHUNK
    order = range(n_chunks - 1, -1, -1) if reverse else range(n_chunks)
    state = [s_scr[0], s_scr[1]]
    for c in order:
        r0 = c * CHUNK
        bc = bcum[r0:r0 + CHUNK]
        bl = bc[0:1] if reverse else bc[CHUNK - 1:CHUNK]
        qt = q[r0:r0 + CHUNK] * jnp.exp(bc) * (DK_HEAD ** -0.5)
        kt = k[r0:r0 + CHUNK] * jnp.exp(-bc)
        ke = k[r0:r0 + CHUNK] * jnp.exp(bl - bc)
        yield
        for p in range(2):
            cs = slice(p * LANES, (p + 1) * LANES)
            qt_p = qt[:, cs]
            qs = jnp.concatenate([jnp.where(lane < DK_HEAD, qt_p, 0.0),
                                  jnp.where(lane >= DK_HEAD, qt_p, 0.0)], axis=0).astype(BF16)
            att = lax.dot_general(qs, kt[:, cs].astype(BF16), NT, preferred_element_type=F32)
            att = jnp.where(keep, att, 0.0).astype(BF16)
            s_p = state[p]
            o_inter = jnp.dot(qs, s_p.astype(BF16), preferred_element_type=F32)
            ke_t = jnp.transpose(ke[:, cs]).astype(BF16)
            upd = []
            for hh in range(2):
                h = 2 * p + hh
                v_h = v_ref[r0:r0 + CHUNK, h * DV_HEAD:(h + 1) * DV_HEAD]
                both = jnp.dot(jnp.concatenate([att[hh * CHUNK:(hh + 1) * CHUNK], ke_t], axis=0), v_h,
                               preferred_element_type=F32)
                o_ref[r0:r0 + CHUNK, h * DV_HEAD:(h + 1) * DV_HEAD] = (
                    both[0:CHUNK] + o_inter[hh * CHUNK:(hh + 1) * CHUNK])
                upd.append(both[CHUNK:CHUNK + LANES])
            bl_col = bl_cols[p * LANES:(p + 1) * LANES, c * LANES:(c + 1) * LANES]
            state[p] = jnp.exp(bl_col) * s_p + jnp.where(srow < DK_HEAD, upd[0], upd[1])
            yield
    s_scr[0] = state[0]
    s_scr[1] = state[1]


def _gla_kernel(*refs, n_tiles, nb):
    seqs = [refs[6 * r:6 * r + 6] for r in range(nb)]
    tril_ref, triu_ref, blk_ref, s0_ref, of_ref, ob_ref, sout_ref, sf_scr, sb_scr = refs[6 * nb:]
    j = pl.program_id(1)

    @pl.when(j == 0)
    def _():
        sf_scr[...] = s0_ref[:, 0]
        sb_scr[...] = s0_ref[:, 1]

    scans = []
    for r, (qkf_ref, vf_ref, laf_ref, qkb_ref, vb_ref, lab_ref) in enumerate(seqs):
        scans.append(_gla_direction(qkf_ref, vf_ref, laf_ref, tril_ref, blk_ref, of_ref.at[r], sf_scr.at[r], False))
        scans.append(_gla_direction(qkb_ref, vb_ref, lab_ref, triu_ref, blk_ref, ob_ref.at[r], sb_scr.at[r], True))
    while scans:
        scans = [gen for gen in scans if next(gen, "done") != "done"]

    @pl.when(j == n_tiles - 1)
    def _():
        sout_ref[:, 0] = sf_scr[...]
        sout_ref[:, 1] = sb_scr[...]


def _gla(qk, v, la, s0, n_seq, n_tiles, tile_off):
    nb = 2 if n_seq % 2 == 0 else 1
    seq_len = n_tiles * TM
    blk = np.arange(TM) // CHUNK
    same = blk[:, None] == blk[None, :]
    r = np.arange(TM)
    tril = jnp.asarray((same & (r[None, :] <= r[:, None])).astype(np.float32)).astype(BF16)
    triu = jnp.asarray((same & (r[None, :] >= r[:, None])).astype(np.float32)).astype(BF16)
    chunk_of_lane = np.arange(TM // CHUNK * LANES) // LANES
    blkm = jnp.asarray((blk[:, None] == chunk_of_lane[None, :]).astype(np.float32)).astype(BF16)
    def tok(w, r, backward, cb=0):
        def index(s, j):
            jj = (n_tiles - 1 - j) if backward else j
            return ((s * nb + r) * n_tiles + jj + tile_off, cb)
        return pl.BlockSpec((TM, w), index)

    const = lambda a: pl.BlockSpec(a.shape, lambda s, j: (0,) * a.ndim)
    st = pl.BlockSpec((nb, 2, 2, LANES, LANES), lambda s, j: (s, 0, 0, 0, 0))
    seq_specs, seq_args = [], []
    for r in range(nb):
        seq_specs += [tok(2 * DK_TOT, r, False), tok(DV_TOT, r, False), tok(DK_TOT, r, False, 0),
                      tok(2 * DK_TOT, r, True), tok(DV_TOT, r, True), tok(DK_TOT, r, True, 1)]
        seq_args += [qk, v, la, qk, v, la]
    o_f, o_b, s_out = pl.pallas_call(
        functools.partial(_gla_kernel, n_tiles=n_tiles, nb=nb),
        grid=(n_seq // nb, n_tiles),
        in_specs=seq_specs + [const(tril), const(triu), const(blkm), st],
        out_specs=[pl.BlockSpec((nb, TM, DV_TOT), lambda s, j: (s, j, 0)),
                   pl.BlockSpec((nb, TM, DV_TOT), lambda s, j: (s, n_tiles - 1 - j, 0)), st],
        out_shape=[jax.ShapeDtypeStruct((n_seq, seq_len, DV_TOT), F32),
                   jax.ShapeDtypeStruct((n_seq, seq_len, DV_TOT), F32),
                   jax.ShapeDtypeStruct(s0.shape, F32)],
        scratch_shapes=[pltpu.VMEM((nb, 2, LANES, LANES), F32), pltpu.VMEM((nb, 2, LANES, LANES), F32)],
        compiler_params=_cparams(("arbitrary", "arbitrary"), VMEM_LIMIT),
        name="gla",
    )(*seq_args, tril, triu, blkm, s0)
    return o_f.reshape(n_seq * seq_len, DV_TOT), o_b.reshape(n_seq * seq_len, DV_TOT), s_out


def _mix_conv(u_ref, cw_ref, pad_scr, shf_scr, cv_scr, seq_len):
    n_seq = TM // seq_len
    pr = seq_len + 2 * CONV_PAD
    qr = seq_len + 3 * SEG

    u = u_ref[...].astype(F32)
    hg = u[:, 0:C_CONV] * jax.nn.sigmoid(u[:, C_CONV:2 * C_CONV])
    zero = jnp.zeros((CONV_PAD, C_CONV), F32)
    for s in range(n_seq):
        pad_scr[s * pr:s * pr + CONV_PAD, :] = zero
        pad_scr[s * pr + CONV_PAD:s * pr + CONV_PAD + seq_len, :] = hg[s * seq_len:(s + 1) * seq_len]
        pad_scr[s * pr + CONV_PAD + seq_len:(s + 1) * pr, :] = zero
        for b in range(1, SEG):
            shf_scr[b - 1, s * qr:(s + 1) * qr, :] = pad_scr[s * pr + b:s * pr + b + qr, :]
    rb = 8 * SEG
    per_seq = seq_len // rb
    for gi in range(C_CONV // LANES):
        gs = slice(gi * LANES, (gi + 1) * LANES)
        wv = [jnp.broadcast_to(cw_ref[tap:tap + 1, gs], (SEG, LANES)) for tap in range(CONV_K)]

        def conv_rows(i, carry, gs=gs, wv=wv):
            s = lax.shift_right_logical(i, per_seq.bit_length() - 1)
            r0 = (i & (per_seq - 1)) * rb
            nj = rb // SEG
            acc = [None] * nj
            for b in range(SEG):
                taps = [(a, SEG * a + b - (CONV_PAD - CONV_K // 2)) for a in range(CONV_PAD * 2 // SEG)]
                taps = [(a, t) for a, t in taps if 0 <= t < CONV_K]
                loaded = {}
                for a, t in taps:
                    for j in range(nj):
                        if a + j not in loaded:
                            if b == 0:
                                row = pl.multiple_of(s * pr + r0 + (a + j) * SEG, SEG)
                                loaded[a + j] = pad_scr[pl.ds(row, SEG), gs]
                            else:
                                row = pl.multiple_of(s * qr + r0 + (a + j) * SEG, SEG)
                                loaded[a + j] = shf_scr[b - 1, pl.ds(row, SEG), gs]
                        term = wv[t] * loaded[a + j]
                        acc[j] = term if acc[j] is None else acc[j] + term
            for j in range(nj):
                cv_scr[pl.ds(pl.multiple_of(s * seq_len + r0 + j * SEG, SEG), SEG), gs] = acc[j]
            return carry

        lax.fori_loop(0, TM // rb, conv_rows, 0)


def _mix_rest(x_ref, g_ref, of_ref, ob_ref, mod_ref, cv_scr, cb_ref, lng_ref, lnb_ref, gng_ref, wout_ref,
              n2g_ref, rwh_ref, rwl_ref, rb_ref, le_ref, ut_ref, x1_ref, grp_ref, route_ref, cnt_ref):
    cv = cv_scr[...] + cb_ref[...]
    mu = jnp.mean(cv, axis=-1, keepdims=True)
    var = jnp.mean(jnp.square(cv - mu), axis=-1, keepdims=True)
    cv =(cv - mu) * lax.rsqrt(var + EPS) * lng_ref[...] + lnb_ref[...]
    conv_out = (cv * jax.nn.sigmoid(cv)).astype(BF16)
    yield

    o = of_ref[...] + ob_ref[...]
    g = g_ref[...].astype(F32)
    heads = []
    for h in range(N_GLA_HEADS):
        cs = slice(h * DV_HEAD, (h + 1) * DV_HEAD)
        oh = o[:, cs]
        oh = oh * lax.rsqrt(jnp.mean(oh * oh, axis=-1, keepdims=True) + EPS) * gng_ref[...]
        gh = g[:, cs]
        heads.append((oh * (gh * jax.nn.sigmoid(gh))).astype(BF16))
    yield
    y = jnp.dot(conv_out, wout_ref[0:C_CONV, :], preferred_element_type=F32)
    for h in range(N_GLA_HEADS):
        y = y + jnp.dot(heads[h], wout_ref[C_CONV + h * DV_HEAD:C_CONV + (h + 1) * DV_HEAD, :],
                        preferred_element_type=F32)
    yield

    gate1 = mod_ref[0, :, 2 * D_MODEL:3 * D_MODEL]
    shift2 = mod_ref[0, :, 3 * D_MODEL:4 * D_MODEL]
    scale2 = mod_ref[0, :, 4 * D_MODEL:5 * D_MODEL]
    x1 = x_ref[...] + gate1 * y
    x1_ref[...] = x1
    h2 = _rms(x1, n2g_ref[...]) * (1.0 + scale2) + shift2
    h2_hi, h2_lo = _split(h2)
    yield

    ntdot = functools.partial(lax.dot_general, dimension_numbers=NT, preferred_element_type=F32)
    l = (ntdot(rwh_ref[...], h2_hi) + ntdot(rwh_ref[...], h2_lo) + ntdot(rwl_ref[...], h2_hi)
         + rb_ref[...])
    yield
    erow = lax.broadcasted_iota(jnp.int32, (N_EXPERTS, TM), 0).astype(F32)
    vals, idxs, hots = [], [], []
    for _ in range(TOP_K):
        m = jnp.max(l, axis=0, keepdims=True)
        idx = jnp.min(jnp.where(l == m, erow, float(N_EXPERTS)), axis=0, keepdims=True)
        hot = erow == idx
        l = jnp.where(hot, -jnp.inf, l)
        vals.append(m)
        idxs.append(idx)
        hots.append(hot)
        yield
    es = [jnp.exp(vk - vals[0]) for vk in vals]
    inv = 1.0 / (es[0] + es[1] + es[2] + es[3])
    sel = jnp.zeros((N_EXPERTS, TM), F32)
    for hot in hots:
        sel = jnp.where(hot, 1.0, sel)

    cnt = jnp.broadcast_to(jnp.sum(sel, axis=1, keepdims=True), (N_EXPERTS, LANES))
    cnt_ref[0] = cnt
    strip = jnp.ceil(cnt * (1.0 / SEG)) * float(SEG)
    starts = jnp.dot(le_ref[...], strip, precision=HI, preferred_element_type=F32)[:, 0:1]
    slot_of = jnp.dot(sel.astype(BF16), ut_ref[...], preferred_element_type=F32) + starts
    slots = [jnp.sum(jnp.where(hot, slot_of, 0.0), axis=0, keepdims=True) for hot in hots]

    rows = slots + idxs + [e * inv for e in es]
    srow = lax.broadcasted_iota(jnp.int32, (16, TM), 0)
    packed = jnp.zeros((16, TM), F32)
    for j, v in enumerate(rows):
        packed = jnp.where(srow == j, v, packed)
    packed = jnp.concatenate([packed, jnp.zeros((LANES - 16, TM), F32)], axis=0)
    route_ref[...] = jnp.transpose(packed)
    yield

    r = lax.broadcasted_iota(jnp.int32, (NR, TM), 0).astype(F32)
    perm = jnp.zeros((NR, TM), F32)
    for slot in slots:
        perm = jnp.where(slot == r, 1.0, perm)
    yield
    grp_ref[...] = jnp.dot(perm.astype(BF16), h2_hi, preferred_element_type=F32)


def _mix_kernel(*refs, seq_lens):
    k = len(seq_lens)
    ins = [refs[6 * p:6 * p + 6] for p in range(k)]
    cw_ref, *shared = refs[6 * k:6 * k + 12]
    outs = [refs[6 * k + 12 + 4 * p:6 * k + 16 + 4 * p] for p in range(k)]
    scr = [refs[10 * k + 12 + 3 * p:10 * k + 15 + 3 * p] for p in range(k)]
    for p in range(k):
        _mix_conv(ins[p][1], cw_ref, *scr[p], seq_lens[p])
    stages = []
    for p in range(k):
        x_ref, _, g_ref, of_ref, ob_ref, mod_ref = ins[p]
        stages.append(_mix_rest(x_ref, g_ref, of_ref, ob_ref, mod_ref, scr[p][2], *shared, *outs[p]))
    while stages:
        stages = [gen for gen in stages if next(gen, "done") != "done"]


def _mix(paths, u, g, mod3, wts):
    (conv_w, conv_b, ln_g, ln_b, gng, wout, n2g, rwh, rwl, rb) = wts
    le = jnp.asarray(np.tril(np.ones((N_EXPERTS, N_EXPERTS), np.float32), -1))
    ut = jnp.asarray(np.triu(np.ones((TM, TM), np.float32), 1)).astype(BF16)
    shared = (conv_w, conv_b, ln_g, ln_b, gng, wout, n2g, rwh, rwl, rb, le, ut)
    const = lambda a: pl.BlockSpec(a.shape, lambda i: (0,) * a.ndim)
    n_steps = max(x.shape[0] // TM for x, *_ in paths)
    in_specs, args, out_specs, out_shape, scratch = [], [], [], [], []
    for x, o_f, o_b, seg_fn, tile_off, seq_len in paths:
        n = x.shape[0] // TM
        cur = lambda i, n=n: jnp.minimum(i, n - 1)
        loc = lambda w, cur=cur: pl.BlockSpec((TM, w), lambda i: (cur(i), 0))
        uni = lambda w, cur=cur, off=tile_off: pl.BlockSpec((TM, w), lambda i: (cur(i) + off, 0))
        in_specs += [loc(D_MODEL), uni(2 * C_CONV), uni(DV_TOT), loc(DV_TOT), loc(DV_TOT),
                     pl.BlockSpec((1, 1, 6 * D_MODEL), lambda i, cur=cur, f=seg_fn: (f(cur(i)), 0, 0))]
        args += [x, u, g, o_f, o_b, mod3]
        out_specs += [loc(D_MODEL), pl.BlockSpec((NR, D_MODEL), lambda i, cur=cur: (cur(i), 0)), loc(LANES),
                      pl.BlockSpec((1, N_EXPERTS, LANES), lambda i, cur=cur: (cur(i), 0, 0))]
        out_shape += [jax.ShapeDtypeStruct((n * TM, D_MODEL), F32),
                      jax.ShapeDtypeStruct((n * NR, D_MODEL), F32),
                      jax.ShapeDtypeStruct((n * TM, LANES), F32),
                      jax.ShapeDtypeStruct((n, N_EXPERTS, LANES), F32)]
        n_seq = TM // seq_len
        scratch += [pltpu.VMEM((n_seq * (seq_len + 2 * CONV_PAD), C_CONV), F32),
                    pltpu.VMEM((SEG - 1, n_seq * (seq_len + 3 * SEG), C_CONV), F32),
                    pltpu.VMEM((TM, C_CONV), F32)]
    outs = pl.pallas_call(
        functools.partial(_mix_kernel, seq_lens=tuple(p[5] for p in paths)),
        grid=(n_steps,),
        in_specs=in_specs + [const(a) for a in shared],
        out_specs=out_specs,
        out_shape=out_shape,
        scratch_shapes=scratch,
        compiler_params=_cparams(("arbitrary",), VMEM_LIMIT),
        name="mix",
    )(*args, *shared)
    return [outs[4 * p:4 * p + 4] for p in range(len(paths))]


def _strip_copies(tile, ls_ref, gs_ref, sg_ref, local, rows_hbm, sem, to_hbm, wait):
    def body(e, carry):
        j = tile * N_EXPERTS + e
        n = pl.multiple_of(sg_ref[j], SEG)

        @pl.when(n > 0)
        def _():
            loc = local.at[pl.ds(pl.multiple_of(ls_ref[j], SEG), n)]
            glob = rows_hbm.at[pl.ds(pl.multiple_of(gs_ref[j], SEG), n)]
            cp = pltpu.make_async_copy(loc, glob, sem) if to_hbm else pltpu.make_async_copy(glob, loc, sem)
            if wait:
                cp.wait()
            else:
                cp.start()
        return carry

    lax.fori_loop(0, N_EXPERTS, body, 0)


def _expert_kernel(te_ref, nv_ref, first_ref, next_ref, slot_ref, rows_ref, ptr_ref, mid_ref, src_ref, dst_ref,
                   len_ref, csrc_ref, clen_ref, grpc_hbm, grpl_hbm, wgu_hbm, bgu_ref, wdn_hbm, bdn_ref,
                   y_ref, xbuf, wgu_f, wdn_f, wgu_s, wdn_s, xsem, sem, *, ctx_rows):
    i = pl.program_id(0)

    def weight_copies(e, s):
        return (pltpu.make_async_copy(wgu_hbm.at[e], wgu_f.at[s], sem.at[0, s]),
                pltpu.make_async_copy(wdn_hbm.at[e], wdn_f.at[s], sem.at[1, s]))

    def row_copies(tile, xs, wait):
        def strip(hbm, src, dst, n):
            cp = pltpu.make_async_copy(hbm.at[pl.ds(pl.multiple_of(src, SEG), n)],
                                       xbuf.at[xs, pl.ds(pl.multiple_of(dst, SEG), n)], xsem.at[xs])
            cp.wait() if wait else cp.start()

        def strips_of(hbm, base):
            def body(j, carry):
                n = pl.multiple_of(len_ref[j], SEG)

                @pl.when(n > 0)
                def _():
                    strip(hbm, src_ref[j] - base, dst_ref[j], n)
                return carry
            return body

        lax.fori_loop(ptr_ref[tile], mid_ref[tile], strips_of(grpc_hbm, 0), 0)
        lax.fori_loop(mid_ref[tile], ptr_ref[tile + 1], strips_of(grpl_hbm, ctx_rows), 0)
        n = pl.multiple_of(clen_ref[tile], SEG)
        src = csrc_ref[tile]

        @pl.when(jnp.logical_and(n > 0, src < ctx_rows))
        def _():
            strip(grpc_hbm, src, 0, n)

        @pl.when(jnp.logical_and(n > 0, src >= ctx_rows))
        def _():
            strip(grpl_hbm, src - ctx_rows, 0, n)

    @pl.when(i == 0)
    def _():
        xbuf[...] = jnp.zeros(xbuf.shape, F32)
        row_copies(i, 0, False)

    @pl.when(i + 1 < nv_ref[0])
    def _():
        row_copies(i + 1, (i + 1) % 2, False)

    @pl.when(i < nv_ref[0])
    def _():
        e = te_ref[i]
        s = slot_ref[i]
        row_copies(i, i % 2, True)

        @pl.when(first_ref[i] == 1)
        def _():
            @pl.when(i == 0)
            def _():
                for cp in weight_copies(e, s):
                    cp.start()

            for cp in weight_copies(e, s):
                cp.wait()
            nxt = next_ref[i]

            @pl.when(nxt >= 0)
            def _():
                for cp in weight_copies(nxt, 1 - s):
                    cp.start()

            rows = 64

            def cast(r, carry):
                sl = pl.ds(pl.multiple_of(r * rows, rows), rows)
                wgu_s[sl, :] = wgu_f[s, sl, :].astype(BF16)
                wdn_s[sl, :] = wdn_f[s, sl, :].astype(BF16)
                return carry

            lax.fori_loop(0, D_MODEL // rows, cast, 0)

        def compute(m):
            x = xbuf[i % 2, 0:m, :].astype(BF16)
            gu = jnp.dot(x, wgu_s[...], preferred_element_type=F32) + bgu_ref[0]
            gate = jnp.minimum(gu[:, 0:D_FF], SWIGLU_LIMIT)
            up = jnp.clip(gu[:, D_FF:2 * D_FF], -SWIGLU_LIMIT, SWIGLU_LIMIT)
            act = (gate * jax.nn.sigmoid(SWIGLU_ALPHA * gate) * (up + 1.0)).astype(BF16)
            y_ref[0:m, :] = jnp.dot(act, wdn_s[...], preferred_element_type=F32) + bdn_ref[0]
            if m < TE:
                y_ref[m:TE, :] = jnp.zeros((TE - m, D_MODEL), F32)

        for m in range(TE // 4, TE + 1, TE // 4):
            pl.when(rows_ref[i] == m)(functools.partial(compute, m))

    @pl.when(i >= nv_ref[0])
    def _():
        y_ref[...] = jnp.zeros(y_ref.shape, y_ref.dtype)


def _experts(plan, grp_c, grp_l, n_rows, w_gu, b_gu, w_dn, b_dn):
    nt = n_rows // TE
    exp3 = lambda i, te, *_: (te[i], 0, 0)
    any_spec = pl.BlockSpec(memory_space=pl.ANY)
    return pl.pallas_call(
        functools.partial(_expert_kernel, ctx_rows=grp_c.shape[0]),
        grid_spec=pltpu.PrefetchScalarGridSpec(
            num_scalar_prefetch=13,
            grid=(nt,),
            in_specs=[any_spec, any_spec, any_spec,
                      pl.BlockSpec((1, 1, 2 * D_FF), exp3), any_spec,
                      pl.BlockSpec((1, 1, D_MODEL), exp3)],
            out_specs=pl.BlockSpec((TE, D_MODEL), lambda i, *_: (i, 0)),
            scratch_shapes=[pltpu.VMEM((2, TE, D_MODEL), F32),
                            pltpu.VMEM((2, D_MODEL, 2 * D_FF), F32), pltpu.VMEM((2, D_FF, D_MODEL), F32),
                            pltpu.VMEM((D_MODEL, 2 * D_FF), BF16), pltpu.VMEM((D_FF, D_MODEL), BF16),
                            pltpu.SemaphoreType.DMA((2,)), pltpu.SemaphoreType.DMA((2, 2))]),
        out_shape=jax.ShapeDtypeStruct((n_rows, D_MODEL), F32),
        compiler_params=_cparams(("arbitrary",), VMEM_LIMIT),
        name="experts",
    )(plan["tile_expert"], plan["n_valid"], plan["tile_first"], plan["tile_next"], plan["tile_slot"],
      plan["tile_rows"], plan["seg_ptr"], plan["seg_mid"], plan["seg_src"], plan["seg_dst"], plan["seg_len"],
      plan["carry_src"], plan["carry_len"], grp_c, grp_l, w_gu, b_gu.reshape(N_EXPERTS, 1, 2 * D_FF), w_dn, b_dn.reshape(N_EXPERTS, 1, D_MODEL))


def _final_kernel(ls_ref, gs_ref, sg_ref, x1_ref, route_ref, mod_ref, fg_ref, ys_ref, y_ref,
                  ybuf, sem, *, tile_off, n_tiles):
    i = pl.program_id(0)
    slot = i % 2

    def fetch(step, sl, wait):
        _strip_copies(step + tile_off, ls_ref, gs_ref, sg_ref, ybuf.at[sl], ys_ref, sem.at[sl], False, wait)

    @pl.when(i == 0)
    def _():
        ybuf[...] = jnp.zeros(ybuf.shape, F32)
        fetch(i, slot, False)

    @pl.when(i + 1 < n_tiles)
    def _():
        fetch(i + 1, 1 - slot, False)

    fetch(i, slot, True)

    route = route_ref[...]
    r = lax.broadcasted_iota(jnp.int32, (TM, NR), 1).astype(F32)
    comb = jnp.zeros((TM, NR), F32)
    for kk in range(TOP_K):
        gate = route[:, 2 * TOP_K + kk:2 * TOP_K + kk + 1]
        comb = jnp.where(route[:, kk:kk + 1] == r, gate, comb)
    moe = jnp.dot(comb.astype(BF16), ybuf[slot].astype(BF16), preferred_element_type=F32)
    gate2 = mod_ref[0, :, 5 * D_MODEL:6 * D_MODEL]
    y_ref[...] = _rms(x1_ref[...] + gate2 * moe, fg_ref[...])


def _final(plan, x1, route, mod3, seg_fn, tile_off, fg, ys):
    t_path = x1.shape[0]
    n = t_path // TM
    loc = lambda w: pl.BlockSpec((TM, w), lambda i, *_: (i, 0))
    return pl.pallas_call(
        functools.partial(_final_kernel, tile_off=tile_off, n_tiles=n),
        grid_spec=pltpu.PrefetchScalarGridSpec(
            num_scalar_prefetch=3,
            grid=(n,),
            in_specs=[loc(D_MODEL), loc(LANES),
                      pl.BlockSpec((1, 1, 6 * D_MODEL), lambda i, *_: (seg_fn(i), 0, 0)),
                      pl.BlockSpec((1, D_MODEL), lambda i, *_: (0, 0)),
                      pl.BlockSpec(memory_space=pl.ANY)],
            out_specs=loc(D_MODEL),
            scratch_shapes=[pltpu.VMEM((2, NR, D_MODEL), F32), pltpu.SemaphoreType.DMA((2,))]),
        out_shape=jax.ShapeDtypeStruct((t_path, D_MODEL), F32),
        compiler_params=_cparams(("arbitrary",), VMEM_LIMIT),
        name="final",
    )(plan["lstart"], plan["gstart"], plan["strip"], x1, route, mod3, fg, ys)


def _plan(tile_counts, n_rows, n_ctx):
    nt = n_rows // TE
    n_tok = tile_counts.shape[0]
    strip = (tile_counts + SEG - 1) // SEG * SEG
    lstart = jnp.cumsum(strip, axis=1) - strip
    rows_e = jnp.sum(strip, axis=0)
    rpad = (rows_e + TE - 1) // TE * TE
    ends = jnp.cumsum(rpad)
    offs = ends - rpad
    gstart = offs[None, :] + jnp.cumsum(strip, axis=0) - strip
    n_valid = (ends[-1] // TE).astype(jnp.int32)
    tile_ids = jnp.minimum(jnp.arange(nt, dtype=jnp.int32), n_valid - 1)
    tile_expert = jnp.sum((ends[None, :] <= (tile_ids * TE)[:, None]).astype(jnp.int32), axis=1)
    tile_expert = jnp.minimum(tile_expert, N_EXPERTS - 1).astype(jnp.int32)
    prev = jnp.concatenate([jnp.full((1,), -1, jnp.int32), tile_expert[:-1]])
    tile_first = (tile_expert != prev).astype(jnp.int32)
    tile_slot = (jnp.cumsum(tile_first) - 1) % 2
    pick = lambda table, idx: jnp.sum(
        jnp.where(idx[:, None] == jnp.arange(table.shape[0], dtype=jnp.int32)[None, :], table[None, :], 0), axis=1)
    used = jnp.clip(pick(offs + rows_e, tile_expert) - jnp.arange(nt, dtype=jnp.int32) * TE, 0, TE)
    tile_rows = jnp.maximum((used + TE // 4 - 1) // (TE // 4), 1) * (TE // 4)
    after = pick(ends, tile_expert) // TE
    tile_next = jnp.where(after < n_valid, pick(tile_expert, jnp.minimum(after, nt - 1)), -1)
    tok = jnp.broadcast_to(jnp.arange(n_tok, dtype=jnp.int32)[:, None], strip.shape)
    g0 = gstart.T.reshape(-1)
    ln = strip.T.reshape(-1)
    src0 = (tok * NR + lstart).T.reshape(-1)
    is_ctx = (tok < n_ctx).T.reshape(-1)
    len1 = jnp.minimum(ln, TE - g0 % TE)
    tile0 = g0 // TE
    tiles = jnp.arange(nt + 1, dtype=jnp.int32)[:, None]
    count = lambda m: jnp.sum(m.astype(jnp.int32), axis=1)
    seg_ptr = count(tile0[None, :] < tiles)
    seg_mid = seg_ptr + count((tile0[None, :] == tiles) & is_ctx[None, :])
    into = ((g0 + len1) // TE)[None, :] == tiles[:nt]
    carry_len = jnp.sum(jnp.where(into, (ln - len1)[None, :], 0), axis=1)
    carry_src = jnp.sum(jnp.where(into & (ln > len1)[None, :], (src0 + len1)[None, :], 0), axis=1)
    i32 = lambda a: a.reshape(-1).astype(jnp.int32)
    return dict(strip=i32(strip), lstart=i32(lstart), gstart=i32(gstart), n_valid=n_valid.reshape(1),
                tile_expert=tile_expert, tile_first=i32(tile_first), tile_next=i32(tile_next),
                tile_slot=i32(tile_slot), tile_rows=i32(tile_rows), seg_ptr=i32(seg_ptr), seg_mid=i32(seg_mid), seg_src=i32(src0),
                seg_dst=i32(g0 % TE), seg_len=i32(len1), carry_src=i32(carry_src),
                carry_len=i32(carry_len))


def _layer(x_prompt, x_sample, state, c, c_ctx, ada_w, ada_b, norm1_g, norm2_g, w_in, conv_w, conv_b,
           conv_ln_g, conv_ln_b, gate_w, gate_b, gla_norm_g, w_out, router_w, router_b,
           moe_w_gu, moe_b_gu, moe_w_dn, moe_b_dn, final_g):
    bp, lp, d = x_prompt.shape
    bs, ls, _ = x_sample.shape
    assert lp == TM and ls % TM == 0 and d == D_MODEL
    xp = x_prompt.reshape(bp * lp, d)
    xs = x_sample.reshape(bs * ls, d)
    n_c, n_l = bp, bs * ls // TM
    lat_tiles = ls // TM
    t_all = (n_c + n_l) * TM

    n_cond = 1 + bs
    cond_t = jnp.zeros((d, 8), F32).at[:, 0].set(c_ctx).at[:, 1:n_cond].set(c.T)
    mod3 = _modulation(cond_t, ada_w, ada_b, n_cond).reshape(8, 1, 6 * d)

    row = lambda a: a.reshape(1, -1)
    u, qk, v, g, la = _inproj(xp, xs, mod3, row(norm1_g), w_in, gate_w, gate_b, ls)

    pair = lambda s: s.reshape(s.shape[0], 2, 2, LANES, LANES)
    of_c, ob_c, s_ctx = _gla(qk, v, la, jnp.zeros((bp, 2, 2, LANES, LANES), F32), bp, 1, 0)
    of_l, ob_l, _ = _gla(qk, v, la, pair(state), bs, lat_tiles, n_c)

    cw = jnp.zeros((32, C_CONV), F32).at[:CONV_K].set(conv_w)
    rwh, rwl = _split(router_w.T)
    rb = jnp.broadcast_to(router_b[:, None], (N_EXPERTS, TM))
    wts = (cw, row(conv_b), row(conv_ln_g), row(conv_ln_b), row(gla_norm_g), w_out.astype(BF16),
           row(norm2_g), rwh, rwl, rb)
    seg_c = lambda i: 0
    seg_l = lambda i: 1 + i // lat_tiles
    (x1_c, grp_c, route_c, cnt_c), (x1_l, grp_l, route_l, cnt_l) = _mix(
        [(xp, of_c, ob_c, seg_c, 0, lp), (xs, of_l, ob_l, seg_l, n_c, GRID_W)], u, g, mod3, wts)

    tile_counts = jnp.concatenate([cnt_c[:, :, 0], cnt_l[:, :, 0]]).astype(jnp.int32)
    n_rows = TOP_K * t_all + (SEG - 1) * N_EXPERTS * (n_c + n_l) + N_EXPERTS * TE
    n_rows = (n_rows + TE - 1) // TE * TE
    plan = _plan(tile_counts, n_rows, n_c)
    ysrt = _experts(plan, grp_c, grp_l, n_rows, moe_w_gu, moe_b_gu, moe_w_dn, moe_b_dn)
    y_c = _final(plan, x1_c, route_c, mod3, seg_c, 0, row(final_g), ysrt)
    y_l = _final(plan, x1_l, route_l, mod3, seg_l, n_c, row(final_g), ysrt)
    new_state = s_ctx.reshape(bp, 1, 2, N_GLA_HEADS, DK_HEAD, DV_HEAD)
    return y_c.reshape(bp, lp, d), y_l.reshape(bs, ls, d), new_state


def kernel(x_prompt, x_sample, state_gla, c, c_ctx, ada_w, ada_b, norm1_g, norm2_g, w_in, conv_w,
           conv_b, conv_ln_g, conv_ln_b, gate_w, gate_b, gla_norm_g, w_out, router_w, router_b,
           moe_w_gu, moe_b_gu, moe_w_dn, moe_b_dn, final_g):
    assert ada_w.shape[0] == 1, "single-layer step"
    return _layer(x_prompt, x_sample, state_gla[:, 0], c, c_ctx, ada_w[0], ada_b[0], norm1_g[0],
                  norm2_g[0], w_in[0], conv_w[0], conv_b[0], conv_ln_g[0], conv_ln_b[0], gate_w[0],
                  gate_b[0], gla_norm_g[0], w_out[0], router_w[0], router_b[0], moe_w_gu[0],
                  moe_b_gu[0], moe_w_dn[0], moe_b_dn[0], final_g)
```

```python
import functools

import numpy as np
import jax
import jax.numpy as jnp
from jax import lax
from jax.experimental import pallas as pl
from jax.experimental.pallas import tpu as pltpu

D_MODEL = 1024
GRID_W = 64
C_CONV = D_MODEL // 2
CONV_K = 31
N_GLA_HEADS = 4
DV_HEAD = 128
DK_HEAD = 64
DK_TOT = DK_HEAD * N_GLA_HEADS
DV_TOT = DV_HEAD * N_GLA_HEADS
GATE_RANK = 16
GATE_TEMP = 16.0
CHUNK = 64
N_EXPERTS = 32
TOP_K = 4
D_FF = D_MODEL
SWIGLU_LIMIT = 7.0
SWIGLU_ALPHA = 1.702
EPS = 1e-6

LANES = 128
SEG = 8
TM = 256
TE = 512
NR = TOP_K * TM + TM
CONV_PAD = 16
VMEM_LIMIT = 56 * 1024 * 1024

F32 = jnp.float32
BF16 = jnp.bfloat16
HI = lax.Precision.HIGHEST
NEG = -1e30
TN = (((0,), (0,)), ((), ()))
NT = (((1,), (1,)), ((), ()))

assert N_EXPERTS * (SEG - 1) <= NR - TOP_K * TM


def _split(x):
    hi = x.astype(BF16)
    return hi, (x - hi.astype(F32)).astype(BF16)


def _cparams(sem, vmem=None):
    return pltpu.CompilerParams(dimension_semantics=sem, vmem_limit_bytes=vmem)


def _mod_kernel(ct_ref, w_ref, b_ref, o_ref, *, n_cond):
    ct = ct_ref[...]
    s = ct * jax.nn.sigmoid(ct)
    w = w_ref[...]
    rows = [jnp.sum(s[:, r:r + 1] * w, axis=0, keepdims=True) + b_ref[...] for r in range(n_cond)]
    rows.append(jnp.zeros((8 - n_cond, w.shape[1]), F32))
    o_ref[...] = jnp.concatenate(rows, axis=0)


def _modulation(cond_t, ada_w, ada_b, n_cond):
    d, n = ada_w.shape
    nb = 768
    return pl.pallas_call(
        functools.partial(_mod_kernel, n_cond=n_cond),
        grid=(n // nb,),
        in_specs=[pl.BlockSpec((d, 8), lambda i: (0, 0)),
                  pl.BlockSpec((d, nb), lambda i: (0, i)),
                  pl.BlockSpec((1, nb), lambda i: (0, i))],
        out_specs=pl.BlockSpec((8, nb), lambda i: (0, i)),
        out_shape=jax.ShapeDtypeStruct((8, n), F32),
        compiler_params=_cparams(("arbitrary",)),
        name="mod",
    )(cond_t, ada_w, ada_b.reshape(1, n))


def _rms(x, g):
    return x * lax.rsqrt(jnp.mean(x * x, axis=-1, keepdims=True) + EPS) * g


def _inproj_kernel(xp_ref, xs_ref, mod_ref, g1_ref, wu_ref, wqk_ref, wv_ref, wg_ref, wlow_ref,
                   gwh_ref, gwl_ref, gb_ref, u_ref, qk_ref, v_ref, g_ref, la_ref, *, n_ctx_tiles):
    i = pl.program_id(0)
    shift = mod_ref[0, :, 0:D_MODEL]
    scale = mod_ref[0, :, D_MODEL:2 * D_MODEL]
    dot = functools.partial(jnp.dot, preferred_element_type=F32)
    rows = u_ref.shape[0]
    sub = min(rows, TM)
    def block(r0):
        rs = slice(r0, r0 + sub)
        x = jnp.where(i < n_ctx_tiles, xp_ref[rs, :], xs_ref[rs, :])
        h = (_rms(x, g1_ref[...]) * (1.0 + scale) + shift).astype(BF16)
        yield
        u_ref[rs, :] = dot(h, wu_ref[...]).astype(BF16)
        yield
        qk_ref[rs, :] = dot(h, wqk_ref[...]).astype(BF16)
        v_ref[rs, :] = dot(h, wv_ref[...]).astype(BF16)
        yield
        g_ref[rs, :] = dot(h, wg_ref[...]).astype(BF16)
        low = dot(h, wlow_ref[...])
        yield
        low_hi, low_lo = _split(low)
        z = (dot(low_hi, gwh_ref[...]) + dot(low_lo, gwh_ref[...]) + dot(low_hi, gwl_ref[...])
             + gb_ref[...])
        la_ref[rs, :] = (jnp.minimum(z, 0.0) - jnp.log1p(jnp.exp(-jnp.abs(z)))) * (1.0 / GATE_TEMP)

    blocks = [block(r0) for r0 in range(0, rows, sub)]
    while blocks:
        blocks = [gen for gen in blocks if next(gen, "done") != "done"]


def _inproj(xp, xs, mod3, g1, w_in, gate_w, gate_b, lat_len):
    ti = next(t for t in (1024, 512, TM) if xp.shape[0] % t == 0 and lat_len % t == 0)
    lat_tiles_per_seq = lat_len // ti
    n_c, n_l = xp.shape[0] // ti, xs.shape[0] // ti
    t_all = xp.shape[0] + xs.shape[0]
    sp = np.cumsum([0, C_CONV, C_CONV, DK_TOT, DK_TOT, DV_TOT, DV_TOT, 2 * GATE_RANK])
    wb = w_in.astype(BF16)
    wu, wqk, wv, wg, wlow = (wb[:, sp[0]:sp[2]], wb[:, sp[2]:sp[4]], wb[:, sp[4]:sp[5]],
                             wb[:, sp[5]:sp[6]], wb[:, sp[6]:sp[7]])
    gw = jnp.zeros((2 * GATE_RANK, 2 * DK_TOT), F32)
    gw = gw.at[:GATE_RANK, :DK_TOT].set(gate_w[0]).at[GATE_RANK:, DK_TOT:].set(gate_w[1])
    gwh, gwl = _split(gw)
    gb = gate_b.reshape(1, 2 * DK_TOT)
    const = lambda a: pl.BlockSpec(a.shape, lambda i: (0,) * a.ndim)
    row = lambda w: pl.BlockSpec((ti, w), lambda i: (i, 0))
    seg = lambda i: jnp.where(i < n_c, 0, 1 + jnp.maximum(i - n_c, 0) // lat_tiles_per_seq)
    return pl.pallas_call(
        functools.partial(_inproj_kernel, n_ctx_tiles=n_c),
        grid=(n_c + n_l,),
        in_specs=[pl.BlockSpec((ti, D_MODEL), lambda i: (jnp.minimum(i, n_c - 1), 0)),
                  pl.BlockSpec((ti, D_MODEL), lambda i: (jnp.maximum(i - n_c, 0), 0)),
                  pl.BlockSpec((1, 1, 6 * D_MODEL), lambda i: (seg(i), 0, 0)),
                  const(g1), const(wu), const(wqk), const(wv), const(wg), const(wlow),
                  const(gwh), const(gwl), const(gb)],
        out_specs=[row(2 * C_CONV), row(2 * DK_TOT), row(DV_TOT), row(DV_TOT), row(2 * DK_TOT)],
        out_shape=[jax.ShapeDtypeStruct((t_all, 2 * C_CONV), BF16),
                   jax.ShapeDtypeStruct((t_all, 2 * DK_TOT), BF16),
                   jax.ShapeDtypeStruct((t_all, DV_TOT), BF16),
                   jax.ShapeDtypeStruct((t_all, DV_TOT), BF16),
                   jax.ShapeDtypeStruct((t_all, 2 * DK_TOT), F32)],
        compiler_params=_cparams(("arbitrary",), VMEM_LIMIT),
        name="inproj",
    )(xp, xs, mod3, g1, wu, wqk, wv, wg, wlow, gwh, gwl, gb)


def _gla_direction(qk_ref, v_ref, la_ref, tri_ref, blk_ref, o_ref, s_scr, reverse):
    la_hi, la_lo = _split(la_ref[...])
    tri = tri_ref[...]
    bcum = (jnp.dot(tri, la_hi, preferred_element_type=F32)
            + jnp.dot(tri, la_lo, preferred_element_type=F32))
    blk = blk_ref[...]
    bl_cols = (lax.dot_general(la_hi, blk, TN, preferred_element_type=F32)
               + lax.dot_general(la_lo, blk, TN, preferred_element_type=F32))
    yield
    q = qk_ref[:, 0:DK_TOT].astype(F32)
    k = qk_ref[:, DK_TOT:2 * DK_TOT].astype(F32)
    lane = lax.broadcasted_iota(jnp.int32, (CHUNK, LANES), 1)
    row2 = lax.broadcasted_iota(jnp.int32, (2 * CHUNK, CHUNK), 0) % CHUNK
    col2 = lax.broadcasted_iota(jnp.int32, (2 * CHUNK, CHUNK), 1)
    keep = (col2 >= row2) if reverse else (col2 <= row2)
    srow = lax.broadcasted_iota(jnp.int32, (LANES, LANES), 0)
    n_chunks = TM // CHUNK
    order = range(n_chunks - 1, -1, -1) if reverse else range(n_chunks)
    state = [s_scr[0], s_scr[1]]
    for c in order:
        r0 = c * CHUNK
        bc = bcum[r0:r0 + CHUNK]
        bl = bc[0:1] if reverse else bc[CHUNK - 1:CHUNK]
        qt = q[r0:r0 + CHUNK] * jnp.exp(bc) * (DK_HEAD ** -0.5)
        kt = k[r0:r0 + CHUNK] * jnp.exp(-bc)
        ke = k[r0:r0 + CHUNK] * jnp.exp(bl - bc)
        yield
        for p in range(2):
            cs = slice(p * LANES, (p + 1) * LANES)
            qt_p = qt[:, cs]
            qs = jnp.concatenate([jnp.where(lane < DK_HEAD, qt_p, 0.0),
                                  jnp.where(lane >= DK_HEAD, qt_p, 0.0)], axis=0).astype(BF16)
            att = lax.dot_general(qs, kt[:, cs].astype(BF16), NT, preferred_element_type=F32)
            att = jnp.where(keep, att, 0.0).astype(BF16)
            s_p = state[p]
            o_inter = jnp.dot(qs, s_p.astype(BF16), preferred_element_type=F32)
            ke_t = jnp.transpose(ke[:, cs]).astype(BF16)
            upd = []
            for hh in range(2):
                h = 2 * p + hh
                v_h = v_ref[r0:r0 + CHUNK, h * DV_HEAD:(h + 1) * DV_HEAD]
                both = jnp.dot(jnp.concatenate([att[hh * CHUNK:(hh + 1) * CHUNK], ke_t], axis=0), v_h,
                               preferred_element_type=F32)
                o_ref[r0:r0 + CHUNK, h * DV_HEAD:(h + 1) * DV_HEAD] = (
                    both[0:CHUNK] + o_inter[hh * CHUNK:(hh + 1) * CHUNK])
                upd.append(both[CHUNK:CHUNK + LANES])
            bl_col = bl_cols[p * LANES:(p + 1) * LANES, c * LANES:(c + 1) * LANES]
            state[p] = jnp.exp(bl_col) * s_p + jnp.where(srow < DK_HEAD, upd[0], upd[1])
            yield
    s_scr[0] = state[0]
    s_scr[1] = state[1]


def _gla_kernel(*refs, n_tiles, nb):
    seqs = [refs[6 * r:6 * r + 6] for r in range(nb)]
    tril_ref, triu_ref, blk_ref, s0_ref, of_ref, ob_ref, sout_ref, sf_scr, sb_scr = refs[6 * nb:]
    j = pl.program_id(1)

    @pl.when(j == 0)
    def _():
        sf_scr[...] = s0_ref[:, 0]
        sb_scr[...] = s0_ref[:, 1]

    scans = []
    for r, (qkf_ref, vf_ref, laf_ref, qkb_ref, vb_ref, lab_ref) in enumerate(seqs):
        scans.append(_gla_direction(qkf_ref, vf_ref, laf_ref, tril_ref, blk_ref, of_ref.at[r], sf_scr.at[r], False))
        scans.append(_gla_direction(qkb_ref, vb_ref, lab_ref, triu_ref, blk_ref, ob_ref.at[r], sb_scr.at[r], True))
    while scans:
        scans = [gen for gen in scans if next(gen, "done") != "done"]

    @pl.when(j == n_tiles - 1)
    def _():
        sout_ref[:, 0] = sf_scr[...]
        sout_ref[:, 1] = sb_scr[...]


def _gla(qk, v, la, s0, n_seq, n_tiles, tile_off):
    nb = 2 if n_seq % 2 == 0 else 1
    seq_len = n_tiles * TM
    blk = np.arange(TM) // CHUNK
    same = blk[:, None] == blk[None, :]
    r = np.arange(TM)
    tril = jnp.asarray((same & (r[None, :] <= r[:, None])).astype(np.float32)).astype(BF16)
    triu = jnp.asarray((same & (r[None, :] >= r[:, None])).astype(np.float32)).astype(BF16)
    chunk_of_lane = np.arange(TM // CHUNK * LANES) // LANES
    blkm = jnp.asarray((blk[:, None] == chunk_of_lane[None, :]).astype(np.float32)).astype(BF16)
    def tok(w, r, backward, cb=0):
        def index(s, j):
            jj = (n_tiles - 1 - j) if backward else j
            return ((s * nb + r) * n_tiles + jj + tile_off, cb)
        return pl.BlockSpec((TM, w), index)

    const = lambda a: pl.BlockSpec(a.shape, lambda s, j: (0,) * a.ndim)
    st = pl.BlockSpec((nb, 2, 2, LANES, LANES), lambda s, j: (s, 0, 0, 0, 0))
    seq_specs, seq_args = [], []
    for r in range(nb):
        seq_specs += [tok(2 * DK_TOT, r, False), tok(DV_TOT, r, False), tok(DK_TOT, r, False, 0),
                      tok(2 * DK_TOT, r, True), tok(DV_TOT, r, True), tok(DK_TOT, r, True, 1)]
        seq_args += [qk, v, la, qk, v, la]
    o_f, o_b, s_out = pl.pallas_call(
        functools.partial(_gla_kernel, n_tiles=n_tiles, nb=nb),
        grid=(n_seq // nb, n_tiles),
        in_specs=seq_specs + [const(tril), const(triu), const(blkm), st],
        out_specs=[pl.BlockSpec((nb, TM, DV_TOT), lambda s, j: (s, j, 0)),
                   pl.BlockSpec((nb, TM, DV_TOT), lambda s, j: (s, n_tiles - 1 - j, 0)), st],
        out_shape=[jax.ShapeDtypeStruct((n_seq, seq_len, DV_TOT), F32),
                   jax.ShapeDtypeStruct((n_seq, seq_len, DV_TOT), F32),
                   jax.ShapeDtypeStruct(s0.shape, F32)],
        scratch_shapes=[pltpu.VMEM((nb, 2, LANES, LANES), F32), pltpu.VMEM((nb, 2, LANES, LANES), F32)],
        compiler_params=_cparams(("arbitrary", "arbitrary"), VMEM_LIMIT),
        name="gla",
    )(*seq_args, tril, triu, blkm, s0)
    return o_f.reshape(n_seq * seq_len, DV_TOT), o_b.reshape(n_seq * seq_len, DV_TOT), s_out


def _mix_conv(u_ref, cw_ref, pad_scr, shf_scr, cv_scr, seq_len):
    n_seq = TM // seq_len
    pr = seq_len + 2 * CONV_PAD
    qr = seq_len + 3 * SEG

    u = u_ref[...].astype(F32)
    hg = u[:, 0:C_CONV] * jax.nn.sigmoid(u[:, C_CONV:2 * C_CONV])
    zero = jnp.zeros((CONV_PAD, C_CONV), F32)
    for s in range(n_seq):
        pad_scr[s * pr:s * pr + CONV_PAD, :] = zero
        pad_scr[s * pr + CONV_PAD:s * pr + CONV_PAD + seq_len, :] = hg[s * seq_len:(s + 1) * seq_len]
        pad_scr[s * pr + CONV_PAD + seq_len:(s + 1) * pr, :] = zero
        for b in range(1, SEG):
            shf_scr[b - 1, s * qr:(s + 1) * qr, :] = pad_scr[s * pr + b:s * pr + b + qr, :]
    rb = 8 * SEG
    per_seq = seq_len // rb
    for gi in range(C_CONV // LANES):
        gs = slice(gi * LANES, (gi + 1) * LANES)
        wv = [jnp.broadcast_to(cw_ref[tap:tap + 1, gs], (SEG, LANES)) for tap in range(CONV_K)]

        def conv_rows(i, carry, gs=gs, wv=wv):
            s = lax.shift_right_logical(i, per_seq.bit_length() - 1)
            r0 = (i & (per_seq - 1)) * rb
            nj = rb // SEG
            acc = [None] * nj
            for b in range(SEG):
                taps = [(a, SEG * a + b - (CONV_PAD - CONV_K // 2)) for a in range(CONV_PAD * 2 // SEG)]
                taps = [(a, t) for a, t in taps if 0 <= t < CONV_K]
                loaded = {}
                for a, t in taps:
                    for j in range(nj):
                        if a + j not in loaded:
                            if b == 0:
                                row = pl.multiple_of(s * pr + r0 + (a + j) * SEG, SEG)
                                loaded[a + j] = pad_scr[pl.ds(row, SEG), gs]
                            else:
                                row = pl.multiple_of(s * qr + r0 + (a + j) * SEG, SEG)
                                loaded[a + j] = shf_scr[b - 1, pl.ds(row, SEG), gs]
                        term = wv[t] * loaded[a + j]
                        acc[j] = term if acc[j] is None else acc[j] + term
            for j in range(nj):
                cv_scr[pl.ds(pl.multiple_of(s * seq_len + r0 + j * SEG, SEG), SEG), gs] = acc[j]
            return carry

        lax.fori_loop(0, TM // rb, conv_rows, 0)


def _mix_rest(x_ref, g_ref, of_ref, ob_ref, mod_ref, cv_scr, cb_ref, lng_ref, lnb_ref, gng_ref, wout_ref,
              n2g_ref, rwh_ref, rwl_ref, rb_ref, le_ref, ut_ref, x1_ref, grp_ref, route_ref, cnt_ref):
    cv = cv_scr[...] + cb_ref[...]
    mu = jnp.mean(cv, axis=-1, keepdims=True)
    var = jnp.mean(jnp.square(cv - mu), axis=-1, keepdims=True)
    cv =(cv - mu) * lax.rsqrt(var + EPS) * lng_ref[...] + lnb_ref[...]
    conv_out = (cv * jax.nn.sigmoid(cv)).astype(BF16)
    yield

    o = of_ref[...] + ob_ref[...]
    g = g_ref[...].astype(F32)
    heads = []
    for h in range(N_GLA_HEADS):
        cs = slice(h * DV_HEAD, (h + 1) * DV_HEAD)
        oh = o[:, cs]
        oh = oh * lax.rsqrt(jnp.mean(oh * oh, axis=-1, keepdims=True) + EPS) * gng_ref[...]
        gh = g[:, cs]
        heads.append((oh * (gh * jax.nn.sigmoid(gh))).astype(BF16))
    yield
    y = jnp.dot(conv_out, wout_ref[0:C_CONV, :], preferred_element_type=F32)
    for h in range(N_GLA_HEADS):
        y = y + jnp.dot(heads[h], wout_ref[C_CONV + h * DV_HEAD:C_CONV + (h + 1) * DV_HEAD, :],
                        preferred_element_type=F32)
    yield

    gate1 = mod_ref[0, :, 2 * D_MODEL:3 * D_MODEL]
    shift2 = mod_ref[0, :, 3 * D_MODEL:4 * D_MODEL]
    scale2 = mod_ref[0, :, 4 * D_MODEL:5 * D_MODEL]
    x1 = x_ref[...] + gate1 * y
    x1_ref[...] = x1
    h2 = _rms(x1, n2g_ref[...]) * (1.0 + scale2) + shift2
    h2_hi, h2_lo = _split(h2)
    yield

    ntdot = functools.partial(lax.dot_general, dimension_numbers=NT, preferred_element_type=F32)
    l = (ntdot(rwh_ref[...], h2_hi) + ntdot(rwh_ref[...], h2_lo) + ntdot(rwl_ref[...], h2_hi)
         + rb_ref[...])
    yield
    erow = lax.broadcasted_iota(jnp.int32, (N_EXPERTS, TM), 0).astype(F32)
    vals, idxs, hots = [], [], []
    for _ in range(TOP_K):
        m = jnp.max(l, axis=0, keepdims=True)
        idx = jnp.min(jnp.where(l == m, erow, float(N_EXPERTS)), axis=0, keepdims=True)
        hot = erow == idx
        l = jnp.where(hot, -jnp.inf, l)
        vals.append(m)
        idxs.append(idx)
        hots.append(hot)
        yield
    es = [jnp.exp(vk - vals[0]) for vk in vals]
    inv = 1.0 / (es[0] + es[1] + es[2] + es[3])
    sel = jnp.zeros((N_EXPERTS, TM), F32)
    for hot in hots:
        sel = jnp.where(hot, 1.0, sel)

    cnt = jnp.broadcast_to(jnp.sum(sel, axis=1, keepdims=True), (N_EXPERTS, LANES))
    cnt_ref[0] = cnt
    strip = jnp.ceil(cnt * (1.0 / SEG)) * float(SEG)
    starts = jnp.dot(le_ref[...], strip, precision=HI, preferred_element_type=F32)[:, 0:1]
    slot_of = jnp.dot(sel.astype(BF16), ut_ref[...], preferred_element_type=F32) + starts
    slots = [jnp.sum(jnp.where(hot, slot_of, 0.0), axis=0, keepdims=True) for hot in hots]

    rows = slots + idxs + [e * inv for e in es]
    srow = lax.broadcasted_iota(jnp.int32, (16, TM), 0)
    packed = jnp.zeros((16, TM), F32)
    for j, v in enumerate(rows):
        packed = jnp.where(srow == j, v, packed)
    packed = jnp.concatenate([packed, jnp.zeros((LANES - 16, TM), F32)], axis=0)
    route_ref[...] = jnp.transpose(packed)
    yield

    r = lax.broadcasted_iota(jnp.int32, (NR, TM), 0).astype(F32)
    perm = jnp.zeros((NR, TM), F32)
    for slot in slots:
        perm = jnp.where(slot == r, 1.0, perm)
    yield
    grp_ref[...] = jnp.dot(perm.astype(BF16), h2_hi, preferred_element_type=F32)


def _mix_kernel(*refs, seq_lens):
    k = len(seq_lens)
    ins = [refs[6 * p:6 * p + 6] for p in range(k)]
    cw_ref, *shared = refs[6 * k:6 * k + 12]
    outs = [refs[6 * k + 12 + 4 * p:6 * k + 16 + 4 * p] for p in range(k)]
    scr = [refs[10 * k + 12 + 3 * p:10 * k + 15 + 3 * p] for p in range(k)]
    for p in range(k):
        _mix_conv(ins[p][1], cw_ref, *scr[p], seq_lens[p])
    stages = []
    for p in range(k):
        x_ref, _, g_ref, of_ref, ob_ref, mod_ref = ins[p]
        stages.append(_mix_rest(x_ref, g_ref, of_ref, ob_ref, mod_ref, scr[p][2], *shared, *outs[p]))
    while stages:
        stages = [gen for gen in stages if next(gen, "done") != "done"]


def _mix(paths, u, g, mod3, wts):
    (conv_w, conv_b, ln_g, ln_b, gng, wout, n2g, rwh, rwl, rb) = wts
    le = jnp.asarray(np.tril(np.ones((N_EXPERTS, N_EXPERTS), np.float32), -1))
    ut = jnp.asarray(np.triu(np.ones((TM, TM), np.float32), 1)).astype(BF16)
    shared = (conv_w, conv_b, ln_g, ln_b, gng, wout, n2g, rwh, rwl, rb, le, ut)
    const = lambda a: pl.BlockSpec(a.shape, lambda i: (0,) * a.ndim)
    n_steps = max(x.shape[0] // TM for x, *_ in paths)
    in_specs, args, out_specs, out_shape, scratch = [], [], [], [], []
    for x, o_f, o_b, seg_fn, tile_off, seq_len in paths:
        n = x.shape[0] // TM
        cur = lambda i, n=n: jnp.minimum(i, n - 1)
        loc = lambda w, cur=cur: pl.BlockSpec((TM, w), lambda i: (cur(i), 0))
        uni = lambda w, cur=cur, off=tile_off: pl.BlockSpec((TM, w), lambda i: (cur(i) + off, 0))
        in_specs += [loc(D_MODEL), uni(2 * C_CONV), uni(DV_TOT), loc(DV_TOT), loc(DV_TOT),
                     pl.BlockSpec((1, 1, 6 * D_MODEL), lambda i, cur=cur, f=seg_fn: (f(cur(i)), 0, 0))]
        args += [x, u, g, o_f, o_b, mod3]
        out_specs += [loc(D_MODEL), pl.BlockSpec((NR, D_MODEL), lambda i, cur=cur: (cur(i), 0)), loc(LANES),
                      pl.BlockSpec((1, N_EXPERTS, LANES), lambda i, cur=cur: (cur(i), 0, 0))]
        out_shape += [jax.ShapeDtypeStruct((n * TM, D_MODEL), F32),
                      jax.ShapeDtypeStruct((n * NR, D_MODEL), F32),
                      jax.ShapeDtypeStruct((n * TM, LANES), F32),
                      jax.ShapeDtypeStruct((n, N_EXPERTS, LANES), F32)]
        n_seq = TM // seq_len
        scratch += [pltpu.VMEM((n_seq * (seq_len + 2 * CONV_PAD), C_CONV), F32),
                    pltpu.VMEM((SEG - 1, n_seq * (seq_len + 3 * SEG), C_CONV), F32),
                    pltpu.VMEM((TM, C_CONV), F32)]
    outs = pl.pallas_call(
        functools.partial(_mix_kernel, seq_lens=tuple(p[5] for p in paths)),
        grid=(n_steps,),
        in_specs=in_specs + [const(a) for a in shared],
        out_specs=out_specs,
        out_shape=out_shape,
        scratch_shapes=scratch,
        compiler_params=_cparams(("arbitrary",), VMEM_LIMIT),
        name="mix",
    )(*args, *shared)
    return [outs[4 * p:4 * p + 4] for p in range(len(paths))]


def _expert_kernel(te_ref, nv_ref, first_ref, next_ref, slot_ref, rows_ref, ptr_ref, mid_ref, src_ref, dst_ref,
                   len_ref, csrc_ref, clen_ref, tot_ref, grpc_hbm, grpl_hbm, wgu_hbm, bgu_ref, wdn_hbm, bdn_ref,
                   ysc_hbm, ysl_hbm, xbuf, ybuf, zbuf, wgu_f, wdn_f, wgu_s, wdn_s, xsem, ysem, zsem, sem,
                   *, ctx_rows, n_ctx, n_tok, n_steps):
    i = pl.program_id(0)
    far_c, far_l = grpc_hbm, grpl_hbm

    def weight_copies(e, s):
        return (pltpu.make_async_copy(wgu_hbm.at[e], wgu_f.at[s], sem.at[0, s]),
                pltpu.make_async_copy(wdn_hbm.at[e], wdn_f.at[s], sem.at[1, s]))

    def row_copies(tile, xs, wait, outward=False):
        buf, sems = (ybuf, ysem) if outward else (xbuf, xsem)
        grpc_hbm, grpl_hbm = (ysc_hbm, ysl_hbm) if outward else (far_c, far_l)

        def strip(hbm, src, dst, n):
            far = hbm.at[pl.ds(pl.multiple_of(src, SEG), n)]
            near = buf.at[xs, pl.ds(pl.multiple_of(dst, SEG), n)]
            cp = (pltpu.make_async_copy(near, far, sems.at[xs]) if outward
                  else pltpu.make_async_copy(far, near, sems.at[xs]))
            cp.wait() if wait else cp.start()

        def strips_of(hbm, base):
            def body(j, carry):
                n = pl.multiple_of(len_ref[j], SEG)

                @pl.when(n > 0)
                def _():
                    strip(hbm, src_ref[j] - base, dst_ref[j], n)
                return carry
            return body

        lax.fori_loop(ptr_ref[tile], mid_ref[tile], strips_of(grpc_hbm, 0), 0)
        lax.fori_loop(mid_ref[tile], ptr_ref[tile + 1], strips_of(grpl_hbm, ctx_rows), 0)
        n = pl.multiple_of(clen_ref[tile], SEG)
        src = csrc_ref[tile]

        @pl.when(jnp.logical_and(n > 0, src < ctx_rows))
        def _():
            strip(grpc_hbm, src, 0, n)

        @pl.when(jnp.logical_and(n > 0, src >= ctx_rows))
        def _():
            strip(grpl_hbm, src - ctx_rows, 0, n)

    def zero_unused(wait):
        def body(t, carry, hbm, first):
            n = pl.multiple_of(NR - tot_ref[t], SEG)
            row = pl.multiple_of((t - first) * NR + tot_ref[t], SEG)
            cp = pltpu.make_async_copy(zbuf.at[pl.ds(0, n)], hbm.at[pl.ds(row, n)], zsem)
            cp.wait() if wait else cp.start()
            return carry
        lax.fori_loop(0, n_ctx, functools.partial(body, hbm=ysc_hbm, first=0), 0)
        lax.fori_loop(n_ctx, n_tok, functools.partial(body, hbm=ysl_hbm, first=n_ctx), 0)

    @pl.when(i == 0)
    def _():
        xbuf[...] = jnp.zeros(xbuf.shape, F32)
        zbuf[...] = jnp.zeros(zbuf.shape, F32)
        row_copies(i, 0, False)
        zero_unused(False)
        zero_unused(True)

    @pl.when(i + 1 < nv_ref[0])
    def _():
        row_copies(i + 1, (i + 1) % 2, False)

    @pl.when(jnp.logical_and(i >= 2, i - 2 < nv_ref[0]))
    def _():
        row_copies(i - 2, i % 2, True, outward=True)

    @pl.when(i < nv_ref[0])
    def _():
        e = te_ref[i]
        s = slot_ref[i]
        row_copies(i, i % 2, True)

        @pl.when(first_ref[i] == 1)
        def _():
            @pl.when(i == 0)
            def _():
                for cp in weight_copies(e, s):
                    cp.start()

            for cp in weight_copies(e, s):
                cp.wait()
            nxt = next_ref[i]

            @pl.when(nxt >= 0)
            def _():
                for cp in weight_copies(nxt, 1 - s):
                    cp.start()

            rows = 64

            def cast(r, carry):
                sl = pl.ds(pl.multiple_of(r * rows, rows), rows)
                wgu_s[sl, :] = wgu_f[s, sl, :].astype(BF16)
                wdn_s[sl, :] = wdn_f[s, sl, :].astype(BF16)
                return carry

            lax.fori_loop(0, D_MODEL // rows, cast, 0)

        def compute(m):
            x = xbuf[i % 2, 0:m, :].astype(BF16)
            gu = jnp.dot(x, wgu_s[...], preferred_element_type=F32) + bgu_ref[0]
            gate = jnp.minimum(gu[:, 0:D_FF], SWIGLU_LIMIT)
            up = jnp.clip(gu[:, D_FF:2 * D_FF], -SWIGLU_LIMIT, SWIGLU_LIMIT)
            act = (gate * jax.nn.sigmoid(SWIGLU_ALPHA * gate) * (up + 1.0)).astype(BF16)
            ybuf[i % 2, 0:m, :] = jnp.dot(act, wdn_s[...], preferred_element_type=F32) + bdn_ref[0]

        for m in range(TE // 4, TE + 1, TE // 4):
            pl.when(rows_ref[i] == m)(functools.partial(compute, m))
        row_copies(i, i % 2, False, outward=True)

    @pl.when(i == n_steps - 1)
    def _():
        @pl.when(jnp.logical_and(i >= 1, i - 1 < nv_ref[0]))
        def _():
            row_copies(i - 1, (i - 1) % 2, True, outward=True)

        @pl.when(i < nv_ref[0])
        def _():
            row_copies(i, i % 2, True, outward=True)


def _experts(plan, grp_c, grp_l, n_rows, w_gu, b_gu, w_dn, b_dn):
    nt = n_rows // TE
    exp3 = lambda i, te, *_: (te[i], 0, 0)
    any_spec = pl.BlockSpec(memory_space=pl.ANY)
    n_ctx, n_lat = grp_c.shape[0] // NR, grp_l.shape[0] // NR
    return pl.pallas_call(
        functools.partial(_expert_kernel, ctx_rows=grp_c.shape[0], n_ctx=n_ctx, n_tok=n_ctx + n_lat, n_steps=nt),
        grid_spec=pltpu.PrefetchScalarGridSpec(
            num_scalar_prefetch=14,
            grid=(nt,),
            in_specs=[any_spec, any_spec, any_spec,
                      pl.BlockSpec((1, 1, 2 * D_FF), exp3), any_spec,
                      pl.BlockSpec((1, 1, D_MODEL), exp3)],
            out_specs=[any_spec, any_spec],
            scratch_shapes=[pltpu.VMEM((2, TE, D_MODEL), F32), pltpu.VMEM((2, TE, D_MODEL), F32),
                            pltpu.VMEM((NR - TOP_K * TM, D_MODEL), F32),
                            pltpu.VMEM((2, D_MODEL, 2 * D_FF), F32), pltpu.VMEM((2, D_FF, D_MODEL), F32),
                            pltpu.VMEM((D_MODEL, 2 * D_FF), BF16), pltpu.VMEM((D_FF, D_MODEL), BF16),
                            pltpu.SemaphoreType.DMA((2,)), pltpu.SemaphoreType.DMA((2,)),
                            pltpu.SemaphoreType.DMA, pltpu.SemaphoreType.DMA((2, 2))]),
        out_shape=[jax.ShapeDtypeStruct(grp_c.shape, F32), jax.ShapeDtypeStruct(grp_l.shape, F32)],
        compiler_params=_cparams(("arbitrary",), VMEM_LIMIT),
        name="experts",
    )(plan["tile_expert"], plan["n_valid"], plan["tile_first"], plan["tile_next"], plan["tile_slot"],
      plan["tile_rows"], plan["seg_ptr"], plan["seg_mid"], plan["seg_src"], plan["seg_dst"], plan["seg_len"],
      plan["carry_src"], plan["carry_len"], plan["tile_total"], grp_c, grp_l,
      w_gu, b_gu.reshape(N_EXPERTS, 1, 2 * D_FF), w_dn, b_dn.reshape(N_EXPERTS, 1, D_MODEL))


def _final_kernel(x1_ref, route_ref, mod_ref, fg_ref, ys_ref, y_ref):
    route = route_ref[...]
    r = lax.broadcasted_iota(jnp.int32, (TM, NR), 1).astype(F32)
    comb = jnp.zeros((TM, NR), F32)
    for kk in range(TOP_K):
        gate = route[:, 2 * TOP_K + kk:2 * TOP_K + kk + 1]
        comb = jnp.where(route[:, kk:kk + 1] == r, gate, comb)
    moe = jnp.dot(comb.astype(BF16), ys_ref[...].astype(BF16), preferred_element_type=F32)
    gate2 = mod_ref[0, :, 5 * D_MODEL:6 * D_MODEL]
    y_ref[...] = _rms(x1_ref[...] + gate2 * moe, fg_ref[...])


def _final(x1, route, mod3, seg_fn, fg, ys):
    t_path = x1.shape[0]
    loc = lambda w: pl.BlockSpec((TM, w), lambda i: (i, 0))
    return pl.pallas_call(
        _final_kernel,
        grid=(t_path // TM,),
        in_specs=[loc(D_MODEL), loc(LANES),
                  pl.BlockSpec((1, 1, 6 * D_MODEL), lambda i: (seg_fn(i), 0, 0)),
                  pl.BlockSpec((1, D_MODEL), lambda i: (0, 0)),
                  pl.BlockSpec((NR, D_MODEL), lambda i: (i, 0))],
        out_specs=loc(D_MODEL),
        out_shape=jax.ShapeDtypeStruct((t_path, D_MODEL), F32),
        compiler_params=_cparams(("arbitrary",), VMEM_LIMIT),
        name="final",
    )(x1, route, mod3, fg, ys)


def _plan(tile_counts, n_rows, n_ctx):
    nt = n_rows // TE
    n_tok = tile_counts.shape[0]
    strip = (tile_counts + SEG - 1) // SEG * SEG
    lstart = jnp.cumsum(strip, axis=1) - strip
    rows_e = jnp.sum(strip, axis=0)
    rpad = (rows_e + TE - 1) // TE * TE
    ends = jnp.cumsum(rpad)
    offs = ends - rpad
    gstart = offs[None, :] + jnp.cumsum(strip, axis=0) - strip
    n_valid = (ends[-1] // TE).astype(jnp.int32)
    tile_ids = jnp.minimum(jnp.arange(nt, dtype=jnp.int32), n_valid - 1)
    tile_expert = jnp.sum((ends[None, :] <= (tile_ids * TE)[:, None]).astype(jnp.int32), axis=1)
    tile_expert = jnp.minimum(tile_expert, N_EXPERTS - 1).astype(jnp.int32)
    prev = jnp.concatenate([jnp.full((1,), -1, jnp.int32), tile_expert[:-1]])
    tile_first = (tile_expert != prev).astype(jnp.int32)
    tile_slot = (jnp.cumsum(tile_first) - 1) % 2
    pick = lambda table, idx: jnp.sum(
        jnp.where(idx[:, None] == jnp.arange(table.shape[0], dtype=jnp.int32)[None, :], table[None, :], 0), axis=1)
    used = jnp.clip(pick(offs + rows_e, tile_expert) - jnp.arange(nt, dtype=jnp.int32) * TE, 0, TE)
    tile_rows = jnp.maximum((used + TE // 4 - 1) // (TE // 4), 1) * (TE // 4)
    after = pick(ends, tile_expert) // TE
    tile_next = jnp.where(after < n_valid, pick(tile_expert, jnp.minimum(after, nt - 1)), -1)
    tok = jnp.broadcast_to(jnp.arange(n_tok, dtype=jnp.int32)[:, None], strip.shape)
    g0 = gstart.T.reshape(-1)
    ln = strip.T.reshape(-1)
    src0 = (tok * NR + lstart).T.reshape(-1)
    is_ctx = (tok < n_ctx).T.reshape(-1)
    len1 = jnp.minimum(ln, TE - g0 % TE)
    tile0 = g0 // TE
    tiles = jnp.arange(nt + 1, dtype=jnp.int32)[:, None]
    count = lambda m: jnp.sum(m.astype(jnp.int32), axis=1)
    seg_ptr = count(tile0[None, :] < tiles)
    seg_mid = seg_ptr + count((tile0[None, :] == tiles) & is_ctx[None, :])
    into = ((g0 + len1) // TE)[None, :] == tiles[:nt]
    carry_len = jnp.sum(jnp.where(into, (ln - len1)[None, :], 0), axis=1)
    carry_src = jnp.sum(jnp.where(into & (ln > len1)[None, :], (src0 + len1)[None, :], 0), axis=1)
    i32 = lambda a: a.reshape(-1).astype(jnp.int32)
    return dict(tile_total=i32(jnp.sum(strip, axis=1)), n_valid=n_valid.reshape(1),
                tile_expert=tile_expert, tile_first=i32(tile_first), tile_next=i32(tile_next),
                tile_slot=i32(tile_slot), tile_rows=i32(tile_rows), seg_ptr=i32(seg_ptr), seg_mid=i32(seg_mid), seg_src=i32(src0),
                seg_dst=i32(g0 % TE), seg_len=i32(len1), carry_src=i32(carry_src),
                carry_len=i32(carry_len))


def _layer(x_prompt, x_sample, state, c, c_ctx, ada_w, ada_b, norm1_g, norm2_g, w_in, conv_w, conv_b,
           conv_ln_g, conv_ln_b, gate_w, gate_b, gla_norm_g, w_out, router_w, router_b,
           moe_w_gu, moe_b_gu, moe_w_dn, moe_b_dn, final_g):
    bp, lp, d = x_prompt.shape
    bs, ls, _ = x_sample.shape
    assert lp == TM and ls % TM == 0 and d == D_MODEL
    xp = x_prompt.reshape(bp * lp, d)
    xs = x_sample.reshape(bs * ls, d)
    n_c, n_l = bp, bs * ls // TM
    lat_tiles = ls // TM
    t_all = (n_c + n_l) * TM

    n_cond = 1 + bs
    cond_t = jnp.zeros((d, 8), F32).at[:, 0].set(c_ctx).at[:, 1:n_cond].set(c.T)
    mod3 = _modulation(cond_t, ada_w, ada_b, n_cond).reshape(8, 1, 6 * d)

    row = lambda a: a.reshape(1, -1)
    u, qk, v, g, la = _inproj(xp, xs, mod3, row(norm1_g), w_in, gate_w, gate_b, ls)

    pair = lambda s: s.reshape(s.shape[0], 2, 2, LANES, LANES)
    of_c, ob_c, s_ctx = _gla(qk, v, la, jnp.zeros((bp, 2, 2, LANES, LANES), F32), bp, 1, 0)
    of_l, ob_l, _ = _gla(qk, v, la, pair(state), bs, lat_tiles, n_c)

    cw = jnp.zeros((32, C_CONV), F32).at[:CONV_K].set(conv_w)
    rwh, rwl = _split(router_w.T)
    rb = jnp.broadcast_to(router_b[:, None], (N_EXPERTS, TM))
    wts = (cw, row(conv_b), row(conv_ln_g), row(conv_ln_b), row(gla_norm_g), w_out.astype(BF16),
           row(norm2_g), rwh, rwl, rb)
    seg_c = lambda i: 0
    seg_l = lambda i: 1 + i // lat_tiles
    (x1_c, grp_c, route_c, cnt_c), (x1_l, grp_l, route_l, cnt_l) = _mix(
        [(xp, of_c, ob_c, seg_c, 0, lp), (xs, of_l, ob_l, seg_l, n_c, GRID_W)], u, g, mod3, wts)

    tile_counts = jnp.concatenate([cnt_c[:, :, 0], cnt_l[:, :, 0]]).astype(jnp.int32)
    n_rows = TOP_K * t_all + (SEG - 1) * N_EXPERTS * (n_c + n_l) + N_EXPERTS * TE
    n_rows = (n_rows + TE - 1) // TE * TE
    plan = _plan(tile_counts, n_rows, n_c)
    ys_c, ys_l = _experts(plan, grp_c, grp_l, n_rows, moe_w_gu, moe_b_gu, moe_w_dn, moe_b_dn)
    y_c = _final(x1_c, route_c, mod3, seg_c, row(final_g), ys_c)
    y_l = _final(x1_l, route_l, mod3, seg_l, row(final_g), ys_l)
    new_state = s_ctx.reshape(bp, 1, 2, N_GLA_HEADS, DK_HEAD, DV_HEAD)
    return y_c.reshape(bp, lp, d), y_l.reshape(bs, ls, d), new_state


def kernel(x_prompt, x_sample, state_gla, c, c_ctx, ada_w, ada_b, norm1_g, norm2_g, w_in, conv_w,
           conv_b, conv_ln_g, conv_ln_b, gate_w, gate_b, gla_norm_g, w_out, router_w, router_b,
           moe_w_gu, moe_b_gu, moe_w_dn, moe_b_dn, final_g):
    assert ada_w.shape[0] == 1, "single-layer step"
    return _layer(x_prompt, x_sample, state_gla[:, 0], c, c_ctx, ada_w[0], ada_b[0], norm1_g[0],
                  norm2_g[0], w_in[0], conv_w[0], conv_b[0], conv_ln_g[0], conv_ln_b[0], gate_w[0],
                  gate_b[0], gla_norm_g[0], w_out[0], router_w[0], router_b[0], moe_w_gu[0],
                  moe_b_gu[0], moe_w_dn[0], moe_b_dn[0], final_g)
```

```python
import functools

import numpy as np
import jax
import jax.numpy as jnp
from jax import lax
from jax.experimental import pallas as pl
from jax.experimental.pallas import tpu as pltpu

D_MODEL = 1024
GRID_W = 64
C_CONV = D_MODEL // 2
CONV_K = 31
N_GLA_HEADS = 4
DV_HEAD = 128
DK_HEAD = 64
DK_TOT = DK_HEAD * N_GLA_HEADS
DV_TOT = DV_HEAD * N_GLA_HEADS
GATE_RANK = 16
GATE_TEMP = 16.0
CHUNK = 64
N_EXPERTS = 32
TOP_K = 4
D_FF = D_MODEL
SWIGLU_LIMIT = 7.0
SWIGLU_ALPHA = 1.702
EPS = 1e-6

LANES = 128
SEG = 8
TM = 256
TE = 512
NR = TOP_K * TM + TM
CONV_PAD = 16
V7X_VMEM_BYTES = 64 * 1024 * 1024
VMEM_LIMIT = V7X_VMEM_BYTES - 8 * 1024 * 1024

F32 = jnp.float32
BF16 = jnp.bfloat16
HI = lax.Precision.HIGHEST
TN =(((0,), (0,)), ((), ()))
NT = (((1,), (1,)), ((), ()))

assert N_EXPERTS * (SEG - 1) <= NR - TOP_K * TM


def _split(x):
    hi = x.astype(BF16)
    return hi, (x - hi.astype(F32)).astype(BF16)


def _cparams(sem, vmem=None):
    return pltpu.CompilerParams(dimension_semantics=sem, vmem_limit_bytes=vmem)


def _mod_kernel(ct_ref, w_ref, b_ref, o_ref, *, n_cond):
    ct = ct_ref[...]
    s = ct * jax.nn.sigmoid(ct)
    w = w_ref[...]
    rows = [jnp.sum(s[:, r:r + 1] * w, axis=0, keepdims=True) + b_ref[...] for r in range(n_cond)]
    rows.append(jnp.zeros((8 - n_cond, w.shape[1]), F32))
    o_ref[...] = jnp.concatenate(rows, axis=0)


def _modulation(cond_t, ada_w, ada_b, n_cond):
    d, n = ada_w.shape
    nb = 768
    return pl.pallas_call(
        functools.partial(_mod_kernel, n_cond=n_cond),
        grid=(n // nb,),
        in_specs=[pl.BlockSpec((d, 8), lambda i: (0, 0)),
                  pl.BlockSpec((d, nb), lambda i: (0, i)),
                  pl.BlockSpec((1, nb), lambda i: (0, i))],
        out_specs=pl.BlockSpec((8, nb), lambda i: (0, i)),
        out_shape=jax.ShapeDtypeStruct((8, n), F32),
        compiler_params=_cparams(("arbitrary",)),
        name="mod",
    )(cond_t, ada_w, ada_b.reshape(1, n))


def _rms(x, g):
    return x * lax.rsqrt(jnp.mean(x * x, axis=-1, keepdims=True) + EPS) * g


def _inproj_kernel(xp_ref, xs_ref, mod_ref, g1_ref, wu_ref, wqk_ref, wv_ref, wg_ref, wlow_ref,
                   gwh_ref, gwl_ref, gb_ref, u_ref, qk_ref, v_ref, g_ref, la_ref, *, n_ctx_tiles):
    i = pl.program_id(0)
    shift = mod_ref[0, :, 0:D_MODEL]
    scale = mod_ref[0, :, D_MODEL:2 * D_MODEL]
    dot = functools.partial(jnp.dot, preferred_element_type=F32)
    rows = u_ref.shape[0]
    sub = min(rows, TM)
    def block(r0):
        rs = slice(r0, r0 + sub)
        x = jnp.where(i < n_ctx_tiles, xp_ref[rs, :], xs_ref[rs, :])
        h = (_rms(x, g1_ref[...]) * (1.0 + scale) + shift).astype(BF16)
        yield
        u_ref[rs, :] = dot(h, wu_ref[...]).astype(BF16)
        yield
        qk_ref[rs, :] = dot(h, wqk_ref[...]).astype(BF16)
        v_ref[rs, :] = dot(h, wv_ref[...]).astype(BF16)
        yield
        g_ref[rs, :] = dot(h, wg_ref[...]).astype(BF16)
        low = dot(h, wlow_ref[...])
        yield
        low_hi, low_lo = _split(low)
        z = (dot(low_hi, gwh_ref[...]) + dot(low_lo, gwh_ref[...]) + dot(low_hi, gwl_ref[...])
             + gb_ref[...])
        la_ref[rs, :] = (jnp.minimum(z, 0.0) - jnp.log1p(jnp.exp(-jnp.abs(z)))) * (1.0 / GATE_TEMP)

    blocks = [block(r0) for r0 in range(0, rows, sub)]
    while blocks:
        blocks = [gen for gen in blocks if next(gen, "done") != "done"]


def _inproj(xp, xs, mod3, g1, w_in, gate_w, gate_b, lat_len):
    ti = next(t for t in (1024, 512, TM) if xp.shape[0] % t == 0 and lat_len % t == 0)
    lat_tiles_per_seq = lat_len // ti
    n_c, n_l = xp.shape[0] // ti, xs.shape[0] // ti
    t_all = xp.shape[0] + xs.shape[0]
    sp = np.cumsum([0, C_CONV, C_CONV, DK_TOT, DK_TOT, DV_TOT, DV_TOT, 2 * GATE_RANK])
    wu, wqk, wv, wg, wlow = (w_in[:, a:b].astype(BF16) for a, b in
                             ((sp[0], sp[2]), (sp[2], sp[4]), (sp[4], sp[5]), (sp[5], sp[6]), (sp[6], sp[7])))
    gw = jnp.zeros((2 * GATE_RANK, 2 * DK_TOT), F32)
    gw = gw.at[:GATE_RANK, :DK_TOT].set(gate_w[0]).at[GATE_RANK:, DK_TOT:].set(gate_w[1])
    gwh, gwl = _split(gw)
    gb = gate_b.reshape(1, 2 * DK_TOT)
    const = lambda a: pl.BlockSpec(a.shape, lambda i: (0,) * a.ndim)
    row = lambda w: pl.BlockSpec((ti, w), lambda i: (i, 0))
    seg = lambda i: jnp.where(i < n_c, 0, 1 + jnp.maximum(i - n_c, 0) // lat_tiles_per_seq)
    return pl.pallas_call(
        functools.partial(_inproj_kernel, n_ctx_tiles=n_c),
        grid=(n_c + n_l,),
        in_specs=[pl.BlockSpec((ti, D_MODEL), lambda i: (jnp.minimum(i, n_c - 1), 0)),
                  pl.BlockSpec((ti, D_MODEL), lambda i: (jnp.maximum(i - n_c, 0), 0)),
                  pl.BlockSpec((1, 1, 6 * D_MODEL), lambda i: (seg(i), 0, 0)),
                  const(g1), const(wu), const(wqk), const(wv), const(wg), const(wlow),
                  const(gwh), const(gwl), const(gb)],
        out_specs=[row(2 * C_CONV), row(2 * DK_TOT), row(DV_TOT), row(DV_TOT), row(2 * DK_TOT)],
        out_shape=[jax.ShapeDtypeStruct((t_all, 2 * C_CONV), BF16),
                   jax.ShapeDtypeStruct((t_all, 2 * DK_TOT), BF16),
                   jax.ShapeDtypeStruct((t_all, DV_TOT), BF16),
                   jax.ShapeDtypeStruct((t_all, DV_TOT), BF16),
                   jax.ShapeDtypeStruct((t_all, 2 * DK_TOT), F32)],
        compiler_params=_cparams(("arbitrary",), VMEM_LIMIT),
        name="inproj",
    )(xp, xs, mod3, g1, wu, wqk, wv, wg, wlow, gwh, gwl, gb)


def _gla_direction(qk_ref, v_ref, la_ref, tri_ref, blk_ref, o_ref, s_scr, reverse):
    la_hi, la_lo = _split(la_ref[...])
    tri = tri_ref[...]
    bcum = (jnp.dot(tri, la_hi, preferred_element_type=F32)
            + jnp.dot(tri, la_lo, preferred_element_type=F32))
    blk = blk_ref[...]
    bl_cols = (lax.dot_general(la_hi, blk, TN, preferred_element_type=F32)
               + lax.dot_general(la_lo, blk, TN, preferred_element_type=F32))
    yield
    q = qk_ref[:, 0:DK_TOT].astype(F32)
    k = qk_ref[:, DK_TOT:2 * DK_TOT].astype(F32)
    lane = lax.broadcasted_iota(jnp.int32, (CHUNK, LANES), 1)
    row2 = lax.broadcasted_iota(jnp.int32, (2 * CHUNK, CHUNK), 0) % CHUNK
    col2 = lax.broadcasted_iota(jnp.int32, (2 * CHUNK, CHUNK), 1)
    keep = (col2 >= row2) if reverse else (col2 <= row2)
    srow = lax.broadcasted_iota(jnp.int32, (LANES, LANES), 0)
    n_chunks = TM // CHUNK
    order = range(n_chunks - 1, -1, -1) if reverse else range(n_chunks)
    state = [s_scr[0], s_scr[1]]
    for c in order:
        r0 = c * CHUNK
        bc = bcum[r0:r0 + CHUNK]
        bl = bc[0:1] if reverse else bc[CHUNK - 1:CHUNK]
        qt = q[r0:r0 + CHUNK] * jnp.exp(bc) * (DK_HEAD ** -0.5)
        kt = k[r0:r0 + CHUNK] * jnp.exp(-bc)
        ke = k[r0:r0 + CHUNK] * jnp.exp(bl - bc)
        yield
        for p in range(2):
            cs = slice(p * LANES, (p + 1) * LANES)
            qt_p = qt[:, cs]
            qs = jnp.concatenate([jnp.where(lane < DK_HEAD, qt_p, 0.0),
                                  jnp.where(lane >= DK_HEAD, qt_p, 0.0)], axis=0).astype(BF16)
            att = lax.dot_general(qs, kt[:, cs].astype(BF16), NT, preferred_element_type=F32)
            att = jnp.where(keep, att, 0.0).astype(BF16)
            s_p = state[p]
            o_inter = jnp.dot(qs, s_p.astype(BF16), preferred_element_type=F32)
            ke_t = jnp.transpose(ke[:, cs]).astype(BF16)
            upd = []
            for hh in range(2):
                h = 2 * p + hh
                v_h = v_ref[r0:r0 + CHUNK, h * DV_HEAD:(h + 1) * DV_HEAD]
                both = jnp.dot(jnp.concatenate([att[hh * CHUNK:(hh + 1) * CHUNK], ke_t], axis=0), v_h,
                               preferred_element_type=F32)
                o_ref[r0:r0 + CHUNK, h * DV_HEAD:(h + 1) * DV_HEAD] = (
                    both[0:CHUNK] + o_inter[hh * CHUNK:(hh + 1) * CHUNK])
                upd.append(both[CHUNK:CHUNK + LANES])
            bl_col = bl_cols[p * LANES:(p + 1) * LANES, c * LANES:(c + 1) * LANES]
            state[p] = jnp.exp(bl_col) * s_p + jnp.where(srow < DK_HEAD, upd[0], upd[1])
            yield
    s_scr[0] = state[0]
    s_scr[1] = state[1]


def _gla_kernel(*refs, n_tiles, nb):
    seqs = [refs[6 * r:6 * r + 6] for r in range(nb)]
    tril_ref, triu_ref, blk_ref, s0_ref, of_ref, ob_ref, sout_ref, sf_scr, sb_scr = refs[6 * nb:]
    j = pl.program_id(1)

    @pl.when(j == 0)
    def _():
        sf_scr[...] = s0_ref[:, 0]
        sb_scr[...] = s0_ref[:, 1]

    scans = []
    for r, (qkf_ref, vf_ref, laf_ref, qkb_ref, vb_ref, lab_ref) in enumerate(seqs):
        scans.append(_gla_direction(qkf_ref, vf_ref, laf_ref, tril_ref, blk_ref, of_ref.at[r], sf_scr.at[r], False))
        scans.append(_gla_direction(qkb_ref, vb_ref, lab_ref, triu_ref, blk_ref, ob_ref.at[r], sb_scr.at[r], True))
    while scans:
        scans = [gen for gen in scans if next(gen, "done") != "done"]

    @pl.when(j == n_tiles - 1)
    def _():
        sout_ref[:, 0] = sf_scr[...]
        sout_ref[:, 1] = sb_scr[...]


def _gla(qk, v, la, s0, n_seq, n_tiles, tile_off):
    nb = next(b for b in (4, 2, 1) if n_seq % b == 0)
    seq_len = n_tiles * TM
    blk = np.arange(TM) // CHUNK
    same = blk[:, None] == blk[None, :]
    r = np.arange(TM)
    tril = jnp.asarray((same & (r[None, :] <= r[:, None])).astype(np.float32)).astype(BF16)
    triu = jnp.asarray((same & (r[None, :] >= r[:, None])).astype(np.float32)).astype(BF16)
    chunk_of_lane = np.arange(TM // CHUNK * LANES) // LANES
    blkm = jnp.asarray((blk[:, None] == chunk_of_lane[None, :]).astype(np.float32)).astype(BF16)
    def tok(w, r, backward, cb=0):
        def index(s, j):
            jj = (n_tiles - 1 - j) if backward else j
            return ((s * nb + r) * n_tiles + jj + tile_off, cb)
        return pl.BlockSpec((TM, w), index)

    const = lambda a: pl.BlockSpec(a.shape, lambda s, j: (0,) * a.ndim)
    st = pl.BlockSpec((nb, 2, 2, LANES, LANES), lambda s, j: (s, 0, 0, 0, 0))
    seq_specs, seq_args = [], []
    for r in range(nb):
        seq_specs += [tok(2 * DK_TOT, r, False), tok(DV_TOT, r, False), tok(DK_TOT, r, False, 0),
                      tok(2 * DK_TOT, r, True), tok(DV_TOT, r, True), tok(DK_TOT, r, True, 1)]
        seq_args += [qk, v, la, qk, v, la]
    o_f, o_b, s_out = pl.pallas_call(
        functools.partial(_gla_kernel, n_tiles=n_tiles, nb=nb),
        grid=(n_seq // nb, n_tiles),
        in_specs=seq_specs + [const(tril), const(triu), const(blkm), st],
        out_specs=[pl.BlockSpec((nb, TM, DV_TOT), lambda s, j: (s, j, 0)),
                   pl.BlockSpec((nb, TM, DV_TOT), lambda s, j: (s, n_tiles - 1 - j, 0)), st],
        out_shape=[jax.ShapeDtypeStruct((n_seq, seq_len, DV_TOT), F32),
                   jax.ShapeDtypeStruct((n_seq, seq_len, DV_TOT), F32),
                   jax.ShapeDtypeStruct(s0.shape, F32)],
        scratch_shapes=[pltpu.VMEM((nb, 2, LANES, LANES), F32), pltpu.VMEM((nb, 2, LANES, LANES), F32)],
        compiler_params=_cparams(("arbitrary", "arbitrary"), VMEM_LIMIT),
        name="gla",
    )(*seq_args, tril, triu, blkm, s0)
    return o_f.reshape(n_seq * seq_len, DV_TOT), o_b.reshape(n_seq * seq_len, DV_TOT), s_out


def _mix_conv(u_ref, cw_ref, pad_scr, shf_scr, cv_scr, seq_len):
    n_seq = TM // seq_len
    pr = seq_len + 2 * CONV_PAD
    qr = seq_len + 3 * SEG

    u = u_ref[...].astype(F32)
    hg = u[:, 0:C_CONV] * jax.nn.sigmoid(u[:, C_CONV:2 * C_CONV])
    zero = jnp.zeros((CONV_PAD, C_CONV), F32)
    for s in range(n_seq):
        pad_scr[s * pr:s * pr + CONV_PAD, :] = zero
        pad_scr[s * pr + CONV_PAD:s * pr + CONV_PAD + seq_len, :] = hg[s * seq_len:(s + 1) * seq_len]
        pad_scr[s * pr + CONV_PAD + seq_len:(s + 1) * pr, :] = zero
        for b in range(1, SEG):
            shf_scr[b - 1, s * qr:(s + 1) * qr, :] = pad_scr[s * pr + b:s * pr + b + qr, :]
    rb = 8 * SEG
    per_seq = seq_len // rb
    for gi in range(C_CONV // LANES):
        gs = slice(gi * LANES, (gi + 1) * LANES)
        wv = [jnp.broadcast_to(cw_ref[tap:tap + 1, gs], (SEG, LANES)) for tap in range(CONV_K)]

        def conv_rows(i, carry, gs=gs, wv=wv):
            s = lax.shift_right_logical(i, per_seq.bit_length() - 1)
            r0 = (i & (per_seq - 1)) * rb
            nj = rb // SEG
            acc = [None] * nj
            for b in range(SEG):
                taps = [(a, SEG * a + b - (CONV_PAD - CONV_K // 2)) for a in range(CONV_PAD * 2 // SEG)]
                taps = [(a, t) for a, t in taps if 0 <= t < CONV_K]
                loaded = {}
                for a, t in taps:
                    for j in range(nj):
                        if a + j not in loaded:
                            if b == 0:
                                row = pl.multiple_of(s * pr + r0 + (a + j) * SEG, SEG)
                                loaded[a + j] = pad_scr[pl.ds(row, SEG), gs]
                            else:
                                row = pl.multiple_of(s * qr + r0 + (a + j) * SEG, SEG)
                                loaded[a + j] = shf_scr[b - 1, pl.ds(row, SEG), gs]
                        term = wv[t] * loaded[a + j]
                        acc[j] = term if acc[j] is None else acc[j] + term
            for j in range(nj):
                cv_scr[pl.ds(pl.multiple_of(s * seq_len + r0 + j * SEG, SEG), SEG), gs] = acc[j]
            return carry

        lax.fori_loop(0, TM // rb, conv_rows, 0)


def _mix_rest(x_ref, g_ref, of_ref, ob_ref, mod_ref, cv_scr, cb_ref, lng_ref, lnb_ref, gng_ref, wout_ref,
              n2g_ref, rwh_ref, rwl_ref, rb_ref, le_ref, ut_ref, x1_ref, grp_ref, route_ref, cnt_ref):
    cv = cv_scr[...] + cb_ref[...]
    mu = jnp.mean(cv, axis=-1, keepdims=True)
    var = jnp.mean(jnp.square(cv - mu), axis=-1, keepdims=True)
    cv =(cv - mu) * lax.rsqrt(var + EPS) * lng_ref[...] + lnb_ref[...]
    conv_out = (cv * jax.nn.sigmoid(cv)).astype(BF16)
    yield

    o = of_ref[...] + ob_ref[...]
    g = g_ref[...].astype(F32)
    heads = []
    for h in range(N_GLA_HEADS):
        cs = slice(h * DV_HEAD, (h + 1) * DV_HEAD)
        oh = o[:, cs]
        oh = oh * lax.rsqrt(jnp.mean(oh * oh, axis=-1, keepdims=True) + EPS) * gng_ref[...]
        gh = g[:, cs]
        heads.append((oh * (gh * jax.nn.sigmoid(gh))).astype(BF16))
    yield
    y = jnp.dot(conv_out, wout_ref[0:C_CONV, :], preferred_element_type=F32)
    for h in range(N_GLA_HEADS):
        y = y + jnp.dot(heads[h], wout_ref[C_CONV + h * DV_HEAD:C_CONV + (h + 1) * DV_HEAD, :],
                        preferred_element_type=F32)
    yield

    gate1 = mod_ref[0, :, 2 * D_MODEL:3 * D_MODEL]
    shift2 = mod_ref[0, :, 3 * D_MODEL:4 * D_MODEL]
    scale2 = mod_ref[0, :, 4 * D_MODEL:5 * D_MODEL]
    x1 = x_ref[...] + gate1 * y
    x1_ref[...] = x1
    h2 = _rms(x1, n2g_ref[...]) * (1.0 + scale2) + shift2
    h2_hi, h2_lo = _split(h2)
    yield

    ntdot = functools.partial(lax.dot_general, dimension_numbers=NT, preferred_element_type=F32)
    l = (ntdot(rwh_ref[...], h2_hi) + ntdot(rwh_ref[...], h2_lo) + ntdot(rwl_ref[...], h2_hi)
         + rb_ref[...])
    yield
    erow = lax.broadcasted_iota(jnp.int32, (N_EXPERTS, TM), 0).astype(F32)
    vals, idxs, hots = [], [], []
    for _ in range(TOP_K):
        m = jnp.max(l, axis=0, keepdims=True)
        idx = jnp.min(jnp.where(l == m, erow, float(N_EXPERTS)), axis=0, keepdims=True)
        hot = erow == idx
        l = jnp.where(hot, -jnp.inf, l)
        vals.append(m)
        idxs.append(idx)
        hots.append(hot)
        yield
    es = [jnp.exp(vk - vals[0]) for vk in vals]
    inv = 1.0 / (es[0] + es[1] + es[2] + es[3])
    sel = jnp.zeros((N_EXPERTS, TM), F32)
    for hot in hots:
        sel = jnp.where(hot, 1.0, sel)

    cnt = jnp.broadcast_to(jnp.sum(sel, axis=1, keepdims=True), (N_EXPERTS, LANES))
    cnt_ref[0] = cnt
    strip = jnp.ceil(cnt * (1.0 / SEG)) * float(SEG)
    starts = jnp.dot(le_ref[...], strip, precision=HI, preferred_element_type=F32)[:, 0:1]
    slot_of = jnp.dot(sel.astype(BF16), ut_ref[...], preferred_element_type=F32) + starts
    slots = [jnp.sum(jnp.where(hot, slot_of, 0.0), axis=0, keepdims=True) for hot in hots]

    rows = slots + idxs + [e * inv for e in es]
    srow = lax.broadcasted_iota(jnp.int32, (16, TM), 0)
    packed = jnp.zeros((16, TM), F32)
    for j, v in enumerate(rows):
        packed = jnp.where(srow == j, v, packed)
    packed = jnp.concatenate([packed, jnp.zeros((LANES - 16, TM), F32)], axis=0)
    route_ref[...] = jnp.transpose(packed)
    yield

    r = lax.broadcasted_iota(jnp.int32, (NR, TM), 0).astype(F32)
    perm = jnp.zeros((NR, TM), F32)
    for slot in slots:
        perm = jnp.where(slot == r, 1.0, perm)
    yield
    grp_ref[...] = jnp.dot(perm.astype(BF16), h2_hi, preferred_element_type=F32)


def _mix_kernel(*refs, seq_lens):
    k = len(seq_lens)
    ins = [refs[6 * p:6 * p + 6] for p in range(k)]
    cw_ref, *shared = refs[6 * k:6 * k + 12]
    outs = [refs[6 * k + 12 + 4 * p:6 * k + 16 + 4 * p] for p in range(k)]
    scr = [refs[10 * k + 12 + 3 * p:10 * k + 15 + 3 * p] for p in range(k)]
    for p in range(k):
        _mix_conv(ins[p][1], cw_ref, *scr[p], seq_lens[p])
    stages = []
    for p in range(k):
        x_ref, _, g_ref, of_ref, ob_ref, mod_ref = ins[p]
        stages.append(_mix_rest(x_ref, g_ref, of_ref, ob_ref, mod_ref, scr[p][2], *shared, *outs[p]))
    while stages:
        stages = [gen for gen in stages if next(gen, "done") != "done"]


def _mix(paths, u, g, mod3, wts):
    (conv_w, conv_b, ln_g, ln_b, gng, wout, n2g, rwh, rwl, rb) = wts
    le = jnp.asarray(np.tril(np.ones((N_EXPERTS, N_EXPERTS), np.float32), -1))
    ut = jnp.asarray(np.triu(np.ones((TM, TM), np.float32), 1)).astype(BF16)
    shared = (conv_w, conv_b, ln_g, ln_b, gng, wout, n2g, rwh, rwl, rb, le, ut)
    const = lambda a: pl.BlockSpec(a.shape, lambda i: (0,) * a.ndim)
    n_steps = max(x.shape[0] // TM for x, *_ in paths)
    in_specs, args, out_specs, out_shape, scratch = [], [], [], [], []
    for x, o_f, o_b, seg_fn, tile_off, seq_len in paths:
        n = x.shape[0] // TM
        cur = lambda i, n=n: jnp.minimum(i, n - 1)
        loc = lambda w, cur=cur: pl.BlockSpec((TM, w), lambda i: (cur(i), 0))
        uni = lambda w, cur=cur, off=tile_off: pl.BlockSpec((TM, w), lambda i: (cur(i) + off, 0))
        in_specs += [loc(D_MODEL), uni(2 * C_CONV), uni(DV_TOT), loc(DV_TOT), loc(DV_TOT),
                     pl.BlockSpec((1, 1, 6 * D_MODEL), lambda i, cur=cur, f=seg_fn: (f(cur(i)), 0, 0))]
        args += [x, u, g, o_f, o_b, mod3]
        out_specs += [loc(D_MODEL), pl.BlockSpec((NR, D_MODEL), lambda i, cur=cur: (cur(i), 0)), loc(LANES),
                      pl.BlockSpec((1, N_EXPERTS, LANES), lambda i, cur=cur: (cur(i), 0, 0))]
        out_shape += [jax.ShapeDtypeStruct((n * TM, D_MODEL), F32),
                      jax.ShapeDtypeStruct((n * NR, D_MODEL), F32),
                      jax.ShapeDtypeStruct((n * TM, LANES), F32),
                      jax.ShapeDtypeStruct((n, N_EXPERTS, LANES), F32)]
        n_seq = TM // seq_len
        scratch += [pltpu.VMEM((n_seq * (seq_len + 2 * CONV_PAD), C_CONV), F32),
                    pltpu.VMEM((SEG - 1, n_seq * (seq_len + 3 * SEG), C_CONV), F32),
                    pltpu.VMEM((TM, C_CONV), F32)]
    outs = pl.pallas_call(
        functools.partial(_mix_kernel, seq_lens=tuple(p[5] for p in paths)),
        grid=(n_steps,),
        in_specs=in_specs + [const(a) for a in shared],
        out_specs=out_specs,
        out_shape=out_shape,
        scratch_shapes=scratch,
        compiler_params=_cparams(("arbitrary",), VMEM_LIMIT),
        name="mix",
    )(*args, *shared)
    return [outs[4 * p:4 * p + 4] for p in range(len(paths))]


def _strip_copies(tile, ls_ref, gs_ref, sg_ref, local, rows_hbm, sem, to_hbm, wait):
    def body(e, carry):
        j = tile * N_EXPERTS + e
        n = pl.multiple_of(sg_ref[j], SEG)

        @pl.when(n > 0)
        def _():
            loc = local.at[pl.ds(pl.multiple_of(ls_ref[j], SEG), n)]
            glob = rows_hbm.at[pl.ds(pl.multiple_of(gs_ref[j], SEG), n)]
            cp = pltpu.make_async_copy(loc, glob, sem) if to_hbm else pltpu.make_async_copy(glob, loc, sem)
            if wait:
                cp.wait()
            else:
                cp.start()
        return carry

    lax.fori_loop(0, N_EXPERTS, body, 0)


def _expert_kernel(te_ref, nv_ref, first_ref, next_ref, slot_ref, rows_ref, ptr_ref, mid_ref, src_ref, dst_ref,
                   len_ref, csrc_ref, clen_ref, grpc_hbm, grpl_hbm, wgu_hbm, bgu_ref, wdn_hbm, bdn_ref,
                   y_ref, xbuf, wgu_f, wdn_f, wgu_s, wdn_s, xsem, sem, *, ctx_rows):
    i = pl.program_id(0)

    def weight_copies(e, s):
        return (pltpu.make_async_copy(wgu_hbm.at[e], wgu_f.at[s], sem.at[0, s]),
                pltpu.make_async_copy(wdn_hbm.at[e], wdn_f.at[s], sem.at[1, s]))

    def row_copies(tile, xs, wait):
        def strip(hbm, src, dst, n):
            cp = pltpu.make_async_copy(hbm.at[pl.ds(pl.multiple_of(src, SEG), n)],
                                       xbuf.at[xs, pl.ds(pl.multiple_of(dst, SEG), n)], xsem.at[xs])
            cp.wait() if wait else cp.start()

        def strips_of(hbm, base):
            def body(j, carry):
                n = pl.multiple_of(len_ref[j], SEG)

                @pl.when(n > 0)
                def _():
                    strip(hbm, src_ref[j] - base, dst_ref[j], n)
                return carry
            return body

        lax.fori_loop(ptr_ref[tile], mid_ref[tile], strips_of(grpc_hbm, 0), 0)
        lax.fori_loop(mid_ref[tile], ptr_ref[tile + 1], strips_of(grpl_hbm, ctx_rows), 0)
        n = pl.multiple_of(clen_ref[tile], SEG)
        src = csrc_ref[tile]

        @pl.when(jnp.logical_and(n > 0, src < ctx_rows))
        def _():
            strip(grpc_hbm, src, 0, n)

        @pl.when(jnp.logical_and(n > 0, src >= ctx_rows))
        def _():
            strip(grpl_hbm, src - ctx_rows, 0, n)

    @pl.when(i == 0)
    def _():
        xbuf[...] = jnp.zeros(xbuf.shape, F32)
        row_copies(i, 0, False)

    @pl.when(i + 1 < nv_ref[0])
    def _():
        row_copies(i + 1, (i + 1) % 2, False)

    @pl.when(i < nv_ref[0])
    def _():
        e = te_ref[i]
        s = slot_ref[i]
        row_copies(i, i % 2, True)

        @pl.when(first_ref[i] == 1)
        def _():
            @pl.when(i == 0)
            def _():
                for cp in weight_copies(e, s):
                    cp.start()

            for cp in weight_copies(e, s):
                cp.wait()
            nxt = next_ref[i]

            @pl.when(nxt >= 0)
            def _():
                for cp in weight_copies(nxt, 1 - s):
                    cp.start()

            rows = 64

            def cast(r, carry):
                sl = pl.ds(pl.multiple_of(r * rows, rows), rows)
                wgu_s[sl, :] = wgu_f[s, sl, :].astype(BF16)
                wdn_s[sl, :] = wdn_f[s, sl, :].astype(BF16)
                return carry

            lax.fori_loop(0, D_MODEL // rows, cast, 0)

        def compute(m):
            x = xbuf[i % 2, 0:m, :].astype(BF16)
            gu = jnp.dot(x, wgu_s[...], preferred_element_type=F32) + bgu_ref[0]
            gate = jnp.minimum(gu[:, 0:D_FF], SWIGLU_LIMIT)
            up = jnp.clip(gu[:, D_FF:2 * D_FF], -SWIGLU_LIMIT, SWIGLU_LIMIT)
            act = (gate * jax.nn.sigmoid(SWIGLU_ALPHA * gate) * (up + 1.0)).astype(BF16)
            y_ref[0:m, :] = jnp.dot(act, wdn_s[...], preferred_element_type=F32) + bdn_ref[0]
            if m < TE:
                y_ref[m:TE, :] = jnp.zeros((TE - m, D_MODEL), F32)

        for m in range(TE // 4, TE + 1, TE // 4):
            pl.when(rows_ref[i] == m)(functools.partial(compute, m))

    @pl.when(i >= nv_ref[0])
    def _():
        y_ref[...] = jnp.zeros(y_ref.shape, y_ref.dtype)


def _experts(plan, grp_c, grp_l, n_rows, w_gu, b_gu, w_dn, b_dn):
    nt = n_rows // TE
    exp3 = lambda i, te, *_: (te[i], 0, 0)
    any_spec = pl.BlockSpec(memory_space=pl.ANY)
    return pl.pallas_call(
        functools.partial(_expert_kernel, ctx_rows=grp_c.shape[0]),
        grid_spec=pltpu.PrefetchScalarGridSpec(
            num_scalar_prefetch=13,
            grid=(nt,),
            in_specs=[any_spec, any_spec, any_spec,
                      pl.BlockSpec((1, 1, 2 * D_FF), exp3), any_spec,
                      pl.BlockSpec((1, 1, D_MODEL), exp3)],
            out_specs=pl.BlockSpec((TE, D_MODEL), lambda i, *_: (i, 0)),
            scratch_shapes=[pltpu.VMEM((2, TE, D_MODEL), F32),
                            pltpu.VMEM((2, D_MODEL, 2 * D_FF), F32), pltpu.VMEM((2, D_FF, D_MODEL), F32),
                            pltpu.VMEM((D_MODEL, 2 * D_FF), BF16), pltpu.VMEM((D_FF, D_MODEL), BF16),
                            pltpu.SemaphoreType.DMA((2,)), pltpu.SemaphoreType.DMA((2, 2))]),
        out_shape=jax.ShapeDtypeStruct((n_rows, D_MODEL), F32),
        compiler_params=_cparams(("arbitrary",), VMEM_LIMIT),
        name="experts",
    )(plan["tile_expert"], plan["n_valid"], plan["tile_first"], plan["tile_next"], plan["tile_slot"],
      plan["tile_rows"], plan["seg_ptr"], plan["seg_mid"], plan["seg_src"], plan["seg_dst"], plan["seg_len"],
      plan["carry_src"], plan["carry_len"], grp_c, grp_l, w_gu, b_gu.reshape(N_EXPERTS, 1, 2 * D_FF), w_dn, b_dn.reshape(N_EXPERTS, 1, D_MODEL))


def _final_kernel(ls_ref, gs_ref, sg_ref, x1_ref, route_ref, mod_ref, fg_ref, ys_ref, y_ref,
                  ybuf, sem, *, tile_off, n_tiles):
    i = pl.program_id(0)
    slot = i % 2

    def fetch(step, sl, wait):
        _strip_copies(step + tile_off, ls_ref, gs_ref, sg_ref, ybuf.at[sl], ys_ref, sem.at[sl], False, wait)

    @pl.when(i == 0)
    def _():
        ybuf[...] = jnp.zeros(ybuf.shape, F32)
        fetch(i, slot, False)

    @pl.when(i + 1 < n_tiles)
    def _():
        fetch(i + 1, 1 - slot, False)

    fetch(i, slot, True)

    route = route_ref[...]
    r = lax.broadcasted_iota(jnp.int32, (TM, NR), 1).astype(F32)
    comb = jnp.zeros((TM, NR), F32)
    for kk in range(TOP_K):
        gate = route[:, 2 * TOP_K + kk:2 * TOP_K + kk + 1]
        comb = jnp.where(route[:, kk:kk + 1] == r, gate, comb)
    moe = jnp.dot(comb.astype(BF16), ybuf[slot].astype(BF16), preferred_element_type=F32)
    gate2 = mod_ref[0, :, 5 * D_MODEL:6 * D_MODEL]
    y_ref[...] = _rms(x1_ref[...] + gate2 * moe, fg_ref[...])


def _final(plan, x1, route, mod3, seg_fn, tile_off, fg, ys):
    t_path = x1.shape[0]
    n = t_path // TM
    loc = lambda w: pl.BlockSpec((TM, w), lambda i, *_: (i, 0))
    return pl.pallas_call(
        functools.partial(_final_kernel, tile_off=tile_off, n_tiles=n),
        grid_spec=pltpu.PrefetchScalarGridSpec(
            num_scalar_prefetch=3,
            grid=(n,),
            in_specs=[loc(D_MODEL), loc(LANES),
                      pl.BlockSpec((1, 1, 6 * D_MODEL), lambda i, *_: (seg_fn(i), 0, 0)),
                      pl.BlockSpec((1, D_MODEL), lambda i, *_: (0, 0)),
                      pl.BlockSpec(memory_space=pl.ANY)],
            out_specs=loc(D_MODEL),
            scratch_shapes=[pltpu.VMEM((2, NR, D_MODEL), F32), pltpu.SemaphoreType.DMA((2,))]),
        out_shape=jax.ShapeDtypeStruct((t_path, D_MODEL), F32),
        compiler_params=_cparams(("arbitrary",), VMEM_LIMIT),
        name="final",
    )(plan["lstart"], plan["gstart"], plan["strip"], x1, route, mod3, fg, ys)


def _plan(tile_counts, n_rows, n_ctx):
    nt = n_rows // TE
    n_tok = tile_counts.shape[0]
    strip = (tile_counts + SEG - 1) // SEG * SEG
    lstart = jnp.cumsum(strip, axis=1) - strip
    rows_e = jnp.sum(strip, axis=0)
    rpad = (rows_e + TE - 1) // TE * TE
    ends = jnp.cumsum(rpad)
    offs = ends - rpad
    gstart = offs[None, :] + jnp.cumsum(strip, axis=0) - strip
    n_valid = (ends[-1] // TE).astype(jnp.int32)
    tile_ids = jnp.minimum(jnp.arange(nt, dtype=jnp.int32), n_valid - 1)
    tile_expert = jnp.sum((ends[None, :] <= (tile_ids * TE)[:, None]).astype(jnp.int32), axis=1)
    tile_expert = jnp.minimum(tile_expert, N_EXPERTS - 1).astype(jnp.int32)
    prev = jnp.concatenate([jnp.full((1,), -1, jnp.int32), tile_expert[:-1]])
    tile_first = (tile_expert != prev).astype(jnp.int32)
    tile_slot = (jnp.cumsum(tile_first) - 1) % 2
    pick = lambda table, idx: jnp.sum(
        jnp.where(idx[:, None] == jnp.arange(table.shape[0], dtype=jnp.int32)[None, :], table[None, :], 0), axis=1)
    used = jnp.clip(pick(offs + rows_e, tile_expert) - jnp.arange(nt, dtype=jnp.int32) * TE, 0, TE)
    tile_rows = jnp.maximum((used + TE // 4 - 1) // (TE // 4), 1) * (TE // 4)
    after = pick(ends, tile_expert) // TE
    tile_next = jnp.where(after < n_valid, pick(tile_expert, jnp.minimum(after, nt - 1)), -1)
    tok = jnp.broadcast_to(jnp.arange(n_tok, dtype=jnp.int32)[:, None], strip.shape)
    g0 = gstart.T.reshape(-1)
    ln = strip.T.reshape(-1)
    src0 = (tok * NR + lstart).T.reshape(-1)
    is_ctx = (tok < n_ctx).T.reshape(-1)
    len1 = jnp.minimum(ln, TE - g0 % TE)
    tile0 = g0 // TE
    tiles = jnp.arange(nt + 1, dtype=jnp.int32)[:, None]
    count = lambda m: jnp.sum(m.astype(jnp.int32), axis=1)
    seg_ptr = count(tile0[None, :] < tiles)
    seg_mid = seg_ptr + count((tile0[None, :] == tiles) & is_ctx[None, :])
    into = ((g0 + len1) // TE)[None, :] == tiles[:nt]
    carry_len = jnp.sum(jnp.where(into, (ln - len1)[None, :], 0), axis=1)
    carry_src = jnp.sum(jnp.where(into & (ln > len1)[None, :], (src0 + len1)[None, :], 0), axis=1)
    i32 = lambda a: a.reshape(-1).astype(jnp.int32)
    return dict(strip=i32(strip), lstart=i32(lstart), gstart=i32(gstart), n_valid=n_valid.reshape(1),
                tile_expert=tile_expert, tile_first=i32(tile_first), tile_next=i32(tile_next),
                tile_slot=i32(tile_slot), tile_rows=i32(tile_rows), seg_ptr=i32(seg_ptr), seg_mid=i32(seg_mid), seg_src=i32(src0),
                seg_dst=i32(g0 % TE), seg_len=i32(len1), carry_src=i32(carry_src),
                carry_len=i32(carry_len))


def _layer(x_prompt, x_sample, state, c, c_ctx, ada_w, ada_b, norm1_g, norm2_g, w_in, conv_w, conv_b,
           conv_ln_g, conv_ln_b, gate_w, gate_b, gla_norm_g, w_out, router_w, router_b,
           moe_w_gu, moe_b_gu, moe_w_dn, moe_b_dn, final_g):
    bp, lp, d = x_prompt.shape
    bs, ls, _ = x_sample.shape
    assert lp == TM and ls % TM == 0 and d == D_MODEL
    xp = x_prompt.reshape(bp * lp, d)
    xs = x_sample.reshape(bs * ls, d)
    n_c, n_l = bp, bs * ls // TM
    lat_tiles = ls // TM
    t_all = (n_c + n_l) * TM

    n_cond = 1 + bs
    cond_t = jnp.zeros((d, 8), F32).at[:, 0].set(c_ctx).at[:, 1:n_cond].set(c.T)
    mod3 = _modulation(cond_t, ada_w, ada_b, n_cond).reshape(8, 1, 6 * d)

    row = lambda a: a.reshape(1, -1)
    u, qk, v, g, la = _inproj(xp, xs, mod3, row(norm1_g), w_in, gate_w, gate_b, ls)

    pair = lambda s: s.reshape(s.shape[0], 2, 2, LANES, LANES)
    of_c, ob_c, s_ctx = _gla(qk, v, la, jnp.zeros((bp, 2, 2, LANES, LANES), F32), bp, 1, 0)
    of_l, ob_l, _ = _gla(qk, v, la, pair(state), bs, lat_tiles, n_c)

    cw = jnp.zeros((32, C_CONV), F32).at[:CONV_K].set(conv_w)
    rwh, rwl = _split(router_w.T)
    rb = jnp.broadcast_to(router_b[:, None], (N_EXPERTS, TM))
    wts = (cw, row(conv_b), row(conv_ln_g), row(conv_ln_b), row(gla_norm_g), w_out.astype(BF16),
           row(norm2_g), rwh, rwl, rb)
    seg_c = lambda i: 0
    seg_l = lambda i: 1 + i // lat_tiles
    (x1_c, grp_c, route_c, cnt_c), (x1_l, grp_l, route_l, cnt_l) = _mix(
        [(xp, of_c, ob_c, seg_c, 0, lp), (xs, of_l, ob_l, seg_l, n_c, GRID_W)], u, g, mod3, wts)

    tile_counts = jnp.concatenate([cnt_c[:, :, 0], cnt_l[:, :, 0]]).astype(jnp.int32)
    n_rows = TOP_K * t_all + (SEG - 1) * N_EXPERTS * (n_c + n_l) + N_EXPERTS * TE
    n_rows = (n_rows + TE - 1) // TE * TE
    plan = _plan(tile_counts, n_rows, n_c)
    ysrt = _experts(plan, grp_c, grp_l, n_rows, moe_w_gu, moe_b_gu, moe_w_dn, moe_b_dn)
    y_c = _final(plan, x1_c, route_c, mod3, seg_c, 0, row(final_g), ysrt)
    y_l = _final(plan, x1_l, route_l, mod3, seg_l, n_c, row(final_g), ysrt)
    new_state = s_ctx.reshape(bp, 1, 2, N_GLA_HEADS, DK_HEAD, DV_HEAD)
    return y_c.reshape(bp, lp, d), y_l.reshape(bs, ls, d), new_state


def kernel(x_prompt, x_sample, state_gla, c, c_ctx, ada_w, ada_b, norm1_g, norm2_g, w_in, conv_w,
           conv_b, conv_ln_g, conv_ln_b, gate_w, gate_b, gla_norm_g, w_out, router_w, router_b,
           moe_w_gu, moe_b_gu, moe_w_dn, moe_b_dn, final_g):
    assert ada_w.shape[0] == 1, "single-layer step"
    return _layer(x_prompt, x_sample, state_gla[:, 0], c, c_ctx, ada_w[0], ada_b[0], norm1_g[0],
                  norm2_g[0], w_in[0], conv_w[0], conv_b[0], conv_ln_g[0], conv_ln_b[0], gate_w[0],
                  gate_b[0], gla_norm_g[0], w_out[0], router_w[0], router_b[0], moe_w_gu[0],
                  moe_b_gu[0], moe_w_dn[0], moe_b_dn[0], final_g)
```

```python
import functools

import numpy as np
import jax
import jax.numpy as jnp
from jax import lax
from jax.experimental import pallas as pl
from jax.experimental.pallas import tpu as pltpu

D_MODEL = 1024
GRID_W = 64
C_CONV = D_MODEL // 2
CONV_K = 31
N_GLA_HEADS = 4
DV_HEAD = 128
DK_HEAD = 64
DK_TOT = DK_HEAD * N_GLA_HEADS
DV_TOT = DV_HEAD * N_GLA_HEADS
GATE_RANK = 16
GATE_TEMP = 16.0
CHUNK = 64
N_EXPERTS = 32
TOP_K = 4
D_FF = D_MODEL
SWIGLU_LIMIT = 7.0
SWIGLU_ALPHA = 1.702
EPS = 1e-6

LANES = 128
SEG = 8
TM = 256
TE = 1024
NR = TOP_K * TM + TM
CONV_PAD = 16
V7X_VMEM_BYTES = 64 * 1024 * 1024
VMEM_LIMIT = V7X_VMEM_BYTES - 8 * 1024 * 1024

F32 = jnp.float32
BF16 = jnp.bfloat16
HI = lax.Precision.HIGHEST
TN =(((0,), (0,)), ((), ()))
NT = (((1,), (1,)), ((), ()))

assert N_EXPERTS * (SEG - 1) <= NR - TOP_K * TM


def _split(x):
    hi = x.astype(BF16)
    return hi, (x - hi.astype(F32)).astype(BF16)


def _cparams(sem, vmem=None):
    return pltpu.CompilerParams(dimension_semantics=sem, vmem_limit_bytes=vmem)


def _mod_kernel(ct_ref, w_ref, b_ref, o_ref, *, n_cond):
    ct = ct_ref[...]
    s = ct * jax.nn.sigmoid(ct)
    w = w_ref[...]
    rows = [jnp.sum(s[:, r:r + 1] * w, axis=0, keepdims=True) + b_ref[...] for r in range(n_cond)]
    rows.append(jnp.zeros((8 - n_cond, w.shape[1]), F32))
    o_ref[...] = jnp.concatenate(rows, axis=0)


def _modulation(cond_t, ada_w, ada_b, n_cond):
    d, n = ada_w.shape
    nb = 768
    return pl.pallas_call(
        functools.partial(_mod_kernel, n_cond=n_cond),
        grid=(n // nb,),
        in_specs=[pl.BlockSpec((d, 8), lambda i: (0, 0)),
                  pl.BlockSpec((d, nb), lambda i: (0, i)),
                  pl.BlockSpec((1, nb), lambda i: (0, i))],
        out_specs=pl.BlockSpec((8, nb), lambda i: (0, i)),
        out_shape=jax.ShapeDtypeStruct((8, n), F32),
        compiler_params=_cparams(("arbitrary",)),
        name="mod",
    )(cond_t, ada_w, ada_b.reshape(1, n))


def _rms(x, g):
    return x * lax.rsqrt(jnp.mean(x * x, axis=-1, keepdims=True) + EPS) * g


def _inproj_kernel(xp_ref, xs_ref, mod_ref, g1_ref, wu_ref, wqk_ref, wv_ref, wg_ref, wlow_ref,
                   gwh_ref, gwl_ref, gb_ref, u_ref, qk_ref, v_ref, g_ref, la_ref, *, n_ctx_tiles):
    i = pl.program_id(0)
    shift = mod_ref[0, :, 0:D_MODEL]
    scale = mod_ref[0, :, D_MODEL:2 * D_MODEL]
    dot = functools.partial(jnp.dot, preferred_element_type=F32)
    rows = u_ref.shape[0]
    sub = min(rows, TM)
    def block(r0):
        rs = slice(r0, r0 + sub)
        x = jnp.where(i < n_ctx_tiles, xp_ref[rs, :], xs_ref[rs, :])
        h = (_rms(x, g1_ref[...]) * (1.0 + scale) + shift).astype(BF16)
        yield
        u_ref[rs, :] = dot(h, wu_ref[...]).astype(BF16)
        yield
        qk_ref[rs, :] = dot(h, wqk_ref[...]).astype(BF16)
        v_ref[rs, :] = dot(h, wv_ref[...]).astype(BF16)
        yield
        g_ref[rs, :] = dot(h, wg_ref[...]).astype(BF16)
        low = dot(h, wlow_ref[...])
        yield
        low_hi, low_lo = _split(low)
        z = (dot(low_hi, gwh_ref[...]) + dot(low_lo, gwh_ref[...]) + dot(low_hi, gwl_ref[...])
             + gb_ref[...])
        la_ref[rs, :] = (jnp.minimum(z, 0.0) - jnp.log1p(jnp.exp(-jnp.abs(z)))) * (1.0 / GATE_TEMP)

    blocks = [block(r0) for r0 in range(0, rows, sub)]
    while blocks:
        blocks = [gen for gen in blocks if next(gen, "done") != "done"]


def _inproj(xp, xs, mod3, g1, w_in, gate_w, gate_b, lat_len):
    ti = next(t for t in (1024, 512, TM) if xp.shape[0] % t == 0 and lat_len % t == 0)
    lat_tiles_per_seq = lat_len // ti
    n_c, n_l = xp.shape[0] // ti, xs.shape[0] // ti
    t_all = xp.shape[0] + xs.shape[0]
    sp = np.cumsum([0, C_CONV, C_CONV, DK_TOT, DK_TOT, DV_TOT, DV_TOT, 2 * GATE_RANK])
    wu, wqk, wv, wg, wlow = (w_in[:, a:b].astype(BF16) for a, b in
                             ((sp[0], sp[2]), (sp[2], sp[4]), (sp[4], sp[5]), (sp[5], sp[6]), (sp[6], sp[7])))
    gw = jnp.zeros((2 * GATE_RANK, 2 * DK_TOT), F32)
    gw = gw.at[:GATE_RANK, :DK_TOT].set(gate_w[0]).at[GATE_RANK:, DK_TOT:].set(gate_w[1])
    gwh, gwl = _split(gw)
    gb = gate_b.reshape(1, 2 * DK_TOT)
    const = lambda a: pl.BlockSpec(a.shape, lambda i: (0,) * a.ndim)
    row = lambda w: pl.BlockSpec((ti, w), lambda i: (i, 0))
    seg = lambda i: jnp.where(i < n_c, 0, 1 + jnp.maximum(i - n_c, 0) // lat_tiles_per_seq)
    return pl.pallas_call(
        functools.partial(_inproj_kernel, n_ctx_tiles=n_c),
        grid=(n_c + n_l,),
        in_specs=[pl.BlockSpec((ti, D_MODEL), lambda i: (jnp.minimum(i, n_c - 1), 0)),
                  pl.BlockSpec((ti, D_MODEL), lambda i: (jnp.maximum(i - n_c, 0), 0)),
                  pl.BlockSpec((1, 1, 6 * D_MODEL), lambda i: (seg(i), 0, 0)),
                  const(g1), const(wu), const(wqk), const(wv), const(wg), const(wlow),
                  const(gwh), const(gwl), const(gb)],
        out_specs=[row(2 * C_CONV), row(2 * DK_TOT), row(DV_TOT), row(DV_TOT), row(2 * DK_TOT)],
        out_shape=[jax.ShapeDtypeStruct((t_all, 2 * C_CONV), BF16),
                   jax.ShapeDtypeStruct((t_all, 2 * DK_TOT), BF16),
                   jax.ShapeDtypeStruct((t_all, DV_TOT), BF16),
                   jax.ShapeDtypeStruct((t_all, DV_TOT), BF16),
                   jax.ShapeDtypeStruct((t_all, 2 * DK_TOT), F32)],
        compiler_params=_cparams(("arbitrary",), VMEM_LIMIT),
        name="inproj",
    )(xp, xs, mod3, g1, wu, wqk, wv, wg, wlow, gwh, gwl, gb)


def _gla_direction(qk_ref, v_ref, la_ref, tri_ref, blk_ref, o_ref, s_scr, reverse):
    la_hi, la_lo = _split(la_ref[...])
    tri = tri_ref[...]
    bcum = (jnp.dot(tri, la_hi, preferred_element_type=F32)
            + jnp.dot(tri, la_lo, preferred_element_type=F32))
    blk = blk_ref[...]
    bl_cols = (lax.dot_general(la_hi, blk, TN, preferred_element_type=F32)
               + lax.dot_general(la_lo, blk, TN, preferred_element_type=F32))
    yield
    q = qk_ref[:, 0:DK_TOT].astype(F32)
    k = qk_ref[:, DK_TOT:2 * DK_TOT].astype(F32)
    lane = lax.broadcasted_iota(jnp.int32, (CHUNK, LANES), 1)
    row2 = lax.broadcasted_iota(jnp.int32, (2 * CHUNK, CHUNK), 0) % CHUNK
    col2 = lax.broadcasted_iota(jnp.int32, (2 * CHUNK, CHUNK), 1)
    keep = (col2 >= row2) if reverse else (col2 <= row2)
    srow = lax.broadcasted_iota(jnp.int32, (LANES, LANES), 0)
    n_chunks = TM // CHUNK
    order = range(n_chunks - 1, -1, -1) if reverse else range(n_chunks)
    state = [s_scr[0], s_scr[1]]
    for c in order:
        r0 = c * CHUNK
        bc = bcum[r0:r0 + CHUNK]
        bl = bc[0:1] if reverse else bc[CHUNK - 1:CHUNK]
        qt = q[r0:r0 + CHUNK] * jnp.exp(bc) * (DK_HEAD ** -0.5)
        kt = k[r0:r0 + CHUNK] * jnp.exp(-bc)
        ke = k[r0:r0 + CHUNK] * jnp.exp(bl - bc)
        yield
        for p in range(2):
            cs = slice(p * LANES, (p + 1) * LANES)
            qt_p = qt[:, cs]
            qs = jnp.concatenate([jnp.where(lane < DK_HEAD, qt_p, 0.0),
                                  jnp.where(lane >= DK_HEAD, qt_p, 0.0)], axis=0).astype(BF16)
            att = lax.dot_general(qs, kt[:, cs].astype(BF16), NT, preferred_element_type=F32)
            att = jnp.where(keep, att, 0.0).astype(BF16)
            s_p = state[p]
            o_inter = jnp.dot(qs, s_p.astype(BF16), preferred_element_type=F32)
            ke_t = jnp.transpose(ke[:, cs]).astype(BF16)
            upd = []
            for hh in range(2):
                h = 2 * p + hh
                v_h = v_ref[r0:r0 + CHUNK, h * DV_HEAD:(h + 1) * DV_HEAD]
                both = jnp.dot(jnp.concatenate([att[hh * CHUNK:(hh + 1) * CHUNK], ke_t], axis=0), v_h,
                               preferred_element_type=F32)
                o_ref[r0:r0 + CHUNK, h * DV_HEAD:(h + 1) * DV_HEAD] = (
                    both[0:CHUNK] + o_inter[hh * CHUNK:(hh + 1) * CHUNK])
                upd.append(both[CHUNK:CHUNK + LANES])
            bl_col = bl_cols[p * LANES:(p + 1) * LANES, c * LANES:(c + 1) * LANES]
            state[p] = jnp.exp(bl_col) * s_p + jnp.where(srow < DK_HEAD, upd[0], upd[1])
            yield
    s_scr[0] = state[0]
    s_scr[1] = state[1]


def _gla_kernel(*refs, n_tiles, nb):
    seqs = [refs[6 * r:6 * r + 6] for r in range(nb)]
    tril_ref, triu_ref, blk_ref, s0_ref, of_ref, ob_ref, sout_ref, sf_scr, sb_scr = refs[6 * nb:]
    j = pl.program_id(1)

    @pl.when(j == 0)
    def _():
        sf_scr[...] = s0_ref[:, 0]
        sb_scr[...] = s0_ref[:, 1]

    scans = []
    for r, (qkf_ref, vf_ref, laf_ref, qkb_ref, vb_ref, lab_ref) in enumerate(seqs):
        scans.append(_gla_direction(qkf_ref, vf_ref, laf_ref, tril_ref, blk_ref, of_ref.at[r], sf_scr.at[r], False))
        scans.append(_gla_direction(qkb_ref, vb_ref, lab_ref, triu_ref, blk_ref, ob_ref.at[r], sb_scr.at[r], True))
    while scans:
        scans = [gen for gen in scans if next(gen, "done") != "done"]

    @pl.when(j == n_tiles - 1)
    def _():
        sout_ref[:, 0] = sf_scr[...]
        sout_ref[:, 1] = sb_scr[...]


def _gla(qk, v, la, s0, n_seq, n_tiles, tile_off):
    nb = next(b for b in (4, 2, 1) if n_seq % b == 0)
    seq_len = n_tiles * TM
    blk = np.arange(TM) // CHUNK
    same = blk[:, None] == blk[None, :]
    r = np.arange(TM)
    tril = jnp.asarray((same & (r[None, :] <= r[:, None])).astype(np.float32)).astype(BF16)
    triu = jnp.asarray((same & (r[None, :] >= r[:, None])).astype(np.float32)).astype(BF16)
    chunk_of_lane = np.arange(TM // CHUNK * LANES) // LANES
    blkm = jnp.asarray((blk[:, None] == chunk_of_lane[None, :]).astype(np.float32)).astype(BF16)
    def tok(w, r, backward, cb=0):
        def index(s, j):
            jj = (n_tiles - 1 - j) if backward else j
            return ((s * nb + r) * n_tiles + jj + tile_off, cb)
        return pl.BlockSpec((TM, w), index)

    const = lambda a: pl.BlockSpec(a.shape, lambda s, j: (0,) * a.ndim)
    st = pl.BlockSpec((nb, 2, 2, LANES, LANES), lambda s, j: (s, 0, 0, 0, 0))
    seq_specs, seq_args = [], []
    for r in range(nb):
        seq_specs += [tok(2 * DK_TOT, r, False), tok(DV_TOT, r, False), tok(DK_TOT, r, False, 0),
                      tok(2 * DK_TOT, r, True), tok(DV_TOT, r, True), tok(DK_TOT, r, True, 1)]
        seq_args += [qk, v, la, qk, v, la]
    o_f, o_b, s_out = pl.pallas_call(
        functools.partial(_gla_kernel, n_tiles=n_tiles, nb=nb),
        grid=(n_seq // nb, n_tiles),
        in_specs=seq_specs + [const(tril), const(triu), const(blkm), st],
        out_specs=[pl.BlockSpec((nb, TM, DV_TOT), lambda s, j: (s, j, 0)),
                   pl.BlockSpec((nb, TM, DV_TOT), lambda s, j: (s, n_tiles - 1 - j, 0)), st],
        out_shape=[jax.ShapeDtypeStruct((n_seq, seq_len, DV_TOT), F32),
                   jax.ShapeDtypeStruct((n_seq, seq_len, DV_TOT), F32),
                   jax.ShapeDtypeStruct(s0.shape, F32)],
        scratch_shapes=[pltpu.VMEM((nb, 2, LANES, LANES), F32), pltpu.VMEM((nb, 2, LANES, LANES), F32)],
        compiler_params=_cparams(("arbitrary", "arbitrary"), VMEM_LIMIT),
        name="gla",
    )(*seq_args, tril, triu, blkm, s0)
    return o_f.reshape(n_seq * seq_len, DV_TOT), o_b.reshape(n_seq * seq_len, DV_TOT), s_out


def _mix_conv(u_ref, cw_ref, pad_scr, shf_scr, cv_scr, seq_len):
    n_seq = TM // seq_len
    pr = seq_len + 2 * CONV_PAD
    qr = seq_len + 3 * SEG

    u = u_ref[...].astype(F32)
    hg = u[:, 0:C_CONV] * jax.nn.sigmoid(u[:, C_CONV:2 * C_CONV])
    zero = jnp.zeros((CONV_PAD, C_CONV), F32)
    for s in range(n_seq):
        pad_scr[s * pr:s * pr + CONV_PAD, :] = zero
        pad_scr[s * pr + CONV_PAD:s * pr + CONV_PAD + seq_len, :] = hg[s * seq_len:(s + 1) * seq_len]
        pad_scr[s * pr + CONV_PAD + seq_len:(s + 1) * pr, :] = zero
        for b in range(1, SEG):
            shf_scr[b - 1, s * qr:(s + 1) * qr, :] = pad_scr[s * pr + b:s * pr + b + qr, :]
    rb = 8 * SEG
    per_seq = seq_len // rb
    for gi in range(C_CONV // LANES):
        gs = slice(gi * LANES, (gi + 1) * LANES)
        wv = [jnp.broadcast_to(cw_ref[tap:tap + 1, gs], (SEG, LANES)) for tap in range(CONV_K)]

        def conv_rows(i, carry, gs=gs, wv=wv):
            s = lax.shift_right_logical(i, per_seq.bit_length() - 1)
            r0 = (i & (per_seq - 1)) * rb
            nj = rb // SEG
            acc = [None] * nj
            for b in range(SEG):
                taps = [(a, SEG * a + b - (CONV_PAD - CONV_K // 2)) for a in range(CONV_PAD * 2 // SEG)]
                taps = [(a, t) for a, t in taps if 0 <= t < CONV_K]
                loaded = {}
                for a, t in taps:
                    for j in range(nj):
                        if a + j not in loaded:
                            if b == 0:
                                row = pl.multiple_of(s * pr + r0 + (a + j) * SEG, SEG)
                                loaded[a + j] = pad_scr[pl.ds(row, SEG), gs]
                            else:
                                row = pl.multiple_of(s * qr + r0 + (a + j) * SEG, SEG)
                                loaded[a + j] = shf_scr[b - 1, pl.ds(row, SEG), gs]
                        term = wv[t] * loaded[a + j]
                        acc[j] = term if acc[j] is None else acc[j] + term
            for j in range(nj):
                cv_scr[pl.ds(pl.multiple_of(s * seq_len + r0 + j * SEG, SEG), SEG), gs] = acc[j]
            return carry

        lax.fori_loop(0, TM // rb, conv_rows, 0)


def _mix_rest(x_ref, g_ref, of_ref, ob_ref, mod_ref, cv_scr, cb_ref, lng_ref, lnb_ref, gng_ref, wout_ref,
              n2g_ref, rwh_ref, rwl_ref, rb_ref, le_ref, ut_ref, x1_ref, grp_ref, route_ref, cnt_ref):
    cv = cv_scr[...] + cb_ref[...]
    mu = jnp.mean(cv, axis=-1, keepdims=True)
    var = jnp.mean(jnp.square(cv - mu), axis=-1, keepdims=True)
    cv =(cv - mu) * lax.rsqrt(var + EPS) * lng_ref[...] + lnb_ref[...]
    conv_out = (cv * jax.nn.sigmoid(cv)).astype(BF16)
    yield

    o = of_ref[...] + ob_ref[...]
    g = g_ref[...].astype(F32)
    heads = []
    for h in range(N_GLA_HEADS):
        cs = slice(h * DV_HEAD, (h + 1) * DV_HEAD)
        oh = o[:, cs]
        oh = oh * lax.rsqrt(jnp.mean(oh * oh, axis=-1, keepdims=True) + EPS) * gng_ref[...]
        gh = g[:, cs]
        heads.append((oh * (gh * jax.nn.sigmoid(gh))).astype(BF16))
    yield
    y = jnp.dot(conv_out, wout_ref[0:C_CONV, :], preferred_element_type=F32)
    for h in range(N_GLA_HEADS):
        y = y + jnp.dot(heads[h], wout_ref[C_CONV + h * DV_HEAD:C_CONV + (h + 1) * DV_HEAD, :],
                        preferred_element_type=F32)
    yield

    gate1 = mod_ref[0, :, 2 * D_MODEL:3 * D_MODEL]
    shift2 = mod_ref[0, :, 3 * D_MODEL:4 * D_MODEL]
    scale2 = mod_ref[0, :, 4 * D_MODEL:5 * D_MODEL]
    x1 = x_ref[...] + gate1 * y
    x1_ref[...] = x1
    h2 = _rms(x1, n2g_ref[...]) * (1.0 + scale2) + shift2
    h2_hi, h2_lo = _split(h2)
    yield

    ntdot = functools.partial(lax.dot_general, dimension_numbers=NT, preferred_element_type=F32)
    l = (ntdot(rwh_ref[...], h2_hi) + ntdot(rwh_ref[...], h2_lo) + ntdot(rwl_ref[...], h2_hi)
         + rb_ref[...])
    yield
    erow = lax.broadcasted_iota(jnp.int32, (N_EXPERTS, TM), 0).astype(F32)
    vals, idxs, hots = [], [], []
    for _ in range(TOP_K):
        m = jnp.max(l, axis=0, keepdims=True)
        idx = jnp.min(jnp.where(l == m, erow, float(N_EXPERTS)), axis=0, keepdims=True)
        hot = erow == idx
        l = jnp.where(hot, -jnp.inf, l)
        vals.append(m)
        idxs.append(idx)
        hots.append(hot)
        yield
    es = [jnp.exp(vk - vals[0]) for vk in vals]
    inv = 1.0 / (es[0] + es[1] + es[2] + es[3])
    sel = jnp.zeros((N_EXPERTS, TM), F32)
    for hot in hots:
        sel = jnp.where(hot, 1.0, sel)

    cnt = jnp.broadcast_to(jnp.sum(sel, axis=1, keepdims=True), (N_EXPERTS, LANES))
    cnt_ref[0] = cnt
    strip = jnp.ceil(cnt * (1.0 / SEG)) * float(SEG)
    starts = jnp.dot(le_ref[...], strip, precision=HI, preferred_element_type=F32)[:, 0:1]
    slot_of = jnp.dot(sel.astype(BF16), ut_ref[...], preferred_element_type=F32) + starts
    slots = [jnp.sum(jnp.where(hot, slot_of, 0.0), axis=0, keepdims=True) for hot in hots]

    rows = slots + idxs + [e * inv for e in es]
    srow = lax.broadcasted_iota(jnp.int32, (16, TM), 0)
    packed = jnp.zeros((16, TM), F32)
    for j, v in enumerate(rows):
        packed = jnp.where(srow == j, v, packed)
    packed = jnp.concatenate([packed, jnp.zeros((LANES - 16, TM), F32)], axis=0)
    route_ref[...] = jnp.transpose(packed)
    yield

    r = lax.broadcasted_iota(jnp.int32, (NR, TM), 0).astype(F32)
    perm = jnp.zeros((NR, TM), F32)
    for slot in slots:
        perm = jnp.where(slot == r, 1.0, perm)
    yield
    grp_ref[...] = jnp.dot(perm.astype(BF16), h2_hi, preferred_element_type=F32)


def _mix_kernel(*refs, seq_lens):
    k = len(seq_lens)
    ins = [refs[6 * p:6 * p + 6] for p in range(k)]
    cw_ref, *shared = refs[6 * k:6 * k + 12]
    outs = [refs[6 * k + 12 + 4 * p:6 * k + 16 + 4 * p] for p in range(k)]
    scr = [refs[10 * k + 12 + 3 * p:10 * k + 15 + 3 * p] for p in range(k)]
    for p in range(k):
        _mix_conv(ins[p][1], cw_ref, *scr[p], seq_lens[p])
    stages = []
    for p in range(k):
        x_ref, _, g_ref, of_ref, ob_ref, mod_ref = ins[p]
        stages.append(_mix_rest(x_ref, g_ref, of_ref, ob_ref, mod_ref, scr[p][2], *shared, *outs[p]))
    while stages:
        stages = [gen for gen in stages if next(gen, "done") != "done"]


def _mix(paths, u, g, mod3, wts):
    (conv_w, conv_b, ln_g, ln_b, gng, wout, n2g, rwh, rwl, rb) = wts
    le = jnp.asarray(np.tril(np.ones((N_EXPERTS, N_EXPERTS), np.float32), -1))
    ut = jnp.asarray(np.triu(np.ones((TM, TM), np.float32), 1)).astype(BF16)
    shared = (conv_w, conv_b, ln_g, ln_b, gng, wout, n2g, rwh, rwl, rb, le, ut)
    const = lambda a: pl.BlockSpec(a.shape, lambda i: (0,) * a.ndim)
    n_steps = max(x.shape[0] // TM for x, *_ in paths)
    in_specs, args, out_specs, out_shape, scratch = [], [], [], [], []
    for x, o_f, o_b, seg_fn, tile_off, seq_len in paths:
        n = x.shape[0] // TM
        cur = lambda i, n=n: jnp.minimum(i, n - 1)
        loc = lambda w, cur=cur: pl.BlockSpec((TM, w), lambda i: (cur(i), 0))
        uni = lambda w, cur=cur, off=tile_off: pl.BlockSpec((TM, w), lambda i: (cur(i) + off, 0))
        in_specs += [loc(D_MODEL), uni(2 * C_CONV), uni(DV_TOT), loc(DV_TOT), loc(DV_TOT),
                     pl.BlockSpec((1, 1, 6 * D_MODEL), lambda i, cur=cur, f=seg_fn: (f(cur(i)), 0, 0))]
        args += [x, u, g, o_f, o_b, mod3]
        out_specs += [loc(D_MODEL), pl.BlockSpec((NR, D_MODEL), lambda i, cur=cur: (cur(i), 0)), loc(LANES),
                      pl.BlockSpec((1, N_EXPERTS, LANES), lambda i, cur=cur: (cur(i), 0, 0))]
        out_shape += [jax.ShapeDtypeStruct((n * TM, D_MODEL), F32),
                      jax.ShapeDtypeStruct((n * NR, D_MODEL), F32),
                      jax.ShapeDtypeStruct((n * TM, LANES), F32),
                      jax.ShapeDtypeStruct((n, N_EXPERTS, LANES), F32)]
        n_seq = TM // seq_len
        scratch += [pltpu.VMEM((n_seq * (seq_len + 2 * CONV_PAD), C_CONV), F32),
                    pltpu.VMEM((SEG - 1, n_seq * (seq_len + 3 * SEG), C_CONV), F32),
                    pltpu.VMEM((TM, C_CONV), F32)]
    outs = pl.pallas_call(
        functools.partial(_mix_kernel, seq_lens=tuple(p[5] for p in paths)),
        grid=(n_steps,),
        in_specs=in_specs + [const(a) for a in shared],
        out_specs=out_specs,
        out_shape=out_shape,
        scratch_shapes=scratch,
        compiler_params=_cparams(("arbitrary",), VMEM_LIMIT),
        name="mix",
    )(*args, *shared)
    return [outs[4 * p:4 * p + 4] for p in range(len(paths))]


def _strip_copies(tile, ls_ref, gs_ref, sg_ref, local, rows_hbm, sem, to_hbm, wait):
    def body(e, carry):
        j = tile * N_EXPERTS + e
        n = pl.multiple_of(sg_ref[j], SEG)

        @pl.when(n > 0)
        def _():
            loc = local.at[pl.ds(pl.multiple_of(ls_ref[j], SEG), n)]
            glob = rows_hbm.at[pl.ds(pl.multiple_of(gs_ref[j], SEG), n)]
            cp = pltpu.make_async_copy(loc, glob, sem) if to_hbm else pltpu.make_async_copy(glob, loc, sem)
            if wait:
                cp.wait()
            else:
                cp.start()
        return carry

    lax.fori_loop(0, N_EXPERTS, body, 0)


def _expert_kernel(te_ref, nv_ref, first_ref, next_ref, rows_ref, ptr_ref, mid_ref, src_ref, dst_ref,
                   len_ref, csrc_ref, clen_ref, grpc_hbm, grpl_hbm, wgu_hbm, bgu_ref, wdn_hbm, bdn_ref,
                   y_ref, xbuf, wgu_f, wdn_f, wgu_s, wdn_s, xsem, sem, *, ctx_rows):
    i = pl.program_id(0)

    def weight_copies(e):
        return (pltpu.make_async_copy(wgu_hbm.at[e], wgu_f, sem.at[0]),
                pltpu.make_async_copy(wdn_hbm.at[e], wdn_f, sem.at[1]))

    def row_copies(tile, xs, wait):
        def strip(hbm, src, dst, n):
            cp = pltpu.make_async_copy(hbm.at[pl.ds(pl.multiple_of(src, SEG), n)],
                                       xbuf.at[xs, pl.ds(pl.multiple_of(dst, SEG), n)], xsem.at[xs])
            cp.wait() if wait else cp.start()

        def strips_of(hbm, base):
            def body(j, carry):
                n = pl.multiple_of(len_ref[j], SEG)

                @pl.when(n > 0)
                def _():
                    strip(hbm, src_ref[j] - base, dst_ref[j], n)
                return carry
            return body

        lax.fori_loop(ptr_ref[tile], mid_ref[tile], strips_of(grpc_hbm, 0), 0)
        lax.fori_loop(mid_ref[tile], ptr_ref[tile + 1], strips_of(grpl_hbm, ctx_rows), 0)
        n = pl.multiple_of(clen_ref[tile], SEG)
        src = csrc_ref[tile]

        @pl.when(jnp.logical_and(n > 0, src < ctx_rows))
        def _():
            strip(grpc_hbm, src, 0, n)

        @pl.when(jnp.logical_and(n > 0, src >= ctx_rows))
        def _():
            strip(grpl_hbm, src - ctx_rows, 0, n)

    @pl.when(i == 0)
    def _():
        xbuf[...] = jnp.zeros(xbuf.shape, F32)
        row_copies(i, 0, False)

    @pl.when(i + 1 < nv_ref[0])
    def _():
        row_copies(i + 1, (i + 1) % 2, False)

    @pl.when(i < nv_ref[0])
    def _():
        e = te_ref[i]
        row_copies(i, i % 2, True)

        @pl.when(first_ref[i] == 1)
        def _():
            @pl.when(i == 0)
            def _():
                for cp in weight_copies(e):
                    cp.start()

            for cp in weight_copies(e):
                cp.wait()
            rows = 64

            def cast(r, carry):
                sl = pl.ds(pl.multiple_of(r * rows, rows), rows)
                wgu_s[sl, :] = wgu_f[sl, :].astype(BF16)
                wdn_s[sl, :] = wdn_f[sl, :].astype(BF16)
                return carry

            lax.fori_loop(0, D_MODEL // rows, cast, 0)
            nxt = next_ref[i]

            @pl.when(nxt >= 0)
            def _():
                for cp in weight_copies(nxt):
                    cp.start()

        def compute(m):
            x = xbuf[i % 2, 0:m, :].astype(BF16)
            gu = jnp.dot(x, wgu_s[...], preferred_element_type=F32) + bgu_ref[0]
            gate = jnp.minimum(gu[:, 0:D_FF], SWIGLU_LIMIT)
            up = jnp.clip(gu[:, D_FF:2 * D_FF], -SWIGLU_LIMIT, SWIGLU_LIMIT)
            act = (gate * jax.nn.sigmoid(SWIGLU_ALPHA * gate) * (up + 1.0)).astype(BF16)
            y_ref[0:m, :] = jnp.dot(act, wdn_s[...], preferred_element_type=F32) + bdn_ref[0]
            if m < TE:
                y_ref[m:TE, :] = jnp.zeros((TE - m, D_MODEL), F32)

        for m in range(TE // 4, TE + 1, TE // 4):
            pl.when(rows_ref[i] == m)(functools.partial(compute, m))

    @pl.when(i >= nv_ref[0])
    def _():
        y_ref[...] = jnp.zeros(y_ref.shape, y_ref.dtype)


def _experts(plan, grp_c, grp_l, n_rows, w_gu, b_gu, w_dn, b_dn):
    nt = n_rows // TE
    exp3 = lambda i, te, *_: (te[i], 0, 0)
    any_spec = pl.BlockSpec(memory_space=pl.ANY)
    return pl.pallas_call(
        functools.partial(_expert_kernel, ctx_rows=grp_c.shape[0]),
        grid_spec=pltpu.PrefetchScalarGridSpec(
            num_scalar_prefetch=12,
            grid=(nt,),
            in_specs=[any_spec, any_spec, any_spec,
                      pl.BlockSpec((1, 1, 2 * D_FF), exp3), any_spec,
                      pl.BlockSpec((1, 1, D_MODEL), exp3)],
            out_specs=pl.BlockSpec((TE, D_MODEL), lambda i, *_: (i, 0)),
            scratch_shapes=[pltpu.VMEM((2, TE, D_MODEL), F32),
                            pltpu.VMEM((D_MODEL, 2 * D_FF), F32), pltpu.VMEM((D_FF, D_MODEL), F32),
                            pltpu.VMEM((D_MODEL, 2 * D_FF), BF16), pltpu.VMEM((D_FF, D_MODEL), BF16),
                            pltpu.SemaphoreType.DMA((2,)), pltpu.SemaphoreType.DMA((2,))]),
        out_shape=jax.ShapeDtypeStruct((n_rows, D_MODEL), F32),
        compiler_params=_cparams(("arbitrary",), VMEM_LIMIT),
        name="experts",
    )(plan["tile_expert"], plan["n_valid"], plan["tile_first"], plan["tile_next"], plan["tile_rows"], plan["seg_ptr"], plan["seg_mid"], plan["seg_src"], plan["seg_dst"], plan["seg_len"],
      plan["carry_src"], plan["carry_len"], grp_c, grp_l, w_gu, b_gu.reshape(N_EXPERTS, 1, 2 * D_FF), w_dn, b_dn.reshape(N_EXPERTS, 1, D_MODEL))


def _final_kernel(ls_ref, gs_ref, sg_ref, x1_ref, route_ref, mod_ref, fg_ref, ys_ref, y_ref,
                  ybuf, sem, *, tile_off, n_tiles):
    i = pl.program_id(0)
    slot = i % 2

    def fetch(step, sl, wait):
        _strip_copies(step + tile_off, ls_ref, gs_ref, sg_ref, ybuf.at[sl], ys_ref, sem.at[sl], False, wait)

    @pl.when(i == 0)
    def _():
        ybuf[...] = jnp.zeros(ybuf.shape, F32)
        fetch(i, slot, False)

    @pl.when(i + 1 < n_tiles)
    def _():
        fetch(i + 1, 1 - slot, False)

    fetch(i, slot, True)

    route = route_ref[...]
    r = lax.broadcasted_iota(jnp.int32, (TM, NR), 1).astype(F32)
    comb = jnp.zeros((TM, NR), F32)
    for kk in range(TOP_K):
        gate = route[:, 2 * TOP_K + kk:2 * TOP_K + kk + 1]
        comb = jnp.where(route[:, kk:kk + 1] == r, gate, comb)
    moe = jnp.dot(comb.astype(BF16), ybuf[slot].astype(BF16), preferred_element_type=F32)
    gate2 = mod_ref[0, :, 5 * D_MODEL:6 * D_MODEL]
    y_ref[...] = _rms(x1_ref[...] + gate2 * moe, fg_ref[...])


def _final(plan, x1, route, mod3, seg_fn, tile_off, fg, ys):
    t_path = x1.shape[0]
    n = t_path // TM
    loc = lambda w: pl.BlockSpec((TM, w), lambda i, *_: (i, 0))
    return pl.pallas_call(
        functools.partial(_final_kernel, tile_off=tile_off, n_tiles=n),
        grid_spec=pltpu.PrefetchScalarGridSpec(
            num_scalar_prefetch=3,
            grid=(n,),
            in_specs=[loc(D_MODEL), loc(LANES),
                      pl.BlockSpec((1, 1, 6 * D_MODEL), lambda i, *_: (seg_fn(i), 0, 0)),
                      pl.BlockSpec((1, D_MODEL), lambda i, *_: (0, 0)),
                      pl.BlockSpec(memory_space=pl.ANY)],
            out_specs=loc(D_MODEL),
            scratch_shapes=[pltpu.VMEM((2, NR, D_MODEL), F32), pltpu.SemaphoreType.DMA((2,))]),
        out_shape=jax.ShapeDtypeStruct((t_path, D_MODEL), F32),
        compiler_params=_cparams(("arbitrary",), VMEM_LIMIT),
        name="final",
    )(plan["lstart"], plan["gstart"], plan["strip"], x1, route, mod3, fg, ys)


def _plan(tile_counts, n_rows, n_ctx):
    nt = n_rows // TE
    n_tok = tile_counts.shape[0]
    strip = (tile_counts + SEG - 1) // SEG * SEG
    lstart = jnp.cumsum(strip, axis=1) - strip
    rows_e = jnp.sum(strip, axis=0)
    rpad = (rows_e + TE - 1) // TE * TE
    ends = jnp.cumsum(rpad)
    offs = ends - rpad
    gstart = offs[None, :] + jnp.cumsum(strip, axis=0) - strip
    n_valid = (ends[-1] // TE).astype(jnp.int32)
    tile_ids = jnp.minimum(jnp.arange(nt, dtype=jnp.int32), n_valid - 1)
    tile_expert = jnp.sum((ends[None, :] <= (tile_ids * TE)[:, None]).astype(jnp.int32), axis=1)
    tile_expert = jnp.minimum(tile_expert, N_EXPERTS - 1).astype(jnp.int32)
    prev = jnp.concatenate([jnp.full((1,), -1, jnp.int32), tile_expert[:-1]])
    tile_first = (tile_expert != prev).astype(jnp.int32)
    pick = lambda table, idx: jnp.sum(
        jnp.where(idx[:, None] == jnp.arange(table.shape[0], dtype=jnp.int32)[None, :], table[None, :], 0), axis=1)
    used = jnp.clip(pick(offs + rows_e, tile_expert) - jnp.arange(nt, dtype=jnp.int32) * TE, 0, TE)
    tile_rows = jnp.maximum((used + TE // 4 - 1) // (TE // 4), 1) * (TE // 4)
    after = pick(ends, tile_expert) // TE
    tile_next = jnp.where(after < n_valid, pick(tile_expert, jnp.minimum(after, nt - 1)), -1)
    tok = jnp.broadcast_to(jnp.arange(n_tok, dtype=jnp.int32)[:, None], strip.shape)
    g0 = gstart.T.reshape(-1)
    ln = strip.T.reshape(-1)
    src0 = (tok * NR + lstart).T.reshape(-1)
    is_ctx = (tok < n_ctx).T.reshape(-1)
    len1 = jnp.minimum(ln, TE - g0 % TE)
    tile0 = g0 // TE
    tiles = jnp.arange(nt + 1, dtype=jnp.int32)[:, None]
    count = lambda m: jnp.sum(m.astype(jnp.int32), axis=1)
    seg_ptr = count(tile0[None, :] < tiles)
    seg_mid = seg_ptr + count((tile0[None, :] == tiles) & is_ctx[None, :])
    into = ((g0 + len1) // TE)[None, :] == tiles[:nt]
    carry_len = jnp.sum(jnp.where(into, (ln - len1)[None, :], 0), axis=1)
    carry_src = jnp.sum(jnp.where(into & (ln > len1)[None, :], (src0 + len1)[None, :], 0), axis=1)
    i32 = lambda a: a.reshape(-1).astype(jnp.int32)
    return dict(strip=i32(strip), lstart=i32(lstart), gstart=i32(gstart), n_valid=n_valid.reshape(1),
                tile_expert=tile_expert, tile_first=i32(tile_first), tile_next=i32(tile_next),
                tile_rows=i32(tile_rows), seg_ptr=i32(seg_ptr), seg_mid=i32(seg_mid), seg_src=i32(src0),
                seg_dst=i32(g0 % TE), seg_len=i32(len1), carry_src=i32(carry_src),
                carry_len=i32(carry_len))


def _layer(x_prompt, x_sample, state, c, c_ctx, ada_w, ada_b, norm1_g, norm2_g, w_in, conv_w, conv_b,
           conv_ln_g, conv_ln_b, gate_w, gate_b, gla_norm_g, w_out, router_w, router_b,
           moe_w_gu, moe_b_gu, moe_w_dn, moe_b_dn, final_g):
    bp, lp, d = x_prompt.shape
    bs, ls, _ = x_sample.shape
    assert lp == TM and ls % TM == 0 and d == D_MODEL
    xp = x_prompt.reshape(bp * lp, d)
    xs = x_sample.reshape(bs * ls, d)
    n_c, n_l = bp, bs * ls // TM
    lat_tiles = ls // TM
    t_all = (n_c + n_l) * TM

    n_cond = 1 + bs
    cond_t = jnp.zeros((d, 8), F32).at[:, 0].set(c_ctx).at[:, 1:n_cond].set(c.T)
    mod3 = _modulation(cond_t, ada_w, ada_b, n_cond).reshape(8, 1, 6 * d)

    row = lambda a: a.reshape(1, -1)
    u, qk, v, g, la = _inproj(xp, xs, mod3, row(norm1_g), w_in, gate_w, gate_b, ls)

    pair = lambda s: s.reshape(s.shape[0], 2, 2, LANES, LANES)
    of_c, ob_c, s_ctx = _gla(qk, v, la, jnp.zeros((bp, 2, 2, LANES, LANES), F32), bp, 1, 0)
    of_l, ob_l, _ = _gla(qk, v, la, pair(state), bs, lat_tiles, n_c)

    cw = jnp.zeros((32, C_CONV), F32).at[:CONV_K].set(conv_w)
    rwh, rwl = _split(router_w.T)
    rb = jnp.broadcast_to(router_b[:, None], (N_EXPERTS, TM))
    wts = (cw, row(conv_b), row(conv_ln_g), row(conv_ln_b), row(gla_norm_g), w_out.astype(BF16),
           row(norm2_g), rwh, rwl, rb)
    seg_c = lambda i: 0
    seg_l = lambda i: 1 + i // lat_tiles
    (x1_c, grp_c, route_c, cnt_c), (x1_l, grp_l, route_l, cnt_l) = _mix(
        [(xp, of_c, ob_c, seg_c, 0, lp), (xs, of_l, ob_l, seg_l, n_c, GRID_W)], u, g, mod3, wts)

    tile_counts = jnp.concatenate([cnt_c[:, :, 0], cnt_l[:, :, 0]]).astype(jnp.int32)
    n_rows = TOP_K * t_all + (SEG - 1) * N_EXPERTS * (n_c + n_l) + N_EXPERTS * TE
    n_rows = (n_rows + TE - 1) // TE * TE
    plan = _plan(tile_counts, n_rows, n_c)
    ysrt = _experts(plan, grp_c, grp_l, n_rows, moe_w_gu, moe_b_gu, moe_w_dn, moe_b_dn)
    y_c = _final(plan, x1_c, route_c, mod3, seg_c, 0, row(final_g), ysrt)
    y_l = _final(plan, x1_l, route_l, mod3, seg_l, n_c, row(final_g), ysrt)
    new_state = s_ctx.reshape(bp, 1, 2, N_GLA_HEADS, DK_HEAD, DV_HEAD)
    return y_c.reshape(bp, lp, d), y_l.reshape(bs, ls, d), new_state


def kernel(x_prompt, x_sample, state_gla, c, c_ctx, ada_w, ada_b, norm1_g, norm2_g, w_in, conv_w,
           conv_b, conv_ln_g, conv_ln_b, gate_w, gate_b, gla_norm_g, w_out, router_w, router_b,
           moe_w_gu, moe_b_gu, moe_w_dn, moe_b_dn, final_g):
    assert ada_w.shape[0] == 1, "single-layer step"
    return _layer(x_prompt, x_sample, state_gla[:, 0], c, c_ctx, ada_w[0], ada_b[0], norm1_g[0],
                  norm2_g[0], w_in[0], conv_w[0], conv_b[0], conv_ln_g[0], conv_ln_b[0], gate_w[0],
                  gate_b[0], gla_norm_g[0], w_out[0], router_w[0], router_b[0], moe_w_gu[0],
                  moe_b_gu[0], moe_w_dn[0], moe_b_dn[0], final_g)
```

```python
import functools

import numpy as np
import jax
import jax.numpy as jnp
from jax import lax
from jax.experimental import pallas as pl
from jax.experimental.pallas import tpu as pltpu

D_MODEL = 1024
GRID_W = 64
C_CONV = D_MODEL // 2
CONV_K = 31
N_GLA_HEADS = 4
DV_HEAD = 128
DK_HEAD = 64
DK_TOT = DK_HEAD * N_GLA_HEADS
DV_TOT = DV_HEAD * N_GLA_HEADS
GATE_RANK = 16
GATE_TEMP = 16.0
CHUNK = 64
N_EXPERTS = 32
TOP_K = 4
D_FF = D_MODEL
SWIGLU_LIMIT = 7.0
SWIGLU_ALPHA = 1.702
EPS = 1e-6

LANES = 128
SEG = 8
TM = 256
TE = 1024
NR = TOP_K * TM + TM
CONV_PAD = 16
V7X_VMEM_BYTES = 64 * 1024 * 1024
VMEM_LIMIT = V7X_VMEM_BYTES - 8 * 1024 * 1024

F32 = jnp.float32
BF16 = jnp.bfloat16
HI = lax.Precision.HIGHEST
TN =(((0,), (0,)), ((), ()))
NT = (((1,), (1,)), ((), ()))

assert N_EXPERTS * (SEG - 1) <= NR - TOP_K * TM


def _split(x):
    hi = x.astype(BF16)
    return hi, (x - hi.astype(F32)).astype(BF16)


def _cparams(sem, vmem=None):
    return pltpu.CompilerParams(dimension_semantics=sem, vmem_limit_bytes=vmem)


def _mod_kernel(ct_ref, w_ref, b_ref, o_ref, *, n_cond):
    ct = ct_ref[...]
    s = ct * jax.nn.sigmoid(ct)
    w = w_ref[...]
    rows = [jnp.sum(s[:, r:r + 1] * w, axis=0, keepdims=True) + b_ref[...] for r in range(n_cond)]
    rows.append(jnp.zeros((8 - n_cond, w.shape[1]), F32))
    o_ref[...] = jnp.concatenate(rows, axis=0)


def _modulation(cond_t, ada_w, ada_b, n_cond):
    d, n = ada_w.shape
    nb = 768
    return pl.pallas_call(
        functools.partial(_mod_kernel, n_cond=n_cond),
        grid=(n // nb,),
        in_specs=[pl.BlockSpec((d, 8), lambda i: (0, 0)),
                  pl.BlockSpec((d, nb), lambda i: (0, i)),
                  pl.BlockSpec((1, nb), lambda i: (0, i))],
        out_specs=pl.BlockSpec((8, nb), lambda i: (0, i)),
        out_shape=jax.ShapeDtypeStruct((8, n), F32),
        compiler_params=_cparams(("arbitrary",)),
        name="mod",
    )(cond_t, ada_w, ada_b.reshape(1, n))


def _rms(x, g):
    return x * lax.rsqrt(jnp.mean(x * x, axis=-1, keepdims=True) + EPS) * g


def _inproj_kernel(xp_ref, xs_ref, mod_ref, g1_ref, wu_ref, wqk_ref, wv_ref, wg_ref, wlow_ref,
                   gwh_ref, gwl_ref, gb_ref, u_ref, qk_ref, v_ref, g_ref, la_ref, *, n_ctx_tiles):
    i = pl.program_id(0)
    shift = mod_ref[0, :, 0:D_MODEL]
    scale = mod_ref[0, :, D_MODEL:2 * D_MODEL]
    dot = functools.partial(jnp.dot, preferred_element_type=F32)
    rows = u_ref.shape[0]
    sub = min(rows, TM)
    def block(r0):
        rs = slice(r0, r0 + sub)
        x = jnp.where(i < n_ctx_tiles, xp_ref[rs, :], xs_ref[rs, :])
        h = (_rms(x, g1_ref[...]) * (1.0 + scale) + shift).astype(BF16)
        yield
        u_ref[rs, :] = dot(h, wu_ref[...]).astype(BF16)
        yield
        qk_ref[rs, :] = dot(h, wqk_ref[...]).astype(BF16)
        v_ref[rs, :] = dot(h, wv_ref[...]).astype(BF16)
        yield
        g_ref[rs, :] = dot(h, wg_ref[...]).astype(BF16)
        low = dot(h, wlow_ref[...])
        yield
        low_hi, low_lo = _split(low)
        z = (dot(low_hi, gwh_ref[...]) + dot(low_lo, gwh_ref[...]) + dot(low_hi, gwl_ref[...])
             + gb_ref[...])
        la_ref[rs, :] = (jnp.minimum(z, 0.0) - jnp.log1p(jnp.exp(-jnp.abs(z)))) * (1.0 / GATE_TEMP)

    blocks = [block(r0) for r0 in range(0, rows, sub)]
    while blocks:
        blocks = [gen for gen in blocks if next(gen, "done") != "done"]


def _inproj(xp, xs, mod3, g1, w_in, gate_w, gate_b, lat_len):
    ti = next(t for t in (1024, 512, TM) if xp.shape[0] % t == 0 and lat_len % t == 0)
    lat_tiles_per_seq = lat_len // ti
    n_c, n_l = xp.shape[0] // ti, xs.shape[0] // ti
    t_all = xp.shape[0] + xs.shape[0]
    sp = np.cumsum([0, C_CONV, C_CONV, DK_TOT, DK_TOT, DV_TOT, DV_TOT, 2 * GATE_RANK])
    wu, wqk, wv, wg, wlow = (w_in[:, a:b].astype(BF16) for a, b in
                             ((sp[0], sp[2]), (sp[2], sp[4]), (sp[4], sp[5]), (sp[5], sp[6]), (sp[6], sp[7])))
    gw = jnp.zeros((2 * GATE_RANK, 2 * DK_TOT), F32)
    gw = gw.at[:GATE_RANK, :DK_TOT].set(gate_w[0]).at[GATE_RANK:, DK_TOT:].set(gate_w[1])
    gwh, gwl = _split(gw)
    gb = gate_b.reshape(1, 2 * DK_TOT)
    const = lambda a: pl.BlockSpec(a.shape, lambda i: (0,) * a.ndim)
    row = lambda w: pl.BlockSpec((ti, w), lambda i: (i, 0))
    seg = lambda i: jnp.where(i < n_c, 0, 1 + jnp.maximum(i - n_c, 0) // lat_tiles_per_seq)
    return pl.pallas_call(
        functools.partial(_inproj_kernel, n_ctx_tiles=n_c),
        grid=(n_c + n_l,),
        in_specs=[pl.BlockSpec((ti, D_MODEL), lambda i: (jnp.minimum(i, n_c - 1), 0)),
                  pl.BlockSpec((ti, D_MODEL), lambda i: (jnp.maximum(i - n_c, 0), 0)),
                  pl.BlockSpec((1, 1, 6 * D_MODEL), lambda i: (seg(i), 0, 0)),
                  const(g1), const(wu), const(wqk), const(wv), const(wg), const(wlow),
                  const(gwh), const(gwl), const(gb)],
        out_specs=[row(2 * C_CONV), row(2 * DK_TOT), row(DV_TOT), row(DV_TOT), row(2 * DK_TOT)],
        out_shape=[jax.ShapeDtypeStruct((t_all, 2 * C_CONV), BF16),
                   jax.ShapeDtypeStruct((t_all, 2 * DK_TOT), BF16),
                   jax.ShapeDtypeStruct((t_all, DV_TOT), BF16),
                   jax.ShapeDtypeStruct((t_all, DV_TOT), BF16),
                   jax.ShapeDtypeStruct((t_all, 2 * DK_TOT), F32)],
        compiler_params=_cparams(("arbitrary",), VMEM_LIMIT),
        name="inproj",
    )(xp, xs, mod3, g1, wu, wqk, wv, wg, wlow, gwh, gwl, gb)


def _gla_direction(qk_ref, v_ref, la_ref, tri_ref, blk_ref, o_ref, s_scr, reverse):
    la_hi, la_lo = _split(la_ref[...])
    tri = tri_ref[...]
    bcum = (jnp.dot(tri, la_hi, preferred_element_type=F32)
            + jnp.dot(tri, la_lo, preferred_element_type=F32))
    blk = blk_ref[...]
    bl_cols = (lax.dot_general(la_hi, blk, TN, preferred_element_type=F32)
               + lax.dot_general(la_lo, blk, TN, preferred_element_type=F32))
    yield
    q = qk_ref[:, 0:DK_TOT].astype(F32)
    k = qk_ref[:, DK_TOT:2 * DK_TOT].astype(F32)
    lane = lax.broadcasted_iota(jnp.int32, (CHUNK, LANES), 1)
    row2 = lax.broadcasted_iota(jnp.int32, (2 * CHUNK, CHUNK), 0) % CHUNK
    col2 = lax.broadcasted_iota(jnp.int32, (2 * CHUNK, CHUNK), 1)
    keep = (col2 >= row2) if reverse else (col2 <= row2)
    srow = lax.broadcasted_iota(jnp.int32, (LANES, LANES), 0)
    n_chunks = TM // CHUNK
    order = range(n_chunks - 1, -1, -1) if reverse else range(n_chunks)
    state = [s_scr[0], s_scr[1]]
    for c in order:
        r0 = c * CHUNK
        bc = bcum[r0:r0 + CHUNK]
        bl = bc[0:1] if reverse else bc[CHUNK - 1:CHUNK]
        qt = q[r0:r0 + CHUNK] * jnp.exp(bc) * (DK_HEAD ** -0.5)
        kt = k[r0:r0 + CHUNK] * jnp.exp(-bc)
        ke = k[r0:r0 + CHUNK] * jnp.exp(bl - bc)
        yield
        for p in range(2):
            cs = slice(p * LANES, (p + 1) * LANES)
            qt_p = qt[:, cs]
            qs = jnp.concatenate([jnp.where(lane < DK_HEAD, qt_p, 0.0),
                                  jnp.where(lane >= DK_HEAD, qt_p, 0.0)], axis=0).astype(BF16)
            att = lax.dot_general(qs, kt[:, cs].astype(BF16), NT, preferred_element_type=F32)
            att = jnp.where(keep, att, 0.0).astype(BF16)
            s_p = state[p]
            o_inter = jnp.dot(qs, s_p.astype(BF16), preferred_element_type=F32)
            ke_t = jnp.transpose(ke[:, cs]).astype(BF16)
            upd = []
            for hh in range(2):
                h = 2 * p + hh
                v_h = v_ref[r0:r0 + CHUNK, h * DV_HEAD:(h + 1) * DV_HEAD]
                both = jnp.dot(jnp.concatenate([att[hh * CHUNK:(hh + 1) * CHUNK], ke_t], axis=0), v_h,
                               preferred_element_type=F32)
                o_ref[r0:r0 + CHUNK, h * DV_HEAD:(h + 1) * DV_HEAD] = (
                    both[0:CHUNK] + o_inter[hh * CHUNK:(hh + 1) * CHUNK])
                upd.append(both[CHUNK:CHUNK + LANES])
            bl_col = bl_cols[p * LANES:(p + 1) * LANES, c * LANES:(c + 1) * LANES]
            state[p] = jnp.exp(bl_col) * s_p + jnp.where(srow < DK_HEAD, upd[0], upd[1])
            yield
    s_scr[0] = state[0]
    s_scr[1] = state[1]


def _gla_kernel(*refs, n_tiles, nb, zero_init):
    seqs = [refs[6 * r:6 * r + 6] for r in range(nb)]
    tril_ref, triu_ref, blk_ref = refs[6 * nb:6 * nb + 3]
    s0_ref = None if zero_init else refs[6 * nb + 3]
    of_ref, ob_ref, sout_ref, sf_scr, sb_scr = refs[-5:]
    j = pl.program_id(1)

    @pl.when(j == 0)
    def _():
        sf_scr[...] = jnp.zeros(sf_scr.shape, F32) if zero_init else s0_ref[:, 0]
        sb_scr[...] = jnp.zeros(sb_scr.shape, F32) if zero_init else s0_ref[:, 1]

    scans = []
    for r, (qkf_ref, vf_ref, laf_ref, qkb_ref, vb_ref, lab_ref) in enumerate(seqs):
        scans.append(_gla_direction(qkf_ref, vf_ref, laf_ref, tril_ref, blk_ref, of_ref.at[r], sf_scr.at[r], False))
        scans.append(_gla_direction(qkb_ref, vb_ref, lab_ref, triu_ref, blk_ref, ob_ref.at[r], sb_scr.at[r], True))
    while scans:
        scans = [gen for gen in scans if next(gen, "done") != "done"]

    @pl.when(j == n_tiles - 1)
    def _():
        sout_ref[:, 0] = sf_scr[...]
        sout_ref[:, 1] = sb_scr[...]


def _gla(qk, v, la, s0, n_seq, n_tiles, tile_off):
    nb = next(b for b in (4, 2, 1) if n_seq % b == 0)
    seq_len = n_tiles * TM
    blk = np.arange(TM) // CHUNK
    same = blk[:, None] == blk[None, :]
    r = np.arange(TM)
    tril = jnp.asarray((same & (r[None, :] <= r[:, None])).astype(np.float32)).astype(BF16)
    triu = jnp.asarray((same & (r[None, :] >= r[:, None])).astype(np.float32)).astype(BF16)
    chunk_of_lane = np.arange(TM // CHUNK * LANES) // LANES
    blkm = jnp.asarray((blk[:, None] == chunk_of_lane[None, :]).astype(np.float32)).astype(BF16)
    def tok(w, r, backward, cb=0):
        def index(s, j):
            jj = (n_tiles - 1 - j) if backward else j
            return ((s * nb + r) * n_tiles + jj + tile_off, cb)
        return pl.BlockSpec((TM, w), index)

    const = lambda a: pl.BlockSpec(a.shape, lambda s, j: (0,) * a.ndim)
    st = pl.BlockSpec((nb, 2, 2, LANES, LANES), lambda s, j: (s, 0, 0, 0, 0))
    seq_specs, seq_args = [], []
    for r in range(nb):
        seq_specs += [tok(2 * DK_TOT, r, False), tok(DV_TOT, r, False), tok(DK_TOT, r, False, 0),
                      tok(2 * DK_TOT, r, True), tok(DV_TOT, r, True), tok(DK_TOT, r, True, 1)]
        seq_args += [qk, v, la, qk, v, la]
    state_in = ([], []) if s0 is None else ([st], [s0])
    o_f, o_b, s_out = pl.pallas_call(
        functools.partial(_gla_kernel, n_tiles=n_tiles, nb=nb, zero_init=s0 is None),
        grid=(n_seq // nb, n_tiles),
        in_specs=seq_specs + [const(tril), const(triu), const(blkm)] + state_in[0],
        out_specs=[pl.BlockSpec((nb, TM, DV_TOT), lambda s, j: (s, j, 0)),
                   pl.BlockSpec((nb, TM, DV_TOT), lambda s, j: (s, n_tiles - 1 - j, 0)), st],
        out_shape=[jax.ShapeDtypeStruct((n_seq, seq_len, DV_TOT), F32),
                   jax.ShapeDtypeStruct((n_seq, seq_len, DV_TOT), F32),
                   jax.ShapeDtypeStruct((n_seq, 2, 2, LANES, LANES), F32)],
        scratch_shapes=[pltpu.VMEM((nb, 2, LANES, LANES), F32), pltpu.VMEM((nb, 2, LANES, LANES), F32)],
        compiler_params=_cparams(("arbitrary", "arbitrary"), VMEM_LIMIT),
        name="gla",
    )(*seq_args, tril, triu, blkm, *state_in[1])
    return o_f.reshape(n_seq * seq_len, DV_TOT), o_b.reshape(n_seq * seq_len, DV_TOT), s_out


def _mix_conv(u_ref, cw_ref, pad_scr, shf_scr, cv_scr, seq_len):
    n_seq = TM // seq_len
    pr = seq_len + 2 * CONV_PAD
    qr = seq_len + 3 * SEG

    u = u_ref[...].astype(F32)
    hg = u[:, 0:C_CONV] * jax.nn.sigmoid(u[:, C_CONV:2 * C_CONV])
    zero = jnp.zeros((CONV_PAD, C_CONV), F32)
    for s in range(n_seq):
        pad_scr[s * pr:s * pr + CONV_PAD, :] = zero
        pad_scr[s * pr + CONV_PAD:s * pr + CONV_PAD + seq_len, :] = hg[s * seq_len:(s + 1) * seq_len]
        pad_scr[s * pr + CONV_PAD + seq_len:(s + 1) * pr, :] = zero
        for b in range(1, SEG):
            shf_scr[b - 1, s * qr:(s + 1) * qr, :] = pad_scr[s * pr + b:s * pr + b + qr, :]
    rb = 8 * SEG
    per_seq = seq_len // rb
    for gi in range(C_CONV // LANES):
        gs = slice(gi * LANES, (gi + 1) * LANES)
        wv = [jnp.broadcast_to(cw_ref[tap:tap + 1, gs], (SEG, LANES)) for tap in range(CONV_K)]

        def conv_rows(i, carry, gs=gs, wv=wv):
            s = lax.shift_right_logical(i, per_seq.bit_length() - 1)
            r0 = (i & (per_seq - 1)) * rb
            nj = rb // SEG
            acc = [None] * nj
            for b in range(SEG):
                taps = [(a, SEG * a + b - (CONV_PAD - CONV_K // 2)) for a in range(CONV_PAD * 2 // SEG)]
                taps = [(a, t) for a, t in taps if 0 <= t < CONV_K]
                loaded = {}
                for a, t in taps:
                    for j in range(nj):
                        if a + j not in loaded:
                            if b == 0:
                                row = pl.multiple_of(s * pr + r0 + (a + j) * SEG, SEG)
                                loaded[a + j] = pad_scr[pl.ds(row, SEG), gs]
                            else:
                                row = pl.multiple_of(s * qr + r0 + (a + j) * SEG, SEG)
                                loaded[a + j] = shf_scr[b - 1, pl.ds(row, SEG), gs]
                        term = wv[t] * loaded[a + j]
                        acc[j] = term if acc[j] is None else acc[j] + term
            for j in range(nj):
                cv_scr[pl.ds(pl.multiple_of(s * seq_len + r0 + j * SEG, SEG), SEG), gs] = acc[j]
            return carry

        lax.fori_loop(0, TM // rb, conv_rows, 0)


def _mix_rest(x_ref, g_ref, of_ref, ob_ref, mod_ref, cv_scr, cb_ref, lng_ref, lnb_ref, gng_ref, wout_ref,
              n2g_ref, rwh_ref, rwl_ref, rb_ref, le_ref, ut_ref, x1_ref, grp_ref, route_ref, cnt_ref):
    cv = cv_scr[...] + cb_ref[...]
    mu = jnp.mean(cv, axis=-1, keepdims=True)
    var = jnp.mean(jnp.square(cv - mu), axis=-1, keepdims=True)
    cv =(cv - mu) * lax.rsqrt(var + EPS) * lng_ref[...] + lnb_ref[...]
    conv_out = (cv * jax.nn.sigmoid(cv)).astype(BF16)
    yield

    o = of_ref[...] + ob_ref[...]
    g = g_ref[...].astype(F32)
    heads = []
    for h in range(N_GLA_HEADS):
        cs = slice(h * DV_HEAD, (h + 1) * DV_HEAD)
        oh = o[:, cs]
        oh = oh * lax.rsqrt(jnp.mean(oh * oh, axis=-1, keepdims=True) + EPS) * gng_ref[...]
        gh = g[:, cs]
        heads.append((oh * (gh * jax.nn.sigmoid(gh))).astype(BF16))
    yield
    y = jnp.dot(conv_out, wout_ref[0:C_CONV, :], preferred_element_type=F32)
    for h in range(N_GLA_HEADS):
        y = y + jnp.dot(heads[h], wout_ref[C_CONV + h * DV_HEAD:C_CONV + (h + 1) * DV_HEAD, :],
                        preferred_element_type=F32)
    yield

    gate1 = mod_ref[0, :, 2 * D_MODEL:3 * D_MODEL]
    shift2 = mod_ref[0, :, 3 * D_MODEL:4 * D_MODEL]
    scale2 = mod_ref[0, :, 4 * D_MODEL:5 * D_MODEL]
    x1 = x_ref[...] + gate1 * y
    x1_ref[...] = x1
    h2 = _rms(x1, n2g_ref[...]) * (1.0 + scale2) + shift2
    h2_hi, h2_lo = _split(h2)
    yield

    ntdot = functools.partial(lax.dot_general, dimension_numbers=NT, preferred_element_type=F32)
    l = (ntdot(rwh_ref[...], h2_hi) + ntdot(rwh_ref[...], h2_lo) + ntdot(rwl_ref[...], h2_hi)
         + rb_ref[...])
    yield
    erow = lax.broadcasted_iota(jnp.int32, (N_EXPERTS, TM), 0).astype(F32)
    vals, idxs, hots = [], [], []
    for _ in range(TOP_K):
        m = jnp.max(l, axis=0, keepdims=True)
        idx = jnp.min(jnp.where(l == m, erow, float(N_EXPERTS)), axis=0, keepdims=True)
        hot = erow == idx
        l = jnp.where(hot, -jnp.inf, l)
        vals.append(m)
        idxs.append(idx)
        hots.append(hot)
        yield
    es = [jnp.exp(vk - vals[0]) for vk in vals]
    inv = 1.0 / (es[0] + es[1] + es[2] + es[3])
    sel = jnp.zeros((N_EXPERTS, TM), F32)
    for hot in hots:
        sel = jnp.where(hot, 1.0, sel)

    cnt = jnp.broadcast_to(jnp.sum(sel, axis=1, keepdims=True), (N_EXPERTS, LANES))
    cnt_ref[0] = cnt
    strip = jnp.ceil(cnt * (1.0 / SEG)) * float(SEG)
    starts = jnp.dot(le_ref[...], strip, precision=HI, preferred_element_type=F32)[:, 0:1]
    slot_of = jnp.dot(sel.astype(BF16), ut_ref[...], preferred_element_type=F32) + starts
    slots = [jnp.sum(jnp.where(hot, slot_of, 0.0), axis=0, keepdims=True) for hot in hots]

    rows = slots + idxs + [e * inv for e in es]
    srow = lax.broadcasted_iota(jnp.int32, (16, TM), 0)
    packed = jnp.zeros((16, TM), F32)
    for j, v in enumerate(rows):
        packed = jnp.where(srow == j, v, packed)
    packed = jnp.concatenate([packed, jnp.zeros((LANES - 16, TM), F32)], axis=0)
    route_ref[...] = jnp.transpose(packed)
    yield

    r = lax.broadcasted_iota(jnp.int32, (NR, TM), 0).astype(F32)
    perm = jnp.zeros((NR, TM), F32)
    for slot in slots:
        perm = jnp.where(slot == r, 1.0, perm)
    yield
    grp_ref[...] = jnp.dot(perm.astype(BF16), h2_hi, preferred_element_type=F32)


def _mix_kernel(*refs, seq_lens):
    k = len(seq_lens)
    ins = [refs[6 * p:6 * p + 6] for p in range(k)]
    cw_ref, *shared = refs[6 * k:6 * k + 12]
    outs = [refs[6 * k + 12 + 4 * p:6 * k + 16 + 4 * p] for p in range(k)]
    scr = [refs[10 * k + 12 + 3 * p:10 * k + 15 + 3 * p] for p in range(k)]
    for p in range(k):
        _mix_conv(ins[p][1], cw_ref, *scr[p], seq_lens[p])
    stages = []
    for p in range(k):
        x_ref, _, g_ref, of_ref, ob_ref, mod_ref = ins[p]
        stages.append(_mix_rest(x_ref, g_ref, of_ref, ob_ref, mod_ref, scr[p][2], *shared, *outs[p]))
    while stages:
        stages = [gen for gen in stages if next(gen, "done") != "done"]


def _mix(paths, u, g, mod3, wts):
    (conv_w, conv_b, ln_g, ln_b, gng, wout, n2g, rwh, rwl, rb) = wts
    le = jnp.asarray(np.tril(np.ones((N_EXPERTS, N_EXPERTS), np.float32), -1))
    ut = jnp.asarray(np.triu(np.ones((TM, TM), np.float32), 1)).astype(BF16)
    shared = (conv_w, conv_b, ln_g, ln_b, gng, wout, n2g, rwh, rwl, rb, le, ut)
    const = lambda a: pl.BlockSpec(a.shape, lambda i: (0,) * a.ndim)
    n_steps = max(x.shape[0] // TM for x, *_ in paths)
    in_specs, args, out_specs, out_shape, scratch = [], [], [], [], []
    for x, o_f, o_b, seg_fn, tile_off, seq_len in paths:
        n = x.shape[0] // TM
        cur = lambda i, n=n: jnp.minimum(i, n - 1)
        loc = lambda w, cur=cur: pl.BlockSpec((TM, w), lambda i: (cur(i), 0))
        uni = lambda w, cur=cur, off=tile_off: pl.BlockSpec((TM, w), lambda i: (cur(i) + off, 0))
        in_specs += [loc(D_MODEL), uni(2 * C_CONV), uni(DV_TOT), loc(DV_TOT), loc(DV_TOT),
                     pl.BlockSpec((1, 1, 6 * D_MODEL), lambda i, cur=cur, f=seg_fn: (f(cur(i)), 0, 0))]
        args += [x, u, g, o_f, o_b, mod3]
        out_specs += [loc(D_MODEL), pl.BlockSpec((NR, D_MODEL), lambda i, cur=cur: (cur(i), 0)), loc(LANES),
                      pl.BlockSpec((1, N_EXPERTS, LANES), lambda i, cur=cur: (cur(i), 0, 0))]
        out_shape += [jax.ShapeDtypeStruct((n * TM, D_MODEL), F32),
                      jax.ShapeDtypeStruct((n * NR, D_MODEL), F32),
                      jax.ShapeDtypeStruct((n * TM, LANES), F32),
                      jax.ShapeDtypeStruct((n, N_EXPERTS, LANES), F32)]
        n_seq = TM // seq_len
        scratch += [pltpu.VMEM((n_seq * (seq_len + 2 * CONV_PAD), C_CONV), F32),
                    pltpu.VMEM((SEG - 1, n_seq * (seq_len + 3 * SEG), C_CONV), F32),
                    pltpu.VMEM((TM, C_CONV), F32)]
    outs = pl.pallas_call(
        functools.partial(_mix_kernel, seq_lens=tuple(p[5] for p in paths)),
        grid=(n_steps,),
        in_specs=in_specs + [const(a) for a in shared],
        out_specs=out_specs,
        out_shape=out_shape,
        scratch_shapes=scratch,
        compiler_params=_cparams(("arbitrary",), VMEM_LIMIT),
        name="mix",
    )(*args, *shared)
    return [outs[4 * p:4 * p + 4] for p in range(len(paths))]


def _strip_copies(tile, ls_ref, gs_ref, sg_ref, local, rows_hbm, sem, to_hbm, wait):
    def body(e, carry):
        j = tile * N_EXPERTS + e
        n = pl.multiple_of(sg_ref[j], SEG)

        @pl.when(n > 0)
        def _():
            loc = local.at[pl.ds(pl.multiple_of(ls_ref[j], SEG), n)]
            glob = rows_hbm.at[pl.ds(pl.multiple_of(gs_ref[j], SEG), n)]
            cp = pltpu.make_async_copy(loc, glob, sem) if to_hbm else pltpu.make_async_copy(glob, loc, sem)
            if wait:
                cp.wait()
            else:
                cp.start()
        return carry

    lax.fori_loop(0, N_EXPERTS, body, 0)


def _expert_kernel(te_ref, nv_ref, first_ref, next_ref, rows_ref, ptr_ref, mid_ref, src_ref, dst_ref,
                   len_ref, csrc_ref, clen_ref, grpc_hbm, grpl_hbm, wgu_hbm, bgu_ref, wdn_hbm, bdn_ref,
                   y_ref, xbuf, wgu_f, wdn_f, wgu_s, wdn_s, xsem, sem, *, ctx_rows):
    i = pl.program_id(0)

    def weight_copies(e):
        return (pltpu.make_async_copy(wgu_hbm.at[e], wgu_f, sem.at[0]),
                pltpu.make_async_copy(wdn_hbm.at[e], wdn_f, sem.at[1]))

    def row_copies(tile, xs, wait):
        def strip(hbm, src, dst, n):
            cp = pltpu.make_async_copy(hbm.at[pl.ds(pl.multiple_of(src, SEG), n)],
                                       xbuf.at[xs, pl.ds(pl.multiple_of(dst, SEG), n)], xsem.at[xs])
            cp.wait() if wait else cp.start()

        def strips_of(hbm, base):
            def body(j, carry):
                n = pl.multiple_of(len_ref[j], SEG)

                @pl.when(n > 0)
                def _():
                    strip(hbm, src_ref[j] - base, dst_ref[j], n)
                return carry
            return body

        lax.fori_loop(ptr_ref[tile], mid_ref[tile], strips_of(grpc_hbm, 0), 0)
        lax.fori_loop(mid_ref[tile], ptr_ref[tile + 1], strips_of(grpl_hbm, ctx_rows), 0)
        n = pl.multiple_of(clen_ref[tile], SEG)
        src = csrc_ref[tile]

        @pl.when(jnp.logical_and(n > 0, src < ctx_rows))
        def _():
            strip(grpc_hbm, src, 0, n)

        @pl.when(jnp.logical_and(n > 0, src >= ctx_rows))
        def _():
            strip(grpl_hbm, src - ctx_rows, 0, n)

    @pl.when(i == 0)
    def _():
        xbuf[...] = jnp.zeros(xbuf.shape, F32)
        row_copies(i, 0, False)

    @pl.when(i + 1 < nv_ref[0])
    def _():
        row_copies(i + 1, (i + 1) % 2, False)

    @pl.when(i < nv_ref[0])
    def _():
        e = te_ref[i]
        row_copies(i, i % 2, True)

        @pl.when(first_ref[i] == 1)
        def _():
            @pl.when(i == 0)
            def _():
                for cp in weight_copies(e):
                    cp.start()

            for cp in weight_copies(e):
                cp.wait()
            rows = 64

            def cast(r, carry):
                sl = pl.ds(pl.multiple_of(r * rows, rows), rows)
                wgu_s[sl, :] = wgu_f[sl, :].astype(BF16)
                wdn_s[sl, :] = wdn_f[sl, :].astype(BF16)
                return carry

            lax.fori_loop(0, D_MODEL // rows, cast, 0)
            nxt = next_ref[i]

            @pl.when(nxt >= 0)
            def _():
                for cp in weight_copies(nxt):
                    cp.start()

        def compute(m):
            x = xbuf[i % 2, 0:m, :].astype(BF16)
            gu = jnp.dot(x, wgu_s[...], preferred_element_type=F32) + bgu_ref[0]
            gate = jnp.minimum(gu[:, 0:D_FF], SWIGLU_LIMIT)
            up = jnp.clip(gu[:, D_FF:2 * D_FF], -SWIGLU_LIMIT, SWIGLU_LIMIT)
            act = (gate * jax.nn.sigmoid(SWIGLU_ALPHA * gate) * (up + 1.0)).astype(BF16)
            y_ref[0:m, :] = jnp.dot(act, wdn_s[...], preferred_element_type=F32) + bdn_ref[0]
            if m < TE:
                y_ref[m:TE, :] = jnp.zeros((TE - m, D_MODEL), F32)

        for m in range(TE // 4, TE + 1, TE // 4):
            pl.when(rows_ref[i] == m)(functools.partial(compute, m))

    @pl.when(i >= nv_ref[0])
    def _():
        y_ref[...] = jnp.zeros(y_ref.shape, y_ref.dtype)


def _experts(plan, grp_c, grp_l, n_rows, w_gu, b_gu, w_dn, b_dn):
    nt = n_rows // TE
    exp3 = lambda i, te, *_: (te[i], 0, 0)
    any_spec = pl.BlockSpec(memory_space=pl.ANY)
    return pl.pallas_call(
        functools.partial(_expert_kernel, ctx_rows=grp_c.shape[0]),
        grid_spec=pltpu.PrefetchScalarGridSpec(
            num_scalar_prefetch=12,
            grid=(nt,),
            in_specs=[any_spec, any_spec, any_spec,
                      pl.BlockSpec((1, 1, 2 * D_FF), exp3), any_spec,
                      pl.BlockSpec((1, 1, D_MODEL), exp3)],
            out_specs=pl.BlockSpec((TE, D_MODEL), lambda i, *_: (i, 0)),
            scratch_shapes=[pltpu.VMEM((2, TE, D_MODEL), F32),
                            pltpu.VMEM((D_MODEL, 2 * D_FF), F32), pltpu.VMEM((D_FF, D_MODEL), F32),
                            pltpu.VMEM((D_MODEL, 2 * D_FF), BF16), pltpu.VMEM((D_FF, D_MODEL), BF16),
                            pltpu.SemaphoreType.DMA((2,)), pltpu.SemaphoreType.DMA((2,))]),
        out_shape=jax.ShapeDtypeStruct((n_rows, D_MODEL), F32),
        compiler_params=_cparams(("arbitrary",), VMEM_LIMIT),
        name="experts",
    )(plan["tile_expert"], plan["n_valid"], plan["tile_first"], plan["tile_next"], plan["tile_rows"], plan["seg_ptr"], plan["seg_mid"], plan["seg_src"], plan["seg_dst"], plan["seg_len"],
      plan["carry_src"], plan["carry_len"], grp_c, grp_l, w_gu, b_gu.reshape(N_EXPERTS, 1, 2 * D_FF), w_dn, b_dn.reshape(N_EXPERTS, 1, D_MODEL))


def _final_kernel(ls_ref, gs_ref, sg_ref, x1_ref, route_ref, mod_ref, fg_ref, ys_ref, y_ref,
                  ybuf, sem, *, tile_off, n_tiles):
    i = pl.program_id(0)
    slot = i % 2

    def fetch(step, sl, wait):
        _strip_copies(step + tile_off, ls_ref, gs_ref, sg_ref, ybuf.at[sl], ys_ref, sem.at[sl], False, wait)

    @pl.when(i == 0)
    def _():
        ybuf[...] = jnp.zeros(ybuf.shape, F32)
        fetch(i, slot, False)

    @pl.when(i + 1 < n_tiles)
    def _():
        fetch(i + 1, 1 - slot, False)

    fetch(i, slot, True)

    route = route_ref[...]
    r = lax.broadcasted_iota(jnp.int32, (TM, NR), 1).astype(F32)
    comb = jnp.zeros((TM, NR), F32)
    for kk in range(TOP_K):
        gate = route[:, 2 * TOP_K + kk:2 * TOP_K + kk + 1]
        comb = jnp.where(route[:, kk:kk + 1] == r, gate, comb)
    moe = jnp.dot(comb.astype(BF16), ybuf[slot].astype(BF16), preferred_element_type=F32)
    gate2 = mod_ref[0, :, 5 * D_MODEL:6 * D_MODEL]
    y_ref[...] = _rms(x1_ref[...] + gate2 * moe, fg_ref[...])


def _final(plan, x1, route, mod3, seg_fn, tile_off, fg, ys):
    t_path = x1.shape[0]
    n = t_path // TM
    loc = lambda w: pl.BlockSpec((TM, w), lambda i, *_: (i, 0))
    return pl.pallas_call(
        functools.partial(_final_kernel, tile_off=tile_off, n_tiles=n),
        grid_spec=pltpu.PrefetchScalarGridSpec(
            num_scalar_prefetch=3,
            grid=(n,),
            in_specs=[loc(D_MODEL), loc(LANES),
                      pl.BlockSpec((1, 1, 6 * D_MODEL), lambda i, *_: (seg_fn(i), 0, 0)),
                      pl.BlockSpec((1, D_MODEL), lambda i, *_: (0, 0)),
                      pl.BlockSpec(memory_space=pl.ANY)],
            out_specs=loc(D_MODEL),
            scratch_shapes=[pltpu.VMEM((2, NR, D_MODEL), F32), pltpu.SemaphoreType.DMA((2,))]),
        out_shape=jax.ShapeDtypeStruct((t_path, D_MODEL), F32),
        compiler_params=_cparams(("arbitrary",), VMEM_LIMIT),
        name="final",
    )(plan["lstart"], plan["gstart"], plan["strip"], x1, route, mod3, fg, ys)


def _plan(tile_counts, n_rows, n_ctx):
    nt = n_rows // TE
    n_tok = tile_counts.shape[0]
    strip = (tile_counts + SEG - 1) // SEG * SEG
    lstart = jnp.cumsum(strip, axis=1) - strip
    rows_e = jnp.sum(strip, axis=0)
    rpad = (rows_e + TE - 1) // TE * TE
    ends = jnp.cumsum(rpad)
    offs = ends - rpad
    gstart = offs[None, :] + jnp.cumsum(strip, axis=0) - strip
    n_valid = (ends[-1] // TE).astype(jnp.int32)
    tile_ids = jnp.minimum(jnp.arange(nt, dtype=jnp.int32), n_valid - 1)
    tile_expert = jnp.sum((ends[None, :] <= (tile_ids * TE)[:, None]).astype(jnp.int32), axis=1)
    tile_expert = jnp.minimum(tile_expert, N_EXPERTS - 1).astype(jnp.int32)
    prev = jnp.concatenate([jnp.full((1,), -1, jnp.int32), tile_expert[:-1]])
    tile_first = (tile_expert != prev).astype(jnp.int32)
    pick = lambda table, idx: jnp.sum(
        jnp.where(idx[:, None] == jnp.arange(table.shape[0], dtype=jnp.int32)[None, :], table[None, :], 0), axis=1)
    used = jnp.clip(pick(offs + rows_e, tile_expert) - jnp.arange(nt, dtype=jnp.int32) * TE, 0, TE)
    tile_rows = jnp.maximum((used + TE // 4 - 1) // (TE // 4), 1) * (TE // 4)
    after = pick(ends, tile_expert) // TE
    tile_next = jnp.where(after < n_valid, pick(tile_expert, jnp.minimum(after, nt - 1)), -1)
    tok = jnp.broadcast_to(jnp.arange(n_tok, dtype=jnp.int32)[:, None], strip.shape)
    g0 = gstart.T.reshape(-1)
    ln = strip.T.reshape(-1)
    src0 = (tok * NR + lstart).T.reshape(-1)
    is_ctx = (tok < n_ctx).T.reshape(-1)
    len1 = jnp.minimum(ln, TE - g0 % TE)
    tile0 = g0 // TE
    tiles = jnp.arange(nt + 1, dtype=jnp.int32)[:, None]
    count = lambda m: jnp.sum(m.astype(jnp.int32), axis=1)
    seg_ptr = count(tile0[None, :] < tiles)
    seg_mid = seg_ptr + count((tile0[None, :] == tiles) & is_ctx[None, :])
    into = ((g0 + len1) // TE)[None, :] == tiles[:nt]
    carry_len = jnp.sum(jnp.where(into, (ln - len1)[None, :], 0), axis=1)
    carry_src = jnp.sum(jnp.where(into & (ln > len1)[None, :], (src0 + len1)[None, :], 0), axis=1)
    i32 = lambda a: a.reshape(-1).astype(jnp.int32)
    return dict(strip=i32(strip), lstart=i32(lstart), gstart=i32(gstart), n_valid=n_valid.reshape(1),
                tile_expert=tile_expert, tile_first=i32(tile_first), tile_next=i32(tile_next),
                tile_rows=i32(tile_rows), seg_ptr=i32(seg_ptr), seg_mid=i32(seg_mid), seg_src=i32(src0),
                seg_dst=i32(g0 % TE), seg_len=i32(len1), carry_src=i32(carry_src),
                carry_len=i32(carry_len))


def _layer(x_prompt, x_sample, state, c, c_ctx, ada_w, ada_b, norm1_g, norm2_g, w_in, conv_w, conv_b,
           conv_ln_g, conv_ln_b, gate_w, gate_b, gla_norm_g, w_out, router_w, router_b,
           moe_w_gu, moe_b_gu, moe_w_dn, moe_b_dn, final_g):
    bp, lp, d = x_prompt.shape
    bs, ls, _ = x_sample.shape
    assert lp == TM and ls % TM == 0 and d == D_MODEL
    xp = x_prompt.reshape(bp * lp, d)
    xs = x_sample.reshape(bs * ls, d)
    n_c, n_l = bp, bs * ls // TM
    lat_tiles = ls // TM
    t_all = (n_c + n_l) * TM

    n_cond = 1 + bs
    cond_t = jnp.concatenate([c_ctx[None, :], c, jnp.zeros((8 - n_cond, d), F32)], axis=0).T
    mod3 = _modulation(cond_t, ada_w, ada_b, n_cond).reshape(8, 1, 6 * d)

    row = lambda a: a.reshape(1, -1)
    u, qk, v, g, la = _inproj(xp, xs, mod3, row(norm1_g), w_in, gate_w, gate_b, ls)

    pair = lambda s: s.reshape(s.shape[0], 2, 2, LANES, LANES)
    of_c, ob_c, s_ctx = _gla(qk, v, la, None, bp, 1, 0)
    of_l, ob_l, _ = _gla(qk, v, la, pair(state), bs, lat_tiles, n_c)

    cw = jnp.zeros((32, C_CONV), F32).at[:CONV_K].set(conv_w)
    rwh, rwl = _split(router_w.T)
    rb = jnp.broadcast_to(router_b[:, None], (N_EXPERTS, TM))
    wts = (cw, row(conv_b), row(conv_ln_g), row(conv_ln_b), row(gla_norm_g), w_out.astype(BF16),
           row(norm2_g), rwh, rwl, rb)
    seg_c = lambda i: 0
    seg_l = lambda i: 1 + i // lat_tiles
    (x1_c, grp_c, route_c, cnt_c), (x1_l, grp_l, route_l, cnt_l) = _mix(
        [(xp, of_c, ob_c, seg_c, 0, lp), (xs, of_l, ob_l, seg_l, n_c, GRID_W)], u, g, mod3, wts)

    tile_counts = jnp.concatenate([cnt_c[:, :, 0], cnt_l[:, :, 0]]).astype(jnp.int32)
    n_rows = TOP_K * t_all + (SEG - 1) * N_EXPERTS * (n_c + n_l) + N_EXPERTS * TE
    n_rows = (n_rows + TE - 1) // TE * TE
    plan = _plan(tile_counts, n_rows, n_c)
    ysrt = _experts(plan, grp_c, grp_l, n_rows, moe_w_gu, moe_b_gu, moe_w_dn, moe_b_dn)
    y_c = _final(plan, x1_c, route_c, mod3, seg_c, 0, row(final_g), ysrt)
    y_l = _final(plan, x1_l, route_l, mod3, seg_l, n_c, row(final_g), ysrt)
    new_state = s_ctx.reshape(bp, 1, 2, N_GLA_HEADS, DK_HEAD, DV_HEAD)
    return y_c.reshape(bp, lp, d), y_l.reshape(bs, ls, d), new_state


def kernel(x_prompt, x_sample, state_gla, c, c_ctx, ada_w, ada_b, norm1_g, norm2_g, w_in, conv_w,
           conv_b, conv_ln_g, conv_ln_b, gate_w, gate_b, gla_norm_g, w_out, router_w, router_b,
           moe_w_gu, moe_b_gu, moe_w_dn, moe_b_dn, final_g):
    assert ada_w.shape[0] == 1, "single-layer step"
    return _layer(x_prompt, x_sample, state_gla[:, 0], c, c_ctx, ada_w[0], ada_b[0], norm1_g[0],
                  norm2_g[0], w_in[0], conv_w[0], conv_b[0], conv_ln_g[0], conv_ln_b[0], gate_w[0],
                  gate_b[0], gla_norm_g[0], w_out[0], router_w[0], router_b[0], moe_w_gu[0],
                  moe_b_gu[0], moe_w_dn[0], moe_b_dn[0], final_g)
```

```python
import functools

import numpy as np
import jax
import jax.numpy as jnp
from jax import lax
from jax.experimental import pallas as pl
from jax.experimental.pallas import tpu as pltpu

D_MODEL = 1024
GRID_W = 64
C_CONV = D_MODEL // 2
CONV_K = 31
N_GLA_HEADS = 4
DV_HEAD = 128
DK_HEAD = 64
DK_TOT = DK_HEAD * N_GLA_HEADS
DV_TOT = DV_HEAD * N_GLA_HEADS
GATE_RANK = 16
GATE_TEMP = 16.0
CHUNK = 64
N_EXPERTS = 32
TOP_K = 4
D_FF = D_MODEL
SWIGLU_LIMIT = 7.0
SWIGLU_ALPHA = 1.702
EPS = 1e-6

LANES = 128
SEG = 8
TM = 256
TE = 1024
NR = TOP_K * TM + TM
CONV_PAD = 16
V7X_VMEM_BYTES = 64 * 1024 * 1024
VMEM_LIMIT = V7X_VMEM_BYTES - 8 * 1024 * 1024

F32 = jnp.float32
BF16 = jnp.bfloat16
HI = lax.Precision.HIGHEST
TN =(((0,), (0,)), ((), ()))
NT = (((1,), (1,)), ((), ()))

assert N_EXPERTS * (SEG - 1) <= NR - TOP_K * TM


def _split(x):
    hi = x.astype(BF16)
    return hi, (x - hi.astype(F32)).astype(BF16)


def _cparams(sem, vmem=None):
    return pltpu.CompilerParams(dimension_semantics=sem, vmem_limit_bytes=vmem)


def _mod_kernel(ct_ref, w_ref, b_ref, o_ref, *, n_cond):
    ct = ct_ref[...]
    s = ct * jax.nn.sigmoid(ct)
    w = w_ref[...]
    rows = [jnp.sum(s[:, r:r + 1] * w, axis=0, keepdims=True) + b_ref[...] for r in range(n_cond)]
    rows.append(jnp.zeros((8 - n_cond, w.shape[1]), F32))
    o_ref[...] = jnp.concatenate(rows, axis=0)


def _modulation(cond_t, ada_w, ada_b, n_cond):
    d, n = ada_w.shape
    nb = 768
    return pl.pallas_call(
        functools.partial(_mod_kernel, n_cond=n_cond),
        grid=(n // nb,),
        in_specs=[pl.BlockSpec((d, 8), lambda i: (0, 0)),
                  pl.BlockSpec((d, nb), lambda i: (0, i)),
                  pl.BlockSpec((1, nb), lambda i: (0, i))],
        out_specs=pl.BlockSpec((8, nb), lambda i: (0, i)),
        out_shape=jax.ShapeDtypeStruct((8, n), F32),
        compiler_params=_cparams(("arbitrary",)),
        name="mod",
    )(cond_t, ada_w, ada_b.reshape(1, n))


def _rms(x, g):
    return x * lax.rsqrt(jnp.mean(x * x, axis=-1, keepdims=True) + EPS) * g


def _inproj_kernel(xp_ref, xs_ref, mod_ref, g1_ref, wu_ref, wqk_ref, wv_ref, wg_ref, wlow_ref,
                   gwh_ref, gwl_ref, gb_ref, u_ref, qk_ref, v_ref, g_ref, la_ref, *, n_ctx_tiles):
    i = pl.program_id(0)
    shift = mod_ref[0, :, 0:D_MODEL]
    scale = mod_ref[0, :, D_MODEL:2 * D_MODEL]
    dot = functools.partial(jnp.dot, preferred_element_type=F32)
    rows = u_ref.shape[0]
    sub = min(rows, TM)
    def block(r0):
        rs = slice(r0, r0 + sub)
        x = jnp.where(i < n_ctx_tiles, xp_ref[rs, :], xs_ref[rs, :])
        h = (_rms(x, g1_ref[...]) * (1.0 + scale) + shift).astype(BF16)
        yield
        u_ref[rs, :] = dot(h, wu_ref[...]).astype(BF16)
        yield
        qk_ref[rs, :] = dot(h, wqk_ref[...]).astype(BF16)
        v_ref[rs, :] = dot(h, wv_ref[...]).astype(BF16)
        yield
        g_ref[rs, :] = dot(h, wg_ref[...]).astype(BF16)
        low = dot(h, wlow_ref[...])
        yield
        low_hi, low_lo = _split(low)
        z = (dot(low_hi, gwh_ref[...]) + dot(low_lo, gwh_ref[...]) + dot(low_hi, gwl_ref[...])
             + gb_ref[...])
        la_ref[rs, :] = (jnp.minimum(z, 0.0) - jnp.log1p(jnp.exp(-jnp.abs(z)))) * (1.0 / GATE_TEMP)

    blocks = [block(r0) for r0 in range(0, rows, sub)]
    while blocks:
        blocks = [gen for gen in blocks if next(gen, "done") != "done"]


def _inproj(xp, xs, mod3, g1, w_in, gate_w, gate_b, lat_len):
    ti = next(t for t in (1024, 512, TM) if xp.shape[0] % t == 0 and lat_len % t == 0)
    lat_tiles_per_seq = lat_len // ti
    n_c, n_l = xp.shape[0] // ti, xs.shape[0] // ti
    t_all = xp.shape[0] + xs.shape[0]
    sp = np.cumsum([0, C_CONV, C_CONV, DK_TOT, DK_TOT, DV_TOT, DV_TOT, 2 * GATE_RANK])
    wu, wqk, wv, wg, wlow = (w_in[:, a:b].astype(BF16) for a, b in
                             ((sp[0], sp[2]), (sp[2], sp[4]), (sp[4], sp[5]), (sp[5], sp[6]), (sp[6], sp[7])))
    gw = jnp.zeros((2 * GATE_RANK, 2 * DK_TOT), F32)
    gw = gw.at[:GATE_RANK, :DK_TOT].set(gate_w[0]).at[GATE_RANK:, DK_TOT:].set(gate_w[1])
    gwh, gwl = _split(gw)
    gb = gate_b.reshape(1, 2 * DK_TOT)
    const = lambda a: pl.BlockSpec(a.shape, lambda i: (0,) * a.ndim)
    row = lambda w: pl.BlockSpec((ti, w), lambda i: (i, 0))
    seg = lambda i: jnp.where(i < n_c, 0, 1 + jnp.maximum(i - n_c, 0) // lat_tiles_per_seq)
    return pl.pallas_call(
        functools.partial(_inproj_kernel, n_ctx_tiles=n_c),
        grid=(n_c + n_l,),
        in_specs=[pl.BlockSpec((ti, D_MODEL), lambda i: (jnp.minimum(i, n_c - 1), 0)),
                  pl.BlockSpec((ti, D_MODEL), lambda i: (jnp.maximum(i - n_c, 0), 0)),
                  pl.BlockSpec((1, 1, 6 * D_MODEL), lambda i: (seg(i), 0, 0)),
                  const(g1), const(wu), const(wqk), const(wv), const(wg), const(wlow),
                  const(gwh), const(gwl), const(gb)],
        out_specs=[row(2 * C_CONV), row(2 * DK_TOT), row(DV_TOT), row(DV_TOT), row(2 * DK_TOT)],
        out_shape=[jax.ShapeDtypeStruct((t_all, 2 * C_CONV), BF16),
                   jax.ShapeDtypeStruct((t_all, 2 * DK_TOT), BF16),
                   jax.ShapeDtypeStruct((t_all, DV_TOT), BF16),
                   jax.ShapeDtypeStruct((t_all, DV_TOT), BF16),
                   jax.ShapeDtypeStruct((t_all, 2 * DK_TOT), F32)],
        compiler_params=_cparams(("arbitrary",), VMEM_LIMIT),
        name="inproj",
    )(xp, xs, mod3, g1, wu, wqk, wv, wg, wlow, gwh, gwl, gb)


def _gla_direction(qk_ref, v_ref, la_ref, tri_ref, blk_ref, o_ref, s_scr, reverse):
    la_hi, la_lo = _split(la_ref[...])
    tri = tri_ref[...]
    bcum = (jnp.dot(tri, la_hi, preferred_element_type=F32)
            + jnp.dot(tri, la_lo, preferred_element_type=F32))
    blk = blk_ref[...]
    bl_cols = (lax.dot_general(la_hi, blk, TN, preferred_element_type=F32)
               + lax.dot_general(la_lo, blk, TN, preferred_element_type=F32))
    yield
    q = qk_ref[:, 0:DK_TOT].astype(F32)
    k = qk_ref[:, DK_TOT:2 * DK_TOT].astype(F32)
    lane = lax.broadcasted_iota(jnp.int32, (CHUNK, LANES), 1)
    row2 = lax.broadcasted_iota(jnp.int32, (2 * CHUNK, CHUNK), 0) % CHUNK
    col2 = lax.broadcasted_iota(jnp.int32, (2 * CHUNK, CHUNK), 1)
    keep = (col2 >= row2) if reverse else (col2 <= row2)
    srow = lax.broadcasted_iota(jnp.int32, (LANES, LANES), 0)
    n_chunks = TM // CHUNK
    order = range(n_chunks - 1, -1, -1) if reverse else range(n_chunks)
    state = [s_scr[0], s_scr[1]]
    for c in order:
        r0 = c * CHUNK
        bc = bcum[r0:r0 + CHUNK]
        bl = bc[0:1] if reverse else bc[CHUNK - 1:CHUNK]
        qt = q[r0:r0 + CHUNK] * jnp.exp(bc) * (DK_HEAD ** -0.5)
        kt = k[r0:r0 + CHUNK] * jnp.exp(-bc)
        ke = k[r0:r0 + CHUNK] * jnp.exp(bl - bc)
        yield
        for p in range(2):
            cs = slice(p * LANES, (p + 1) * LANES)
            qt_p = qt[:, cs]
            qs = jnp.concatenate([jnp.where(lane < DK_HEAD, qt_p, 0.0),
                                  jnp.where(lane >= DK_HEAD, qt_p, 0.0)], axis=0).astype(BF16)
            att = lax.dot_general(qs, kt[:, cs].astype(BF16), NT, preferred_element_type=F32)
            att = jnp.where(keep, att, 0.0).astype(BF16)
            s_p = state[p]
            o_inter = jnp.dot(qs, s_p.astype(BF16), preferred_element_type=F32)
            ke_t = jnp.transpose(ke[:, cs]).astype(BF16)
            upd = []
            for hh in range(2):
                h = 2 * p + hh
                v_h = v_ref[r0:r0 + CHUNK, h * DV_HEAD:(h + 1) * DV_HEAD]
                both = jnp.dot(jnp.concatenate([att[hh * CHUNK:(hh + 1) * CHUNK], ke_t], axis=0), v_h,
                               preferred_element_type=F32)
                o_ref[r0:r0 + CHUNK, h * DV_HEAD:(h + 1) * DV_HEAD] = (
                    both[0:CHUNK] + o_inter[hh * CHUNK:(hh + 1) * CHUNK])
                upd.append(both[CHUNK:CHUNK + LANES])
            bl_col = bl_cols[p * LANES:(p + 1) * LANES, c * LANES:(c + 1) * LANES]
            state[p] = jnp.exp(bl_col) * s_p + jnp.where(srow < DK_HEAD, upd[0], upd[1])
            yield
    s_scr[0] = state[0]
    s_scr[1] = state[1]


def _gla_kernel(*refs, n_tiles, nb, zero_init):
    per = 4 if n_tiles == 1 else 6
    seqs = [refs[per * r:per * r + per] for r in range(nb)]
    if n_tiles == 1:
        seqs = [(qk, v, laf, qk, v, lab) for qk, v, laf, lab in seqs]
    tril_ref, triu_ref, blk_ref = refs[per * nb:per * nb + 3]
    s0_ref = None if zero_init else refs[per * nb + 3]
    of_ref, ob_ref, sout_ref, sf_scr, sb_scr = refs[-5:]
    j = pl.program_id(1)

    @pl.when(j == 0)
    def _():
        sf_scr[...] = jnp.zeros(sf_scr.shape, F32) if zero_init else s0_ref[:, 0]
        sb_scr[...] = jnp.zeros(sb_scr.shape, F32) if zero_init else s0_ref[:, 1]

    scans = []
    for r, (qkf_ref, vf_ref, laf_ref, qkb_ref, vb_ref, lab_ref) in enumerate(seqs):
        scans.append(_gla_direction(qkf_ref, vf_ref, laf_ref, tril_ref, blk_ref, of_ref.at[r], sf_scr.at[r], False))
        scans.append(_gla_direction(qkb_ref, vb_ref, lab_ref, triu_ref, blk_ref, ob_ref.at[r], sb_scr.at[r], True))
    while scans:
        scans = [gen for gen in scans if next(gen, "done") != "done"]

    @pl.when(j == n_tiles - 1)
    def _():
        sout_ref[:, 0] = sf_scr[...]
        sout_ref[:, 1] = sb_scr[...]


def _gla(qk, v, la, s0, n_seq, n_tiles, tile_off):
    nb = next(b for b in (4, 2, 1) if n_seq % b == 0)
    seq_len = n_tiles * TM
    blk = np.arange(TM) // CHUNK
    same = blk[:, None] == blk[None, :]
    r = np.arange(TM)
    tril = jnp.asarray((same & (r[None, :] <= r[:, None])).astype(np.float32)).astype(BF16)
    triu = jnp.asarray((same & (r[None, :] >= r[:, None])).astype(np.float32)).astype(BF16)
    chunk_of_lane = np.arange(TM // CHUNK * LANES) // LANES
    blkm = jnp.asarray((blk[:, None] == chunk_of_lane[None, :]).astype(np.float32)).astype(BF16)
    def tok(w, r, backward, cb=0):
        def index(s, j):
            jj = (n_tiles - 1 - j) if backward else j
            return ((s * nb + r) * n_tiles + jj + tile_off, cb)
        return pl.BlockSpec((TM, w), index)

    const = lambda a: pl.BlockSpec(a.shape, lambda s, j: (0,) * a.ndim)
    st = pl.BlockSpec((nb, 2, 2, LANES, LANES), lambda s, j: (s, 0, 0, 0, 0))
    seq_specs, seq_args = [], []
    for r in range(nb):
        if n_tiles == 1:
            seq_specs += [tok(2 * DK_TOT, r, False), tok(DV_TOT, r, False), tok(DK_TOT, r, False, 0),
                          tok(DK_TOT, r, True, 1)]
            seq_args += [qk, v, la, la]
        else:
            seq_specs += [tok(2 * DK_TOT, r, False), tok(DV_TOT, r, False), tok(DK_TOT, r, False, 0),
                          tok(2 * DK_TOT, r, True), tok(DV_TOT, r, True), tok(DK_TOT, r, True, 1)]
            seq_args += [qk, v, la, qk, v, la]
    state_in = ([], []) if s0 is None else ([st], [s0])
    o_f, o_b, s_out = pl.pallas_call(
        functools.partial(_gla_kernel, n_tiles=n_tiles, nb=nb, zero_init=s0 is None),
        grid=(n_seq // nb, n_tiles),
        in_specs=seq_specs + [const(tril), const(triu), const(blkm)] + state_in[0],
        out_specs=[pl.BlockSpec((nb, TM, DV_TOT), lambda s, j: (s, j, 0)),
                   pl.BlockSpec((nb, TM, DV_TOT), lambda s, j: (s, n_tiles - 1 - j, 0)), st],
        out_shape=[jax.ShapeDtypeStruct((n_seq, seq_len, DV_TOT), F32),
                   jax.ShapeDtypeStruct((n_seq, seq_len, DV_TOT), F32),
                   jax.ShapeDtypeStruct((n_seq, 2, 2, LANES, LANES), F32)],
        scratch_shapes=[pltpu.VMEM((nb, 2, LANES, LANES), F32), pltpu.VMEM((nb, 2, LANES, LANES), F32)],
        compiler_params=_cparams(("arbitrary", "arbitrary"), VMEM_LIMIT),
        name="gla",
    )(*seq_args, tril, triu, blkm, *state_in[1])
    return o_f.reshape(n_seq * seq_len, DV_TOT), o_b.reshape(n_seq * seq_len, DV_TOT), s_out


def _mix_conv(u_ref, cw_ref, pad_scr, shf_scr, cv_scr, seq_len):
    n_seq = TM // seq_len
    pr = seq_len + 2 * CONV_PAD
    qr = seq_len + 3 * SEG

    u = u_ref[...].astype(F32)
    hg = u[:, 0:C_CONV] * jax.nn.sigmoid(u[:, C_CONV:2 * C_CONV])
    zero = jnp.zeros((CONV_PAD, C_CONV), F32)
    for s in range(n_seq):
        pad_scr[s * pr:s * pr + CONV_PAD, :] = zero
        pad_scr[s * pr + CONV_PAD:s * pr + CONV_PAD + seq_len, :] = hg[s * seq_len:(s + 1) * seq_len]
        pad_scr[s * pr + CONV_PAD + seq_len:(s + 1) * pr, :] = zero
        for b in range(1, SEG):
            shf_scr[b - 1, s * qr:(s + 1) * qr, :] = pad_scr[s * pr + b:s * pr + b + qr, :]
    rb = 8 * SEG
    per_seq = seq_len // rb
    for gi in range(C_CONV // LANES):
        gs = slice(gi * LANES, (gi + 1) * LANES)
        wv = [jnp.broadcast_to(cw_ref[tap:tap + 1, gs], (SEG, LANES)) for tap in range(CONV_K)]

        def conv_rows(i, carry, gs=gs, wv=wv):
            s = lax.shift_right_logical(i, per_seq.bit_length() - 1)
            r0 = (i & (per_seq - 1)) * rb
            nj = rb // SEG
            acc = [None] * nj
            for b in range(SEG):
                taps = [(a, SEG * a + b - (CONV_PAD - CONV_K // 2)) for a in range(CONV_PAD * 2 // SEG)]
                taps = [(a, t) for a, t in taps if 0 <= t < CONV_K]
                loaded = {}
                for a, t in taps:
                    for j in range(nj):
                        if a + j not in loaded:
                            if b == 0:
                                row = pl.multiple_of(s * pr + r0 + (a + j) * SEG, SEG)
                                loaded[a + j] = pad_scr[pl.ds(row, SEG), gs]
                            else:
                                row = pl.multiple_of(s * qr + r0 + (a + j) * SEG, SEG)
                                loaded[a + j] = shf_scr[b - 1, pl.ds(row, SEG), gs]
                        term = wv[t] * loaded[a + j]
                        acc[j] = term if acc[j] is None else acc[j] + term
            for j in range(nj):
                cv_scr[pl.ds(pl.multiple_of(s * seq_len + r0 + j * SEG, SEG), SEG), gs] = acc[j]
            return carry

        lax.fori_loop(0, TM // rb, conv_rows, 0)


def _mix_rest(x_ref, g_ref, of_ref, ob_ref, mod_ref, cv_scr, cb_ref, lng_ref, lnb_ref, gng_ref, wout_ref,
              n2g_ref, rwh_ref, rwl_ref, rb_ref, le_ref, ut_ref, x1_ref, grp_ref, route_ref, cnt_ref):
    cv = cv_scr[...] + cb_ref[...]
    mu = jnp.mean(cv, axis=-1, keepdims=True)
    var = jnp.mean(jnp.square(cv - mu), axis=-1, keepdims=True)
    cv =(cv - mu) * lax.rsqrt(var + EPS) * lng_ref[...] + lnb_ref[...]
    conv_out = (cv * jax.nn.sigmoid(cv)).astype(BF16)
    yield

    o = of_ref[...] + ob_ref[...]
    g = g_ref[...].astype(F32)
    heads = []
    for h in range(N_GLA_HEADS):
        cs = slice(h * DV_HEAD, (h + 1) * DV_HEAD)
        oh = o[:, cs]
        oh = oh * lax.rsqrt(jnp.mean(oh * oh, axis=-1, keepdims=True) + EPS) * gng_ref[...]
        gh = g[:, cs]
        heads.append((oh * (gh * jax.nn.sigmoid(gh))).astype(BF16))
    yield
    y = jnp.dot(conv_out, wout_ref[0:C_CONV, :], preferred_element_type=F32)
    for h in range(N_GLA_HEADS):
        y = y + jnp.dot(heads[h], wout_ref[C_CONV + h * DV_HEAD:C_CONV + (h + 1) * DV_HEAD, :],
                        preferred_element_type=F32)
    yield

    gate1 = mod_ref[0, :, 2 * D_MODEL:3 * D_MODEL]
    shift2 = mod_ref[0, :, 3 * D_MODEL:4 * D_MODEL]
    scale2 = mod_ref[0, :, 4 * D_MODEL:5 * D_MODEL]
    x1 = x_ref[...] + gate1 * y
    x1_ref[...] = x1
    h2 = _rms(x1, n2g_ref[...]) * (1.0 + scale2) + shift2
    h2_hi, h2_lo = _split(h2)
    yield

    ntdot = functools.partial(lax.dot_general, dimension_numbers=NT, preferred_element_type=F32)
    l = (ntdot(rwh_ref[...], h2_hi) + ntdot(rwh_ref[...], h2_lo) + ntdot(rwl_ref[...], h2_hi)
         + rb_ref[...])
    yield
    erow = lax.broadcasted_iota(jnp.int32, (N_EXPERTS, TM), 0).astype(F32)
    vals, idxs, hots = [], [], []
    for _ in range(TOP_K):
        m = jnp.max(l, axis=0, keepdims=True)
        idx = jnp.min(jnp.where(l == m, erow, float(N_EXPERTS)), axis=0, keepdims=True)
        hot = erow == idx
        l = jnp.where(hot, -jnp.inf, l)
        vals.append(m)
        idxs.append(idx)
        hots.append(hot)
        yield
    es = [jnp.exp(vk - vals[0]) for vk in vals]
    inv = 1.0 / (es[0] + es[1] + es[2] + es[3])
    sel = jnp.zeros((N_EXPERTS, TM), F32)
    for hot in hots:
        sel = jnp.where(hot, 1.0, sel)

    cnt = jnp.broadcast_to(jnp.sum(sel, axis=1, keepdims=True), (N_EXPERTS, LANES))
    cnt_ref[0] = cnt
    strip = jnp.ceil(cnt * (1.0 / SEG)) * float(SEG)
    starts = jnp.dot(le_ref[...], strip, precision=HI, preferred_element_type=F32)[:, 0:1]
    slot_of = jnp.dot(sel.astype(BF16), ut_ref[...], preferred_element_type=F32) + starts
    slots = [jnp.sum(jnp.where(hot, slot_of, 0.0), axis=0, keepdims=True) for hot in hots]

    rows = slots + idxs + [e * inv for e in es]
    srow = lax.broadcasted_iota(jnp.int32, (16, TM), 0)
    packed = jnp.zeros((16, TM), F32)
    for j, v in enumerate(rows):
        packed = jnp.where(srow == j, v, packed)
    packed = jnp.concatenate([packed, jnp.zeros((LANES - 16, TM), F32)], axis=0)
    route_ref[...] = jnp.transpose(packed)
    yield

    r = lax.broadcasted_iota(jnp.int32, (NR, TM), 0).astype(F32)
    perm = jnp.zeros((NR, TM), F32)
    for slot in slots:
        perm = jnp.where(slot == r, 1.0, perm)
    yield
    grp_ref[...] = jnp.dot(perm.astype(BF16), h2_hi, preferred_element_type=F32)


def _mix_kernel(*refs, seq_lens):
    k = len(seq_lens)
    ins = [refs[6 * p:6 * p + 6] for p in range(k)]
    cw_ref, *shared = refs[6 * k:6 * k + 12]
    outs = [refs[6 * k + 12 + 4 * p:6 * k + 16 + 4 * p] for p in range(k)]
    scr = [refs[10 * k + 12 + 3 * p:10 * k + 15 + 3 * p] for p in range(k)]
    for p in range(k):
        _mix_conv(ins[p][1], cw_ref, *scr[p], seq_lens[p])
    stages = []
    for p in range(k):
        x_ref, _, g_ref, of_ref, ob_ref, mod_ref = ins[p]
        stages.append(_mix_rest(x_ref, g_ref, of_ref, ob_ref, mod_ref, scr[p][2], *shared, *outs[p]))
    while stages:
        stages = [gen for gen in stages if next(gen, "done") != "done"]


def _mix(paths, u, g, mod3, wts):
    (conv_w, conv_b, ln_g, ln_b, gng, wout, n2g, rwh, rwl, rb) = wts
    le = jnp.asarray(np.tril(np.ones((N_EXPERTS, N_EXPERTS), np.float32), -1))
    ut = jnp.asarray(np.triu(np.ones((TM, TM), np.float32), 1)).astype(BF16)
    shared = (conv_w, conv_b, ln_g, ln_b, gng, wout, n2g, rwh, rwl, rb, le, ut)
    const = lambda a: pl.BlockSpec(a.shape, lambda i: (0,) * a.ndim)
    n_steps = max(x.shape[0] // TM for x, *_ in paths)
    in_specs, args, out_specs, out_shape, scratch = [], [], [], [], []
    for x, o_f, o_b, seg_fn, tile_off, seq_len in paths:
        n = x.shape[0] // TM
        cur = lambda i, n=n: jnp.minimum(i, n - 1)
        loc = lambda w, cur=cur: pl.BlockSpec((TM, w), lambda i: (cur(i), 0))
        uni = lambda w, cur=cur, off=tile_off: pl.BlockSpec((TM, w), lambda i: (cur(i) + off, 0))
        in_specs += [loc(D_MODEL), uni(2 * C_CONV), uni(DV_TOT), loc(DV_TOT), loc(DV_TOT),
                     pl.BlockSpec((1, 1, 6 * D_MODEL), lambda i, cur=cur, f=seg_fn: (f(cur(i)), 0, 0))]
        args += [x, u, g, o_f, o_b, mod3]
        out_specs += [loc(D_MODEL), pl.BlockSpec((NR, D_MODEL), lambda i, cur=cur: (cur(i), 0)), loc(LANES),
                      pl.BlockSpec((1, N_EXPERTS, LANES), lambda i, cur=cur: (cur(i), 0, 0))]
        out_shape += [jax.ShapeDtypeStruct((n * TM, D_MODEL), F32),
                      jax.ShapeDtypeStruct((n * NR, D_MODEL), F32),
                      jax.ShapeDtypeStruct((n * TM, LANES), F32),
                      jax.ShapeDtypeStruct((n, N_EXPERTS, LANES), F32)]
        n_seq = TM // seq_len
        scratch += [pltpu.VMEM((n_seq * (seq_len + 2 * CONV_PAD), C_CONV), F32),
                    pltpu.VMEM((SEG - 1, n_seq * (seq_len + 3 * SEG), C_CONV), F32),
                    pltpu.VMEM((TM, C_CONV), F32)]
    outs = pl.pallas_call(
        functools.partial(_mix_kernel, seq_lens=tuple(p[5] for p in paths)),
        grid=(n_steps,),
        in_specs=in_specs + [const(a) for a in shared],
        out_specs=out_specs,
        out_shape=out_shape,
        scratch_shapes=scratch,
        compiler_params=_cparams(("arbitrary",), VMEM_LIMIT),
        name="mix",
    )(*args, *shared)
    return [outs[4 * p:4 * p + 4] for p in range(len(paths))]


def _strip_copies(tile, ls_ref, gs_ref, sg_ref, local, rows_hbm, sem, to_hbm, wait):
    def body(e, carry):
        j = tile * N_EXPERTS + e
        n = pl.multiple_of(sg_ref[j], SEG)

        @pl.when(n > 0)
        def _():
            loc = local.at[pl.ds(pl.multiple_of(ls_ref[j], SEG), n)]
            glob = rows_hbm.at[pl.ds(pl.multiple_of(gs_ref[j], SEG), n)]
            cp = pltpu.make_async_copy(loc, glob, sem) if to_hbm else pltpu.make_async_copy(glob, loc, sem)
            if wait:
                cp.wait()
            else:
                cp.start()
        return carry

    lax.fori_loop(0, N_EXPERTS, body, 0)


def _expert_kernel(te_ref, nv_ref, first_ref, next_ref, rows_ref, ptr_ref, mid_ref, src_ref, dst_ref,
                   len_ref, csrc_ref, clen_ref, grpc_hbm, grpl_hbm, wgu_hbm, bgu_ref, wdn_hbm, bdn_ref,
                   y_ref, xbuf, wgu_f, wdn_f, wgu_s, wdn_s, xsem, sem, *, ctx_rows):
    i = pl.program_id(0)

    def weight_copies(e):
        return (pltpu.make_async_copy(wgu_hbm.at[e], wgu_f, sem.at[0]),
                pltpu.make_async_copy(wdn_hbm.at[e], wdn_f, sem.at[1]))

    def row_copies(tile, xs, wait):
        def strip(hbm, src, dst, n):
            cp = pltpu.make_async_copy(hbm.at[pl.ds(pl.multiple_of(src, SEG), n)],
                                       xbuf.at[xs, pl.ds(pl.multiple_of(dst, SEG), n)], xsem.at[xs])
            cp.wait() if wait else cp.start()

        def strips_of(hbm, base):
            def body(j, carry):
                n = pl.multiple_of(len_ref[j], SEG)

                @pl.when(n > 0)
                def _():
                    strip(hbm, src_ref[j] - base, dst_ref[j], n)
                return carry
            return body

        lax.fori_loop(ptr_ref[tile], mid_ref[tile], strips_of(grpc_hbm, 0), 0)
        lax.fori_loop(mid_ref[tile], ptr_ref[tile + 1], strips_of(grpl_hbm, ctx_rows), 0)
        n = pl.multiple_of(clen_ref[tile], SEG)
        src = csrc_ref[tile]

        @pl.when(jnp.logical_and(n > 0, src < ctx_rows))
        def _():
            strip(grpc_hbm, src, 0, n)

        @pl.when(jnp.logical_and(n > 0, src >= ctx_rows))
        def _():
            strip(grpl_hbm, src - ctx_rows, 0, n)

    @pl.when(i == 0)
    def _():
        xbuf[...] = jnp.zeros(xbuf.shape, F32)
        row_copies(i, 0, False)

    @pl.when(i + 1 < nv_ref[0])
    def _():
        row_copies(i + 1, (i + 1) % 2, False)

    @pl.when(i < nv_ref[0])
    def _():
        e = te_ref[i]
        row_copies(i, i % 2, True)

        @pl.when(first_ref[i] == 1)
        def _():
            @pl.when(i == 0)
            def _():
                for cp in weight_copies(e):
                    cp.start()

            for cp in weight_copies(e):
                cp.wait()
            rows = 64

            def cast(r, carry):
                sl = pl.ds(pl.multiple_of(r * rows, rows), rows)
                wgu_s[sl, :] = wgu_f[sl, :].astype(BF16)
                wdn_s[sl, :] = wdn_f[sl, :].astype(BF16)
                return carry

            lax.fori_loop(0, D_MODEL // rows, cast, 0)
            nxt = next_ref[i]

            @pl.when(nxt >= 0)
            def _():
                for cp in weight_copies(nxt):
                    cp.start()

        def compute(m):
            x = xbuf[i % 2, 0:m, :].astype(BF16)
            gu = jnp.dot(x, wgu_s[...], preferred_element_type=F32) + bgu_ref[0]
            gate = jnp.minimum(gu[:, 0:D_FF], SWIGLU_LIMIT)
            up = jnp.clip(gu[:, D_FF:2 * D_FF], -SWIGLU_LIMIT, SWIGLU_LIMIT)
            act = (gate * jax.nn.sigmoid(SWIGLU_ALPHA * gate) * (up + 1.0)).astype(BF16)
            y_ref[0:m, :] = jnp.dot(act, wdn_s[...], preferred_element_type=F32) + bdn_ref[0]
            if m < TE:
                y_ref[m:TE, :] = jnp.zeros((TE - m, D_MODEL), F32)

        for m in range(TE // 4, TE + 1, TE // 4):
            pl.when(rows_ref[i] == m)(functools.partial(compute, m))

    @pl.when(i >= nv_ref[0])
    def _():
        y_ref[...] = jnp.zeros(y_ref.shape, y_ref.dtype)


def _experts(plan, grp_c, grp_l, n_rows, w_gu, b_gu, w_dn, b_dn):
    nt = n_rows // TE
    exp3 = lambda i, te, *_: (te[i], 0, 0)
    any_spec = pl.BlockSpec(memory_space=pl.ANY)
    return pl.pallas_call(
        functools.partial(_expert_kernel, ctx_rows=grp_c.shape[0]),
        grid_spec=pltpu.PrefetchScalarGridSpec(
            num_scalar_prefetch=12,
            grid=(nt,),
            in_specs=[any_spec, any_spec, any_spec,
                      pl.BlockSpec((1, 1, 2 * D_FF), exp3), any_spec,
                      pl.BlockSpec((1, 1, D_MODEL), exp3)],
            out_specs=pl.BlockSpec((TE, D_MODEL), lambda i, *_: (i, 0)),
            scratch_shapes=[pltpu.VMEM((2, TE, D_MODEL), F32),
                            pltpu.VMEM((D_MODEL, 2 * D_FF), F32), pltpu.VMEM((D_FF, D_MODEL), F32),
                            pltpu.VMEM((D_MODEL, 2 * D_FF), BF16), pltpu.VMEM((D_FF, D_MODEL), BF16),
                            pltpu.SemaphoreType.DMA((2,)), pltpu.SemaphoreType.DMA((2,))]),
        out_shape=jax.ShapeDtypeStruct((n_rows, D_MODEL), F32),
        compiler_params=_cparams(("arbitrary",), VMEM_LIMIT),
        name="experts",
    )(plan["tile_expert"], plan["n_valid"], plan["tile_first"], plan["tile_next"], plan["tile_rows"], plan["seg_ptr"], plan["seg_mid"], plan["seg_src"], plan["seg_dst"], plan["seg_len"],
      plan["carry_src"], plan["carry_len"], grp_c, grp_l, w_gu, b_gu.reshape(N_EXPERTS, 1, 2 * D_FF), w_dn, b_dn.reshape(N_EXPERTS, 1, D_MODEL))


def _final_kernel(ls_ref, gs_ref, sg_ref, x1_ref, route_ref, mod_ref, fg_ref, ys_ref, y_ref,
                  ybuf, sem, *, tile_off, n_tiles):
    i = pl.program_id(0)
    slot = i % 2

    def fetch(step, sl, wait):
        _strip_copies(step + tile_off, ls_ref, gs_ref, sg_ref, ybuf.at[sl], ys_ref, sem.at[sl], False, wait)

    @pl.when(i == 0)
    def _():
        ybuf[...] = jnp.zeros(ybuf.shape, F32)
        fetch(i, slot, False)

    @pl.when(i + 1 < n_tiles)
    def _():
        fetch(i + 1, 1 - slot, False)

    fetch(i, slot, True)

    route = route_ref[...]
    r = lax.broadcasted_iota(jnp.int32, (TM, NR), 1).astype(F32)
    comb = jnp.zeros((TM, NR), F32)
    for kk in range(TOP_K):
        gate = route[:, 2 * TOP_K + kk:2 * TOP_K + kk + 1]
        comb = jnp.where(route[:, kk:kk + 1] == r, gate, comb)
    moe = jnp.dot(comb.astype(BF16), ybuf[slot].astype(BF16), preferred_element_type=F32)
    gate2 = mod_ref[0, :, 5 * D_MODEL:6 * D_MODEL]
    y_ref[...] = _rms(x1_ref[...] + gate2 * moe, fg_ref[...])


def _final(plan, x1, route, mod3, seg_fn, tile_off, fg, ys):
    t_path = x1.shape[0]
    n = t_path // TM
    loc = lambda w: pl.BlockSpec((TM, w), lambda i, *_: (i, 0))
    return pl.pallas_call(
        functools.partial(_final_kernel, tile_off=tile_off, n_tiles=n),
        grid_spec=pltpu.PrefetchScalarGridSpec(
            num_scalar_prefetch=3,
            grid=(n,),
            in_specs=[loc(D_MODEL), loc(LANES),
                      pl.BlockSpec((1, 1, 6 * D_MODEL), lambda i, *_: (seg_fn(i), 0, 0)),
                      pl.BlockSpec((1, D_MODEL), lambda i, *_: (0, 0)),
                      pl.BlockSpec(memory_space=pl.ANY)],
            out_specs=loc(D_MODEL),
            scratch_shapes=[pltpu.VMEM((2, NR, D_MODEL), F32), pltpu.SemaphoreType.DMA((2,))]),
        out_shape=jax.ShapeDtypeStruct((t_path, D_MODEL), F32),
        compiler_params=_cparams(("arbitrary",), VMEM_LIMIT),
        name="final",
    )(plan["lstart"], plan["gstart"], plan["strip"], x1, route, mod3, fg, ys)


def _plan(tile_counts, n_rows, n_ctx):
    nt = n_rows // TE
    n_tok = tile_counts.shape[0]
    strip = (tile_counts + SEG - 1) // SEG * SEG
    lstart = jnp.cumsum(strip, axis=1) - strip
    rows_e = jnp.sum(strip, axis=0)
    rpad = (rows_e + TE - 1) // TE * TE
    ends = jnp.cumsum(rpad)
    offs = ends - rpad
    gstart = offs[None, :] + jnp.cumsum(strip, axis=0) - strip
    n_valid = (ends[-1] // TE).astype(jnp.int32)
    tile_ids = jnp.minimum(jnp.arange(nt, dtype=jnp.int32), n_valid - 1)
    tile_expert = jnp.sum((ends[None, :] <= (tile_ids * TE)[:, None]).astype(jnp.int32), axis=1)
    tile_expert = jnp.minimum(tile_expert, N_EXPERTS - 1).astype(jnp.int32)
    prev = jnp.concatenate([jnp.full((1,), -1, jnp.int32), tile_expert[:-1]])
    tile_first = (tile_expert != prev).astype(jnp.int32)
    pick = lambda table, idx: jnp.sum(
        jnp.where(idx[:, None] == jnp.arange(table.shape[0], dtype=jnp.int32)[None, :], table[None, :], 0), axis=1)
    used = jnp.clip(pick(offs + rows_e, tile_expert) - jnp.arange(nt, dtype=jnp.int32) * TE, 0, TE)
    tile_rows = jnp.maximum((used + TE // 4 - 1) // (TE // 4), 1) * (TE // 4)
    after = pick(ends, tile_expert) // TE
    tile_next = jnp.where(after < n_valid, pick(tile_expert, jnp.minimum(after, nt - 1)), -1)
    tok = jnp.broadcast_to(jnp.arange(n_tok, dtype=jnp.int32)[:, None], strip.shape)
    g0 = gstart.T.reshape(-1)
    ln = strip.T.reshape(-1)
    src0 = (tok * NR + lstart).T.reshape(-1)
    is_ctx = (tok < n_ctx).T.reshape(-1)
    len1 = jnp.minimum(ln, TE - g0 % TE)
    tile0 = g0 // TE
    tiles = jnp.arange(nt + 1, dtype=jnp.int32)[:, None]
    count = lambda m: jnp.sum(m.astype(jnp.int32), axis=1)
    seg_ptr = count(tile0[None, :] < tiles)
    seg_mid = seg_ptr + count((tile0[None, :] == tiles) & is_ctx[None, :])
    into = ((g0 + len1) // TE)[None, :] == tiles[:nt]
    carry_len = jnp.sum(jnp.where(into, (ln - len1)[None, :], 0), axis=1)
    carry_src = jnp.sum(jnp.where(into & (ln > len1)[None, :], (src0 + len1)[None, :], 0), axis=1)
    i32 = lambda a: a.reshape(-1).astype(jnp.int32)
    return dict(strip=i32(strip), lstart=i32(lstart), gstart=i32(gstart), n_valid=n_valid.reshape(1),
                tile_expert=tile_expert, tile_first=i32(tile_first), tile_next=i32(tile_next),
                tile_rows=i32(tile_rows), seg_ptr=i32(seg_ptr), seg_mid=i32(seg_mid), seg_src=i32(src0),
                seg_dst=i32(g0 % TE), seg_len=i32(len1), carry_src=i32(carry_src),
                carry_len=i32(carry_len))


def _layer(x_prompt, x_sample, state, c, c_ctx, ada_w, ada_b, norm1_g, norm2_g, w_in, conv_w, conv_b,
           conv_ln_g, conv_ln_b, gate_w, gate_b, gla_norm_g, w_out, router_w, router_b,
           moe_w_gu, moe_b_gu, moe_w_dn, moe_b_dn, final_g):
    bp, lp, d = x_prompt.shape
    bs, ls, _ = x_sample.shape
    assert lp == TM and ls % TM == 0 and d == D_MODEL
    xp = x_prompt.reshape(bp * lp, d)
    xs = x_sample.reshape(bs * ls, d)
    n_c, n_l = bp, bs * ls // TM
    lat_tiles = ls // TM
    t_all = (n_c + n_l) * TM

    n_cond = 1 + bs
    cond_t = jnp.concatenate([c_ctx[None, :], c, jnp.zeros((8 - n_cond, d), F32)], axis=0).T
    mod3 = _modulation(cond_t, ada_w, ada_b, n_cond).reshape(8, 1, 6 * d)

    row = lambda a: a.reshape(1, -1)
    u, qk, v, g, la = _inproj(xp, xs, mod3, row(norm1_g), w_in, gate_w, gate_b, ls)

    pair = lambda s: s.reshape(s.shape[0], 2, 2, LANES, LANES)
    of_c, ob_c, s_ctx = _gla(qk, v, la, None, bp, 1, 0)
    of_l, ob_l, _ = _gla(qk, v, la, pair(state), bs, lat_tiles, n_c)

    cw = jnp.zeros((32, C_CONV), F32).at[:CONV_K].set(conv_w)
    rwh, rwl = _split(router_w.T)
    rb = jnp.broadcast_to(router_b[:, None], (N_EXPERTS, TM))
    wts = (cw, row(conv_b), row(conv_ln_g), row(conv_ln_b), row(gla_norm_g), w_out.astype(BF16),
           row(norm2_g), rwh, rwl, rb)
    seg_c = lambda i: 0
    seg_l = lambda i: 1 + i // lat_tiles
    (x1_c, grp_c, route_c, cnt_c), (x1_l, grp_l, route_l, cnt_l) = _mix(
        [(xp, of_c, ob_c, seg_c, 0, lp), (xs, of_l, ob_l, seg_l, n_c, GRID_W)], u, g, mod3, wts)

    tile_counts = jnp.concatenate([cnt_c[:, :, 0], cnt_l[:, :, 0]]).astype(jnp.int32)
    n_rows = TOP_K * t_all + (SEG - 1) * N_EXPERTS * (n_c + n_l) + N_EXPERTS * TE
    n_rows = (n_rows + TE - 1) // TE * TE
    plan = _plan(tile_counts, n_rows, n_c)
    ysrt = _experts(plan, grp_c, grp_l, n_rows, moe_w_gu, moe_b_gu, moe_w_dn, moe_b_dn)
    y_c = _final(plan, x1_c, route_c, mod3, seg_c, 0, row(final_g), ysrt)
    y_l = _final(plan, x1_l, route_l, mod3, seg_l, n_c, row(final_g), ysrt)
    new_state = s_ctx.reshape(bp, 1, 2, N_GLA_HEADS, DK_HEAD, DV_HEAD)
    return y_c.reshape(bp, lp, d), y_l.reshape(bs, ls, d), new_state


def kernel(x_prompt, x_sample, state_gla, c, c_ctx, ada_w, ada_b, norm1_g, norm2_g, w_in, conv_w,
           conv_b, conv_ln_g, conv_ln_b, gate_w, gate_b, gla_norm_g, w_out, router_w, router_b,
           moe_w_gu, moe_b_gu, moe_w_dn, moe_b_dn, final_g):
    assert ada_w.shape[0] == 1, "single-layer step"
    return _layer(x_prompt, x_sample, state_gla[:, 0], c, c_ctx, ada_w[0], ada_b[0], norm1_g[0],
                  norm2_g[0], w_in[0], conv_w[0], conv_b[0], conv_ln_g[0], conv_ln_b[0], gate_w[0],
                  gate_b[0], gla_norm_g[0], w_out[0], router_w[0], router_b[0], moe_w_gu[0],
                  moe_b_gu[0], moe_w_dn[0], moe_b_dn[0], final_g)
```

```python
import functools

import numpy as np
import jax
import jax.numpy as jnp
from jax import lax
from jax.experimental import pallas as pl
from jax.experimental.pallas import tpu as pltpu

D_MODEL = 1024
GRID_W = 64
C_CONV = D_MODEL // 2
CONV_K = 31
N_GLA_HEADS = 4
DV_HEAD = 128
DK_HEAD = 64
DK_TOT = DK_HEAD * N_GLA_HEADS
DV_TOT = DV_HEAD * N_GLA_HEADS
GATE_RANK = 16
GATE_TEMP = 16.0
CHUNK = 64
N_EXPERTS = 32
TOP_K = 4
D_FF = D_MODEL
SWIGLU_LIMIT = 7.0
SWIGLU_ALPHA = 1.702
EPS = 1e-6

LANES = 128
SEG = 8
TM = 256
TE = 1024
NR = TOP_K * TM + TM
CONV_PAD = 16
FINAL_RING = 3
V7X_VMEM_BYTES = 64 * 1024 * 1024
VMEM_LIMIT = V7X_VMEM_BYTES - 8 * 1024 * 1024

F32 = jnp.float32
BF16 = jnp.bfloat16
HI = lax.Precision.HIGHEST
TN =(((0,), (0,)), ((), ()))
NT = (((1,), (1,)), ((), ()))

assert N_EXPERTS * (SEG - 1) <= NR - TOP_K * TM


def _split(x):
    hi = x.astype(BF16)
    return hi, (x - hi.astype(F32)).astype(BF16)


def _cparams(sem, vmem=None):
    return pltpu.CompilerParams(dimension_semantics=sem, vmem_limit_bytes=vmem)


def _mod_kernel(ct_ref, w_ref, b_ref, o_ref, *, n_cond):
    ct = ct_ref[...]
    s = ct * jax.nn.sigmoid(ct)
    w = w_ref[...]
    rows = [jnp.sum(s[:, r:r + 1] * w, axis=0, keepdims=True) + b_ref[...] for r in range(n_cond)]
    rows.append(jnp.zeros((8 - n_cond, w.shape[1]), F32))
    o_ref[...] = jnp.concatenate(rows, axis=0)


def _modulation(cond_t, ada_w, ada_b, n_cond):
    d, n = ada_w.shape
    nb = 768
    return pl.pallas_call(
        functools.partial(_mod_kernel, n_cond=n_cond),
        grid=(n // nb,),
        in_specs=[pl.BlockSpec((d, 8), lambda i: (0, 0)),
                  pl.BlockSpec((d, nb), lambda i: (0, i)),
                  pl.BlockSpec((1, nb), lambda i: (0, i))],
        out_specs=pl.BlockSpec((8, nb), lambda i: (0, i)),
        out_shape=jax.ShapeDtypeStruct((8, n), F32),
        compiler_params=_cparams(("arbitrary",)),
        name="mod",
    )(cond_t, ada_w, ada_b.reshape(1, n))


def _rms(x, g):
    return x * lax.rsqrt(jnp.mean(x * x, axis=-1, keepdims=True) + EPS) * g


def _inproj_kernel(xp_ref, xs_ref, mod_ref, g1_ref, wu_ref, wqk_ref, wv_ref, wg_ref, wlow_ref,
                   gwh_ref, gwl_ref, gb_ref, u_ref, qk_ref, v_ref, g_ref, la_ref, *, n_ctx_tiles):
    i = pl.program_id(0)
    shift = mod_ref[0, :, 0:D_MODEL]
    scale = mod_ref[0, :, D_MODEL:2 * D_MODEL]
    dot = functools.partial(jnp.dot, preferred_element_type=F32)
    rows = u_ref.shape[0]
    sub = min(rows, TM)
    def block(r0):
        rs = slice(r0, r0 + sub)
        x = jnp.where(i < n_ctx_tiles, xp_ref[rs, :], xs_ref[rs, :])
        h = (_rms(x, g1_ref[...]) * (1.0 + scale) + shift).astype(BF16)
        yield
        u_ref[rs, :] = dot(h, wu_ref[...]).astype(BF16)
        yield
        qk_ref[rs, :] = dot(h, wqk_ref[...]).astype(BF16)
        v_ref[rs, :] = dot(h, wv_ref[...]).astype(BF16)
        yield
        g_ref[rs, :] = dot(h, wg_ref[...]).astype(BF16)
        low = dot(h, wlow_ref[...])
        yield
        low_hi, low_lo = _split(low)
        z = (dot(low_hi, gwh_ref[...]) + dot(low_lo, gwh_ref[...]) + dot(low_hi, gwl_ref[...])
             + gb_ref[...])
        la_ref[rs, :] = (jnp.minimum(z, 0.0) - jnp.log1p(jnp.exp(-jnp.abs(z)))) * (1.0 / GATE_TEMP)

    blocks = [block(r0) for r0 in range(0, rows, sub)]
    while blocks:
        blocks = [gen for gen in blocks if next(gen, "done") != "done"]


def _inproj(xp, xs, mod3, g1, w_in, gate_w, gate_b, lat_len):
    ti = next(t for t in (1024, 512, TM) if xp.shape[0] % t == 0 and lat_len % t == 0)
    lat_tiles_per_seq = lat_len // ti
    n_c, n_l = xp.shape[0] // ti, xs.shape[0] // ti
    t_all = xp.shape[0] + xs.shape[0]
    sp = np.cumsum([0, C_CONV, C_CONV, DK_TOT, DK_TOT, DV_TOT, DV_TOT, 2 * GATE_RANK])
    wu, wqk, wv, wg, wlow = (w_in[:, a:b].astype(BF16) for a, b in
                             ((sp[0], sp[2]), (sp[2], sp[4]), (sp[4], sp[5]), (sp[5], sp[6]), (sp[6], sp[7])))
    gw = jnp.zeros((2 * GATE_RANK, 2 * DK_TOT), F32)
    gw = gw.at[:GATE_RANK, :DK_TOT].set(gate_w[0]).at[GATE_RANK:, DK_TOT:].set(gate_w[1])
    gwh, gwl = _split(gw)
    gb = gate_b.reshape(1, 2 * DK_TOT)
    const = lambda a: pl.BlockSpec(a.shape, lambda i: (0,) * a.ndim)
    row = lambda w: pl.BlockSpec((ti, w), lambda i: (i, 0))
    seg = lambda i: jnp.where(i < n_c, 0, 1 + jnp.maximum(i - n_c, 0) // lat_tiles_per_seq)
    return pl.pallas_call(
        functools.partial(_inproj_kernel, n_ctx_tiles=n_c),
        grid=(n_c + n_l,),
        in_specs=[pl.BlockSpec((ti, D_MODEL), lambda i: (jnp.minimum(i, n_c - 1), 0)),
                  pl.BlockSpec((ti, D_MODEL), lambda i: (jnp.maximum(i - n_c, 0), 0)),
                  pl.BlockSpec((1, 1, 6 * D_MODEL), lambda i: (seg(i), 0, 0)),
                  const(g1), const(wu), const(wqk), const(wv), const(wg), const(wlow),
                  const(gwh), const(gwl), const(gb)],
        out_specs=[row(2 * C_CONV), row(2 * DK_TOT), row(DV_TOT), row(DV_TOT), row(2 * DK_TOT)],
        out_shape=[jax.ShapeDtypeStruct((t_all, 2 * C_CONV), BF16),
                   jax.ShapeDtypeStruct((t_all, 2 * DK_TOT), BF16),
                   jax.ShapeDtypeStruct((t_all, DV_TOT), BF16),
                   jax.ShapeDtypeStruct((t_all, DV_TOT), BF16),
                   jax.ShapeDtypeStruct((t_all, 2 * DK_TOT), F32)],
        compiler_params=_cparams(("arbitrary",), VMEM_LIMIT),
        name="inproj",
    )(xp, xs, mod3, g1, wu, wqk, wv, wg, wlow, gwh, gwl, gb)


def _gla_direction(qk_ref, v_ref, la_ref, tri_ref, blk_ref, o_ref, s_scr, reverse):
    la_hi, la_lo = _split(la_ref[...])
    tri = tri_ref[...]
    bcum = (jnp.dot(tri, la_hi, preferred_element_type=F32)
            + jnp.dot(tri, la_lo, preferred_element_type=F32))
    blk = blk_ref[...]
    bl_cols = (lax.dot_general(la_hi, blk, TN, preferred_element_type=F32)
               + lax.dot_general(la_lo, blk, TN, preferred_element_type=F32))
    yield
    q = qk_ref[:, 0:DK_TOT].astype(F32)
    k = qk_ref[:, DK_TOT:2 * DK_TOT].astype(F32)
    lane = lax.broadcasted_iota(jnp.int32, (CHUNK, LANES), 1)
    row2 = lax.broadcasted_iota(jnp.int32, (2 * CHUNK, CHUNK), 0) % CHUNK
    col2 = lax.broadcasted_iota(jnp.int32, (2 * CHUNK, CHUNK), 1)
    keep = (col2 >= row2) if reverse else (col2 <= row2)
    srow = lax.broadcasted_iota(jnp.int32, (LANES, LANES), 0)
    n_chunks = TM // CHUNK
    order = range(n_chunks - 1, -1, -1) if reverse else range(n_chunks)
    state = [s_scr[0], s_scr[1]]
    for c in order:
        r0 = c * CHUNK
        bc = bcum[r0:r0 + CHUNK]
        bl = bc[0:1] if reverse else bc[CHUNK - 1:CHUNK]
        qt = q[r0:r0 + CHUNK] * jnp.exp(bc) * (DK_HEAD ** -0.5)
        kt = k[r0:r0 + CHUNK] * jnp.exp(-bc)
        ke = k[r0:r0 + CHUNK] * jnp.exp(bl - bc)
        yield
        for p in range(2):
            cs = slice(p * LANES, (p + 1) * LANES)
            qt_p = qt[:, cs]
            qs = jnp.concatenate([jnp.where(lane < DK_HEAD, qt_p, 0.0),
                                  jnp.where(lane >= DK_HEAD, qt_p, 0.0)], axis=0).astype(BF16)
            att = lax.dot_general(qs, kt[:, cs].astype(BF16), NT, preferred_element_type=F32)
            att = jnp.where(keep, att, 0.0).astype(BF16)
            s_p = state[p]
            o_inter = jnp.dot(qs, s_p.astype(BF16), preferred_element_type=F32)
            ke_t = jnp.transpose(ke[:, cs]).astype(BF16)
            upd = []
            for hh in range(2):
                h = 2 * p + hh
                v_h = v_ref[r0:r0 + CHUNK, h * DV_HEAD:(h + 1) * DV_HEAD]
                both = jnp.dot(jnp.concatenate([att[hh * CHUNK:(hh + 1) * CHUNK], ke_t], axis=0), v_h,
                               preferred_element_type=F32)
                o_ref[r0:r0 + CHUNK, h * DV_HEAD:(h + 1) * DV_HEAD] = (
                    both[0:CHUNK] + o_inter[hh * CHUNK:(hh + 1) * CHUNK])
                upd.append(both[CHUNK:CHUNK + LANES])
            bl_col = bl_cols[p * LANES:(p + 1) * LANES, c * LANES:(c + 1) * LANES]
            state[p] = jnp.exp(bl_col) * s_p + jnp.where(srow < DK_HEAD, upd[0], upd[1])
            yield
    s_scr[0] = state[0]
    s_scr[1] = state[1]


def _gla_kernel(*refs, n_tiles, nb, zero_init):
    seqs = [refs[6 * r:6 * r + 6] for r in range(nb)]
    tril_ref, triu_ref, blk_ref = refs[6 * nb:6 * nb + 3]
    s0_ref = None if zero_init else refs[6 * nb + 3]
    of_ref, ob_ref, sout_ref, sf_scr, sb_scr = refs[-5:]
    j = pl.program_id(1)

    @pl.when(j == 0)
    def _():
        sf_scr[...] = jnp.zeros(sf_scr.shape, F32) if zero_init else s0_ref[:, 0]
        sb_scr[...] = jnp.zeros(sb_scr.shape, F32) if zero_init else s0_ref[:, 1]

    scans = []
    for r, (qkf_ref, vf_ref, laf_ref, qkb_ref, vb_ref, lab_ref) in enumerate(seqs):
        scans.append(_gla_direction(qkf_ref, vf_ref, laf_ref, tril_ref, blk_ref, of_ref.at[r], sf_scr.at[r], False))
        scans.append(_gla_direction(qkb_ref, vb_ref, lab_ref, triu_ref, blk_ref, ob_ref.at[r], sb_scr.at[r], True))
    while scans:
        scans = [gen for gen in scans if next(gen, "done") != "done"]

    @pl.when(j == n_tiles - 1)
    def _():
        sout_ref[:, 0] = sf_scr[...]
        sout_ref[:, 1] = sb_scr[...]


def _gla(qk, v, la, s0, n_seq, n_tiles, tile_off):
    nb = next(b for b in (4, 2, 1) if n_seq % b == 0)
    seq_len = n_tiles * TM
    blk = np.arange(TM) // CHUNK
    same = blk[:, None] == blk[None, :]
    r = np.arange(TM)
    tril = jnp.asarray((same & (r[None, :] <= r[:, None])).astype(np.float32)).astype(BF16)
    triu = jnp.asarray((same & (r[None, :] >= r[:, None])).astype(np.float32)).astype(BF16)
    chunk_of_lane = np.arange(TM // CHUNK * LANES) // LANES
    blkm = jnp.asarray((blk[:, None] == chunk_of_lane[None, :]).astype(np.float32)).astype(BF16)
    def tok(w, r, backward, cb=0):
        def index(s, j):
            jj = (n_tiles - 1 - j) if backward else j
            return ((s * nb + r) * n_tiles + jj + tile_off, cb)
        return pl.BlockSpec((TM, w), index)

    const = lambda a: pl.BlockSpec(a.shape, lambda s, j: (0,) * a.ndim)
    st = pl.BlockSpec((nb, 2, 2, LANES, LANES), lambda s, j: (s, 0, 0, 0, 0))
    seq_specs, seq_args = [], []
    for r in range(nb):
        seq_specs += [tok(2 * DK_TOT, r, False), tok(DV_TOT, r, False), tok(DK_TOT, r, False, 0),
                      tok(2 * DK_TOT, r, True), tok(DV_TOT, r, True), tok(DK_TOT, r, True, 1)]
        seq_args += [qk, v, la, qk, v, la]
    state_in = ([], []) if s0 is None else ([st], [s0])
    o_f, o_b, s_out = pl.pallas_call(
        functools.partial(_gla_kernel, n_tiles=n_tiles, nb=nb, zero_init=s0 is None),
        grid=(n_seq // nb, n_tiles),
        in_specs=seq_specs + [const(tril), const(triu), const(blkm)] + state_in[0],
        out_specs=[pl.BlockSpec((nb, TM, DV_TOT), lambda s, j: (s, j, 0)),
                   pl.BlockSpec((nb, TM, DV_TOT), lambda s, j: (s, n_tiles - 1 - j, 0)), st],
        out_shape=[jax.ShapeDtypeStruct((n_seq, seq_len, DV_TOT), F32),
                   jax.ShapeDtypeStruct((n_seq, seq_len, DV_TOT), F32),
                   jax.ShapeDtypeStruct((n_seq, 2, 2, LANES, LANES), F32)],
        scratch_shapes=[pltpu.VMEM((nb, 2, LANES, LANES), F32), pltpu.VMEM((nb, 2, LANES, LANES), F32)],
        compiler_params=_cparams(("arbitrary", "arbitrary"), VMEM_LIMIT),
        name="gla",
    )(*seq_args, tril, triu, blkm, *state_in[1])
    return o_f.reshape(n_seq * seq_len, DV_TOT), o_b.reshape(n_seq * seq_len, DV_TOT), s_out


def _mix_conv(u_ref, cw_ref, pad_scr, shf_scr, cv_scr, seq_len):
    n_seq = TM // seq_len
    pr = seq_len + 2 * CONV_PAD
    qr = seq_len + 3 * SEG

    u = u_ref[...].astype(F32)
    hg = u[:, 0:C_CONV] * jax.nn.sigmoid(u[:, C_CONV:2 * C_CONV])
    zero = jnp.zeros((CONV_PAD, C_CONV), F32)
    for s in range(n_seq):
        pad_scr[s * pr:s * pr + CONV_PAD, :] = zero
        pad_scr[s * pr + CONV_PAD:s * pr + CONV_PAD + seq_len, :] = hg[s * seq_len:(s + 1) * seq_len]
        pad_scr[s * pr + CONV_PAD + seq_len:(s + 1) * pr, :] = zero
        for b in range(1, SEG):
            shf_scr[b - 1, s * qr:(s + 1) * qr, :] = pad_scr[s * pr + b:s * pr + b + qr, :]
    rb = 8 * SEG
    per_seq = seq_len // rb
    for gi in range(C_CONV // LANES):
        gs = slice(gi * LANES, (gi + 1) * LANES)
        wv = [jnp.broadcast_to(cw_ref[tap:tap + 1, gs], (SEG, LANES)) for tap in range(CONV_K)]

        def conv_rows(i, carry, gs=gs, wv=wv):
            s = lax.shift_right_logical(i, per_seq.bit_length() - 1)
            r0 = (i & (per_seq - 1)) * rb
            nj = rb // SEG
            acc = [None] * nj
            for b in range(SEG):
                taps = [(a, SEG * a + b - (CONV_PAD - CONV_K // 2)) for a in range(CONV_PAD * 2 // SEG)]
                taps = [(a, t) for a, t in taps if 0 <= t < CONV_K]
                loaded = {}
                for a, t in taps:
                    for j in range(nj):
                        if a + j not in loaded:
                            if b == 0:
                                row = pl.multiple_of(s * pr + r0 + (a + j) * SEG, SEG)
                                loaded[a + j] = pad_scr[pl.ds(row, SEG), gs]
                            else:
                                row = pl.multiple_of(s * qr + r0 + (a + j) * SEG, SEG)
                                loaded[a + j] = shf_scr[b - 1, pl.ds(row, SEG), gs]
                        term = wv[t] * loaded[a + j]
                        acc[j] = term if acc[j] is None else acc[j] + term
            for j in range(nj):
                cv_scr[pl.ds(pl.multiple_of(s * seq_len + r0 + j * SEG, SEG), SEG), gs] = acc[j]
            return carry

        lax.fori_loop(0, TM // rb, conv_rows, 0)


def _mix_rest(x_ref, g_ref, of_ref, ob_ref, mod_ref, cv_scr, cb_ref, lng_ref, lnb_ref, gng_ref, wout_ref,
              n2g_ref, rwh_ref, rwl_ref, rb_ref, le_ref, ut_ref, x1_ref, grp_ref, route_ref, cnt_ref):
    cv = cv_scr[...] + cb_ref[...]
    mu = jnp.mean(cv, axis=-1, keepdims=True)
    var = jnp.mean(jnp.square(cv - mu), axis=-1, keepdims=True)
    cv =(cv - mu) * lax.rsqrt(var + EPS) * lng_ref[...] + lnb_ref[...]
    conv_out = (cv * jax.nn.sigmoid(cv)).astype(BF16)
    yield

    o = of_ref[...] + ob_ref[...]
    g = g_ref[...].astype(F32)
    heads = []
    for h in range(N_GLA_HEADS):
        cs = slice(h * DV_HEAD, (h + 1) * DV_HEAD)
        oh = o[:, cs]
        oh = oh * lax.rsqrt(jnp.mean(oh * oh, axis=-1, keepdims=True) + EPS) * gng_ref[...]
        gh = g[:, cs]
        heads.append((oh * (gh * jax.nn.sigmoid(gh))).astype(BF16))
    yield
    y = jnp.dot(conv_out, wout_ref[0:C_CONV, :], preferred_element_type=F32)
    for h in range(N_GLA_HEADS):
        y = y + jnp.dot(heads[h], wout_ref[C_CONV + h * DV_HEAD:C_CONV + (h + 1) * DV_HEAD, :],
                        preferred_element_type=F32)
    yield

    gate1 = mod_ref[0, :, 2 * D_MODEL:3 * D_MODEL]
    shift2 = mod_ref[0, :, 3 * D_MODEL:4 * D_MODEL]
    scale2 = mod_ref[0, :, 4 * D_MODEL:5 * D_MODEL]
    x1 = x_ref[...] + gate1 * y
    x1_ref[...] = x1
    h2 = _rms(x1, n2g_ref[...]) * (1.0 + scale2) + shift2
    h2_hi, h2_lo = _split(h2)
    yield

    ntdot = functools.partial(lax.dot_general, dimension_numbers=NT, preferred_element_type=F32)
    l = (ntdot(rwh_ref[...], h2_hi) + ntdot(rwh_ref[...], h2_lo) + ntdot(rwl_ref[...], h2_hi)
         + rb_ref[...])
    yield
    erow = lax.broadcasted_iota(jnp.int32, (N_EXPERTS, TM), 0).astype(F32)
    vals, idxs, hots = [], [], []
    for _ in range(TOP_K):
        m = jnp.max(l, axis=0, keepdims=True)
        idx = jnp.min(jnp.where(l == m, erow, float(N_EXPERTS)), axis=0, keepdims=True)
        hot = erow == idx
        l = jnp.where(hot, -jnp.inf, l)
        vals.append(m)
        idxs.append(idx)
        hots.append(hot)
        yield
    es = [jnp.exp(vk - vals[0]) for vk in vals]
    inv = 1.0 / (es[0] + es[1] + es[2] + es[3])
    sel = jnp.zeros((N_EXPERTS, TM), F32)
    for hot in hots:
        sel = jnp.where(hot, 1.0, sel)

    cnt = jnp.broadcast_to(jnp.sum(sel, axis=1, keepdims=True), (N_EXPERTS, LANES))
    cnt_ref[0] = cnt
    strip = jnp.ceil(cnt * (1.0 / SEG)) * float(SEG)
    starts = jnp.dot(le_ref[...], strip, precision=HI, preferred_element_type=F32)[:, 0:1]
    slot_of = jnp.dot(sel.astype(BF16), ut_ref[...], preferred_element_type=F32) + starts
    slots = [jnp.sum(jnp.where(hot, slot_of, 0.0), axis=0, keepdims=True) for hot in hots]

    rows = slots + idxs + [e * inv for e in es]
    srow = lax.broadcasted_iota(jnp.int32, (16, TM), 0)
    packed = jnp.zeros((16, TM), F32)
    for j, v in enumerate(rows):
        packed = jnp.where(srow == j, v, packed)
    packed = jnp.concatenate([packed, jnp.zeros((LANES - 16, TM), F32)], axis=0)
    route_ref[...] = jnp.transpose(packed)
    yield

    r = lax.broadcasted_iota(jnp.int32, (NR, TM), 0).astype(F32)
    perm = jnp.zeros((NR, TM), F32)
    for slot in slots:
        perm = jnp.where(slot == r, 1.0, perm)
    yield
    grp_ref[...] = jnp.dot(perm.astype(BF16), h2_hi, preferred_element_type=F32)


def _mix_kernel(*refs, seq_lens):
    k = len(seq_lens)
    ins = [refs[6 * p:6 * p + 6] for p in range(k)]
    cw_ref, *shared = refs[6 * k:6 * k + 12]
    outs = [refs[6 * k + 12 + 4 * p:6 * k + 16 + 4 * p] for p in range(k)]
    scr = [refs[10 * k + 12 + 3 * p:10 * k + 15 + 3 * p] for p in range(k)]
    for p in range(k):
        _mix_conv(ins[p][1], cw_ref, *scr[p], seq_lens[p])
    stages = []
    for p in range(k):
        x_ref, _, g_ref, of_ref, ob_ref, mod_ref = ins[p]
        stages.append(_mix_rest(x_ref, g_ref, of_ref, ob_ref, mod_ref, scr[p][2], *shared, *outs[p]))
    while stages:
        stages = [gen for gen in stages if next(gen, "done") != "done"]


def _mix(paths, u, g, mod3, wts):
    (conv_w, conv_b, ln_g, ln_b, gng, wout, n2g, rwh, rwl, rb) = wts
    le = jnp.asarray(np.tril(np.ones((N_EXPERTS, N_EXPERTS), np.float32), -1))
    ut = jnp.asarray(np.triu(np.ones((TM, TM), np.float32), 1)).astype(BF16)
    shared = (conv_w, conv_b, ln_g, ln_b, gng, wout, n2g, rwh, rwl, rb, le, ut)
    const = lambda a: pl.BlockSpec(a.shape, lambda i: (0,) * a.ndim)
    n_steps = max(x.shape[0] // TM for x, *_ in paths)
    in_specs, args, out_specs, out_shape, scratch = [], [], [], [], []
    for x, o_f, o_b, seg_fn, tile_off, seq_len in paths:
        n = x.shape[0] // TM
        cur = lambda i, n=n: jnp.minimum(i, n - 1)
        loc = lambda w, cur=cur: pl.BlockSpec((TM, w), lambda i: (cur(i), 0))
        uni = lambda w, cur=cur, off=tile_off: pl.BlockSpec((TM, w), lambda i: (cur(i) + off, 0))
        in_specs += [loc(D_MODEL), uni(2 * C_CONV), uni(DV_TOT), loc(DV_TOT), loc(DV_TOT),
                     pl.BlockSpec((1, 1, 6 * D_MODEL), lambda i, cur=cur, f=seg_fn: (f(cur(i)), 0, 0))]
        args += [x, u, g, o_f, o_b, mod3]
        out_specs += [loc(D_MODEL), pl.BlockSpec((NR, D_MODEL), lambda i, cur=cur: (cur(i), 0)), loc(LANES),
                      pl.BlockSpec((1, N_EXPERTS, LANES), lambda i, cur=cur: (cur(i), 0, 0))]
        out_shape += [jax.ShapeDtypeStruct((n * TM, D_MODEL), F32),
                      jax.ShapeDtypeStruct((n * NR, D_MODEL), F32),
                      jax.ShapeDtypeStruct((n * TM, LANES), F32),
                      jax.ShapeDtypeStruct((n, N_EXPERTS, LANES), F32)]
        n_seq = TM // seq_len
        scratch += [pltpu.VMEM((n_seq * (seq_len + 2 * CONV_PAD), C_CONV), F32),
                    pltpu.VMEM((SEG - 1, n_seq * (seq_len + 3 * SEG), C_CONV), F32),
                    pltpu.VMEM((TM, C_CONV), F32)]
    outs = pl.pallas_call(
        functools.partial(_mix_kernel, seq_lens=tuple(p[5] for p in paths)),
        grid=(n_steps,),
        in_specs=in_specs + [const(a) for a in shared],
        out_specs=out_specs,
        out_shape=out_shape,
        scratch_shapes=scratch,
        compiler_params=_cparams(("arbitrary",), VMEM_LIMIT),
        name="mix",
    )(*args, *shared)
    return [outs[4 * p:4 * p + 4] for p in range(len(paths))]


def _strip_copies(tile, ls_ref, gs_ref, sg_ref, local, rows_hbm, sem, to_hbm, wait):
    def body(e, carry):
        j = tile * N_EXPERTS + e
        n = pl.multiple_of(sg_ref[j], SEG)

        @pl.when(n > 0)
        def _():
            loc = local.at[pl.ds(pl.multiple_of(ls_ref[j], SEG), n)]
            glob = rows_hbm.at[pl.ds(pl.multiple_of(gs_ref[j], SEG), n)]
            cp = pltpu.make_async_copy(loc, glob, sem) if to_hbm else pltpu.make_async_copy(glob, loc, sem)
            if wait:
                cp.wait()
            else:
                cp.start()
        return carry

    lax.fori_loop(0, N_EXPERTS, body, 0)


def _expert_kernel(te_ref, nv_ref, first_ref, next_ref, rows_ref, ptr_ref, mid_ref, src_ref, dst_ref,
                   len_ref, csrc_ref, clen_ref, grpc_hbm, grpl_hbm, wgu_hbm, bgu_ref, wdn_hbm, bdn_ref,
                   y_ref, xbuf, wgu_f, wdn_f, wgu_s, wdn_s, xsem, sem, *, ctx_rows):
    i = pl.program_id(0)

    def weight_copies(e):
        return (pltpu.make_async_copy(wgu_hbm.at[e], wgu_f, sem.at[0]),
                pltpu.make_async_copy(wdn_hbm.at[e], wdn_f, sem.at[1]))

    def row_copies(tile, xs, wait):
        def strip(hbm, src, dst, n):
            cp = pltpu.make_async_copy(hbm.at[pl.ds(pl.multiple_of(src, SEG), n)],
                                       xbuf.at[xs, pl.ds(pl.multiple_of(dst, SEG), n)], xsem.at[xs])
            cp.wait() if wait else cp.start()

        def strips_of(hbm, base):
            def body(j, carry):
                n = pl.multiple_of(len_ref[j], SEG)

                @pl.when(n > 0)
                def _():
                    strip(hbm, src_ref[j] - base, dst_ref[j], n)
                return carry
            return body

        lax.fori_loop(ptr_ref[tile], mid_ref[tile], strips_of(grpc_hbm, 0), 0)
        lax.fori_loop(mid_ref[tile], ptr_ref[tile + 1], strips_of(grpl_hbm, ctx_rows), 0)
        n = pl.multiple_of(clen_ref[tile], SEG)
        src = csrc_ref[tile]

        @pl.when(jnp.logical_and(n > 0, src < ctx_rows))
        def _():
            strip(grpc_hbm, src, 0, n)

        @pl.when(jnp.logical_and(n > 0, src >= ctx_rows))
        def _():
            strip(grpl_hbm, src - ctx_rows, 0, n)

    @pl.when(i == 0)
    def _():
        xbuf[...] = jnp.zeros(xbuf.shape, F32)
        row_copies(i, 0, False)

    @pl.when(i + 1 < nv_ref[0])
    def _():
        row_copies(i + 1, (i + 1) % 2, False)

    @pl.when(i < nv_ref[0])
    def _():
        e = te_ref[i]
        row_copies(i, i % 2, True)

        @pl.when(first_ref[i] == 1)
        def _():
            @pl.when(i == 0)
            def _():
                for cp in weight_copies(e):
                    cp.start()

            for cp in weight_copies(e):
                cp.wait()
            rows = 64

            def cast(r, carry):
                sl = pl.ds(pl.multiple_of(r * rows, rows), rows)
                wgu_s[sl, :] = wgu_f[sl, :].astype(BF16)
                wdn_s[sl, :] = wdn_f[sl, :].astype(BF16)
                return carry

            lax.fori_loop(0, D_MODEL // rows, cast, 0)
            nxt = next_ref[i]

            @pl.when(nxt >= 0)
            def _():
                for cp in weight_copies(nxt):
                    cp.start()

        def compute(m):
            x = xbuf[i % 2, 0:m, :].astype(BF16)
            gu = jnp.dot(x, wgu_s[...], preferred_element_type=F32) + bgu_ref[0]
            gate = jnp.minimum(gu[:, 0:D_FF], SWIGLU_LIMIT)
            up = jnp.clip(gu[:, D_FF:2 * D_FF], -SWIGLU_LIMIT, SWIGLU_LIMIT)
            act = (gate * jax.nn.sigmoid(SWIGLU_ALPHA * gate) * (up + 1.0)).astype(BF16)
            y_ref[0:m, :] = jnp.dot(act, wdn_s[...], preferred_element_type=F32) + bdn_ref[0]
            if m < TE:
                y_ref[m:TE, :] = jnp.zeros((TE - m, D_MODEL), F32)

        for m in range(TE // 4, TE + 1, TE // 4):
            pl.when(rows_ref[i] == m)(functools.partial(compute, m))

    @pl.when(i >= nv_ref[0])
    def _():
        y_ref[...] = jnp.zeros(y_ref.shape, y_ref.dtype)


def _experts(plan, grp_c, grp_l, n_rows, w_gu, b_gu, w_dn, b_dn):
    nt = n_rows // TE
    exp3 = lambda i, te, *_: (te[i], 0, 0)
    any_spec = pl.BlockSpec(memory_space=pl.ANY)
    return pl.pallas_call(
        functools.partial(_expert_kernel, ctx_rows=grp_c.shape[0]),
        grid_spec=pltpu.PrefetchScalarGridSpec(
            num_scalar_prefetch=12,
            grid=(nt,),
            in_specs=[any_spec, any_spec, any_spec,
                      pl.BlockSpec((1, 1, 2 * D_FF), exp3), any_spec,
                      pl.BlockSpec((1, 1, D_MODEL), exp3)],
            out_specs=pl.BlockSpec((TE, D_MODEL), lambda i, *_: (i, 0)),
            scratch_shapes=[pltpu.VMEM((2, TE, D_MODEL), F32),
                            pltpu.VMEM((D_MODEL, 2 * D_FF), F32), pltpu.VMEM((D_FF, D_MODEL), F32),
                            pltpu.VMEM((D_MODEL, 2 * D_FF), BF16), pltpu.VMEM((D_FF, D_MODEL), BF16),
                            pltpu.SemaphoreType.DMA((2,)), pltpu.SemaphoreType.DMA((2,))]),
        out_shape=jax.ShapeDtypeStruct((n_rows, D_MODEL), F32),
        compiler_params=_cparams(("arbitrary",), VMEM_LIMIT),
        name="experts",
    )(plan["tile_expert"], plan["n_valid"], plan["tile_first"], plan["tile_next"], plan["tile_rows"], plan["seg_ptr"], plan["seg_mid"], plan["seg_src"], plan["seg_dst"], plan["seg_len"],
      plan["carry_src"], plan["carry_len"], grp_c, grp_l, w_gu, b_gu.reshape(N_EXPERTS, 1, 2 * D_FF), w_dn, b_dn.reshape(N_EXPERTS, 1, D_MODEL))


def _final_kernel(ls_ref, gs_ref, sg_ref, x1_ref, route_ref, mod_ref, fg_ref, ys_ref, y_ref,
                  ybuf, sem, *, tile_off, n_tiles):
    i = pl.program_id(0)
    slot = i % FINAL_RING
    ahead = FINAL_RING - 1

    def fetch(step, sl, wait):
        _strip_copies(step + tile_off, ls_ref, gs_ref, sg_ref, ybuf.at[sl], ys_ref, sem.at[sl], False, wait)

    @pl.when(i == 0)
    def _():
        ybuf[...] = jnp.zeros(ybuf.shape, F32)
        for a in range(min(ahead, n_tiles)):
            fetch(a, a, False)

    @pl.when(i + ahead < n_tiles)
    def _():
        fetch(i + ahead, (i + ahead) % FINAL_RING, False)

    fetch(i, slot, True)

    route = route_ref[...]
    r = lax.broadcasted_iota(jnp.int32, (TM, NR), 1).astype(F32)
    comb = jnp.zeros((TM, NR), F32)
    for kk in range(TOP_K):
        gate = route[:, 2 * TOP_K + kk:2 * TOP_K + kk + 1]
        comb = jnp.where(route[:, kk:kk + 1] == r, gate, comb)
    moe = jnp.dot(comb.astype(BF16), ybuf[slot].astype(BF16), preferred_element_type=F32)
    gate2 = mod_ref[0, :, 5 * D_MODEL:6 * D_MODEL]
    y_ref[...] = _rms(x1_ref[...] + gate2 * moe, fg_ref[...])


def _final(plan, x1, route, mod3, seg_fn, tile_off, fg, ys):
    t_path = x1.shape[0]
    n = t_path // TM
    loc = lambda w: pl.BlockSpec((TM, w), lambda i, *_: (i, 0))
    return pl.pallas_call(
        functools.partial(_final_kernel, tile_off=tile_off, n_tiles=n),
        grid_spec=pltpu.PrefetchScalarGridSpec(
            num_scalar_prefetch=3,
            grid=(n,),
            in_specs=[loc(D_MODEL), loc(LANES),
                      pl.BlockSpec((1, 1, 6 * D_MODEL), lambda i, *_: (seg_fn(i), 0, 0)),
                      pl.BlockSpec((1, D_MODEL), lambda i, *_: (0, 0)),
                      pl.BlockSpec(memory_space=pl.ANY)],
            out_specs=loc(D_MODEL),
            scratch_shapes=[pltpu.VMEM((FINAL_RING, NR, D_MODEL), F32),
                            pltpu.SemaphoreType.DMA((FINAL_RING,))]),
        out_shape=jax.ShapeDtypeStruct((t_path, D_MODEL), F32),
        compiler_params=_cparams(("arbitrary",), VMEM_LIMIT),
        name="final",
    )(plan["lstart"], plan["gstart"], plan["strip"], x1, route, mod3, fg, ys)


def _plan(tile_counts, n_rows, n_ctx):
    nt = n_rows // TE
    n_tok = tile_counts.shape[0]
    strip = (tile_counts + SEG - 1) // SEG * SEG
    lstart = jnp.cumsum(strip, axis=1) - strip
    rows_e = jnp.sum(strip, axis=0)
    rpad = (rows_e + TE - 1) // TE * TE
    ends = jnp.cumsum(rpad)
    offs = ends - rpad
    gstart = offs[None, :] + jnp.cumsum(strip, axis=0) - strip
    n_valid = (ends[-1] // TE).astype(jnp.int32)
    tile_ids = jnp.minimum(jnp.arange(nt, dtype=jnp.int32), n_valid - 1)
    tile_expert = jnp.sum((ends[None, :] <= (tile_ids * TE)[:, None]).astype(jnp.int32), axis=1)
    tile_expert = jnp.minimum(tile_expert, N_EXPERTS - 1).astype(jnp.int32)
    prev = jnp.concatenate([jnp.full((1,), -1, jnp.int32), tile_expert[:-1]])
    tile_first = (tile_expert != prev).astype(jnp.int32)
    pick = lambda table, idx: jnp.sum(
        jnp.where(idx[:, None] == jnp.arange(table.shape[0], dtype=jnp.int32)[None, :], table[None, :], 0), axis=1)
    used = jnp.clip(pick(offs + rows_e, tile_expert) - jnp.arange(nt, dtype=jnp.int32) * TE, 0, TE)
    tile_rows = jnp.maximum((used + TE // 4 - 1) // (TE // 4), 1) * (TE // 4)
    after = pick(ends, tile_expert) // TE
    tile_next = jnp.where(after < n_valid, pick(tile_expert, jnp.minimum(after, nt - 1)), -1)
    tok = jnp.broadcast_to(jnp.arange(n_tok, dtype=jnp.int32)[:, None], strip.shape)
    g0 = gstart.T.reshape(-1)
    ln = strip.T.reshape(-1)
    src0 = (tok * NR + lstart).T.reshape(-1)
    is_ctx = (tok < n_ctx).T.reshape(-1)
    len1 = jnp.minimum(ln, TE - g0 % TE)
    tile0 = g0 // TE
    tiles = jnp.arange(nt + 1, dtype=jnp.int32)[:, None]
    count = lambda m: jnp.sum(m.astype(jnp.int32), axis=1)
    seg_ptr = count(tile0[None, :] < tiles)
    seg_mid = seg_ptr + count((tile0[None, :] == tiles) & is_ctx[None, :])
    into = ((g0 + len1) // TE)[None, :] == tiles[:nt]
    carry_len = jnp.sum(jnp.where(into, (ln - len1)[None, :], 0), axis=1)
    carry_src = jnp.sum(jnp.where(into & (ln > len1)[None, :], (src0 + len1)[None, :], 0), axis=1)
    i32 = lambda a: a.reshape(-1).astype(jnp.int32)
    return dict(strip=i32(strip), lstart=i32(lstart), gstart=i32(gstart), n_valid=n_valid.reshape(1),
                tile_expert=tile_expert, tile_first=i32(tile_first), tile_next=i32(tile_next),
                tile_rows=i32(tile_rows), seg_ptr=i32(seg_ptr), seg_mid=i32(seg_mid), seg_src=i32(src0),
                seg_dst=i32(g0 % TE), seg_len=i32(len1), carry_src=i32(carry_src),
                carry_len=i32(carry_len))


def _layer(x_prompt, x_sample, state, c, c_ctx, ada_w, ada_b, norm1_g, norm2_g, w_in, conv_w, conv_b,
           conv_ln_g, conv_ln_b, gate_w, gate_b, gla_norm_g, w_out, router_w, router_b,
           moe_w_gu, moe_b_gu, moe_w_dn, moe_b_dn, final_g):
    bp, lp, d = x_prompt.shape
    bs, ls, _ = x_sample.shape
    assert lp == TM and ls % TM == 0 and d == D_MODEL
    xp = x_prompt.reshape(bp * lp, d)
    xs = x_sample.reshape(bs * ls, d)
    n_c, n_l = bp, bs * ls // TM
    lat_tiles = ls // TM
    t_all = (n_c + n_l) * TM

    n_cond = 1 + bs
    cond_t = jnp.concatenate([c_ctx[None, :], c, jnp.zeros((8 - n_cond, d), F32)], axis=0).T
    mod3 = _modulation(cond_t, ada_w, ada_b, n_cond).reshape(8, 1, 6 * d)

    row = lambda a: a.reshape(1, -1)
    u, qk, v, g, la = _inproj(xp, xs, mod3, row(norm1_g), w_in, gate_w, gate_b, ls)

    pair = lambda s: s.reshape(s.shape[0], 2, 2, LANES, LANES)
    of_c, ob_c, s_ctx = _gla(qk, v, la, None, bp, 1, 0)
    of_l, ob_l, _ = _gla(qk, v, la, pair(state), bs, lat_tiles, n_c)

    cw = jnp.zeros((32, C_CONV), F32).at[:CONV_K].set(conv_w)
    rwh, rwl = _split(router_w.T)
    rb = jnp.broadcast_to(router_b[:, None], (N_EXPERTS, TM))
    wts = (cw, row(conv_b), row(conv_ln_g), row(conv_ln_b), row(gla_norm_g), w_out.astype(BF16),
           row(norm2_g), rwh, rwl, rb)
    seg_c = lambda i: 0
    seg_l = lambda i: 1 + i // lat_tiles
    (x1_c, grp_c, route_c, cnt_c), (x1_l, grp_l, route_l, cnt_l) = _mix(
        [(xp, of_c, ob_c, seg_c, 0, lp), (xs, of_l, ob_l, seg_l, n_c, GRID_W)], u, g, mod3, wts)

    tile_counts = jnp.concatenate([cnt_c[:, :, 0], cnt_l[:, :, 0]]).astype(jnp.int32)
    n_rows = TOP_K * t_all + (SEG - 1) * N_EXPERTS * (n_c + n_l) + N_EXPERTS * TE
    n_rows = (n_rows + TE - 1) // TE * TE
    plan = _plan(tile_counts, n_rows, n_c)
    ysrt = _experts(plan, grp_c, grp_l, n_rows, moe_w_gu, moe_b_gu, moe_w_dn, moe_b_dn)
    y_c = _final(plan, x1_c, route_c, mod3, seg_c, 0, row(final_g), ysrt)
    y_l = _final(plan, x1_l, route_l, mod3, seg_l, n_c, row(final_g), ysrt)
    new_state = s_ctx.reshape(bp, 1, 2, N_GLA_HEADS, DK_HEAD, DV_HEAD)
    return y_c.reshape(bp, lp, d), y_l.reshape(bs, ls, d), new_state


def kernel(x_prompt, x_sample, state_gla, c, c_ctx, ada_w, ada_b, norm1_g, norm2_g, w_in, conv_w,
           conv_b, conv_ln_g, conv_ln_b, gate_w, gate_b, gla_norm_g, w_out, router_w, router_b,
           moe_w_gu, moe_b_gu, moe_w_dn, moe_b_dn, final_g):
    assert ada_w.shape[0] == 1, "single-layer step"
    return _layer(x_prompt, x_sample, state_gla[:, 0], c, c_ctx, ada_w[0], ada_b[0], norm1_g[0],
                  norm2_g[0], w_in[0], conv_w[0], conv_b[0], conv_ln_g[0], conv_ln_b[0], gate_w[0],
                  gate_b[0], gla_norm_g[0], w_out[0], router_w[0], router_b[0], moe_w_gu[0],
                  moe_b_gu[0], moe_w_dn[0], moe_b_dn[0], final_g)
```

```python
import functools

import numpy as np
import jax
import jax.numpy as jnp
from jax import lax
from jax.experimental import pallas as pl
from jax.experimental.pallas import tpu as pltpu

D_MODEL = 1024
GRID_W = 64
C_CONV = D_MODEL // 2
CONV_K = 31
N_GLA_HEADS = 4
DV_HEAD = 128
DK_HEAD = 64
DK_TOT = DK_HEAD * N_GLA_HEADS
DV_TOT = DV_HEAD * N_GLA_HEADS
GATE_RANK = 16
GATE_TEMP = 16.0
CHUNK = 64
N_EXPERTS = 32
TOP_K = 4
D_FF = D_MODEL
SWIGLU_LIMIT = 7.0
SWIGLU_ALPHA = 1.702
EPS = 1e-6

LANES = 128
SEG = 8
TM = 256
TE = 1024
NR = TOP_K * TM + TM
CONV_PAD = 16
FINAL_RING = 3
X_RING = 3
V7X_VMEM_BYTES = 64 * 1024 * 1024
VMEM_LIMIT = V7X_VMEM_BYTES - 8 * 1024 * 1024

F32 = jnp.float32
BF16 = jnp.bfloat16
HI = lax.Precision.HIGHEST
TN =(((0,), (0,)), ((), ()))
NT = (((1,), (1,)), ((), ()))

assert N_EXPERTS * (SEG - 1) <= NR - TOP_K * TM


def _split(x):
    hi = x.astype(BF16)
    return hi, (x - hi.astype(F32)).astype(BF16)


def _cparams(sem, vmem=None):
    return pltpu.CompilerParams(dimension_semantics=sem, vmem_limit_bytes=vmem)


def _mod_kernel(ct_ref, w_ref, b_ref, o_ref, *, n_cond):
    ct = ct_ref[...]
    s = ct * jax.nn.sigmoid(ct)
    w = w_ref[...]
    rows = [jnp.sum(s[:, r:r + 1] * w, axis=0, keepdims=True) + b_ref[...] for r in range(n_cond)]
    rows.append(jnp.zeros((8 - n_cond, w.shape[1]), F32))
    o_ref[...] = jnp.concatenate(rows, axis=0)


def _modulation(cond_t, ada_w, ada_b, n_cond):
    d, n = ada_w.shape
    nb = 768
    return pl.pallas_call(
        functools.partial(_mod_kernel, n_cond=n_cond),
        grid=(n // nb,),
        in_specs=[pl.BlockSpec((d, 8), lambda i: (0, 0)),
                  pl.BlockSpec((d, nb), lambda i: (0, i)),
                  pl.BlockSpec((1, nb), lambda i: (0, i))],
        out_specs=pl.BlockSpec((8, nb), lambda i: (0, i)),
        out_shape=jax.ShapeDtypeStruct((8, n), F32),
        compiler_params=_cparams(("arbitrary",)),
        name="mod",
    )(cond_t, ada_w, ada_b.reshape(1, n))


def _rms(x, g):
    return x * lax.rsqrt(jnp.mean(x * x, axis=-1, keepdims=True) + EPS) * g


def _inproj_kernel(xp_ref, xs_ref, mod_ref, g1_ref, wu_ref, wqk_ref, wv_ref, wg_ref, wlow_ref,
                   gwh_ref, gwl_ref, gb_ref, u_ref, qk_ref, v_ref, g_ref, la_ref, *, n_ctx_tiles):
    i = pl.program_id(0)
    shift = mod_ref[0, :, 0:D_MODEL]
    scale = mod_ref[0, :, D_MODEL:2 * D_MODEL]
    dot = functools.partial(jnp.dot, preferred_element_type=F32)
    rows = u_ref.shape[0]
    sub = min(rows, TM)
    def block(r0):
        rs = slice(r0, r0 + sub)
        x = jnp.where(i < n_ctx_tiles, xp_ref[rs, :], xs_ref[rs, :])
        h = (_rms(x, g1_ref[...]) * (1.0 + scale) + shift).astype(BF16)
        yield
        u_ref[rs, :] = dot(h, wu_ref[...]).astype(BF16)
        yield
        qk_ref[rs, :] = dot(h, wqk_ref[...]).astype(BF16)
        v_ref[rs, :] = dot(h, wv_ref[...]).astype(BF16)
        yield
        g_ref[rs, :] = dot(h, wg_ref[...]).astype(BF16)
        low = dot(h, wlow_ref[...])
        yield
        low_hi, low_lo = _split(low)
        z = (dot(low_hi, gwh_ref[...]) + dot(low_lo, gwh_ref[...]) + dot(low_hi, gwl_ref[...])
             + gb_ref[...])
        la_ref[rs, :] = (jnp.minimum(z, 0.0) - jnp.log1p(jnp.exp(-jnp.abs(z)))) * (1.0 / GATE_TEMP)

    blocks = [block(r0) for r0 in range(0, rows, sub)]
    while blocks:
        blocks = [gen for gen in blocks if next(gen, "done") != "done"]


def _inproj(xp, xs, mod3, g1, w_in, gate_w, gate_b, lat_len):
    ti = next(t for t in (1024, 512, TM) if xp.shape[0] % t == 0 and lat_len % t == 0)
    lat_tiles_per_seq = lat_len // ti
    n_c, n_l = xp.shape[0] // ti, xs.shape[0] // ti
    t_all = xp.shape[0] + xs.shape[0]
    sp = np.cumsum([0, C_CONV, C_CONV, DK_TOT, DK_TOT, DV_TOT, DV_TOT, 2 * GATE_RANK])
    wu, wqk, wv, wg, wlow = (w_in[:, a:b].astype(BF16) for a, b in
                             ((sp[0], sp[2]), (sp[2], sp[4]), (sp[4], sp[5]), (sp[5], sp[6]), (sp[6], sp[7])))
    gw = jnp.zeros((2 * GATE_RANK, 2 * DK_TOT), F32)
    gw = gw.at[:GATE_RANK, :DK_TOT].set(gate_w[0]).at[GATE_RANK:, DK_TOT:].set(gate_w[1])
    gwh, gwl = _split(gw)
    gb = gate_b.reshape(1, 2 * DK_TOT)
    const = lambda a: pl.BlockSpec(a.shape, lambda i: (0,) * a.ndim)
    row = lambda w: pl.BlockSpec((ti, w), lambda i: (i, 0))
    seg = lambda i: jnp.where(i < n_c, 0, 1 + jnp.maximum(i - n_c, 0) // lat_tiles_per_seq)
    return pl.pallas_call(
        functools.partial(_inproj_kernel, n_ctx_tiles=n_c),
        grid=(n_c + n_l,),
        in_specs=[pl.BlockSpec((ti, D_MODEL), lambda i: (jnp.minimum(i, n_c - 1), 0)),
                  pl.BlockSpec((ti, D_MODEL), lambda i: (jnp.maximum(i - n_c, 0), 0)),
                  pl.BlockSpec((1, 1, 6 * D_MODEL), lambda i: (seg(i), 0, 0)),
                  const(g1), const(wu), const(wqk), const(wv), const(wg), const(wlow),
                  const(gwh), const(gwl), const(gb)],
        out_specs=[row(2 * C_CONV), row(2 * DK_TOT), row(DV_TOT), row(DV_TOT), row(2 * DK_TOT)],
        out_shape=[jax.ShapeDtypeStruct((t_all, 2 * C_CONV), BF16),
                   jax.ShapeDtypeStruct((t_all, 2 * DK_TOT), BF16),
                   jax.ShapeDtypeStruct((t_all, DV_TOT), BF16),
                   jax.ShapeDtypeStruct((t_all, DV_TOT), BF16),
                   jax.ShapeDtypeStruct((t_all, 2 * DK_TOT), F32)],
        compiler_params=_cparams(("arbitrary",), VMEM_LIMIT),
        name="inproj",
    )(xp, xs, mod3, g1, wu, wqk, wv, wg, wlow, gwh, gwl, gb)


def _gla_direction(qk_ref, v_ref, la_ref, tri_ref, blk_ref, o_ref, s_scr, reverse):
    la_hi, la_lo = _split(la_ref[...])
    tri = tri_ref[...]
    bcum = (jnp.dot(tri, la_hi, preferred_element_type=F32)
            + jnp.dot(tri, la_lo, preferred_element_type=F32))
    blk = blk_ref[...]
    bl_cols = (lax.dot_general(la_hi, blk, TN, preferred_element_type=F32)
               + lax.dot_general(la_lo, blk, TN, preferred_element_type=F32))
    yield
    q = qk_ref[:, 0:DK_TOT].astype(F32)
    k = qk_ref[:, DK_TOT:2 * DK_TOT].astype(F32)
    lane = lax.broadcasted_iota(jnp.int32, (CHUNK, LANES), 1)
    row2 = lax.broadcasted_iota(jnp.int32, (2 * CHUNK, CHUNK), 0) % CHUNK
    col2 = lax.broadcasted_iota(jnp.int32, (2 * CHUNK, CHUNK), 1)
    keep = (col2 >= row2) if reverse else (col2 <= row2)
    srow = lax.broadcasted_iota(jnp.int32, (LANES, LANES), 0)
    n_chunks = TM // CHUNK
    order = range(n_chunks - 1, -1, -1) if reverse else range(n_chunks)
    state = [s_scr[0], s_scr[1]]
    for c in order:
        r0 = c * CHUNK
        bc = bcum[r0:r0 + CHUNK]
        bl = bc[0:1] if reverse else bc[CHUNK - 1:CHUNK]
        qt = q[r0:r0 + CHUNK] * jnp.exp(bc) * (DK_HEAD ** -0.5)
        kt = k[r0:r0 + CHUNK] * jnp.exp(-bc)
        ke = k[r0:r0 + CHUNK] * jnp.exp(bl - bc)
        yield
        for p in range(2):
            cs = slice(p * LANES, (p + 1) * LANES)
            qt_p = qt[:, cs]
            qs = jnp.concatenate([jnp.where(lane < DK_HEAD, qt_p, 0.0),
                                  jnp.where(lane >= DK_HEAD, qt_p, 0.0)], axis=0).astype(BF16)
            att = lax.dot_general(qs, kt[:, cs].astype(BF16), NT, preferred_element_type=F32)
            att = jnp.where(keep, att, 0.0).astype(BF16)
            s_p = state[p]
            o_inter = jnp.dot(qs, s_p.astype(BF16), preferred_element_type=F32)
            ke_t = jnp.transpose(ke[:, cs]).astype(BF16)
            upd = []
            for hh in range(2):
                h = 2 * p + hh
                v_h = v_ref[r0:r0 + CHUNK, h * DV_HEAD:(h + 1) * DV_HEAD]
                both = jnp.dot(jnp.concatenate([att[hh * CHUNK:(hh + 1) * CHUNK], ke_t], axis=0), v_h,
                               preferred_element_type=F32)
                o_ref[r0:r0 + CHUNK, h * DV_HEAD:(h + 1) * DV_HEAD] = (
                    both[0:CHUNK] + o_inter[hh * CHUNK:(hh + 1) * CHUNK])
                upd.append(both[CHUNK:CHUNK + LANES])
            bl_col = bl_cols[p * LANES:(p + 1) * LANES, c * LANES:(c + 1) * LANES]
            state[p] = jnp.exp(bl_col) * s_p + jnp.where(srow < DK_HEAD, upd[0], upd[1])
            yield
    s_scr[0] = state[0]
    s_scr[1] = state[1]


def _gla_kernel(*refs, n_tiles, nb, zero_init):
    seqs = [refs[6 * r:6 * r + 6] for r in range(nb)]
    tril_ref, triu_ref, blk_ref = refs[6 * nb:6 * nb + 3]
    s0_ref = None if zero_init else refs[6 * nb + 3]
    of_ref, ob_ref, sout_ref, sf_scr, sb_scr = refs[-5:]
    j = pl.program_id(1)

    @pl.when(j == 0)
    def _():
        sf_scr[...] = jnp.zeros(sf_scr.shape, F32) if zero_init else s0_ref[:, 0]
        sb_scr[...] = jnp.zeros(sb_scr.shape, F32) if zero_init else s0_ref[:, 1]

    scans = []
    for r, (qkf_ref, vf_ref, laf_ref, qkb_ref, vb_ref, lab_ref) in enumerate(seqs):
        scans.append(_gla_direction(qkf_ref, vf_ref, laf_ref, tril_ref, blk_ref, of_ref.at[r], sf_scr.at[r], False))
        scans.append(_gla_direction(qkb_ref, vb_ref, lab_ref, triu_ref, blk_ref, ob_ref.at[r], sb_scr.at[r], True))
    while scans:
        scans = [gen for gen in scans if next(gen, "done") != "done"]

    @pl.when(j == n_tiles - 1)
    def _():
        sout_ref[:, 0] = sf_scr[...]
        sout_ref[:, 1] = sb_scr[...]


def _gla(qk, v, la, s0, n_seq, n_tiles, tile_off):
    nb = next(b for b in (4, 2, 1) if n_seq % b == 0)
    seq_len = n_tiles * TM
    blk = np.arange(TM) // CHUNK
    same = blk[:, None] == blk[None, :]
    r = np.arange(TM)
    tril = jnp.asarray((same & (r[None, :] <= r[:, None])).astype(np.float32)).astype(BF16)
    triu = jnp.asarray((same & (r[None, :] >= r[:, None])).astype(np.float32)).astype(BF16)
    chunk_of_lane = np.arange(TM // CHUNK * LANES) // LANES
    blkm = jnp.asarray((blk[:, None] == chunk_of_lane[None, :]).astype(np.float32)).astype(BF16)
    def tok(w, r, backward, cb=0):
        def index(s, j):
            jj = (n_tiles - 1 - j) if backward else j
            return ((s * nb + r) * n_tiles + jj + tile_off, cb)
        return pl.BlockSpec((TM, w), index)

    const = lambda a: pl.BlockSpec(a.shape, lambda s, j: (0,) * a.ndim)
    st = pl.BlockSpec((nb, 2, 2, LANES, LANES), lambda s, j: (s, 0, 0, 0, 0))
    seq_specs, seq_args = [], []
    for r in range(nb):
        seq_specs += [tok(2 * DK_TOT, r, False), tok(DV_TOT, r, False), tok(DK_TOT, r, False, 0),
                      tok(2 * DK_TOT, r, True), tok(DV_TOT, r, True), tok(DK_TOT, r, True, 1)]
        seq_args += [qk, v, la, qk, v, la]
    state_in = ([], []) if s0 is None else ([st], [s0])
    o_f, o_b, s_out = pl.pallas_call(
        functools.partial(_gla_kernel, n_tiles=n_tiles, nb=nb, zero_init=s0 is None),
        grid=(n_seq // nb, n_tiles),
        in_specs=seq_specs + [const(tril), const(triu), const(blkm)] + state_in[0],
        out_specs=[pl.BlockSpec((nb, TM, DV_TOT), lambda s, j: (s, j, 0)),
                   pl.BlockSpec((nb, TM, DV_TOT), lambda s, j: (s, n_tiles - 1 - j, 0)), st],
        out_shape=[jax.ShapeDtypeStruct((n_seq, seq_len, DV_TOT), F32),
                   jax.ShapeDtypeStruct((n_seq, seq_len, DV_TOT), F32),
                   jax.ShapeDtypeStruct((n_seq, 2, 2, LANES, LANES), F32)],
        scratch_shapes=[pltpu.VMEM((nb, 2, LANES, LANES), F32), pltpu.VMEM((nb, 2, LANES, LANES), F32)],
        compiler_params=_cparams(("arbitrary", "arbitrary"), VMEM_LIMIT),
        name="gla",
    )(*seq_args, tril, triu, blkm, *state_in[1])
    return o_f.reshape(n_seq * seq_len, DV_TOT), o_b.reshape(n_seq * seq_len, DV_TOT), s_out


def _mix_conv(u_ref, cw_ref, pad_scr, shf_scr, cv_scr, seq_len):
    n_seq = TM // seq_len
    pr = seq_len + 2 * CONV_PAD
    qr = seq_len + 3 * SEG

    u = u_ref[...].astype(F32)
    hg = u[:, 0:C_CONV] * jax.nn.sigmoid(u[:, C_CONV:2 * C_CONV])
    zero = jnp.zeros((CONV_PAD, C_CONV), F32)
    for s in range(n_seq):
        pad_scr[s * pr:s * pr + CONV_PAD, :] = zero
        pad_scr[s * pr + CONV_PAD:s * pr + CONV_PAD + seq_len, :] = hg[s * seq_len:(s + 1) * seq_len]
        pad_scr[s * pr + CONV_PAD + seq_len:(s + 1) * pr, :] = zero
        for b in range(1, SEG):
            shf_scr[b - 1, s * qr:(s + 1) * qr, :] = pad_scr[s * pr + b:s * pr + b + qr, :]
    rb = 8 * SEG
    per_seq = seq_len // rb
    for gi in range(C_CONV // LANES):
        gs = slice(gi * LANES, (gi + 1) * LANES)
        wv = [jnp.broadcast_to(cw_ref[tap:tap + 1, gs], (SEG, LANES)) for tap in range(CONV_K)]

        def conv_rows(i, carry, gs=gs, wv=wv):
            s = lax.shift_right_logical(i, per_seq.bit_length() - 1)
            r0 = (i & (per_seq - 1)) * rb
            nj = rb // SEG
            acc = [None] * nj
            for b in range(SEG):
                taps = [(a, SEG * a + b - (CONV_PAD - CONV_K // 2)) for a in range(CONV_PAD * 2 // SEG)]
                taps = [(a, t) for a, t in taps if 0 <= t < CONV_K]
                loaded = {}
                for a, t in taps:
                    for j in range(nj):
                        if a + j not in loaded:
                            if b == 0:
                                row = pl.multiple_of(s * pr + r0 + (a + j) * SEG, SEG)
                                loaded[a + j] = pad_scr[pl.ds(row, SEG), gs]
                            else:
                                row = pl.multiple_of(s * qr + r0 + (a + j) * SEG, SEG)
                                loaded[a + j] = shf_scr[b - 1, pl.ds(row, SEG), gs]
                        term = wv[t] * loaded[a + j]
                        acc[j] = term if acc[j] is None else acc[j] + term
            for j in range(nj):
                cv_scr[pl.ds(pl.multiple_of(s * seq_len + r0 + j * SEG, SEG), SEG), gs] = acc[j]
            return carry

        lax.fori_loop(0, TM // rb, conv_rows, 0)


def _mix_rest(x_ref, g_ref, of_ref, ob_ref, mod_ref, cv_scr, cb_ref, lng_ref, lnb_ref, gng_ref, wout_ref,
              n2g_ref, rwh_ref, rwl_ref, rb_ref, le_ref, ut_ref, x1_ref, grp_ref, route_ref, cnt_ref):
    cv = cv_scr[...] + cb_ref[...]
    mu = jnp.mean(cv, axis=-1, keepdims=True)
    var = jnp.mean(jnp.square(cv - mu), axis=-1, keepdims=True)
    cv =(cv - mu) * lax.rsqrt(var + EPS) * lng_ref[...] + lnb_ref[...]
    conv_out = (cv * jax.nn.sigmoid(cv)).astype(BF16)
    yield

    o = of_ref[...] + ob_ref[...]
    g = g_ref[...].astype(F32)
    heads = []
    for h in range(N_GLA_HEADS):
        cs = slice(h * DV_HEAD, (h + 1) * DV_HEAD)
        oh = o[:, cs]
        oh = oh * lax.rsqrt(jnp.mean(oh * oh, axis=-1, keepdims=True) + EPS) * gng_ref[...]
        gh = g[:, cs]
        heads.append((oh * (gh * jax.nn.sigmoid(gh))).astype(BF16))
    yield
    y = jnp.dot(conv_out, wout_ref[0:C_CONV, :], preferred_element_type=F32)
    for h in range(N_GLA_HEADS):
        y = y + jnp.dot(heads[h], wout_ref[C_CONV + h * DV_HEAD:C_CONV + (h + 1) * DV_HEAD, :],
                        preferred_element_type=F32)
    yield

    gate1 = mod_ref[0, :, 2 * D_MODEL:3 * D_MODEL]
    shift2 = mod_ref[0, :, 3 * D_MODEL:4 * D_MODEL]
    scale2 = mod_ref[0, :, 4 * D_MODEL:5 * D_MODEL]
    x1 = x_ref[...] + gate1 * y
    x1_ref[...] = x1
    h2 = _rms(x1, n2g_ref[...]) * (1.0 + scale2) + shift2
    h2_hi, h2_lo = _split(h2)
    yield

    ntdot = functools.partial(lax.dot_general, dimension_numbers=NT, preferred_element_type=F32)
    l = (ntdot(rwh_ref[...], h2_hi) + ntdot(rwh_ref[...], h2_lo) + ntdot(rwl_ref[...], h2_hi)
         + rb_ref[...])
    yield
    erow = lax.broadcasted_iota(jnp.int32, (N_EXPERTS, TM), 0).astype(F32)
    vals, idxs, hots = [], [], []
    for _ in range(TOP_K):
        m = jnp.max(l, axis=0, keepdims=True)
        idx = jnp.min(jnp.where(l == m, erow, float(N_EXPERTS)), axis=0, keepdims=True)
        hot = erow == idx
        l = jnp.where(hot, -jnp.inf, l)
        vals.append(m)
        idxs.append(idx)
        hots.append(hot)
        yield
    es = [jnp.exp(vk - vals[0]) for vk in vals]
    inv = 1.0 / (es[0] + es[1] + es[2] + es[3])
    sel = jnp.zeros((N_EXPERTS, TM), F32)
    for hot in hots:
        sel = jnp.where(hot, 1.0, sel)

    cnt = jnp.broadcast_to(jnp.sum(sel, axis=1, keepdims=True), (N_EXPERTS, LANES))
    cnt_ref[0] = cnt
    strip = jnp.ceil(cnt * (1.0 / SEG)) * float(SEG)
    starts = jnp.dot(le_ref[...], strip, precision=HI, preferred_element_type=F32)[:, 0:1]
    slot_of = jnp.dot(sel.astype(BF16), ut_ref[...], preferred_element_type=F32) + starts
    slots = [jnp.sum(jnp.where(hot, slot_of, 0.0), axis=0, keepdims=True) for hot in hots]

    rows = slots + idxs + [e * inv for e in es]
    srow = lax.broadcasted_iota(jnp.int32, (16, TM), 0)
    packed = jnp.zeros((16, TM), F32)
    for j, v in enumerate(rows):
        packed = jnp.where(srow == j, v, packed)
    packed = jnp.concatenate([packed, jnp.zeros((LANES - 16, TM), F32)], axis=0)
    route_ref[...] = jnp.transpose(packed)
    yield

    r = lax.broadcasted_iota(jnp.int32, (NR, TM), 0).astype(F32)
    perm = jnp.zeros((NR, TM), F32)
    for slot in slots:
        perm = jnp.where(slot == r, 1.0, perm)
    yield
    grp_ref[...] = jnp.dot(perm.astype(BF16), h2_hi, preferred_element_type=F32)


def _mix_kernel(*refs, seq_lens):
    k = len(seq_lens)
    ins = [refs[6 * p:6 * p + 6] for p in range(k)]
    cw_ref, *shared = refs[6 * k:6 * k + 12]
    outs = [refs[6 * k + 12 + 4 * p:6 * k + 16 + 4 * p] for p in range(k)]
    scr = [refs[10 * k + 12 + 3 * p:10 * k + 15 + 3 * p] for p in range(k)]
    for p in range(k):
        _mix_conv(ins[p][1], cw_ref, *scr[p], seq_lens[p])
    stages = []
    for p in range(k):
        x_ref, _, g_ref, of_ref, ob_ref, mod_ref = ins[p]
        stages.append(_mix_rest(x_ref, g_ref, of_ref, ob_ref, mod_ref, scr[p][2], *shared, *outs[p]))
    while stages:
        stages = [gen for gen in stages if next(gen, "done") != "done"]


def _mix(paths, u, g, mod3, wts):
    (conv_w, conv_b, ln_g, ln_b, gng, wout, n2g, rwh, rwl, rb) = wts
    le = jnp.asarray(np.tril(np.ones((N_EXPERTS, N_EXPERTS), np.float32), -1))
    ut = jnp.asarray(np.triu(np.ones((TM, TM), np.float32), 1)).astype(BF16)
    shared = (conv_w, conv_b, ln_g, ln_b, gng, wout, n2g, rwh, rwl, rb, le, ut)
    const = lambda a: pl.BlockSpec(a.shape, lambda i: (0,) * a.ndim)
    n_steps = max(x.shape[0] // TM for x, *_ in paths)
    in_specs, args, out_specs, out_shape, scratch = [], [], [], [], []
    for x, o_f, o_b, seg_fn, tile_off, seq_len in paths:
        n = x.shape[0] // TM
        cur = lambda i, n=n: jnp.minimum(i, n - 1)
        loc = lambda w, cur=cur: pl.BlockSpec((TM, w), lambda i: (cur(i), 0))
        uni = lambda w, cur=cur, off=tile_off: pl.BlockSpec((TM, w), lambda i: (cur(i) + off, 0))
        in_specs += [loc(D_MODEL), uni(2 * C_CONV), uni(DV_TOT), loc(DV_TOT), loc(DV_TOT),
                     pl.BlockSpec((1, 1, 6 * D_MODEL), lambda i, cur=cur, f=seg_fn: (f(cur(i)), 0, 0))]
        args += [x, u, g, o_f, o_b, mod3]
        out_specs += [loc(D_MODEL), pl.BlockSpec((NR, D_MODEL), lambda i, cur=cur: (cur(i), 0)), loc(LANES),
                      pl.BlockSpec((1, N_EXPERTS, LANES), lambda i, cur=cur: (cur(i), 0, 0))]
        out_shape += [jax.ShapeDtypeStruct((n * TM, D_MODEL), F32),
                      jax.ShapeDtypeStruct((n * NR, D_MODEL), F32),
                      jax.ShapeDtypeStruct((n * TM, LANES), F32),
                      jax.ShapeDtypeStruct((n, N_EXPERTS, LANES), F32)]
        n_seq = TM // seq_len
        scratch += [pltpu.VMEM((n_seq * (seq_len + 2 * CONV_PAD), C_CONV), F32),
                    pltpu.VMEM((SEG - 1, n_seq * (seq_len + 3 * SEG), C_CONV), F32),
                    pltpu.VMEM((TM, C_CONV), F32)]
    outs = pl.pallas_call(
        functools.partial(_mix_kernel, seq_lens=tuple(p[5] for p in paths)),
        grid=(n_steps,),
        in_specs=in_specs + [const(a) for a in shared],
        out_specs=out_specs,
        out_shape=out_shape,
        scratch_shapes=scratch,
        compiler_params=_cparams(("arbitrary",), VMEM_LIMIT),
        name="mix",
    )(*args, *shared)
    return [outs[4 * p:4 * p + 4] for p in range(len(paths))]


def _strip_copies(tile, ls_ref, gs_ref, sg_ref, local, rows_hbm, sem, to_hbm, wait):
    def body(e, carry):
        j = tile * N_EXPERTS + e
        n = pl.multiple_of(sg_ref[j], SEG)

        @pl.when(n > 0)
        def _():
            loc = local.at[pl.ds(pl.multiple_of(ls_ref[j], SEG), n)]
            glob = rows_hbm.at[pl.ds(pl.multiple_of(gs_ref[j], SEG), n)]
            cp = pltpu.make_async_copy(loc, glob, sem) if to_hbm else pltpu.make_async_copy(glob, loc, sem)
            if wait:
                cp.wait()
            else:
                cp.start()
        return carry

    lax.fori_loop(0, N_EXPERTS, body, 0)


def _expert_kernel(te_ref, nv_ref, first_ref, next_ref, rows_ref, ptr_ref, mid_ref, src_ref, dst_ref,
                   len_ref, csrc_ref, clen_ref, grpc_hbm, grpl_hbm, wgu_hbm, bgu_ref, wdn_hbm, bdn_ref,
                   y_ref, xbuf, wgu_f, wdn_f, wgu_s, wdn_s, xsem, sem, *, ctx_rows):
    i = pl.program_id(0)

    def weight_copies(e):
        return (pltpu.make_async_copy(wgu_hbm.at[e], wgu_f, sem.at[0]),
                pltpu.make_async_copy(wdn_hbm.at[e], wdn_f, sem.at[1]))

    def row_copies(tile, xs, wait):
        def strip(hbm, src, dst, n):
            cp = pltpu.make_async_copy(hbm.at[pl.ds(pl.multiple_of(src, SEG), n)],
                                       xbuf.at[xs, pl.ds(pl.multiple_of(dst, SEG), n)], xsem.at[xs])
            cp.wait() if wait else cp.start()

        def strips_of(hbm, base):
            def body(j, carry):
                n = pl.multiple_of(len_ref[j], SEG)

                @pl.when(n > 0)
                def _():
                    strip(hbm, src_ref[j] - base, dst_ref[j], n)
                return carry
            return body

        lax.fori_loop(ptr_ref[tile], mid_ref[tile], strips_of(grpc_hbm, 0), 0)
        lax.fori_loop(mid_ref[tile], ptr_ref[tile + 1], strips_of(grpl_hbm, ctx_rows), 0)
        n = pl.multiple_of(clen_ref[tile], SEG)
        src = csrc_ref[tile]

        @pl.when(jnp.logical_and(n > 0, src < ctx_rows))
        def _():
            strip(grpc_hbm, src, 0, n)

        @pl.when(jnp.logical_and(n > 0, src >= ctx_rows))
        def _():
            strip(grpl_hbm, src - ctx_rows, 0, n)

    ahead = X_RING - 1

    @pl.when(i == 0)
    def _():
        xbuf[...] = jnp.zeros(xbuf.shape, F32)
        for a in range(ahead):
            @pl.when(a < nv_ref[0])
            def _(a=a):
                row_copies(a, a, False)

    @pl.when(i + ahead < nv_ref[0])
    def _():
        row_copies(i + ahead, (i + ahead) % X_RING, False)

    @pl.when(i < nv_ref[0])
    def _():
        e = te_ref[i]
        row_copies(i, i % X_RING, True)

        @pl.when(first_ref[i] == 1)
        def _():
            @pl.when(i == 0)
            def _():
                for cp in weight_copies(e):
                    cp.start()

            for cp in weight_copies(e):
                cp.wait()
            rows = 64

            def cast(r, carry):
                sl = pl.ds(pl.multiple_of(r * rows, rows), rows)
                wgu_s[sl, :] = wgu_f[sl, :].astype(BF16)
                wdn_s[sl, :] = wdn_f[sl, :].astype(BF16)
                return carry

            lax.fori_loop(0, D_MODEL // rows, cast, 0)
            nxt = next_ref[i]

            @pl.when(nxt >= 0)
            def _():
                for cp in weight_copies(nxt):
                    cp.start()

        def compute(m):
            x = xbuf[i % X_RING, 0:m, :].astype(BF16)
            gu = jnp.dot(x, wgu_s[...], preferred_element_type=F32) + bgu_ref[0]
            gate = jnp.minimum(gu[:, 0:D_FF], SWIGLU_LIMIT)
            up = jnp.clip(gu[:, D_FF:2 * D_FF], -SWIGLU_LIMIT, SWIGLU_LIMIT)
            act = (gate * jax.nn.sigmoid(SWIGLU_ALPHA * gate) * (up + 1.0)).astype(BF16)
            y_ref[0:m, :] = jnp.dot(act, wdn_s[...], preferred_element_type=F32) + bdn_ref[0]
            if m < TE:
                y_ref[m:TE, :] = jnp.zeros((TE - m, D_MODEL), F32)

        for m in range(TE // 4, TE + 1, TE // 4):
            pl.when(rows_ref[i] == m)(functools.partial(compute, m))

    @pl.when(i >= nv_ref[0])
    def _():
        y_ref[...] = jnp.zeros(y_ref.shape, y_ref.dtype)


def _experts(plan, grp_c, grp_l, n_rows, w_gu, b_gu, w_dn, b_dn):
    nt = n_rows // TE
    exp3 = lambda i, te, *_: (te[i], 0, 0)
    any_spec = pl.BlockSpec(memory_space=pl.ANY)
    return pl.pallas_call(
        functools.partial(_expert_kernel, ctx_rows=grp_c.shape[0]),
        grid_spec=pltpu.PrefetchScalarGridSpec(
            num_scalar_prefetch=12,
            grid=(nt,),
            in_specs=[any_spec, any_spec, any_spec,
                      pl.BlockSpec((1, 1, 2 * D_FF), exp3), any_spec,
                      pl.BlockSpec((1, 1, D_MODEL), exp3)],
            out_specs=pl.BlockSpec((TE, D_MODEL), lambda i, *_: (i, 0)),
            scratch_shapes=[pltpu.VMEM((X_RING, TE, D_MODEL), F32),
                            pltpu.VMEM((D_MODEL, 2 * D_FF), F32), pltpu.VMEM((D_FF, D_MODEL), F32),
                            pltpu.VMEM((D_MODEL, 2 * D_FF), BF16), pltpu.VMEM((D_FF, D_MODEL), BF16),
                            pltpu.SemaphoreType.DMA((X_RING,)), pltpu.SemaphoreType.DMA((2,))]),
        out_shape=jax.ShapeDtypeStruct((n_rows, D_MODEL), F32),
        compiler_params=_cparams(("arbitrary",), VMEM_LIMIT),
        name="experts",
    )(plan["tile_expert"], plan["n_valid"], plan["tile_first"], plan["tile_next"], plan["tile_rows"], plan["seg_ptr"], plan["seg_mid"], plan["seg_src"], plan["seg_dst"], plan["seg_len"],
      plan["carry_src"], plan["carry_len"], grp_c, grp_l, w_gu, b_gu.reshape(N_EXPERTS, 1, 2 * D_FF), w_dn, b_dn.reshape(N_EXPERTS, 1, D_MODEL))


def _final_kernel(ls_ref, gs_ref, sg_ref, x1_ref, route_ref, mod_ref, fg_ref, ys_ref, y_ref,
                  ybuf, sem, *, tile_off, n_tiles):
    i = pl.program_id(0)
    slot = i % FINAL_RING
    ahead = FINAL_RING - 1

    def fetch(step, sl, wait):
        _strip_copies(step + tile_off, ls_ref, gs_ref, sg_ref, ybuf.at[sl], ys_ref, sem.at[sl], False, wait)

    @pl.when(i == 0)
    def _():
        ybuf[...] = jnp.zeros(ybuf.shape, F32)
        for a in range(min(ahead, n_tiles)):
            fetch(a, a, False)

    @pl.when(i + ahead < n_tiles)
    def _():
        fetch(i + ahead, (i + ahead) % FINAL_RING, False)

    fetch(i, slot, True)

    route = route_ref[...]
    r = lax.broadcasted_iota(jnp.int32, (TM, NR), 1).astype(F32)
    comb = jnp.zeros((TM, NR), F32)
    for kk in range(TOP_K):
        gate = route[:, 2 * TOP_K + kk:2 * TOP_K + kk + 1]
        comb = jnp.where(route[:, kk:kk + 1] == r, gate, comb)
    moe = jnp.dot(comb.astype(BF16), ybuf[slot].astype(BF16), preferred_element_type=F32)
    gate2 = mod_ref[0, :, 5 * D_MODEL:6 * D_MODEL]
    y_ref[...] = _rms(x1_ref[...] + gate2 * moe, fg_ref[...])


def _final(plan, x1, route, mod3, seg_fn, tile_off, fg, ys):
    t_path = x1.shape[0]
    n = t_path // TM
    loc = lambda w: pl.BlockSpec((TM, w), lambda i, *_: (i, 0))
    return pl.pallas_call(
        functools.partial(_final_kernel, tile_off=tile_off, n_tiles=n),
        grid_spec=pltpu.PrefetchScalarGridSpec(
            num_scalar_prefetch=3,
            grid=(n,),
            in_specs=[loc(D_MODEL), loc(LANES),
                      pl.BlockSpec((1, 1, 6 * D_MODEL), lambda i, *_: (seg_fn(i), 0, 0)),
                      pl.BlockSpec((1, D_MODEL), lambda i, *_: (0, 0)),
                      pl.BlockSpec(memory_space=pl.ANY)],
            out_specs=loc(D_MODEL),
            scratch_shapes=[pltpu.VMEM((FINAL_RING, NR, D_MODEL), F32),
                            pltpu.SemaphoreType.DMA((FINAL_RING,))]),
        out_shape=jax.ShapeDtypeStruct((t_path, D_MODEL), F32),
        compiler_params=_cparams(("arbitrary",), VMEM_LIMIT),
        name="final",
    )(plan["lstart"], plan["gstart"], plan["strip"], x1, route, mod3, fg, ys)


def _plan(tile_counts, n_rows, n_ctx):
    nt = n_rows // TE
    n_tok = tile_counts.shape[0]
    strip = (tile_counts + SEG - 1) // SEG * SEG
    lstart = jnp.cumsum(strip, axis=1) - strip
    rows_e = jnp.sum(strip, axis=0)
    rpad = (rows_e + TE - 1) // TE * TE
    ends = jnp.cumsum(rpad)
    offs = ends - rpad
    gstart = offs[None, :] + jnp.cumsum(strip, axis=0) - strip
    n_valid = (ends[-1] // TE).astype(jnp.int32)
    tile_ids = jnp.minimum(jnp.arange(nt, dtype=jnp.int32), n_valid - 1)
    tile_expert = jnp.sum((ends[None, :] <= (tile_ids * TE)[:, None]).astype(jnp.int32), axis=1)
    tile_expert = jnp.minimum(tile_expert, N_EXPERTS - 1).astype(jnp.int32)
    prev = jnp.concatenate([jnp.full((1,), -1, jnp.int32), tile_expert[:-1]])
    tile_first = (tile_expert != prev).astype(jnp.int32)
    pick = lambda table, idx: jnp.sum(
        jnp.where(idx[:, None] == jnp.arange(table.shape[0], dtype=jnp.int32)[None, :], table[None, :], 0), axis=1)
    used = jnp.clip(pick(offs + rows_e, tile_expert) - jnp.arange(nt, dtype=jnp.int32) * TE, 0, TE)
    tile_rows = jnp.maximum((used + TE // 4 - 1) // (TE // 4), 1) * (TE // 4)
    after = pick(ends, tile_expert) // TE
    tile_next = jnp.where(after < n_valid, pick(tile_expert, jnp.minimum(after, nt - 1)), -1)
    tok = jnp.broadcast_to(jnp.arange(n_tok, dtype=jnp.int32)[:, None], strip.shape)
    g0 = gstart.T.reshape(-1)
    ln = strip.T.reshape(-1)
    src0 = (tok * NR + lstart).T.reshape(-1)
    is_ctx = (tok < n_ctx).T.reshape(-1)
    len1 = jnp.minimum(ln, TE - g0 % TE)
    tile0 = g0 // TE
    tiles = jnp.arange(nt + 1, dtype=jnp.int32)[:, None]
    count = lambda m: jnp.sum(m.astype(jnp.int32), axis=1)
    seg_ptr = count(tile0[None, :] < tiles)
    seg_mid = seg_ptr + count((tile0[None, :] == tiles) & is_ctx[None, :])
    into = ((g0 + len1) // TE)[None, :] == tiles[:nt]
    carry_len = jnp.sum(jnp.where(into, (ln - len1)[None, :], 0), axis=1)
    carry_src = jnp.sum(jnp.where(into & (ln > len1)[None, :], (src0 + len1)[None, :], 0), axis=1)
    i32 = lambda a: a.reshape(-1).astype(jnp.int32)
    return dict(strip=i32(strip), lstart=i32(lstart), gstart=i32(gstart), n_valid=n_valid.reshape(1),
                tile_expert=tile_expert, tile_first=i32(tile_first), tile_next=i32(tile_next),
                tile_rows=i32(tile_rows), seg_ptr=i32(seg_ptr), seg_mid=i32(seg_mid), seg_src=i32(src0),
                seg_dst=i32(g0 % TE), seg_len=i32(len1), carry_src=i32(carry_src),
                carry_len=i32(carry_len))


def _layer(x_prompt, x_sample, state, c, c_ctx, ada_w, ada_b, norm1_g, norm2_g, w_in, conv_w, conv_b,
           conv_ln_g, conv_ln_b, gate_w, gate_b, gla_norm_g, w_out, router_w, router_b,
           moe_w_gu, moe_b_gu, moe_w_dn, moe_b_dn, final_g):
    bp, lp, d = x_prompt.shape
    bs, ls, _ = x_sample.shape
    assert lp == TM and ls % TM == 0 and d == D_MODEL
    xp = x_prompt.reshape(bp * lp, d)
    xs = x_sample.reshape(bs * ls, d)
    n_c, n_l = bp, bs * ls // TM
    lat_tiles = ls // TM
    t_all = (n_c + n_l) * TM

    n_cond = 1 + bs
    cond_t = jnp.concatenate([c_ctx[None, :], c, jnp.zeros((8 - n_cond, d), F32)], axis=0).T
    mod3 = _modulation(cond_t, ada_w, ada_b, n_cond).reshape(8, 1, 6 * d)

    row = lambda a: a.reshape(1, -1)
    u, qk, v, g, la = _inproj(xp, xs, mod3, row(norm1_g), w_in, gate_w, gate_b, ls)

    pair = lambda s: s.reshape(s.shape[0], 2, 2, LANES, LANES)
    of_c, ob_c, s_ctx = _gla(qk, v, la, None, bp, 1, 0)
    of_l, ob_l, _ = _gla(qk, v, la, pair(state), bs, lat_tiles, n_c)

    cw = jnp.zeros((32, C_CONV), F32).at[:CONV_K].set(conv_w)
    rwh, rwl = _split(router_w.T)
    rb = jnp.broadcast_to(router_b[:, None], (N_EXPERTS, TM))
    wts = (cw, row(conv_b), row(conv_ln_g), row(conv_ln_b), row(gla_norm_g), w_out.astype(BF16),
           row(norm2_g), rwh, rwl, rb)
    seg_c = lambda i: 0
    seg_l = lambda i: 1 + i // lat_tiles
    (x1_c, grp_c, route_c, cnt_c), (x1_l, grp_l, route_l, cnt_l) = _mix(
        [(xp, of_c, ob_c, seg_c, 0, lp), (xs, of_l, ob_l, seg_l, n_c, GRID_W)], u, g, mod3, wts)

    tile_counts = jnp.concatenate([cnt_c[:, :, 0], cnt_l[:, :, 0]]).astype(jnp.int32)
    n_rows = TOP_K * t_all + (SEG - 1) * N_EXPERTS * (n_c + n_l) + N_EXPERTS * TE
    n_rows = (n_rows + TE - 1) // TE * TE
    plan = _plan(tile_counts, n_rows, n_c)
    ysrt = _experts(plan, grp_c, grp_l, n_rows, moe_w_gu, moe_b_gu, moe_w_dn, moe_b_dn)
    y_c = _final(plan, x1_c, route_c, mod3, seg_c, 0, row(final_g), ysrt)
    y_l = _final(plan, x1_l, route_l, mod3, seg_l, n_c, row(final_g), ysrt)
    new_state = s_ctx.reshape(bp, 1, 2, N_GLA_HEADS, DK_HEAD, DV_HEAD)
    return y_c.reshape(bp, lp, d), y_l.reshape(bs, ls, d), new_state


def kernel(x_prompt, x_sample, state_gla, c, c_ctx, ada_w, ada_b, norm1_g, norm2_g, w_in, conv_w,
           conv_b, conv_ln_g, conv_ln_b, gate_w, gate_b, gla_norm_g, w_out, router_w, router_b,
           moe_w_gu, moe_b_gu, moe_w_dn, moe_b_dn, final_g):
    assert ada_w.shape[0] == 1, "single-layer step"
    return _layer(x_prompt, x_sample, state_gla[:, 0], c, c_ctx, ada_w[0], ada_b[0], norm1_g[0],
                  norm2_g[0], w_in[0], conv_w[0], conv_b[0], conv_ln_g[0], conv_ln_b[0], gate_w[0],
                  gate_b[0], gla_norm_g[0], w_out[0], router_w[0], router_b[0], moe_w_gu[0],
                  moe_b_gu[0], moe_w_dn[0], moe_b_dn[0], final_g)
```

```python
import functools

import numpy as np
import jax
import jax.numpy as jnp
from jax import lax
from jax.experimental import pallas as pl
from jax.experimental.pallas import tpu as pltpu

D_MODEL = 1024
GRID_W = 64
C_CONV = D_MODEL // 2
CONV_K = 31
N_GLA_HEADS = 4
DV_HEAD = 128
DK_HEAD = 64
DK_TOT = DK_HEAD * N_GLA_HEADS
DV_TOT = DV_HEAD * N_GLA_HEADS
GATE_RANK = 16
GATE_TEMP = 16.0
CHUNK = 64
N_EXPERTS = 32
TOP_K = 4
D_FF = D_MODEL
SWIGLU_LIMIT = 7.0
SWIGLU_ALPHA = 1.702
EPS = 1e-6

LANES = 128
SEG = 8
TM = 256
TE = 1024
NR = TOP_K * TM + TM
CONV_PAD = 16
FINAL_RING = 3
V7X_VMEM_BYTES = 64 * 1024 * 1024
VMEM_LIMIT = V7X_VMEM_BYTES - 8 * 1024 * 1024

F32 = jnp.float32
BF16 = jnp.bfloat16
HI = lax.Precision.HIGHEST
TN =(((0,), (0,)), ((), ()))
NT = (((1,), (1,)), ((), ()))

assert N_EXPERTS * (SEG - 1) <= NR - TOP_K * TM


def _split(x):
    hi = x.astype(BF16)
    return hi, (x - hi.astype(F32)).astype(BF16)


def _cparams(sem, vmem=None):
    return pltpu.CompilerParams(dimension_semantics=sem, vmem_limit_bytes=vmem)


def _mod_kernel(ct_ref, w_ref, b_ref, o_ref, *, n_cond):
    ct = ct_ref[...]
    s = ct * jax.nn.sigmoid(ct)
    w = w_ref[...]
    rows = [jnp.sum(s[:, r:r + 1] * w, axis=0, keepdims=True) + b_ref[...] for r in range(n_cond)]
    rows.append(jnp.zeros((8 - n_cond, w.shape[1]), F32))
    o_ref[...] = jnp.concatenate(rows, axis=0)


def _modulation(cond_t, ada_w, ada_b, n_cond):
    d, n = ada_w.shape
    nb = 768
    return pl.pallas_call(
        functools.partial(_mod_kernel, n_cond=n_cond),
        grid=(n // nb,),
        in_specs=[pl.BlockSpec((d, 8), lambda i: (0, 0)),
                  pl.BlockSpec((d, nb), lambda i: (0, i)),
                  pl.BlockSpec((1, nb), lambda i: (0, i))],
        out_specs=pl.BlockSpec((8, nb), lambda i: (0, i)),
        out_shape=jax.ShapeDtypeStruct((8, n), F32),
        compiler_params=_cparams(("arbitrary",)),
        name="mod",
    )(cond_t, ada_w, ada_b.reshape(1, n))


def _rms(x, g):
    return x * lax.rsqrt(jnp.mean(x * x, axis=-1, keepdims=True) + EPS) * g


def _inproj_kernel(xp_ref, xs_ref, mod_ref, g1_ref, wu_ref, wqk_ref, wv_ref, wg_ref, wlow_ref,
                   gwh_ref, gwl_ref, gb_ref, u_ref, qk_ref, v_ref, g_ref, la_ref, *, n_ctx_tiles):
    i = pl.program_id(0)
    shift = mod_ref[0, :, 0:D_MODEL]
    scale = mod_ref[0, :, D_MODEL:2 * D_MODEL]
    dot = functools.partial(jnp.dot, preferred_element_type=F32)
    rows = u_ref.shape[0]
    sub = min(rows, TM)
    def block(r0):
        rs = slice(r0, r0 + sub)
        x = jnp.where(i < n_ctx_tiles, xp_ref[rs, :], xs_ref[rs, :])
        h = (_rms(x, g1_ref[...]) * (1.0 + scale) + shift).astype(BF16)
        yield
        u_ref[rs, :] = dot(h, wu_ref[...]).astype(BF16)
        yield
        qk_ref[rs, :] = dot(h, wqk_ref[...]).astype(BF16)
        v_ref[rs, :] = dot(h, wv_ref[...]).astype(BF16)
        yield
        g_ref[rs, :] = dot(h, wg_ref[...]).astype(BF16)
        low = dot(h, wlow_ref[...])
        yield
        low_hi, low_lo = _split(low)
        z = (dot(low_hi, gwh_ref[...]) + dot(low_lo, gwh_ref[...]) + dot(low_hi, gwl_ref[...])
             + gb_ref[...])
        la_ref[rs, :] = (jnp.minimum(z, 0.0) - jnp.log1p(jnp.exp(-jnp.abs(z)))) * (1.0 / GATE_TEMP)

    blocks = [block(r0) for r0 in range(0, rows, sub)]
    while blocks:
        blocks = [gen for gen in blocks if next(gen, "done") != "done"]


def _inproj(xp, xs, mod3, g1, w_in, gate_w, gate_b, lat_len):
    ti = next(t for t in (1024, 512, TM) if xp.shape[0] % t == 0 and lat_len % t == 0)
    lat_tiles_per_seq = lat_len // ti
    n_c, n_l = xp.shape[0] // ti, xs.shape[0] // ti
    t_all = xp.shape[0] + xs.shape[0]
    sp = np.cumsum([0, C_CONV, C_CONV, DK_TOT, DK_TOT, DV_TOT, DV_TOT, 2 * GATE_RANK])
    wu, wqk, wv, wg, wlow = (w_in[:, a:b].astype(BF16) for a, b in
                             ((sp[0], sp[2]), (sp[2], sp[4]), (sp[4], sp[5]), (sp[5], sp[6]), (sp[6], sp[7])))
    gw = jnp.zeros((2 * GATE_RANK, 2 * DK_TOT), F32)
    gw = gw.at[:GATE_RANK, :DK_TOT].set(gate_w[0]).at[GATE_RANK:, DK_TOT:].set(gate_w[1])
    gwh, gwl = _split(gw)
    gb = gate_b.reshape(1, 2 * DK_TOT)
    const = lambda a: pl.BlockSpec(a.shape, lambda i: (0,) * a.ndim)
    row = lambda w: pl.BlockSpec((ti, w), lambda i: (i, 0))
    seg = lambda i: jnp.where(i < n_c, 0, 1 + jnp.maximum(i - n_c, 0) // lat_tiles_per_seq)
    return pl.pallas_call(
        functools.partial(_inproj_kernel, n_ctx_tiles=n_c),
        grid=(n_c + n_l,),
        in_specs=[pl.BlockSpec((ti, D_MODEL), lambda i: (jnp.minimum(i, n_c - 1), 0)),
                  pl.BlockSpec((ti, D_MODEL), lambda i: (jnp.maximum(i - n_c, 0), 0)),
                  pl.BlockSpec((1, 1, 6 * D_MODEL), lambda i: (seg(i), 0, 0)),
                  const(g1), const(wu), const(wqk), const(wv), const(wg), const(wlow),
                  const(gwh), const(gwl), const(gb)],
        out_specs=[row(2 * C_CONV), row(2 * DK_TOT), row(DV_TOT), row(DV_TOT), row(2 * DK_TOT)],
        out_shape=[jax.ShapeDtypeStruct((t_all, 2 * C_CONV), BF16),
                   jax.ShapeDtypeStruct((t_all, 2 * DK_TOT), BF16),
                   jax.ShapeDtypeStruct((t_all, DV_TOT), BF16),
                   jax.ShapeDtypeStruct((t_all, DV_TOT), BF16),
                   jax.ShapeDtypeStruct((t_all, 2 * DK_TOT), F32)],
        compiler_params=_cparams(("arbitrary",), VMEM_LIMIT),
        name="inproj",
    )(xp, xs, mod3, g1, wu, wqk, wv, wg, wlow, gwh, gwl, gb)


def _gla_direction(qk_ref, v_ref, la_ref, tri_ref, blk_ref, o_ref, s_scr, reverse):
    la_hi, la_lo = _split(la_ref[...])
    tri = tri_ref[...]
    bcum = (jnp.dot(tri, la_hi, preferred_element_type=F32)
            + jnp.dot(tri, la_lo, preferred_element_type=F32))
    blk = blk_ref[...]
    bl_cols = (lax.dot_general(la_hi, blk, TN, preferred_element_type=F32)
               + lax.dot_general(la_lo, blk, TN, preferred_element_type=F32))
    yield
    q = qk_ref[:, 0:DK_TOT].astype(F32)
    k = qk_ref[:, DK_TOT:2 * DK_TOT].astype(F32)
    lane = lax.broadcasted_iota(jnp.int32, (CHUNK, LANES), 1)
    row2 = lax.broadcasted_iota(jnp.int32, (2 * CHUNK, CHUNK), 0) % CHUNK
    col2 = lax.broadcasted_iota(jnp.int32, (2 * CHUNK, CHUNK), 1)
    keep = (col2 >= row2) if reverse else (col2 <= row2)
    srow = lax.broadcasted_iota(jnp.int32, (LANES, LANES), 0)
    n_chunks = TM // CHUNK
    order = range(n_chunks - 1, -1, -1) if reverse else range(n_chunks)
    state = [s_scr[0], s_scr[1]]
    for c in order:
        r0 = c * CHUNK
        bc = bcum[r0:r0 + CHUNK]
        bl = bc[0:1] if reverse else bc[CHUNK - 1:CHUNK]
        qt = q[r0:r0 + CHUNK] * jnp.exp(bc) * (DK_HEAD ** -0.5)
        kt = k[r0:r0 + CHUNK] * jnp.exp(-bc)
        ke = k[r0:r0 + CHUNK] * jnp.exp(bl - bc)
        yield
        for p in range(2):
            cs = slice(p * LANES, (p + 1) * LANES)
            qt_p = qt[:, cs]
            qs = jnp.concatenate([jnp.where(lane < DK_HEAD, qt_p, 0.0),
                                  jnp.where(lane >= DK_HEAD, qt_p, 0.0)], axis=0).astype(BF16)
            att = lax.dot_general(qs, kt[:, cs].astype(BF16), NT, preferred_element_type=F32)
            att = jnp.where(keep, att, 0.0).astype(BF16)
            s_p = state[p]
            o_inter = jnp.dot(qs, s_p.astype(BF16), preferred_element_type=F32)
            ke_t = jnp.transpose(ke[:, cs]).astype(BF16)
            upd = []
            for hh in range(2):
                h = 2 * p + hh
                v_h = v_ref[r0:r0 + CHUNK, h * DV_HEAD:(h + 1) * DV_HEAD]
                both = jnp.dot(jnp.concatenate([att[hh * CHUNK:(hh + 1) * CHUNK], ke_t], axis=0), v_h,
                               preferred_element_type=F32)
                o_ref[r0:r0 + CHUNK, h * DV_HEAD:(h + 1) * DV_HEAD] = (
                    both[0:CHUNK] + o_inter[hh * CHUNK:(hh + 1) * CHUNK]).astype(BF16)
                upd.append(both[CHUNK:CHUNK + LANES])
            bl_col = bl_cols[p * LANES:(p + 1) * LANES, c * LANES:(c + 1) * LANES]
            state[p] = jnp.exp(bl_col) * s_p + jnp.where(srow < DK_HEAD, upd[0], upd[1])
            yield
    s_scr[0] = state[0]
    s_scr[1] = state[1]


def _gla_kernel(*refs, n_tiles, nb, zero_init):
    seqs = [refs[6 * r:6 * r + 6] for r in range(nb)]
    tril_ref, triu_ref, blk_ref = refs[6 * nb:6 * nb + 3]
    s0_ref = None if zero_init else refs[6 * nb + 3]
    of_ref, ob_ref, sout_ref, sf_scr, sb_scr = refs[-5:]
    j = pl.program_id(1)

    @pl.when(j == 0)
    def _():
        sf_scr[...] = jnp.zeros(sf_scr.shape, F32) if zero_init else s0_ref[:, 0]
        sb_scr[...] = jnp.zeros(sb_scr.shape, F32) if zero_init else s0_ref[:, 1]

    scans = []
    for r, (qkf_ref, vf_ref, laf_ref, qkb_ref, vb_ref, lab_ref) in enumerate(seqs):
        scans.append(_gla_direction(qkf_ref, vf_ref, laf_ref, tril_ref, blk_ref, of_ref.at[r], sf_scr.at[r], False))
        scans.append(_gla_direction(qkb_ref, vb_ref, lab_ref, triu_ref, blk_ref, ob_ref.at[r], sb_scr.at[r], True))
    while scans:
        scans = [gen for gen in scans if next(gen, "done") != "done"]

    @pl.when(j == n_tiles - 1)
    def _():
        sout_ref[:, 0] = sf_scr[...]
        sout_ref[:, 1] = sb_scr[...]


def _gla(qk, v, la, s0, n_seq, n_tiles, tile_off):
    nb = next(b for b in (4, 2, 1) if n_seq % b == 0)
    seq_len = n_tiles * TM
    blk = np.arange(TM) // CHUNK
    same = blk[:, None] == blk[None, :]
    r = np.arange(TM)
    tril = jnp.asarray((same & (r[None, :] <= r[:, None])).astype(np.float32)).astype(BF16)
    triu = jnp.asarray((same & (r[None, :] >= r[:, None])).astype(np.float32)).astype(BF16)
    chunk_of_lane = np.arange(TM // CHUNK * LANES) // LANES
    blkm = jnp.asarray((blk[:, None] == chunk_of_lane[None, :]).astype(np.float32)).astype(BF16)
    def tok(w, r, backward, cb=0):
        def index(s, j):
            jj = (n_tiles - 1 - j) if backward else j
            return ((s * nb + r) * n_tiles + jj + tile_off, cb)
        return pl.BlockSpec((TM, w), index)

    const = lambda a: pl.BlockSpec(a.shape, lambda s, j: (0,) * a.ndim)
    st = pl.BlockSpec((nb, 2, 2, LANES, LANES), lambda s, j: (s, 0, 0, 0, 0))
    seq_specs, seq_args = [], []
    for r in range(nb):
        seq_specs += [tok(2 * DK_TOT, r, False), tok(DV_TOT, r, False), tok(DK_TOT, r, False, 0),
                      tok(2 * DK_TOT, r, True), tok(DV_TOT, r, True), tok(DK_TOT, r, True, 1)]
        seq_args += [qk, v, la, qk, v, la]
    state_in = ([], []) if s0 is None else ([st], [s0])
    o_f, o_b, s_out = pl.pallas_call(
        functools.partial(_gla_kernel, n_tiles=n_tiles, nb=nb, zero_init=s0 is None),
        grid=(n_seq // nb, n_tiles),
        in_specs=seq_specs + [const(tril), const(triu), const(blkm)] + state_in[0],
        out_specs=[pl.BlockSpec((nb, TM, DV_TOT), lambda s, j: (s, j, 0)),
                   pl.BlockSpec((nb, TM, DV_TOT), lambda s, j: (s, n_tiles - 1 - j, 0)), st],
        out_shape=[jax.ShapeDtypeStruct((n_seq, seq_len, DV_TOT), BF16),
                   jax.ShapeDtypeStruct((n_seq, seq_len, DV_TOT), BF16),
                   jax.ShapeDtypeStruct((n_seq, 2, 2, LANES, LANES), F32)],
        scratch_shapes=[pltpu.VMEM((nb, 2, LANES, LANES), F32), pltpu.VMEM((nb, 2, LANES, LANES), F32)],
        compiler_params=_cparams(("arbitrary", "arbitrary"), VMEM_LIMIT),
        name="gla",
    )(*seq_args, tril, triu, blkm, *state_in[1])
    return o_f.reshape(n_seq * seq_len, DV_TOT), o_b.reshape(n_seq * seq_len, DV_TOT), s_out


def _mix_conv(u_ref, cw_ref, pad_scr, shf_scr, cv_scr, seq_len):
    n_seq = TM // seq_len
    pr = seq_len + 2 * CONV_PAD
    qr = seq_len + 3 * SEG

    u = u_ref[...].astype(F32)
    hg = u[:, 0:C_CONV] * jax.nn.sigmoid(u[:, C_CONV:2 * C_CONV])
    zero = jnp.zeros((CONV_PAD, C_CONV), F32)
    for s in range(n_seq):
        pad_scr[s * pr:s * pr + CONV_PAD, :] = zero
        pad_scr[s * pr + CONV_PAD:s * pr + CONV_PAD + seq_len, :] = hg[s * seq_len:(s + 1) * seq_len]
        pad_scr[s * pr + CONV_PAD + seq_len:(s + 1) * pr, :] = zero
        for b in range(1, SEG):
            shf_scr[b - 1, s * qr:(s + 1) * qr, :] = pad_scr[s * pr + b:s * pr + b + qr, :]
    rb = 8 * SEG
    per_seq = seq_len // rb
    for gi in range(C_CONV // LANES):
        gs = slice(gi * LANES, (gi + 1) * LANES)
        wv = [jnp.broadcast_to(cw_ref[tap:tap + 1, gs], (SEG, LANES)) for tap in range(CONV_K)]

        def conv_rows(i, carry, gs=gs, wv=wv):
            s = lax.shift_right_logical(i, per_seq.bit_length() - 1)
            r0 = (i & (per_seq - 1)) * rb
            nj = rb // SEG
            acc = [None] * nj
            for b in range(SEG):
                taps = [(a, SEG * a + b - (CONV_PAD - CONV_K // 2)) for a in range(CONV_PAD * 2 // SEG)]
                taps = [(a, t) for a, t in taps if 0 <= t < CONV_K]
                loaded = {}
                for a, t in taps:
                    for j in range(nj):
                        if a + j not in loaded:
                            if b == 0:
                                row = pl.multiple_of(s * pr + r0 + (a + j) * SEG, SEG)
                                loaded[a + j] = pad_scr[pl.ds(row, SEG), gs]
                            else:
                                row = pl.multiple_of(s * qr + r0 + (a + j) * SEG, SEG)
                                loaded[a + j] = shf_scr[b - 1, pl.ds(row, SEG), gs]
                        term = wv[t] * loaded[a + j]
                        acc[j] = term if acc[j] is None else acc[j] + term
            for j in range(nj):
                cv_scr[pl.ds(pl.multiple_of(s * seq_len + r0 + j * SEG, SEG), SEG), gs] = acc[j]
            return carry

        lax.fori_loop(0, TM // rb, conv_rows, 0)


def _mix_rest(x_ref, g_ref, of_ref, ob_ref, mod_ref, cv_scr, cb_ref, lng_ref, lnb_ref, gng_ref, wout_ref,
              n2g_ref, rwh_ref, rwl_ref, rb_ref, le_ref, ut_ref, x1_ref, grp_ref, route_ref, cnt_ref):
    cv = cv_scr[...] + cb_ref[...]
    mu = jnp.mean(cv, axis=-1, keepdims=True)
    var = jnp.mean(jnp.square(cv - mu), axis=-1, keepdims=True)
    cv =(cv - mu) * lax.rsqrt(var + EPS) * lng_ref[...] + lnb_ref[...]
    conv_out = (cv * jax.nn.sigmoid(cv)).astype(BF16)
    yield

    o = of_ref[...].astype(F32) + ob_ref[...].astype(F32)
    g = g_ref[...].astype(F32)
    heads = []
    for h in range(N_GLA_HEADS):
        cs = slice(h * DV_HEAD, (h + 1) * DV_HEAD)
        oh = o[:, cs]
        oh = oh * lax.rsqrt(jnp.mean(oh * oh, axis=-1, keepdims=True) + EPS) * gng_ref[...]
        gh = g[:, cs]
        heads.append((oh * (gh * jax.nn.sigmoid(gh))).astype(BF16))
    yield
    y = jnp.dot(conv_out, wout_ref[0:C_CONV, :], preferred_element_type=F32)
    for h in range(N_GLA_HEADS):
        y = y + jnp.dot(heads[h], wout_ref[C_CONV + h * DV_HEAD:C_CONV + (h + 1) * DV_HEAD, :],
                        preferred_element_type=F32)
    yield

    gate1 = mod_ref[0, :, 2 * D_MODEL:3 * D_MODEL]
    shift2 = mod_ref[0, :, 3 * D_MODEL:4 * D_MODEL]
    scale2 = mod_ref[0, :, 4 * D_MODEL:5 * D_MODEL]
    x1 = x_ref[...] + gate1 * y
    x1_ref[...] = x1
    h2 = _rms(x1, n2g_ref[...]) * (1.0 + scale2) + shift2
    h2_hi, h2_lo = _split(h2)
    yield

    ntdot = functools.partial(lax.dot_general, dimension_numbers=NT, preferred_element_type=F32)
    l = (ntdot(rwh_ref[...], h2_hi) + ntdot(rwh_ref[...], h2_lo) + ntdot(rwl_ref[...], h2_hi)
         + rb_ref[...])
    yield
    erow = lax.broadcasted_iota(jnp.int32, (N_EXPERTS, TM), 0).astype(F32)
    vals, idxs, hots = [], [], []
    for _ in range(TOP_K):
        m = jnp.max(l, axis=0, keepdims=True)
        idx = jnp.min(jnp.where(l == m, erow, float(N_EXPERTS)), axis=0, keepdims=True)
        hot = erow == idx
        l = jnp.where(hot, -jnp.inf, l)
        vals.append(m)
        idxs.append(idx)
        hots.append(hot)
        yield
    es = [jnp.exp(vk - vals[0]) for vk in vals]
    inv = 1.0 / (es[0] + es[1] + es[2] + es[3])
    sel = jnp.zeros((N_EXPERTS, TM), F32)
    for hot in hots:
        sel = jnp.where(hot, 1.0, sel)

    cnt = jnp.broadcast_to(jnp.sum(sel, axis=1, keepdims=True), (N_EXPERTS, LANES))
    cnt_ref[0] = cnt
    strip = jnp.ceil(cnt * (1.0 / SEG)) * float(SEG)
    starts = jnp.dot(le_ref[...], strip, precision=HI, preferred_element_type=F32)[:, 0:1]
    slot_of = jnp.dot(sel.astype(BF16), ut_ref[...], preferred_element_type=F32) + starts
    slots = [jnp.sum(jnp.where(hot, slot_of, 0.0), axis=0, keepdims=True) for hot in hots]

    rows = slots + idxs + [e * inv for e in es]
    srow = lax.broadcasted_iota(jnp.int32, (16, TM), 0)
    packed = jnp.zeros((16, TM), F32)
    for j, v in enumerate(rows):
        packed = jnp.where(srow == j, v, packed)
    packed = jnp.concatenate([packed, jnp.zeros((LANES - 16, TM), F32)], axis=0)
    route_ref[...] = jnp.transpose(packed)
    yield

    r = lax.broadcasted_iota(jnp.int32, (NR, TM), 0).astype(F32)
    perm = jnp.zeros((NR, TM), F32)
    for slot in slots:
        perm = jnp.where(slot == r, 1.0, perm)
    yield
    grp_ref[...] = jnp.dot(perm.astype(BF16), h2_hi, preferred_element_type=F32)


def _mix_kernel(*refs, seq_lens):
    k = len(seq_lens)
    ins = [refs[6 * p:6 * p + 6] for p in range(k)]
    cw_ref, *shared = refs[6 * k:6 * k + 12]
    outs = [refs[6 * k + 12 + 4 * p:6 * k + 16 + 4 * p] for p in range(k)]
    scr = [refs[10 * k + 12 + 3 * p:10 * k + 15 + 3 * p] for p in range(k)]
    for p in range(k):
        _mix_conv(ins[p][1], cw_ref, *scr[p], seq_lens[p])
    stages = []
    for p in range(k):
        x_ref, _, g_ref, of_ref, ob_ref, mod_ref = ins[p]
        stages.append(_mix_rest(x_ref, g_ref, of_ref, ob_ref, mod_ref, scr[p][2], *shared, *outs[p]))
    while stages:
        stages = [gen for gen in stages if next(gen, "done") != "done"]


def _mix(paths, u, g, mod3, wts):
    (conv_w, conv_b, ln_g, ln_b, gng, wout, n2g, rwh, rwl, rb) = wts
    le = jnp.asarray(np.tril(np.ones((N_EXPERTS, N_EXPERTS), np.float32), -1))
    ut = jnp.asarray(np.triu(np.ones((TM, TM), np.float32), 1)).astype(BF16)
    shared = (conv_w, conv_b, ln_g, ln_b, gng, wout, n2g, rwh, rwl, rb, le, ut)
    const = lambda a: pl.BlockSpec(a.shape, lambda i: (0,) * a.ndim)
    n_steps = max(x.shape[0] // TM for x, *_ in paths)
    in_specs, args, out_specs, out_shape, scratch = [], [], [], [], []
    for x, o_f, o_b, seg_fn, tile_off, seq_len in paths:
        n = x.shape[0] // TM
        cur = lambda i, n=n: jnp.minimum(i, n - 1)
        loc = lambda w, cur=cur: pl.BlockSpec((TM, w), lambda i: (cur(i), 0))
        uni = lambda w, cur=cur, off=tile_off: pl.BlockSpec((TM, w), lambda i: (cur(i) + off, 0))
        in_specs += [loc(D_MODEL), uni(2 * C_CONV), uni(DV_TOT), loc(DV_TOT), loc(DV_TOT),
                     pl.BlockSpec((1, 1, 6 * D_MODEL), lambda i, cur=cur, f=seg_fn: (f(cur(i)), 0, 0))]
        args += [x, u, g, o_f, o_b, mod3]
        out_specs += [loc(D_MODEL), pl.BlockSpec((NR, D_MODEL), lambda i, cur=cur: (cur(i), 0)), loc(LANES),
                      pl.BlockSpec((1, N_EXPERTS, LANES), lambda i, cur=cur: (cur(i), 0, 0))]
        out_shape += [jax.ShapeDtypeStruct((n * TM, D_MODEL), F32),
                      jax.ShapeDtypeStruct((n * NR, D_MODEL), F32),
                      jax.ShapeDtypeStruct((n * TM, LANES), F32),
                      jax.ShapeDtypeStruct((n, N_EXPERTS, LANES), F32)]
        n_seq = TM // seq_len
        scratch += [pltpu.VMEM((n_seq * (seq_len + 2 * CONV_PAD), C_CONV), F32),
                    pltpu.VMEM((SEG - 1, n_seq * (seq_len + 3 * SEG), C_CONV), F32),
                    pltpu.VMEM((TM, C_CONV), F32)]
    outs = pl.pallas_call(
        functools.partial(_mix_kernel, seq_lens=tuple(p[5] for p in paths)),
        grid=(n_steps,),
        in_specs=in_specs + [const(a) for a in shared],
        out_specs=out_specs,
        out_shape=out_shape,
        scratch_shapes=scratch,
        compiler_params=_cparams(("arbitrary",), VMEM_LIMIT),
        name="mix",
    )(*args, *shared)
    return [outs[4 * p:4 * p + 4] for p in range(len(paths))]


def _strip_copies(tile, ls_ref, gs_ref, sg_ref, local, rows_hbm, sem, to_hbm, wait):
    def body(e, carry):
        j = tile * N_EXPERTS + e
        n = pl.multiple_of(sg_ref[j], SEG)

        @pl.when(n > 0)
        def _():
            loc = local.at[pl.ds(pl.multiple_of(ls_ref[j], SEG), n)]
            glob = rows_hbm.at[pl.ds(pl.multiple_of(gs_ref[j], SEG), n)]
            cp = pltpu.make_async_copy(loc, glob, sem) if to_hbm else pltpu.make_async_copy(glob, loc, sem)
            if wait:
                cp.wait()
            else:
                cp.start()
        return carry

    lax.fori_loop(0, N_EXPERTS, body, 0)


def _expert_kernel(te_ref, nv_ref, first_ref, next_ref, rows_ref, ptr_ref, mid_ref, src_ref, dst_ref,
                   len_ref, csrc_ref, clen_ref, grpc_hbm, grpl_hbm, wgu_hbm, bgu_ref, wdn_hbm, bdn_ref,
                   y_ref, xbuf, wgu_f, wdn_f, wgu_s, wdn_s, xsem, sem, *, ctx_rows):
    i = pl.program_id(0)

    def weight_copies(e):
        return (pltpu.make_async_copy(wgu_hbm.at[e], wgu_f, sem.at[0]),
                pltpu.make_async_copy(wdn_hbm.at[e], wdn_f, sem.at[1]))

    def row_copies(tile, xs, wait):
        def strip(hbm, src, dst, n):
            cp = pltpu.make_async_copy(hbm.at[pl.ds(pl.multiple_of(src, SEG), n)],
                                       xbuf.at[xs, pl.ds(pl.multiple_of(dst, SEG), n)], xsem.at[xs])
            cp.wait() if wait else cp.start()

        def strips_of(hbm, base):
            def body(j, carry):
                n = pl.multiple_of(len_ref[j], SEG)

                @pl.when(n > 0)
                def _():
                    strip(hbm, src_ref[j] - base, dst_ref[j], n)
                return carry
            return body

        lax.fori_loop(ptr_ref[tile], mid_ref[tile], strips_of(grpc_hbm, 0), 0)
        lax.fori_loop(mid_ref[tile], ptr_ref[tile + 1], strips_of(grpl_hbm, ctx_rows), 0)
        n = pl.multiple_of(clen_ref[tile], SEG)
        src = csrc_ref[tile]

        @pl.when(jnp.logical_and(n > 0, src < ctx_rows))
        def _():
            strip(grpc_hbm, src, 0, n)

        @pl.when(jnp.logical_and(n > 0, src >= ctx_rows))
        def _():
            strip(grpl_hbm, src - ctx_rows, 0, n)

    @pl.when(i == 0)
    def _():
        xbuf[...] = jnp.zeros(xbuf.shape, F32)
        row_copies(i, 0, False)

    @pl.when(i + 1 < nv_ref[0])
    def _():
        row_copies(i + 1, (i + 1) % 2, False)

    @pl.when(i < nv_ref[0])
    def _():
        e = te_ref[i]
        row_copies(i, i % 2, True)

        @pl.when(first_ref[i] == 1)
        def _():
            @pl.when(i == 0)
            def _():
                for cp in weight_copies(e):
                    cp.start()

            for cp in weight_copies(e):
                cp.wait()
            rows = 64

            def cast(r, carry):
                sl = pl.ds(pl.multiple_of(r * rows, rows), rows)
                wgu_s[sl, :] = wgu_f[sl, :].astype(BF16)
                wdn_s[sl, :] = wdn_f[sl, :].astype(BF16)
                return carry

            lax.fori_loop(0, D_MODEL // rows, cast, 0)
            nxt = next_ref[i]

            @pl.when(nxt >= 0)
            def _():
                for cp in weight_copies(nxt):
                    cp.start()

        def compute(m):
            x = xbuf[i % 2, 0:m, :].astype(BF16)
            gu = jnp.dot(x, wgu_s[...], preferred_element_type=F32) + bgu_ref[0]
            gate = jnp.minimum(gu[:, 0:D_FF], SWIGLU_LIMIT)
            up = jnp.clip(gu[:, D_FF:2 * D_FF], -SWIGLU_LIMIT, SWIGLU_LIMIT)
            act = (gate * jax.nn.sigmoid(SWIGLU_ALPHA * gate) * (up + 1.0)).astype(BF16)
            y_ref[0:m, :] = jnp.dot(act, wdn_s[...], preferred_element_type=F32) + bdn_ref[0]
            if m < TE:
                y_ref[m:TE, :] = jnp.zeros((TE - m, D_MODEL), F32)

        for m in range(TE // 4, TE + 1, TE // 4):
            pl.when(rows_ref[i] == m)(functools.partial(compute, m))

    @pl.when(i >= nv_ref[0])
    def _():
        y_ref[...] = jnp.zeros(y_ref.shape, y_ref.dtype)


def _experts(plan, grp_c, grp_l, n_rows, w_gu, b_gu, w_dn, b_dn):
    nt = n_rows // TE
    exp3 = lambda i, te, *_: (te[i], 0, 0)
    any_spec = pl.BlockSpec(memory_space=pl.ANY)
    return pl.pallas_call(
        functools.partial(_expert_kernel, ctx_rows=grp_c.shape[0]),
        grid_spec=pltpu.PrefetchScalarGridSpec(
            num_scalar_prefetch=12,
            grid=(nt,),
            in_specs=[any_spec, any_spec, any_spec,
                      pl.BlockSpec((1, 1, 2 * D_FF), exp3), any_spec,
                      pl.BlockSpec((1, 1, D_MODEL), exp3)],
            out_specs=pl.BlockSpec((TE, D_MODEL), lambda i, *_: (i, 0)),
            scratch_shapes=[pltpu.VMEM((2, TE, D_MODEL), F32),
                            pltpu.VMEM((D_MODEL, 2 * D_FF), F32), pltpu.VMEM((D_FF, D_MODEL), F32),
                            pltpu.VMEM((D_MODEL, 2 * D_FF), BF16), pltpu.VMEM((D_FF, D_MODEL), BF16),
                            pltpu.SemaphoreType.DMA((2,)), pltpu.SemaphoreType.DMA((2,))]),
        out_shape=jax.ShapeDtypeStruct((n_rows, D_MODEL), F32),
        compiler_params=_cparams(("arbitrary",), VMEM_LIMIT),
        name="experts",
    )(plan["tile_expert"], plan["n_valid"], plan["tile_first"], plan["tile_next"], plan["tile_rows"], plan["seg_ptr"], plan["seg_mid"], plan["seg_src"], plan["seg_dst"], plan["seg_len"],
      plan["carry_src"], plan["carry_len"], grp_c, grp_l, w_gu, b_gu.reshape(N_EXPERTS, 1, 2 * D_FF), w_dn, b_dn.reshape(N_EXPERTS, 1, D_MODEL))


def _final_kernel(ls_ref, gs_ref, sg_ref, x1_ref, route_ref, mod_ref, fg_ref, ys_ref, y_ref,
                  ybuf, sem, *, tile_off, n_tiles):
    i = pl.program_id(0)
    slot = i % FINAL_RING
    ahead = FINAL_RING - 1

    def fetch(step, sl, wait):
        _strip_copies(step + tile_off, ls_ref, gs_ref, sg_ref, ybuf.at[sl], ys_ref, sem.at[sl], False, wait)

    @pl.when(i == 0)
    def _():
        ybuf[...] = jnp.zeros(ybuf.shape, F32)
        for a in range(min(ahead, n_tiles)):
            fetch(a, a, False)

    @pl.when(i + ahead < n_tiles)
    def _():
        fetch(i + ahead, (i + ahead) % FINAL_RING, False)

    fetch(i, slot, True)

    route = route_ref[...]
    r = lax.broadcasted_iota(jnp.int32, (TM, NR), 1).astype(F32)
    comb = jnp.zeros((TM, NR), F32)
    for kk in range(TOP_K):
        gate = route[:, 2 * TOP_K + kk:2 * TOP_K + kk + 1]
        comb = jnp.where(route[:, kk:kk + 1] == r, gate, comb)
    moe = jnp.dot(comb.astype(BF16), ybuf[slot].astype(BF16), preferred_element_type=F32)
    gate2 = mod_ref[0, :, 5 * D_MODEL:6 * D_MODEL]
    y_ref[...] = _rms(x1_ref[...] + gate2 * moe, fg_ref[...])


def _final(plan, x1, route, mod3, seg_fn, tile_off, fg, ys):
    t_path = x1.shape[0]
    n = t_path // TM
    loc = lambda w: pl.BlockSpec((TM, w), lambda i, *_: (i, 0))
    return pl.pallas_call(
        functools.partial(_final_kernel, tile_off=tile_off, n_tiles=n),
        grid_spec=pltpu.PrefetchScalarGridSpec(
            num_scalar_prefetch=3,
            grid=(n,),
            in_specs=[loc(D_MODEL), loc(LANES),
                      pl.BlockSpec((1, 1, 6 * D_MODEL), lambda i, *_: (seg_fn(i), 0, 0)),
                      pl.BlockSpec((1, D_MODEL), lambda i, *_: (0, 0)),
                      pl.BlockSpec(memory_space=pl.ANY)],
            out_specs=loc(D_MODEL),
            scratch_shapes=[pltpu.VMEM((FINAL_RING, NR, D_MODEL), F32),
                            pltpu.SemaphoreType.DMA((FINAL_RING,))]),
        out_shape=jax.ShapeDtypeStruct((t_path, D_MODEL), F32),
        compiler_params=_cparams(("arbitrary",), VMEM_LIMIT),
        name="final",
    )(plan["lstart"], plan["gstart"], plan["strip"], x1, route, mod3, fg, ys)


def _plan(tile_counts, n_rows, n_ctx):
    nt = n_rows // TE
    n_tok = tile_counts.shape[0]
    strip = (tile_counts + SEG - 1) // SEG * SEG
    lstart = jnp.cumsum(strip, axis=1) - strip
    rows_e = jnp.sum(strip, axis=0)
    rpad = (rows_e + TE - 1) // TE * TE
    ends = jnp.cumsum(rpad)
    offs = ends - rpad
    gstart = offs[None, :] + jnp.cumsum(strip, axis=0) - strip
    n_valid = (ends[-1] // TE).astype(jnp.int32)
    tile_ids = jnp.minimum(jnp.arange(nt, dtype=jnp.int32), n_valid - 1)
    tile_expert = jnp.sum((ends[None, :] <= (tile_ids * TE)[:, None]).astype(jnp.int32), axis=1)
    tile_expert = jnp.minimum(tile_expert, N_EXPERTS - 1).astype(jnp.int32)
    prev = jnp.concatenate([jnp.full((1,), -1, jnp.int32), tile_expert[:-1]])
    tile_first = (tile_expert != prev).astype(jnp.int32)
    pick = lambda table, idx: jnp.sum(
        jnp.where(idx[:, None] == jnp.arange(table.shape[0], dtype=jnp.int32)[None, :], table[None, :], 0), axis=1)
    used = jnp.clip(pick(offs + rows_e, tile_expert) - jnp.arange(nt, dtype=jnp.int32) * TE, 0, TE)
    tile_rows = jnp.maximum((used + TE // 4 - 1) // (TE // 4), 1) * (TE // 4)
    after = pick(ends, tile_expert) // TE
    tile_next = jnp.where(after < n_valid, pick(tile_expert, jnp.minimum(after, nt - 1)), -1)
    tok = jnp.broadcast_to(jnp.arange(n_tok, dtype=jnp.int32)[:, None], strip.shape)
    g0 = gstart.T.reshape(-1)
    ln = strip.T.reshape(-1)
    src0 = (tok * NR + lstart).T.reshape(-1)
    is_ctx = (tok < n_ctx).T.reshape(-1)
    len1 = jnp.minimum(ln, TE - g0 % TE)
    tile0 = g0 // TE
    tiles = jnp.arange(nt + 1, dtype=jnp.int32)[:, None]
    count = lambda m: jnp.sum(m.astype(jnp.int32), axis=1)
    seg_ptr = count(tile0[None, :] < tiles)
    seg_mid = seg_ptr + count((tile0[None, :] == tiles) & is_ctx[None, :])
    into = ((g0 + len1) // TE)[None, :] == tiles[:nt]
    carry_len = jnp.sum(jnp.where(into, (ln - len1)[None, :], 0), axis=1)
    carry_src = jnp.sum(jnp.where(into & (ln > len1)[None, :], (src0 + len1)[None, :], 0), axis=1)
    i32 = lambda a: a.reshape(-1).astype(jnp.int32)
    return dict(strip=i32(strip), lstart=i32(lstart), gstart=i32(gstart), n_valid=n_valid.reshape(1),
                tile_expert=tile_expert, tile_first=i32(tile_first), tile_next=i32(tile_next),
                tile_rows=i32(tile_rows), seg_ptr=i32(seg_ptr), seg_mid=i32(seg_mid), seg_src=i32(src0),
                seg_dst=i32(g0 % TE), seg_len=i32(len1), carry_src=i32(carry_src),
                carry_len=i32(carry_len))


def _layer(x_prompt, x_sample, state, c, c_ctx, ada_w, ada_b, norm1_g, norm2_g, w_in, conv_w, conv_b,
           conv_ln_g, conv_ln_b, gate_w, gate_b, gla_norm_g, w_out, router_w, router_b,
           moe_w_gu, moe_b_gu, moe_w_dn, moe_b_dn, final_g):
    bp, lp, d = x_prompt.shape
    bs, ls, _ = x_sample.shape
    assert lp == TM and ls % TM == 0 and d == D_MODEL
    xp = x_prompt.reshape(bp * lp, d)
    xs = x_sample.reshape(bs * ls, d)
    n_c, n_l = bp, bs * ls // TM
    lat_tiles = ls // TM
    t_all = (n_c + n_l) * TM

    n_cond = 1 + bs
    cond_t = jnp.concatenate([c_ctx[None, :], c, jnp.zeros((8 - n_cond, d), F32)], axis=0).T
    mod3 = _modulation(cond_t, ada_w, ada_b, n_cond).reshape(8, 1, 6 * d)

    row = lambda a: a.reshape(1, -1)
    u, qk, v, g, la = _inproj(xp, xs, mod3, row(norm1_g), w_in, gate_w, gate_b, ls)

    pair = lambda s: s.reshape(s.shape[0], 2, 2, LANES, LANES)
    of_c, ob_c, s_ctx = _gla(qk, v, la, None, bp, 1, 0)
    of_l, ob_l, _ = _gla(qk, v, la, pair(state), bs, lat_tiles, n_c)

    cw = jnp.zeros((32, C_CONV), F32).at[:CONV_K].set(conv_w)
    rwh, rwl = _split(router_w.T)
    rb = jnp.broadcast_to(router_b[:, None], (N_EXPERTS, TM))
    wts = (cw, row(conv_b), row(conv_ln_g), row(conv_ln_b), row(gla_norm_g), w_out.astype(BF16),
           row(norm2_g), rwh, rwl, rb)
    seg_c = lambda i: 0
    seg_l = lambda i: 1 + i // lat_tiles
    (x1_c, grp_c, route_c, cnt_c), (x1_l, grp_l, route_l, cnt_l) = _mix(
        [(xp, of_c, ob_c, seg_c, 0, lp), (xs, of_l, ob_l, seg_l, n_c, GRID_W)], u, g, mod3, wts)

    tile_counts = jnp.concatenate([cnt_c[:, :, 0], cnt_l[:, :, 0]]).astype(jnp.int32)
    n_rows = TOP_K * t_all + (SEG - 1) * N_EXPERTS * (n_c + n_l) + N_EXPERTS * TE
    n_rows = (n_rows + TE - 1) // TE * TE
    plan = _plan(tile_counts, n_rows, n_c)
    ysrt = _experts(plan, grp_c, grp_l, n_rows, moe_w_gu, moe_b_gu, moe_w_dn, moe_b_dn)
    y_c = _final(plan, x1_c, route_c, mod3, seg_c, 0, row(final_g), ysrt)
    y_l = _final(plan, x1_l, route_l, mod3, seg_l, n_c, row(final_g), ysrt)
    new_state = s_ctx.reshape(bp, 1, 2, N_GLA_HEADS, DK_HEAD, DV_HEAD)
    return y_c.reshape(bp, lp, d), y_l.reshape(bs, ls, d), new_state


def kernel(x_prompt, x_sample, state_gla, c, c_ctx, ada_w, ada_b, norm1_g, norm2_g, w_in, conv_w,
           conv_b, conv_ln_g, conv_ln_b, gate_w, gate_b, gla_norm_g, w_out, router_w, router_b,
           moe_w_gu, moe_b_gu, moe_w_dn, moe_b_dn, final_g):
    assert ada_w.shape[0] == 1, "single-layer step"
    return _layer(x_prompt, x_sample, state_gla[:, 0], c, c_ctx, ada_w[0], ada_b[0], norm1_g[0],
                  norm2_g[0], w_in[0], conv_w[0], conv_b[0], conv_ln_g[0], conv_ln_b[0], gate_w[0],
                  gate_b[0], gla_norm_g[0], w_out[0], router_w[0], router_b[0], moe_w_gu[0],
                  moe_b_gu[0], moe_w_dn[0], moe_b_dn[0], final_g)
```
